```python
import jax, jax.numpy as jnp
from jax import lax
import numpy as np

D_MODEL = 1024
BATCH = 32
SEQ = 256
DEPTH = 2
DEC_BATCH = 4
DEC_SEQ = 1024
PAST_LEN = 512

GRID_W = 64
HEAD_DIM = 64
D_A = D_MODEL // 4
LRU_BLOCKS = 4
LRU_BW = D_A // LRU_BLOCKS
LRU_C = 8.0
CONV_W = 4
CONV_PAD_L = 2
D_B = 3 * D_MODEL // 8
H_B = D_B // HEAD_DIM
KV_B = 2
G_B = H_B // KV_B
WINDOW = 128
WIN_BLK = 128
N_BAND = 2 * WINDOW // WIN_BLK + 1
D_C = 3 * D_MODEL // 8
H_C = D_C // HEAD_DIM
NA_ROWS = 8
NA_COLS = 16
D_MIX = D_A + D_B + D_C
D_IN = 2 * D_A + D_B + 2 * KV_B * HEAD_DIM + 3 * D_C
ROPE_AXIS_DIM = HEAD_DIM // 2
ROPE_BASE = 10000.0
N_EXPERTS = 16
EC_FACTOR = 2
D_EXPERT = 2 * D_MODEL
Q_BLK = 128
N_MOD = 6
EPS = 1e-6
NEG_INF = -1e30
SCALE = HEAD_DIM ** -0.5

kernel_name = 'hybrid_flow_backbone_step'


def rms_norm(x, g):
    xf = x.astype(jnp.float32)
    y = xf * lax.rsqrt(jnp.mean(xf * xf, axis=-1, keepdims=True) + EPS)
    return (y * g.astype(jnp.float32)).astype(x.dtype)


def modulate(x, shift, scale):
    return x * (1 + scale) + shift


def adaln_params(cond, w_mod, b_mod):
    m = (jax.nn.silu(cond) @ w_mod + b_mod)[:, None, :]
    return jnp.split(m, N_MOD, axis=-1)


def split_in(u):
    sizes = (D_A, D_A, D_B, KV_B * HEAD_DIM, KV_B * HEAD_DIM, D_C, D_C, D_C)
    offs = [sum(sizes[:i]) for i in range(1, len(sizes))]
    return jnp.split(u, offs, axis=-1)


def to_heads(x, n_heads):
    b, n, _ = x.shape
    return x.reshape(b, n, n_heads, HEAD_DIM).transpose(0, 2, 1, 3)


def from_heads(x):
    b, h, n, hd = x.shape
    return x.transpose(0, 2, 1, 3).reshape(b, n, h * hd)


def axial_rope_tables(n, dtype):
    t = jnp.arange(n, dtype=jnp.int32)
    row = (t // GRID_W).astype(jnp.float32)
    col = (t % GRID_W).astype(jnp.float32)
    half = ROPE_AXIS_DIM // 2
    inv = jnp.power(ROPE_BASE, -jnp.arange(half, dtype=jnp.float32) / half)
    ang_r = row[:, None] * inv[None, :]
    ang_c = col[:, None] * inv[None, :]
    return (jnp.cos(ang_r).astype(dtype), jnp.sin(ang_r).astype(dtype),
            jnp.cos(ang_c).astype(dtype), jnp.sin(ang_c).astype(dtype))


def rope_rotate(x, cos, sin):
    half = x.shape[-1] // 2
    x1, x2 = x[..., :half], x[..., half:]
    return jnp.concatenate([x1 * cos - x2 * sin, x2 * cos + x1 * sin], axis=-1)


def apply_axial_rope(x, tables):
    cr, sr, cc, sc = tables
    return jnp.concatenate([rope_rotate(x[..., :ROPE_AXIS_DIM], cr, sr),
                            rope_rotate(x[..., ROPE_AXIS_DIM:], cc, sc)], axis=-1)


def attn_probs(s, sink):
    if sink is None:
        return jax.nn.softmax(s, axis=-1)
    sk = jnp.broadcast_to(sink.astype(jnp.float32).reshape(sink.shape + (1, 1)), s.shape[:-1] + (1,))
    return jax.nn.softmax(jnp.concatenate([s, sk], axis=-1), axis=-1)[..., :-1]


def dense_ctx_attn(q, k, v, sink):
    b, kv, g, s_len, hd = q.shape
    nb = s_len // Q_BLK
    qb = q.reshape(b, kv, g, nb, Q_BLK, hd).transpose(3, 0, 1, 2, 4, 5)

    def one(qi):
        s = jnp.einsum('bkgqd,bkjd->bkgqj', qi, k).astype(jnp.float32) * SCALE
        p = attn_probs(s, sink).astype(v.dtype)
        return jnp.einsum('bkgqj,bkjd->bkgqd', p, v)

    o = lax.map(one, qb)
    return o.transpose(1, 2, 3, 0, 4, 5).reshape(b, kv, g, s_len, hd)


def window_attn_latent(q, k, v, k_ctx, v_ctx, sink):
    b, kv, g, n, hd = q.shape
    nb = n // WIN_BLK
    nloc = N_BAND * WIN_BLK
    pad = ((0, 0), (0, 0), (WINDOW, WINDOW), (0, 0))

    def band(x):
        xp = jnp.pad(x, pad)
        parts = [xp[:, :, o * WIN_BLK:o * WIN_BLK + n].reshape(b, kv, nb, WIN_BLK, hd) for o in range(N_BAND)]
        return jnp.concatenate(parts, axis=3).transpose(2, 0, 1, 3, 4)

    kb, vb = band(k), band(v)
    qb = q.reshape(b, kv, g, nb, WIN_BLK, hd).transpose(3, 0, 1, 2, 4, 5)
    qi = jnp.arange(WIN_BLK)
    m = jnp.arange(nloc)
    band_mask = jnp.abs(qi[:, None] + WINDOW - m[None, :]) <= WINDOW

    def one(args):
        j, qj, kj, vj = args
        kpos = j * WIN_BLK - WINDOW + m
        mask = band_mask & ((kpos >= 0) & (kpos < n))[None, :]
        s_loc = jnp.einsum('bkgqd,bkmd->bkgqm', qj, kj).astype(jnp.float32) * SCALE
        s_loc = jnp.where(mask, s_loc, NEG_INF)
        s_ctx = jnp.einsum('bkgqd,bkjd->bkgqj', qj, k_ctx).astype(jnp.float32) * SCALE
        p = attn_probs(jnp.concatenate([s_loc, s_ctx], axis=-1), sink).astype(v.dtype)
        return (jnp.einsum('bkgqm,bkmd->bkgqd', p[..., :nloc], vj)
                + jnp.einsum('bkgqj,bkjd->bkgqd', p[..., nloc:], v_ctx))

    o = lax.map(one, (jnp.arange(nb), qb, kb, vb))
    return o.transpose(1, 2, 3, 0, 4, 5).reshape(b, kv, g, n, hd)


def nbr_attn_latent(q, k, v, k_ctx, v_ctx, rpb):
    b, h, n, hd = q.shape
    rows = n // GRID_W
    wr = min(NA_ROWS, rows)
    nloc = wr * GRID_W
    qg = q.reshape(b, h, rows, GRID_W, hd).transpose(2, 0, 1, 3, 4)
    kg = k.reshape(b, h, rows, GRID_W, hd)
    vg = v.reshape(b, h, rows, GRID_W, hd)
    col = jnp.arange(GRID_W)
    cs = jnp.clip(col - NA_COLS // 2, 0, GRID_W - NA_COLS)
    col_mask = (col[None, :] >= cs[:, None]) & (col[None, :] < cs[:, None] + NA_COLS)
    dc_idx = jnp.clip(col[None, :] - col[:, None], -(NA_COLS - 1), NA_COLS - 1) + (NA_COLS - 1)

    def one_row(args):
        r, qr = args
        rs = jnp.clip(r - wr // 2, 0, rows - wr)
        kr = lax.dynamic_slice_in_dim(kg, rs, wr, axis=2)
        vr = lax.dynamic_slice_in_dim(vg, rs, wr, axis=2)
        dr_idx = rs + jnp.arange(wr) - r + (NA_ROWS - 1)
        bias = rpb[:, dr_idx[None, :, None], dc_idx[:, None, :]].astype(jnp.float32)
        s_loc = jnp.einsum('bhqd,bhikd->bhqik', qr, kr).astype(jnp.float32) * SCALE + bias[None]
        s_loc = jnp.where(col_mask[:, None, :], s_loc, NEG_INF).reshape(b, h, GRID_W, nloc)
        s_ctx = jnp.einsum('bhqd,bhjd->bhqj', qr, k_ctx).astype(jnp.float32) * SCALE
        p = jax.nn.softmax(jnp.concatenate([s_loc, s_ctx], axis=-1), axis=-1).astype(v.dtype)
        return (jnp.einsum('bhqm,bhmd->bhqd', p[..., :nloc], vr.reshape(b, h, nloc, hd))
                + jnp.einsum('bhqj,bhjd->bhqd', p[..., nloc:], v_ctx))

    o = lax.map(one_row, (jnp.arange(rows), qg))
    return o.transpose(1, 2, 0, 3, 4).reshape(b, h, n, hd)


def dwconv_centred(x, w, bias):
    y = lax.conv_general_dilated(x, w[:, None, :], window_strides=(1,),
                                 padding=[(CONV_PAD_L, CONV_W - 1 - CONV_PAD_L)],
                                 dimension_numbers=('NWC', 'WIO', 'NWC'),
                                 feature_group_count=x.shape[-1])
    return y + bias


def block_diag(x, w, bias):
    xb = x.reshape(x.shape[:-1] + (LRU_BLOCKS, LRU_BW))
    return jnp.einsum('bnhi,hij->bnhj', xb, w).reshape(x.shape) + bias


def rglru_direction(x, w_r, b_r, w_i, b_i, lam, h0, reverse):
    f32 = jnp.float32
    r = jax.nn.sigmoid(block_diag(x, w_r.astype(f32), b_r.astype(f32)))
    i = jax.nn.sigmoid(block_diag(x, w_i.astype(f32), b_i.astype(f32)))
    log_a = LRU_C * r * jax.nn.log_sigmoid(lam.astype(f32))
    a = jnp.exp(log_a)
    xin = jnp.sqrt(-jnp.expm1(2.0 * log_a)) * (i * x)

    def step(h, inp):
        a_t, x_t = inp
        h = a_t * h + x_t
        return h, h

    h_last, hs = lax.scan(step, h0, (a.swapaxes(0, 1), xin.swapaxes(0, 1)), reverse=reverse)
    return hs.swapaxes(0, 1), h_last


def rglru_mixer(xa, ga, conv_w, conv_b, w_r, b_r, w_i, b_i, lam, h0):
    xc = dwconv_centred(xa, conv_w, conv_b).astype(jnp.float32)
    h0 = h0.astype(jnp.float32)
    y_f, h_f = rglru_direction(xc, w_r[0], b_r[0], w_i[0], b_i[0], lam[0], h0[:, 0], False)
    y_b, h_b = rglru_direction(xc, w_r[1], b_r[1], w_i[1], b_i[1], lam[1], h0[:, 1], True)
    y = (y_f + y_b).astype(xa.dtype) * jax.nn.gelu(ga)
    return y, jnp.stack([h_f, h_b], axis=1).astype(xa.dtype)


def ec_moe(h, w_router, w_gate, w_up, w_down):
    b, n, d = h.shape
    cap = max(1, EC_FACTOR * n // N_EXPERTS)
    aff = jax.nn.softmax((h @ w_router).astype(jnp.float32), axis=-1)
    vals, idx = lax.top_k(aff.transpose(0, 2, 1), cap)
    xs = jax.vmap(lambda hb, ib: hb[ib])(h, idx)
    a = jnp.einsum('becd,edf->becf', xs, w_gate)
    u = jnp.einsum('becd,edf->becf', xs, w_up)
    y = jnp.einsum('becf,efd->becd', jax.nn.silu(a) * u, w_down) * vals[..., None].astype(h.dtype)
    return jax.vmap(lambda ib, yb: jnp.zeros((n, d), h.dtype).at[ib.reshape(-1)].add(yb.reshape(-1, d)))(idx, y)


def finish_layer(x, mix, lw, g1, sh2, sc2, g2):
    x = x + g1 * (mix @ lw['w_out'])
    h2 = modulate(rms_norm(x, lw['g_norm2']), sh2, sc2)
    return x + g2 * ec_moe(h2, lw['w_router'], lw['w_exp_gate'], lw['w_exp_up'], lw['w_exp_down'])


def context_layer(x, c_ctx, lw):
    b, s_len, _ = x.shape
    sh1, sc1, g1, sh2, sc2, g2 = adaln_params(c_ctx[None, :], lw['w_mod'], lw['b_mod'])
    h = modulate(rms_norm(x, lw['g_norm1']), sh1, sc1)
    xa, ga, qb, kb, vb, qc, kc, vc = split_in(h @ lw['w_in'])
    h0 = jnp.zeros((b, 2, D_A), jnp.float32)
    ya, lru_state = rglru_mixer(xa, ga, lw['conv_w'], lw['conv_b'], lw['w_gate_r'], lw['b_gate_r'],
                                lw['w_gate_i'], lw['b_gate_i'], lw['lru_lambda'], h0)
    qb = to_heads(qb, H_B).reshape(b, KV_B, G_B, s_len, HEAD_DIM)
    kb = to_heads(kb, KV_B)
    vb = to_heads(vb, KV_B)
    yb = dense_ctx_attn(qb, kb, vb, lw['sink_logit']).reshape(b, H_B, s_len, HEAD_DIM)
    qc, kc, vc = to_heads(qc, H_C), to_heads(kc, H_C), to_heads(vc, H_C)
    yc = dense_ctx_attn(qc[:, :, None], kc, vc, None)[:, :, 0]
    mix = jnp.concatenate([ya, from_heads(yb), from_heads(yc)], axis=-1)
    x = finish_layer(x, mix, lw, g1, sh2, sc2, g2)
    return x, lru_state, kb, vb, kc, vc


def latent_layer(x, c, lw, lru0, k_win_ctx, v_win_ctx, k_nbr_ctx, v_nbr_ctx, rope):
    b, n, _ = x.shape
    sh1, sc1, g1, sh2, sc2, g2 = adaln_params(c, lw['w_mod'], lw['b_mod'])
    h = modulate(rms_norm(x, lw['g_norm1']), sh1, sc1)
    xa, ga, qb, kb, vb, qc, kc, vc = split_in(h @ lw['w_in'])
    ya, _ = rglru_mixer(xa, ga, lw['conv_w'], lw['conv_b'], lw['w_gate_r'], lw['b_gate_r'],
                        lw['w_gate_i'], lw['b_gate_i'], lw['lru_lambda'], lru0)
    qb = apply_axial_rope(to_heads(qb, H_B), rope).reshape(b, KV_B, G_B, n, HEAD_DIM)
    kb = apply_axial_rope(to_heads(kb, KV_B), rope)
    vb = to_heads(vb, KV_B)
    yb = window_attn_latent(qb, kb, vb, k_win_ctx, v_win_ctx, lw['sink_logit']).reshape(b, H_B, n, HEAD_DIM)
    yc = nbr_attn_latent(to_heads(qc, H_C), to_heads(kc, H_C), to_heads(vc, H_C),
                         k_nbr_ctx, v_nbr_ctx, lw['nbr_bias'])
    mix = jnp.concatenate([ya, from_heads(yb), from_heads(yc)], axis=-1)
    return finish_layer(x, mix, lw, g1, sh2, sc2, g2)


def setup_inputs(seed: int = 0) -> dict:
    key = jax.random.key(seed)
    ks = jax.random.split(key, 32)
    f32 = jnp.float32

    def nrm(k, shape, scale=1.0):
        return jax.random.normal(k, shape, f32) * scale

    def gain(k, shape):
        return 1.0 + 0.02 * jax.random.normal(k, shape, f32)

    u = jax.random.uniform(ks[14], (DEPTH, 2, D_A), f32, 0.9, 0.999)
    s = u ** (1.0 / LRU_C)
    lru_lambda = jnp.log(s) - jnp.log1p(-s)
    return {
        'x_prompt': nrm(ks[0], (BATCH, SEQ, D_MODEL)),
        'x_sample': nrm(ks[1], (DEC_BATCH, DEC_SEQ, D_MODEL)),
        'state_lru': nrm(ks[2], (DEC_BATCH, DEPTH, 2, D_A), 0.5),
        'cache_k_win': nrm(ks[3], (DEC_BATCH, DEPTH, KV_B, PAST_LEN, HEAD_DIM)),
        'cache_v_win': nrm(ks[4], (DEC_BATCH, DEPTH, KV_B, PAST_LEN, HEAD_DIM)),
        'cache_k_nbr': nrm(ks[5], (DEC_BATCH, DEPTH, H_C, PAST_LEN, HEAD_DIM)),
        'cache_v_nbr': nrm(ks[6], (DEC_BATCH, DEPTH, H_C, PAST_LEN, HEAD_DIM)),
        'c': nrm(ks[7], (DEC_BATCH, D_MODEL)),
        'c_ctx': nrm(ks[8], (D_MODEL,)),
        'w_mod': nrm(ks[9], (DEPTH, D_MODEL, N_MOD * D_MODEL), D_MODEL ** -0.5),
        'b_mod': nrm(ks[10], (DEPTH, N_MOD * D_MODEL), 0.02),
        'g_norm1': gain(ks[11], (DEPTH, D_MODEL)),
        'w_in': nrm(ks[12], (DEPTH, D_MODEL, D_IN), D_MODEL ** -0.5),
        'conv_w': nrm(ks[13], (DEPTH, CONV_W, D_A), CONV_W ** -0.5),
        'conv_b': nrm(ks[15], (DEPTH, D_A), 0.02),
        'w_gate_r': nrm(ks[16], (DEPTH, 2, LRU_BLOCKS, LRU_BW, LRU_BW), LRU_BW ** -0.5),
        'b_gate_r': nrm(ks[17], (DEPTH, 2, D_A), 0.02),
        'w_gate_i': nrm(ks[18], (DEPTH, 2, LRU_BLOCKS, LRU_BW, LRU_BW), LRU_BW ** -0.5),
        'b_gate_i': nrm(ks[19], (DEPTH, 2, D_A), 0.02),
        'lru_lambda': lru_lambda,
        'sink_logit': nrm(ks[20], (DEPTH, KV_B, G_B)),
        'nbr_bias': nrm(ks[21], (DEPTH, H_C, 2 * NA_ROWS - 1, 2 * NA_COLS - 1), 0.1),
        'w_out': nrm(ks[22], (DEPTH, D_MIX, D_MODEL), D_MIX ** -0.5),
        'g_norm2': gain(ks[23], (DEPTH, D_MODEL)),
        'w_router': nrm(ks[24], (DEPTH, D_MODEL, N_EXPERTS), D_MODEL ** -0.5),
        'w_exp_gate': nrm(ks[25], (DEPTH, N_EXPERTS, D_MODEL, D_EXPERT), D_MODEL ** -0.5),
        'w_exp_up': nrm(ks[26], (DEPTH, N_EXPERTS, D_MODEL, D_EXPERT), D_MODEL ** -0.5),
        'w_exp_down': nrm(ks[27], (DEPTH, N_EXPERTS, D_EXPERT, D_MODEL), D_EXPERT ** -0.5),
        'g_final': gain(ks[28], (D_MODEL,)),
    }


def reference(x_prompt, x_sample, state_lru, cache_k_win, cache_v_win, cache_k_nbr, cache_v_nbr, c,
              c_ctx, w_mod, b_mod, g_norm1, w_in, conv_w, conv_b, w_gate_r, b_gate_r, w_gate_i, b_gate_i,
              lru_lambda, sink_logit, nbr_bias, w_out, g_norm2, w_router, w_exp_gate, w_exp_up, w_exp_down,
              g_final):
    rope = axial_rope_tables(x_sample.shape[1], x_sample.dtype)
    xp, xs = x_prompt, x_sample
    st_lru, k_win, v_win, k_nbr, v_nbr = [], [], [], [], []
    for l in range(DEPTH):
        lw = {'w_mod': w_mod[l], 'b_mod': b_mod[l], 'g_norm1': g_norm1[l], 'w_in': w_in[l],
              'conv_w': conv_w[l], 'conv_b': conv_b[l], 'w_gate_r': w_gate_r[l], 'b_gate_r': b_gate_r[l],
              'w_gate_i': w_gate_i[l], 'b_gate_i': b_gate_i[l], 'lru_lambda': lru_lambda[l],
              'sink_logit': sink_logit[l], 'nbr_bias': nbr_bias[l], 'w_out': w_out[l], 'g_norm2': g_norm2[l],
              'w_router': w_router[l], 'w_exp_gate': w_exp_gate[l], 'w_exp_up': w_exp_up[l],
              'w_exp_down': w_exp_down[l]}
        xp, s_l, kb_l, vb_l, kc_l, vc_l = context_layer(xp, c_ctx, lw)
        st_lru.append(s_l)
        k_win.append(kb_l)
        v_win.append(vb_l)
        k_nbr.append(kc_l)
        v_nbr.append(vc_l)
        xs = latent_layer(xs, c, lw, state_lru[:, l], cache_k_win[:, l], cache_v_win[:, l],
                          cache_k_nbr[:, l], cache_v_nbr[:, l], rope)
    y_prompt = rms_norm(xp, g_final)
    y_sample = rms_norm(xs, g_final)
    return (y_prompt, y_sample, jnp.stack(st_lru, axis=1), jnp.stack(k_win, axis=1),
            jnp.stack(v_win, axis=1), jnp.stack(k_nbr, axis=1), jnp.stack(v_nbr, axis=1))
```

```python
import functools

import jax
import jax.numpy as jnp
from jax import lax
from jax.experimental import pallas as pl
from jax.experimental.pallas import tpu as pltpu

F32 = jnp.float32
BF16 = jnp.bfloat16

HEAD_DIM = 64
GRID_W = 64
LRU_C = 8.0
CONV_W = 4
KV_B = 2
G_B = 3
H_B = KV_B * G_B
H_C = 6
WINDOW = 128
WIN_BLK = 128
NA_ROWS = 8
NA_COLS = 16
ROPE_BASE = 10000.0
N_EXPERTS = 16
EC_FACTOR = 2
N_MOD = 6
EPS = 1e-6
NEG_INF = -1e30
SCALE = HEAD_DIM ** -0.5

LANES = 128
VMEM_LIMIT_BYTES = 56 * 1024 * 1024

TOKEN_BLOCK = 512
MOD_COL_BLOCK = 1536
EXPERT_F_STEPS = 2
EXPERT_ROW_CHUNK = 256


def _params(n_axes):
    return pltpu.CompilerParams(dimension_semantics=("arbitrary",) * n_axes,
                                vmem_limit_bytes=VMEM_LIMIT_BYTES)


def _dot(a, b):
    return jnp.dot(a, b, preferred_element_type=F32)


def _dot_nt(a, b):
    return lax.dot_general(a, b, (((1,), (1,)), ((), ())), preferred_element_type=F32)


def _adaln_kernel(cond_ref, w_ref, b_ref, o_ref):
    c = cond_ref[...]
    s = c * jax.nn.sigmoid(c)
    o_ref[0] = _dot(s.astype(BF16), w_ref[0].astype(BF16)) + b_ref[0]


def _adaln_call(cond8, w_mod, b_mod3):
    depth, d, n = w_mod.shape
    nb = MOD_COL_BLOCK
    return pl.pallas_call(
        _adaln_kernel,
        grid=(depth, n // nb),
        in_specs=[pl.BlockSpec((8, d), lambda l, j: (0, 0)),
                  pl.BlockSpec((1, d, nb), lambda l, j: (l, 0, j)),
                  pl.BlockSpec((1, 1, nb), lambda l, j: (l, 0, j))],
        out_specs=pl.BlockSpec((1, 8, nb), lambda l, j: (l, 0, j)),
        out_shape=jax.ShapeDtypeStruct((depth, 8, n), F32),
        compiler_params=_params(2),
        name="adaln",
    )(cond8, w_mod, b_mod3)


def _rms_modulate(x, gain, shift, scale):
    y = x * lax.rsqrt(jnp.mean(x * x, axis=-1, keepdims=True) + EPS)
    return (y * gain) * (1.0 + scale) + shift


def _inproj_kernel(*refs, rope, cache, seq):
    if rope:
        x_ref, mod_ref, g_ref, w_ref, cos_ref, sin_ref, ulru_ref, qkv_ref = refs
    elif cache:
        x_ref, mod_ref, g_ref, w_ref, ulru_ref, qkv_ref, kw_ref, vw_ref, kn_ref, vn_ref = refs
    else:
        x_ref, mod_ref, g_ref, w_ref, ulru_ref, qkv_ref = refs
    x = x_ref[0]
    h = _rms_modulate(x, g_ref[...], mod_ref[0, 0:1, :], mod_ref[0, 1:2, :])
    u = _dot(h.astype(BF16), w_ref[...])
    tb = u.shape[0]
    ulru_ref[0] = u[:, :512]
    if rope:
        qk = u[:, 512:1024]
        lane = lax.broadcasted_iota(jnp.int32, (tb, 512), 1)
        first = (lane & 31) < 16
        partner = jnp.where(first, pltpu.roll(qk, 512 - 16, 1), pltpu.roll(qk, 16, 1))
        qk = qk * cos_ref[...] + partner * sin_ref[...]
        qkv_ref[0, :, 0:512] = qk.astype(BF16)
        qkv_ref[0, :, 512:] = u[:, 1024:].astype(BF16)
    else:
        qkv_ref[0] = u[:, 512:].astype(BF16)
    if cache:
        for r in range(tb // seq):
            rows = slice(r * seq, (r + 1) * seq)
            for hh in range(KV_B):
                kw_ref[r, hh] = u[rows, 896 + 64 * hh: 896 + 64 * (hh + 1)]
                vw_ref[r, hh] = u[rows, 1024 + 64 * hh: 1024 + 64 * (hh + 1)]
            for hh in range(H_C):
                kn_ref[r, hh] = u[rows, 1536 + 64 * hh: 1536 + 64 * (hh + 1)]
                vn_ref[r, hh] = u[rows, 1920 + 64 * hh: 1920 + 64 * (hh + 1)]


def _inproj_call(x3, mod_g, gain, w_in_b, rope_tabs=None, cache_seq=None):
    g, n, d = x3.shape
    d_in = w_in_b.shape[1]
    tb = TOKEN_BLOCK
    rope = rope_tabs is not None
    cache = cache_seq is not None
    in_specs = [pl.BlockSpec((1, tb, d), lambda i, j: (i, j, 0)),
                pl.BlockSpec((1, N_MOD, d), lambda i, j: (i, 0, 0)),
                pl.BlockSpec((1, d), lambda i, j: (0, 0)),
                pl.BlockSpec((d, d_in), lambda i, j: (0, 0))]
    args = [x3, mod_g, gain, w_in_b]
    out_shape = [jax.ShapeDtypeStruct((g, n, 512), F32),
                 jax.ShapeDtypeStruct((g, n, d_in - 512), BF16)]
    out_specs = [pl.BlockSpec((1, tb, 512), lambda i, j: (i, j, 0)),
                 pl.BlockSpec((1, tb, d_in - 512), lambda i, j: (i, j, 0))]
    if rope:
        in_specs += [pl.BlockSpec((tb, 512), lambda i, j: (j, 0))] * 2
        args += list(rope_tabs)
    if cache:
        assert g == 1 and tb % cache_seq == 0
        rb = tb // cache_seq
        nreq = n // cache_seq
        for heads in (KV_B, KV_B, H_C, H_C):
            out_shape.append(jax.ShapeDtypeStruct((nreq, heads, cache_seq, HEAD_DIM), F32))
            out_specs.append(pl.BlockSpec((rb, heads, cache_seq, HEAD_DIM), lambda i, j: (j, 0, 0, 0)))
    return pl.pallas_call(
        functools.partial(_inproj_kernel, rope=rope, cache=cache, seq=cache_seq),
        grid=(g, n // tb),
        in_specs=in_specs,
        out_specs=out_specs,
        out_shape=out_shape,
        compiler_params=_params(2),
        name="inproj_rope" if rope else "inproj_ctx",
    )(*args)


def _lru_kernel(u_ref, cw_ref, cb_ref, wg_ref, bg_ref, lam_ref, h0_ref, ya_ref, st_ref,
                af_s, xf_s, ab_s, xb_s, yf_s, yb_s, *, n):
    c = 256
    nblk = n // 8
    u = u_ref[0]
    xa = u[:, :c]
    ga = u[:, c:]
    t = lax.broadcasted_iota(jnp.int32, (n, c), 0)
    cw = cw_ref[...]
    xc = cw[2:3] * xa + cb_ref[...]
    xc = xc + cw[0:1] * jnp.where(t >= 2, pltpu.roll(xa, 2, 0), 0.0)
    xc = xc + cw[1:2] * jnp.where(t >= 1, pltpu.roll(xa, 1, 0), 0.0)
    xc = xc + cw[3:4] * jnp.where(t < n - 1, pltpu.roll(xa, n - 1, 0), 0.0)
    gates = _dot(xc.astype(BF16), wg_ref[...]) + bg_ref[...]
    lam = lam_ref[...]
    log_sig = jnp.minimum(lam, 0.0) - jnp.log1p(jnp.exp(-jnp.abs(lam)))
    sub = lax.broadcasted_iota(jnp.int32, (nblk, 8, c), 1)
    for d, (a_s, x_s) in enumerate(((af_s, xf_s), (ab_s, xb_s))):
        r = jax.nn.sigmoid(gates[:, 2 * d * c:(2 * d + 1) * c])
        i = jax.nn.sigmoid(gates[:, (2 * d + 1) * c:(2 * d + 2) * c])
        log_a = LRU_C * r * log_sig[d:d + 1]
        a = jnp.exp(log_a)
        xin = jnp.sqrt(jnp.tanh(-log_a) * (1.0 + a * a)) * (i * xc)
        a3 = a.reshape(nblk, 8, c)
        x3 = xin.reshape(nblk, 8, c)
        for s in (1, 2, 4):
            if d == 0:
                m = sub >= s
                sh = s
            else:
                m = sub < 8 - s
                sh = 8 - s
            x3 = jnp.where(m, a3 * pltpu.roll(x3, sh, 1) + x3, x3)
            a3 = jnp.where(m, a3 * pltpu.roll(a3, sh, 1), a3)
        a_s[...] = a3
        x_s[...] = x3

    h0 = h0_ref[0]

    def body(k, carry):
        hf, hb = carry
        blk_f = af_s[k] * hf + xf_s[k]
        yf_s[k] = blk_f
        kb = nblk - 1 - k
        blk_b = ab_s[kb] * hb + xb_s[kb]
        yb_s[kb] = blk_b
        return blk_f[7:8], blk_b[0:1]

    hf, hb = lax.fori_loop(0, nblk, body, (h0[0:1], h0[1:2]))
    y = (yf_s[...] + yb_s[...]).reshape(n, c) * jax.nn.gelu(ga)
    ya_ref[0] = y.astype(BF16)
    st_ref[0, 0:1, :] = hf
    st_ref[0, 1:2, :] = hb


def _lru_call(ulru, conv_w, conv_b, w_gates_b, b_gates, lam, h0):
    b, n, _ = ulru.shape
    c = 256
    full = lambda shape: pl.BlockSpec(shape, lambda i: (0,) * len(shape))
    scratch = [pltpu.VMEM((n // 8, 8, c), F32) for _ in range(6)]
    return pl.pallas_call(
        functools.partial(_lru_kernel, n=n),
        grid=(b,),
        in_specs=[pl.BlockSpec((1, n, 2 * c), lambda i: (i, 0, 0)),
                  full((CONV_W, c)), full((1, c)), full((c, 4 * c)), full((1, 4 * c)), full((2, c)),
                  pl.BlockSpec((1, 2, c), lambda i: (i, 0, 0))],
        out_specs=[pl.BlockSpec((1, n, c), lambda i: (i, 0, 0)),
                   pl.BlockSpec((1, 2, c), lambda i: (i, 0, 0))],
        out_shape=[jax.ShapeDtypeStruct((b, n, c), BF16),
                   jax.ShapeDtypeStruct((b, 2, c), F32)],
        scratch_shapes=scratch,
        compiler_params=_params(1),
        name="rglru",
    )(ulru, conv_w, conv_b, w_gates_b, b_gates, lam, h0)


def _softmax_pv(parts, sink):
    mx = None
    for s, _ in parts:
        cur = jnp.max(s, axis=-1, keepdims=True)
        mx = cur if mx is None else jnp.maximum(mx, cur)
    if sink is not None:
        mx = jnp.maximum(mx, sink)
    den = None
    out = None
    for s, v in parts:
        p = jnp.exp(s - mx)
        cur = jnp.sum(p, axis=-1, keepdims=True)
        den = cur if den is None else den + cur
        o = _dot(p.astype(BF16), v)
        out = o if out is None else out + o
    if sink is not None:
        den = den + jnp.exp(sink - mx)
    return out / den


def _ctx_attn_kernel(sink_ref, qkv_ref, ob_ref, oc_ref):
    for kv in range(KV_B):
        k = qkv_ref[0, :, 384 + 64 * kv: 384 + 64 * (kv + 1)]
        v = qkv_ref[0, :, 512 + 64 * kv: 512 + 64 * (kv + 1)]
        for g in range(G_B):
            h = kv * G_B + g
            q = qkv_ref[0, :, 64 * h: 64 * (h + 1)]
            s = _dot_nt(q, k) * SCALE
            o = _softmax_pv([(s, v)], sink_ref[h])
            ob_ref[0, :, 64 * h: 64 * (h + 1)] = o.astype(BF16)
    for h in range(H_C):
        q = qkv_ref[0, :, 640 + 64 * h: 640 + 64 * (h + 1)]
        k = qkv_ref[0, :, 1024 + 64 * h: 1024 + 64 * (h + 1)]
        v = qkv_ref[0, :, 1408 + 64 * h: 1408 + 64 * (h + 1)]
        s = _dot_nt(q, k) * SCALE
        o = _softmax_pv([(s, v)], None)
        oc_ref[0, :, 64 * h: 64 * (h + 1)] = o.astype(BF16)


def _ctx_attn_call(sink6, qkv):
    b, n, w = qkv.shape
    return pl.pallas_call(
        _ctx_attn_kernel,
        grid=(b,),
        in_specs=[pl.BlockSpec(memory_space=pltpu.SMEM),
                  pl.BlockSpec((1, n, w), lambda i: (i, 0, 0))],
        out_specs=[pl.BlockSpec((1, n, 384), lambda i: (i, 0, 0))] * 2,
        out_shape=[jax.ShapeDtypeStruct((b, n, 384), BF16)] * 2,
        compiler_params=_params(1),
        name="ctx_attn",
    )(sink6, qkv)


def _win_attn_kernel(sink_ref, qkv_ref, kc_ref, vc_ref, o_ref, *, n):
    j = pl.program_id(1)
    nloc = 3 * WIN_BLK
    q0 = pl.multiple_of(j * WIN_BLK, WIN_BLK)
    ks = pl.multiple_of(jnp.clip((j - 1) * WIN_BLK, 0, n - nloc), WIN_BLK)
    row = lax.broadcasted_iota(jnp.int32, (G_B * WIN_BLK, nloc), 0)
    col = lax.broadcasted_iota(jnp.int32, (G_B * WIN_BLK, nloc), 1)
    qpos = q0 + (row & (WIN_BLK - 1))
    kpos = ks + col
    mask = jnp.abs(qpos - kpos) <= WINDOW
    rowh = lax.broadcasted_iota(jnp.int32, (G_B * WIN_BLK, 1), 0) // WIN_BLK
    for kv in range(KV_B):
        qblk = qkv_ref[0, pl.ds(q0, WIN_BLK), 192 * kv: 192 * (kv + 1)]
        q3 = jnp.concatenate([qblk[:, 64 * g: 64 * (g + 1)] for g in range(G_B)], axis=0)
        kl = qkv_ref[0, pl.ds(ks, nloc), 384 + 64 * kv: 384 + 64 * (kv + 1)]
        vl = qkv_ref[0, pl.ds(ks, nloc), 512 + 64 * kv: 512 + 64 * (kv + 1)]
        kc = kc_ref[0, 0, kv].astype(BF16)
        vc = vc_ref[0, 0, kv].astype(BF16)
        s_loc = jnp.where(mask, _dot_nt(q3, kl) * SCALE, NEG_INF)
        s_ctx = _dot_nt(q3, kc) * SCALE
        sk = jnp.where(rowh == 0, sink_ref[3 * kv],
                       jnp.where(rowh == 1, sink_ref[3 * kv + 1], sink_ref[3 * kv + 2]))
        o = _softmax_pv([(s_loc, vl), (s_ctx, vc)], sk)
        for g in range(G_B):
            h = kv * G_B + g
            o_ref[0, :, 64 * h: 64 * (h + 1)] = o[g * WIN_BLK:(g + 1) * WIN_BLK].astype(BF16)


def _win_attn_call(sink6, qkv, cache_k, cache_v, layer):
    b, n, w = qkv.shape
    past = cache_k.shape[3]
    cache_spec = pl.BlockSpec((1, 1, KV_B, past, HEAD_DIM), lambda i, j: (i, layer, 0, 0, 0))
    return pl.pallas_call(
        functools.partial(_win_attn_kernel, n=n),
        grid=(b, n // WIN_BLK),
        in_specs=[pl.BlockSpec(memory_space=pltpu.SMEM),
                  pl.BlockSpec((1, n, w), lambda i, j: (i, 0, 0)),
                  cache_spec, cache_spec],
        out_specs=pl.BlockSpec((1, WIN_BLK, 384), lambda i, j: (i, j, 0)),
        out_shape=jax.ShapeDtypeStruct((b, n, 384), BF16),
        compiler_params=_params(2),
        name="win_attn",
    )(sink6, qkv, cache_k, cache_v)


def _nbr_attn_kernel(qkv_ref, bias_ref, kc_ref, vc_ref, o_ref, *, n):
    r = pl.program_id(1)
    rows = n // GRID_W
    nloc = NA_ROWS * GRID_W
    rs = jnp.clip(r - NA_ROWS // 2, 0, rows - NA_ROWS)
    q0 = pl.multiple_of(r * GRID_W, GRID_W)
    k0 = pl.multiple_of(rs * GRID_W, GRID_W)
    for h in range(H_C):
        q = qkv_ref[0, pl.ds(q0, GRID_W), 640 + 64 * h: 640 + 64 * (h + 1)]
        kl = qkv_ref[0, pl.ds(k0, nloc), 1024 + 64 * h: 1024 + 64 * (h + 1)]
        vl = qkv_ref[0, pl.ds(k0, nloc), 1408 + 64 * h: 1408 + 64 * (h + 1)]
        kc = kc_ref[0, 0, h].astype(BF16)
        vc = vc_ref[0, 0, h].astype(BF16)
        s_loc = _dot_nt(q, kl) * SCALE + bias_ref[0, h]
        s_ctx = _dot_nt(q, kc) * SCALE
        o = _softmax_pv([(s_loc, vl), (s_ctx, vc)], None)
        o_ref[0, :, 64 * h: 64 * (h + 1)] = o.astype(BF16)


def _nbr_attn_call(qkv, bias_tab, cache_k, cache_v, layer):
    b, n, w = qkv.shape
    past = cache_k.shape[3]
    rows = n // GRID_W
    nloc = NA_ROWS * GRID_W
    cache_spec = pl.BlockSpec((1, 1, H_C, past, HEAD_DIM), lambda i, j: (i, layer, 0, 0, 0))
    return pl.pallas_call(
        functools.partial(_nbr_attn_kernel, n=n),
        grid=(b, rows),
        in_specs=[pl.BlockSpec((1, n, w), lambda i, j: (i, 0, 0)),
                  pl.BlockSpec((1, H_C, GRID_W, nloc), lambda i, j: (j, 0, 0, 0)),
                  cache_spec, cache_spec],
        out_specs=pl.BlockSpec((1, GRID_W, 384), lambda i, j: (i, j, 0)),
        out_shape=jax.ShapeDtypeStruct((b, n, 384), BF16),
        compiler_params=_params(2),
        name="nbr_attn",
    )(qkv, bias_tab, cache_k, cache_v)


def _outproj_kernel(x_ref, ya_ref, yb_ref, yc_ref, mod_ref, g_ref, wa_ref, wb_ref, wc_ref, wr_ref,
                    xn_ref, h2_ref, aff_ref):
    proj = _dot(ya_ref[0], wa_ref[...]) + _dot(yb_ref[0], wb_ref[...]) + _dot(yc_ref[0], wc_ref[...])
    xn = x_ref[0] + mod_ref[0, 2:3, :] * proj
    xn_ref[0] = xn
    h2 = _rms_modulate(xn, g_ref[...], mod_ref[0, 3:4, :], mod_ref[0, 4:5, :]).astype(BF16)
    h2_ref[0] = h2
    logits = _dot(h2, wr_ref[...])
    lane = lax.broadcasted_iota(jnp.int32, logits.shape, 1)
    valid = lane < N_EXPERTS
    logits = jnp.where(valid, logits, NEG_INF)
    e = jnp.exp(logits - jnp.max(logits, axis=-1, keepdims=True))
    e = jnp.where(valid, e, 0.0)
    aff_ref[0] = e / jnp.sum(e, axis=-1, keepdims=True)


def _outproj_call(x3, ya, yb, yc, mod_g, gain, wo_a, wo_b, wo_c, w_router_b):
    g, n, d = x3.shape
    tb = TOKEN_BLOCK
    tok = lambda w: pl.BlockSpec((1, tb, w), lambda i, j: (i, j, 0))
    full = lambda shape: pl.BlockSpec(shape, lambda i, j: (0,) * len(shape))
    return pl.pallas_call(
        _outproj_kernel,
        grid=(g, n // tb),
        in_specs=[tok(d), tok(256), tok(384), tok(384),
                  pl.BlockSpec((1, N_MOD, d), lambda i, j: (i, 0, 0)),
                  full((1, d)), full((256, d)), full((384, d)), full((384, d)), full((d, LANES))],
        out_specs=[tok(d), tok(d), tok(LANES)],
        out_shape=[jax.ShapeDtypeStruct((g, n, d), F32),
                   jax.ShapeDtypeStruct((g, n, d), BF16),
                   jax.ShapeDtypeStruct((g, n, LANES), F32)],
        compiler_params=_params(2),
        name="outproj_router",
    )(x3, ya, yb, yc, mod_g, gain, wo_a, wo_b, wo_c, w_router_b)


def _route_kernel(aff_ref, h2_ref, xs_ref, vals_ref, rank_ref, *, n, cap):
    aff = aff_ref[0]
    aff_t = aff.T
    jj = lax.broadcasted_iota(jnp.int32, (n, n), 0)
    ii = lax.broadcasted_iota(jnp.int32, (n, n), 1)
    earlier = jnp.where(jj < ii, 1.0, 0.0)
    slot = lax.broadcasted_iota(jnp.int32, (cap, n), 0).astype(F32)
    rank_rows = []
    sel = []
    for e in range(N_EXPERTS):
        a_col = aff[:, e:e + 1]
        a_row = aff_t[e:e + 1, :]
        beats = jnp.where(a_col > a_row, 1.0, jnp.where(a_col == a_row, earlier, 0.0))
        rank = jnp.minimum(jnp.sum(beats, axis=0, keepdims=True), float(cap))
        rank_rows.append(rank)
        onehot = jnp.where(rank == slot, 1.0, 0.0)
        vals_ref[e, 0] = jnp.sum(onehot * a_row, axis=1, keepdims=True)
        sel.append(onehot.astype(BF16))
    rank_t = jnp.concatenate(rank_rows + [jnp.full((LANES - N_EXPERTS, n), float(cap), F32)], axis=0)
    rank_ref[0] = rank_t.T
    xs = _dot(jnp.concatenate(sel, axis=0), h2_ref[0])
    xs_ref[:, 0] = xs.reshape(N_EXPERTS, cap, xs.shape[-1]).astype(BF16)


def _route_call(aff, h2):
    b, n, d = h2.shape
    cap = max(1, EC_FACTOR * n // N_EXPERTS)
    return pl.pallas_call(
        functools.partial(_route_kernel, n=n, cap=cap),
        grid=(b,),
        in_specs=[pl.BlockSpec((1, n, LANES), lambda i: (i, 0, 0)),
                  pl.BlockSpec((1, n, d), lambda i: (i, 0, 0))],
        out_specs=[pl.BlockSpec((N_EXPERTS, 1, cap, d), lambda i: (0, i, 0, 0)),
                   pl.BlockSpec((N_EXPERTS, 1, cap, 1), lambda i: (0, i, 0, 0)),
                   pl.BlockSpec((1, n, LANES), lambda i: (i, 0, 0))],
        out_shape=[jax.ShapeDtypeStruct((N_EXPERTS, b, cap, d), BF16),
                   jax.ShapeDtypeStruct((N_EXPERTS, b, cap, 1), F32),
                   jax.ShapeDtypeStruct((b, n, LANES), F32)],
        compiler_params=_params(1),
        name="route_gather",
    )(aff, h2)


def _expert_kernel(xc_ref, xl_ref, vc_ref, vl_ref, wg_ref, wu_ref, wd_ref, yc_ref, yl_ref, acc_c, acc_l):
    f = pl.program_id(1)
    wg = wg_ref[0, 0].astype(BF16)
    wu = wu_ref[0, 0].astype(BF16)
    wd = wd_ref[0, 0].astype(BF16)
    for x_ref, v_ref, y_ref, acc in ((xc_ref, vc_ref, yc_ref, acc_c), (xl_ref, vl_ref, yl_ref, acc_l)):
        rows = x_ref.shape[1]
        chunk = min(EXPERT_ROW_CHUNK, rows)
        for c0 in range(0, rows, chunk):
            sl = slice(c0, c0 + chunk)
            x = x_ref[0, sl, :]
            a = _dot(x, wg)
            u = _dot(x, wu)
            act = ((a * jax.nn.sigmoid(a)) * u).astype(BF16)
            part = _dot(act, wd)

            @pl.when(f == 0)
            def _():
                acc[sl, :] = part

            @pl.when(f == EXPERT_F_STEPS - 1)
            def _():
                y_ref[0, sl, :] = ((acc[sl, :] + part) * v_ref[0, sl, :]).astype(BF16)


def _expert_call(xs_c, xs_l, vals_c, vals_l, w_gate, w_up, w_down, layer):
    e, rc, d = xs_c.shape
    rl = xs_l.shape[1]
    f_total = w_gate.shape[-1]
    assert EXPERT_F_STEPS == 2
    fb = f_total // EXPERT_F_STEPS
    per_e = lambda rows, w: pl.BlockSpec((1, rows, w), lambda i, j: (i, 0, 0))
    return pl.pallas_call(
        _expert_kernel,
        grid=(e, EXPERT_F_STEPS),
        in_specs=[per_e(rc, d), per_e(rl, d), per_e(rc, 1), per_e(rl, 1),
                  pl.BlockSpec((1, 1, d, fb), lambda i, j: (layer, i, 0, j)),
                  pl.BlockSpec((1, 1, d, fb), lambda i, j: (layer, i, 0, j)),
                  pl.BlockSpec((1, 1, fb, d), lambda i, j: (layer, i, j, 0))],
        out_specs=[per_e(rc, d), per_e(rl, d)],
        out_shape=[jax.ShapeDtypeStruct((e, rc, d), BF16),
                   jax.ShapeDtypeStruct((e, rl, d), BF16)],
        scratch_shapes=[pltpu.VMEM((rc, d), F32), pltpu.VMEM((rl, d), F32)],
        compiler_params=_params(2),
        name="experts",
    )(xs_c, xs_l, vals_c, vals_l, w_gate, w_up, w_down)


def _combine_kernel(y_ref, rank_ref, xn_ref, mod_ref, gf_ref, o_ref, *, n, cap, final):
    ec = N_EXPERTS * cap
    d = xn_ref.shape[-1]
    rank = rank_ref[0].astype(BF16)
    ce = lax.broadcasted_iota(jnp.int32, (LANES, ec), 1) // cap
    ee = lax.broadcasted_iota(jnp.int32, (LANES, ec), 0)
    expand = jnp.where(ce == ee, 1.0, 0.0).astype(BF16)
    rexp = _dot(rank, expand)
    slot = (lax.broadcasted_iota(jnp.int32, (n, ec), 1) % cap).astype(F32)
    onehot = jnp.where(rexp == slot, 1.0, 0.0).astype(BF16)
    y_all = y_ref[:, 0].reshape(ec, d)
    moe = _dot(onehot, y_all)
    x = xn_ref[0] + mod_ref[0, 5:6, :] * moe
    if final:
        x = (x * lax.rsqrt(jnp.mean(x * x, axis=-1, keepdims=True) + EPS)) * gf_ref[...]
    o_ref[0] = x


def _combine_call(y4, rank, xn, mod_g, g_final, final):
    e, b, cap, d = y4.shape
    n = xn.shape[1]
    shared_mod = mod_g.shape[0] == 1
    return pl.pallas_call(
        functools.partial(_combine_kernel, n=n, cap=cap, final=final),
        grid=(b,),
        in_specs=[pl.BlockSpec((e, 1, cap, d), lambda i: (0, i, 0, 0)),
                  pl.BlockSpec((1, n, LANES), lambda i: (i, 0, 0)),
                  pl.BlockSpec((1, n, d), lambda i: (i, 0, 0)),
                  pl.BlockSpec((1, N_MOD, d), (lambda i: (0, 0, 0)) if shared_mod else (lambda i: (i, 0, 0))),
                  pl.BlockSpec((1, d), lambda i: (0, 0))],
        out_specs=pl.BlockSpec((1, n, d), lambda i: (i, 0, 0)),
        out_shape=jax.ShapeDtypeStruct((b, n, d), F32),
        compiler_params=_params(1),
        name="combine",
    )(y4, rank, xn, mod_g, g_final)


def _rope_tables(n):
    t = jnp.arange(n, dtype=jnp.int32)
    row = (t // GRID_W).astype(F32)
    col = (t % GRID_W).astype(F32)
    half = HEAD_DIM // 4
    inv = jnp.power(ROPE_BASE, -jnp.arange(half, dtype=F32) / half)
    ang_r = row[:, None] * inv[None, :]
    ang_c = col[:, None] * inv[None, :]
    cos_h = jnp.concatenate([jnp.cos(ang_r)] * 2 + [jnp.cos(ang_c)] * 2, axis=-1)
    sin_h = jnp.concatenate([-jnp.sin(ang_r), jnp.sin(ang_r), -jnp.sin(ang_c), jnp.sin(ang_c)], axis=-1)
    reps = 512 // HEAD_DIM
    return jnp.tile(cos_h, (1, reps)), jnp.tile(sin_h, (1, reps))


def _nbr_bias_table(rpb, n):
    rows = n // GRID_W
    wr = min(NA_ROWS, rows)
    col = jnp.arange(GRID_W)
    cs = jnp.clip(col - NA_COLS // 2, 0, GRID_W - NA_COLS)
    col_mask = (col[None, :] >= cs[:, None]) & (col[None, :] < cs[:, None] + NA_COLS)
    dc_idx = jnp.clip(col[None, :] - col[:, None], -(NA_COLS - 1), NA_COLS - 1) + (NA_COLS - 1)
    r = jnp.arange(rows)
    rs = jnp.clip(r - wr // 2, 0, rows - wr)
    dr_idx = rs[:, None] + jnp.arange(wr)[None, :] - r[:, None] + (NA_ROWS - 1)
    bias = rpb[:, dr_idx[:, None, :, None], dc_idx[None, :, None, :]].astype(F32)
    bias = jnp.where(col_mask[None, None, :, None, :], bias, NEG_INF)
    return bias.transpose(1, 0, 2, 3, 4).reshape(rows, rpb.shape[0], GRID_W, wr * GRID_W)


def _block_diag(w):
    nb, bw, _ = w.shape
    eye = jnp.eye(nb, dtype=w.dtype)
    return (eye[:, None, :, None] * w[:, :, None, :]).reshape(nb * bw, nb * bw)


def kernel(x_prompt, x_sample, state_lru, cache_k_win, cache_v_win, cache_k_nbr, cache_v_nbr, c, c_ctx, w_mod, b_mod, g_norm1, w_in, conv_w, conv_b, w_gate_r, b_gate_r, w_gate_i, b_gate_i, lru_lambda, sink_logit, nbr_bias, w_out, g_norm2, w_router, w_exp_gate, w_exp_up, w_exp_down, g_final):
    bc, seq, d = x_prompt.shape
    bl, n_lat, _ = x_sample.shape
    depth = w_mod.shape[0]
    assert bl + 1 <= 8 and d == 1024

    cond8 = jnp.zeros((8, d), F32).at[0].set(c_ctx).at[1:1 + bl].set(c)
    mods = _adaln_call(cond8, w_mod, b_mod.reshape(depth, 1, -1)).reshape(depth, 8, N_MOD, d)
    rope_tabs = _rope_tables(n_lat)
    gf = g_final.reshape(1, d)

    xc = x_prompt.reshape(1, bc * seq, d)
    xl = x_sample
    zeros_state = jnp.zeros((bc, 2, 256), F32)
    st_out, kw_out, vw_out, kn_out, vn_out = [], [], [], [], []
    for l in range(depth):
        final = l == depth - 1
        mod_c = mods[l, 0:1]
        mod_l = mods[l, 1:1 + bl]
        g1 = g_norm1[l].reshape(1, d)
        g2 = g_norm2[l].reshape(1, d)
        w_in_b = w_in[l].astype(BF16)
        wo = w_out[l].astype(BF16)
        wo_a, wo_b, wo_c = wo[:256], wo[256:640], wo[640:]
        wr_b = jnp.zeros((d, LANES), BF16).at[:, :N_EXPERTS].set(w_router[l].astype(BF16))
        w_gates = jnp.concatenate([_block_diag(w_gate_r[l, 0]), _block_diag(w_gate_i[l, 0]),
                                   _block_diag(w_gate_r[l, 1]), _block_diag(w_gate_i[l, 1])], axis=1).astype(BF16)
        b_gates = jnp.concatenate([b_gate_r[l, 0], b_gate_i[l, 0], b_gate_r[l, 1], b_gate_i[l, 1]]).reshape(1, -1)
        cb = conv_b[l].reshape(1, -1)
        sink6 = sink_logit[l].reshape(-1)
        bias_tab = _nbr_bias_table(nbr_bias[l], n_lat)

        ulru_c, qkv_c, kw, vw, kn, vn = _inproj_call(xc, mod_c, g1, w_in_b, cache_seq=seq)
        ya_c, st = _lru_call(ulru_c.reshape(bc, seq, 512), conv_w[l], cb, w_gates, b_gates, lru_lambda[l], zeros_state)
        yb_c, yc_c = _ctx_attn_call(sink6, qkv_c.reshape(bc, seq, -1))
        xn_c, h2_c, aff_c = _outproj_call(xc, ya_c.reshape(1, bc * seq, -1), yb_c.reshape(1, bc * seq, -1),
                                          yc_c.reshape(1, bc * seq, -1), mod_c, g2, wo_a, wo_b, wo_c, wr_b)
        xs_c, vals_c, rank_c = _route_call(aff_c.reshape(bc, seq, LANES), h2_c.reshape(bc, seq, d))
        st_out.append(st)
        kw_out.append(kw)
        vw_out.append(vw)
        kn_out.append(kn)
        vn_out.append(vn)

        ulru_l, qkv_l = _inproj_call(xl, mod_l, g1, w_in_b, rope_tabs=rope_tabs)
        ya_l, _ = _lru_call(ulru_l, conv_w[l], cb, w_gates, b_gates, lru_lambda[l], state_lru[:, l])
        yb_l = _win_attn_call(sink6, qkv_l, cache_k_win, cache_v_win, l)
        yc_l = _nbr_attn_call(qkv_l, bias_tab, cache_k_nbr, cache_v_nbr, l)
        xn_l, h2_l, aff_l = _outproj_call(xl, ya_l, yb_l, yc_l, mod_l, g2, wo_a, wo_b, wo_c, wr_b)
        xs_l, vals_l, rank_l = _route_call(aff_l, h2_l)

        cap_c, cap_l = xs_c.shape[2], xs_l.shape[2]
        y_c, y_l = _expert_call(xs_c.reshape(N_EXPERTS, bc * cap_c, d), xs_l.reshape(N_EXPERTS, bl * cap_l, d),
                                vals_c.reshape(N_EXPERTS, bc * cap_c, 1), vals_l.reshape(N_EXPERTS, bl * cap_l, 1),
                                w_exp_gate, w_exp_up, w_exp_down, l)
        xc = _combine_call(y_c.reshape(N_EXPERTS, bc, cap_c, d), rank_c, xn_c.reshape(bc, seq, d), mod_c, gf, final)
        xl = _combine_call(y_l.reshape(N_EXPERTS, bl, cap_l, d), rank_l, xn_l, mod_l, gf, final)
        xc = xc.reshape(1, bc * seq, d)

    y_prompt = xc.reshape(bc, seq, d)
    y_sample = xl
    return (y_prompt, y_sample, jnp.stack(st_out, axis=1), jnp.stack(kw_out, axis=1), jnp.stack(vw_out, axis=1),
            jnp.stack(kn_out, axis=1), jnp.stack(vn_out, axis=1))
```

```python
import functools

import numpy as np
import jax
import jax.numpy as jnp
from jax import lax
from jax.experimental import pallas as pl
from jax.experimental.pallas import tpu as pltpu

F32 = jnp.float32
BF16 = jnp.bfloat16

HEAD_DIM = 64
GRID_W = 64
LRU_C = 8.0
CONV_W = 4
KV_B = 2
G_B = 3
H_B = KV_B * G_B
H_C = 6
WINDOW = 128
WIN_BLK = 128
NA_ROWS = 8
NA_COLS = 16
ROPE_BASE = 10000.0
N_EXPERTS = 16
EC_FACTOR = 2
N_MOD = 6
EPS = 1e-6
NEG_INF = -1e30
SCALE = HEAD_DIM ** -0.5

LANES = 128
VMEM_LIMIT_BYTES = 56 * 1024 * 1024

TOKEN_BLOCK = 512
MOD_COL_BLOCK = 1536
EXPERT_F_STEPS = 2
EXPERT_ROW_CHUNK = 256


def _params(n_axes):
    return pltpu.CompilerParams(dimension_semantics=("arbitrary",) * n_axes,
                                vmem_limit_bytes=VMEM_LIMIT_BYTES)


def _dot(a, b):
    return jnp.dot(a, b, preferred_element_type=F32)


def _dot_nt(a, b):
    return lax.dot_general(a, b, (((1,), (1,)), ((), ())), preferred_element_type=F32)


def _adaln_kernel(cond_ref, w_ref, b_ref, o_ref):
    c = cond_ref[...]
    s = c * jax.nn.sigmoid(c)
    o_ref[0] = _dot(s.astype(BF16), w_ref[0].astype(BF16)) + b_ref[0]


def _adaln_call(cond8, w_mod, b_mod3):
    depth, d, n = w_mod.shape
    nb = MOD_COL_BLOCK
    return pl.pallas_call(
        _adaln_kernel,
        grid=(depth, n // nb),
        in_specs=[pl.BlockSpec((8, d), lambda l, j: (0, 0)),
                  pl.BlockSpec((1, d, nb), lambda l, j: (l, 0, j)),
                  pl.BlockSpec((1, 1, nb), lambda l, j: (l, 0, j))],
        out_specs=pl.BlockSpec((1, 8, nb), lambda l, j: (l, 0, j)),
        out_shape=jax.ShapeDtypeStruct((depth, 8, n), F32),
        compiler_params=_params(2),
        name="adaln",
    )(cond8, w_mod, b_mod3)


def _rms_modulate(x, gain, shift, scale):
    y = x * lax.rsqrt(jnp.mean(x * x, axis=-1, keepdims=True) + EPS)
    return (y * gain) * (1.0 + scale) + shift


def _inproj_kernel(*refs, rope, cache, seq):
    if rope:
        x_ref, mod_ref, g_ref, w_ref, cos_ref, sin_ref, ulru_ref, qkv_ref = refs
    elif cache:
        x_ref, mod_ref, g_ref, w_ref = refs[:4]
        ulru_ref, qkv_ref, kw_ref, vw_ref, kn_ref, vn_ref = refs[-6:]
    else:
        x_ref, mod_ref, g_ref, w_ref, ulru_ref, qkv_ref = refs
    x = x_ref[0]
    h = _rms_modulate(x, g_ref[...], mod_ref[0, 0:1, :], mod_ref[0, 1:2, :])
    u = _dot(h.astype(BF16), w_ref[...])
    tb = u.shape[0]
    ulru_ref[0] = u[:, :512]
    if rope:
        qk = u[:, 512:1024]
        lane = lax.broadcasted_iota(jnp.int32, (tb, 512), 1)
        first = (lane & 31) < 16
        partner = jnp.where(first, pltpu.roll(qk, 512 - 16, 1), pltpu.roll(qk, 16, 1))
        qk = qk * cos_ref[...] + partner * sin_ref[...]
        qkv_ref[0, :, 0:512] = qk.astype(BF16)
        qkv_ref[0, :, 512:] = u[:, 1024:].astype(BF16)
    else:
        qkv_ref[0] = u[:, 512:].astype(BF16)
    if cache:
        for r in range(tb // seq):
            rows = slice(r * seq, (r + 1) * seq)
            for hh in range(KV_B):
                kw_ref[r, 0, hh] = u[rows, 896 + 64 * hh: 896 + 64 * (hh + 1)]
                vw_ref[r, 0, hh] = u[rows, 1024 + 64 * hh: 1024 + 64 * (hh + 1)]
            for hh in range(H_C):
                kn_ref[r, 0, hh] = u[rows, 1536 + 64 * hh: 1536 + 64 * (hh + 1)]
                vn_ref[r, 0, hh] = u[rows, 1920 + 64 * hh: 1920 + 64 * (hh + 1)]


def _inproj_call(x3, mod_g, gain, w_in_b, rope_tabs=None, cache_seq=None, layer=0, depth=1, prev_caches=None):
    g, n, d = x3.shape
    d_in = w_in_b.shape[1]
    tb = TOKEN_BLOCK
    rope = rope_tabs is not None
    cache = cache_seq is not None
    in_specs = [pl.BlockSpec((1, tb, d), lambda i, j: (i, j, 0)),
                pl.BlockSpec((1, N_MOD, d), lambda i, j: (i, 0, 0)),
                pl.BlockSpec((1, d), lambda i, j: (0, 0)),
                pl.BlockSpec((d, d_in), lambda i, j: (0, 0))]
    args = [x3, mod_g, gain, w_in_b]
    out_shape = [jax.ShapeDtypeStruct((g, n, 512), F32),
                 jax.ShapeDtypeStruct((g, n, d_in - 512), BF16)]
    out_specs = [pl.BlockSpec((1, tb, 512), lambda i, j: (i, j, 0)),
                 pl.BlockSpec((1, tb, d_in - 512), lambda i, j: (i, j, 0))]
    if rope:
        in_specs += [pl.BlockSpec((tb, 512), lambda i, j: (j, 0))] * 2
        args += list(rope_tabs)
    aliases = {}
    if cache:
        assert g == 1 and tb % cache_seq == 0
        rb = tb // cache_seq
        nreq = n // cache_seq
        for heads in (KV_B, KV_B, H_C, H_C):
            out_shape.append(jax.ShapeDtypeStruct((nreq, depth, heads, cache_seq, HEAD_DIM), F32))
            out_specs.append(pl.BlockSpec((rb, 1, heads, cache_seq, HEAD_DIM), lambda i, j: (j, layer, 0, 0, 0)))
        if prev_caches is not None:
            for k, buf in enumerate(prev_caches):
                aliases[len(args)] = 2 + k
                in_specs.append(pl.BlockSpec(memory_space=pl.ANY))
                args.append(buf)
    return pl.pallas_call(
        functools.partial(_inproj_kernel, rope=rope, cache=cache, seq=cache_seq),
        grid=(g, n // tb),
        in_specs=in_specs,
        out_specs=out_specs,
        out_shape=out_shape,
        input_output_aliases=aliases,
        compiler_params=_params(2),
        name="inproj_rope" if rope else "inproj_ctx",
    )(*args)


def _lru_kernel(*refs, n):
    u_ref, cw_ref, cb_ref, wg_ref, bg_ref, lam_ref, h0_ref = refs[:7]
    ya_ref, st_ref, af_s, xf_s, ab_s, xb_s, yf_s, yb_s = refs[-8:]
    c = 256
    nblk = n // 8
    u = u_ref[0]
    xa = u[:, :c]
    ga = u[:, c:]
    t = lax.broadcasted_iota(jnp.int32, (n, c), 0)
    cw = cw_ref[...]
    xc = cw[2:3] * xa + cb_ref[...]
    xc = xc + cw[0:1] * jnp.where(t >= 2, pltpu.roll(xa, 2, 0), 0.0)
    xc = xc + cw[1:2] * jnp.where(t >= 1, pltpu.roll(xa, 1, 0), 0.0)
    xc = xc + cw[3:4] * jnp.where(t < n - 1, pltpu.roll(xa, n - 1, 0), 0.0)
    gates = _dot(xc.astype(BF16), wg_ref[...]) + bg_ref[...]
    lam = lam_ref[...]
    log_sig = jnp.minimum(lam, 0.0) - jnp.log1p(jnp.exp(-jnp.abs(lam)))
    sub = lax.broadcasted_iota(jnp.int32, (nblk, 8, c), 1)
    for d, (a_s, x_s) in enumerate(((af_s, xf_s), (ab_s, xb_s))):
        r = jax.nn.sigmoid(gates[:, 2 * d * c:(2 * d + 1) * c])
        i = jax.nn.sigmoid(gates[:, (2 * d + 1) * c:(2 * d + 2) * c])
        log_a = LRU_C * r * log_sig[d:d + 1]
        a = jnp.exp(log_a)
        xin = jnp.sqrt(jnp.tanh(-log_a) * (1.0 + a * a)) * (i * xc)
        a3 = a.reshape(nblk, 8, c)
        x3 = xin.reshape(nblk, 8, c)
        for s in (1, 2, 4):
            if d == 0:
                m = sub >= s
                sh = s
            else:
                m = sub < 8 - s
                sh = 8 - s
            x3 = jnp.where(m, a3 * pltpu.roll(x3, sh, 1) + x3, x3)
            a3 = jnp.where(m, a3 * pltpu.roll(a3, sh, 1), a3)
        a_s[...] = a3
        x_s[...] = x3

    h0 = h0_ref[0, 0]

    def body(k, carry):
        hf, hb = carry
        blk_f = af_s[k] * hf + xf_s[k]
        yf_s[k] = blk_f
        kb = nblk - 1 - k
        blk_b = ab_s[kb] * hb + xb_s[kb]
        yb_s[kb] = blk_b
        return blk_f[7:8], blk_b[0:1]

    hf, hb = lax.fori_loop(0, nblk, body, (h0[0:1], h0[1:2]))
    y = (yf_s[...] + yb_s[...]).reshape(n, c) * jax.nn.gelu(ga)
    ya_ref[0] = y.astype(BF16)
    st_ref[0, 0, 0:1, :] = hf
    st_ref[0, 0, 1:2, :] = hb


def _lru_call(ulru, conv_w, conv_b, w_gates_b, b_gates, lam, h0, h0_layer, st_layer=0, st_depth=1, prev_state=None):
    b, n, _ = ulru.shape
    c = 256
    full = lambda shape: pl.BlockSpec(shape, lambda i: (0,) * len(shape))
    scratch = [pltpu.VMEM((n // 8, 8, c), F32) for _ in range(6)]
    in_specs = [pl.BlockSpec((1, n, 2 * c), lambda i: (i, 0, 0)),
                full((CONV_W, c)), full((1, c)), full((c, 4 * c)), full((1, 4 * c)), full((2, c)),
                pl.BlockSpec((1, 1, 2, c), lambda i: (i, h0_layer, 0, 0))]
    args = [ulru, conv_w, conv_b, w_gates_b, b_gates, lam, h0]
    aliases = {}
    if prev_state is not None:
        aliases[len(args)] = 1
        in_specs.append(pl.BlockSpec(memory_space=pl.ANY))
        args.append(prev_state)
    return pl.pallas_call(
        functools.partial(_lru_kernel, n=n),
        grid=(b,),
        in_specs=in_specs,
        out_specs=[pl.BlockSpec((1, n, c), lambda i: (i, 0, 0)),
                   pl.BlockSpec((1, 1, 2, c), lambda i: (i, st_layer, 0, 0))],
        out_shape=[jax.ShapeDtypeStruct((b, n, c), BF16),
                   jax.ShapeDtypeStruct((b, st_depth, 2, c), F32)],
        scratch_shapes=scratch,
        input_output_aliases=aliases,
        compiler_params=_params(1),
        name="rglru",
    )(*args)


def _softmax_pv(parts, sink):
    mx = None
    for s, _ in parts:
        cur = jnp.max(s, axis=-1, keepdims=True)
        mx = cur if mx is None else jnp.maximum(mx, cur)
    if sink is not None:
        mx = jnp.maximum(mx, sink)
    den = None
    out = None
    for s, v in parts:
        p = jnp.exp(s - mx)
        cur = jnp.sum(p, axis=-1, keepdims=True)
        den = cur if den is None else den + cur
        o = _dot(p.astype(BF16), v)
        out = o if out is None else out + o
    if sink is not None:
        den = den + jnp.exp(sink - mx)
    return out / den


def _ctx_attn_kernel(sink_ref, qkv_ref, ob_ref, oc_ref):
    for kv in range(KV_B):
        k = qkv_ref[0, :, 384 + 64 * kv: 384 + 64 * (kv + 1)]
        v = qkv_ref[0, :, 512 + 64 * kv: 512 + 64 * (kv + 1)]
        for g in range(G_B):
            h = kv * G_B + g
            q = qkv_ref[0, :, 64 * h: 64 * (h + 1)]
            s = _dot_nt(q, k) * SCALE
            o = _softmax_pv([(s, v)], sink_ref[h])
            ob_ref[0, :, 64 * h: 64 * (h + 1)] = o.astype(BF16)
    for h in range(H_C):
        q = qkv_ref[0, :, 640 + 64 * h: 640 + 64 * (h + 1)]
        k = qkv_ref[0, :, 1024 + 64 * h: 1024 + 64 * (h + 1)]
        v = qkv_ref[0, :, 1408 + 64 * h: 1408 + 64 * (h + 1)]
        s = _dot_nt(q, k) * SCALE
        o = _softmax_pv([(s, v)], None)
        oc_ref[0, :, 64 * h: 64 * (h + 1)] = o.astype(BF16)


def _ctx_attn_call(sink6, qkv):
    b, n, w = qkv.shape
    return pl.pallas_call(
        _ctx_attn_kernel,
        grid=(b,),
        in_specs=[pl.BlockSpec(memory_space=pltpu.SMEM),
                  pl.BlockSpec((1, n, w), lambda i: (i, 0, 0))],
        out_specs=[pl.BlockSpec((1, n, 384), lambda i: (i, 0, 0))] * 2,
        out_shape=[jax.ShapeDtypeStruct((b, n, 384), BF16)] * 2,
        compiler_params=_params(1),
        name="ctx_attn",
    )(sink6, qkv)


def _win_attn_kernel(sink_ref, qkv_ref, kc_ref, vc_ref, o_ref, *, n):
    j = pl.program_id(1)
    nloc = 3 * WIN_BLK
    q0 = pl.multiple_of(j * WIN_BLK, WIN_BLK)
    ks = pl.multiple_of(jnp.clip((j - 1) * WIN_BLK, 0, n - nloc), WIN_BLK)
    row = lax.broadcasted_iota(jnp.int32, (G_B * WIN_BLK, nloc), 0)
    col = lax.broadcasted_iota(jnp.int32, (G_B * WIN_BLK, nloc), 1)
    qpos = q0 + (row & (WIN_BLK - 1))
    kpos = ks + col
    mask = jnp.abs(qpos - kpos) <= WINDOW
    rowh = lax.broadcasted_iota(jnp.int32, (G_B * WIN_BLK, 1), 0) // WIN_BLK
    for kv in range(KV_B):
        qblk = qkv_ref[0, pl.ds(q0, WIN_BLK), 192 * kv: 192 * (kv + 1)]
        q3 = jnp.concatenate([qblk[:, 64 * g: 64 * (g + 1)] for g in range(G_B)], axis=0)
        kl = qkv_ref[0, pl.ds(ks, nloc), 384 + 64 * kv: 384 + 64 * (kv + 1)]
        vl = qkv_ref[0, pl.ds(ks, nloc), 512 + 64 * kv: 512 + 64 * (kv + 1)]
        kc = kc_ref[0, 0, kv].astype(BF16)
        vc = vc_ref[0, 0, kv].astype(BF16)
        s_loc = jnp.where(mask, _dot_nt(q3, kl) * SCALE, NEG_INF)
        s_ctx = _dot_nt(q3, kc) * SCALE
        sk = jnp.where(rowh == 0, sink_ref[3 * kv],
                       jnp.where(rowh == 1, sink_ref[3 * kv + 1], sink_ref[3 * kv + 2]))
        o = _softmax_pv([(s_loc, vl), (s_ctx, vc)], sk)
        for g in range(G_B):
            h = kv * G_B + g
            o_ref[0, :, 64 * h: 64 * (h + 1)] = o[g * WIN_BLK:(g + 1) * WIN_BLK].astype(BF16)


def _win_attn_call(sink6, qkv, cache_k, cache_v, layer):
    b, n, w = qkv.shape
    past = cache_k.shape[3]
    cache_spec = pl.BlockSpec((1, 1, KV_B, past, HEAD_DIM), lambda i, j: (i, layer, 0, 0, 0))
    return pl.pallas_call(
        functools.partial(_win_attn_kernel, n=n),
        grid=(b, n // WIN_BLK),
        in_specs=[pl.BlockSpec(memory_space=pltpu.SMEM),
                  pl.BlockSpec((1, n, w), lambda i, j: (i, 0, 0)),
                  cache_spec, cache_spec],
        out_specs=pl.BlockSpec((1, WIN_BLK, 384), lambda i, j: (i, j, 0)),
        out_shape=jax.ShapeDtypeStruct((b, n, 384), BF16),
        compiler_params=_params(2),
        name="win_attn",
    )(sink6, qkv, cache_k, cache_v)


def _nbr_attn_kernel(qkv_ref, bias_ref, kc_ref, vc_ref, o_ref, *, n):
    r = pl.program_id(1)
    rows = n // GRID_W
    nloc = NA_ROWS * GRID_W
    rs = jnp.clip(r - NA_ROWS // 2, 0, rows - NA_ROWS)
    q0 = pl.multiple_of(r * GRID_W, GRID_W)
    k0 = pl.multiple_of(rs * GRID_W, GRID_W)
    for h in range(H_C):
        q = qkv_ref[0, pl.ds(q0, GRID_W), 640 + 64 * h: 640 + 64 * (h + 1)]
        kl = qkv_ref[0, pl.ds(k0, nloc), 1024 + 64 * h: 1024 + 64 * (h + 1)]
        vl = qkv_ref[0, pl.ds(k0, nloc), 1408 + 64 * h: 1408 + 64 * (h + 1)]
        kc = kc_ref[0, 0, h].astype(BF16)
        vc = vc_ref[0, 0, h].astype(BF16)
        s_loc = _dot_nt(q, kl) * SCALE + bias_ref[0, h]
        s_ctx = _dot_nt(q, kc) * SCALE
        o = _softmax_pv([(s_loc, vl), (s_ctx, vc)], None)
        o_ref[0, :, 64 * h: 64 * (h + 1)] = o.astype(BF16)


def _nbr_attn_call(qkv, bias_tab, cache_k, cache_v, layer):
    b, n, w = qkv.shape
    past = cache_k.shape[3]
    rows = n // GRID_W
    nloc = NA_ROWS * GRID_W
    cache_spec = pl.BlockSpec((1, 1, H_C, past, HEAD_DIM), lambda i, j: (i, layer, 0, 0, 0))
    return pl.pallas_call(
        functools.partial(_nbr_attn_kernel, n=n),
        grid=(b, rows),
        in_specs=[pl.BlockSpec((1, n, w), lambda i, j: (i, 0, 0)),
                  pl.BlockSpec((1, H_C, GRID_W, nloc),
                               lambda i, j: (jnp.clip(j - NA_ROWS // 2, 0, rows - NA_ROWS) - j + NA_ROWS - 1, 0, 0, 0)),
                  cache_spec, cache_spec],
        out_specs=pl.BlockSpec((1, GRID_W, 384), lambda i, j: (i, j, 0)),
        out_shape=jax.ShapeDtypeStruct((b, n, 384), BF16),
        compiler_params=_params(2),
        name="nbr_attn",
    )(qkv, bias_tab, cache_k, cache_v)


def _outproj_kernel(x_ref, ya_ref, yb_ref, yc_ref, mod_ref, g_ref, wa_ref, wb_ref, wc_ref, wr_ref,
                    xn_ref, h2_ref, aff_ref):
    proj = _dot(ya_ref[0], wa_ref[...]) + _dot(yb_ref[0], wb_ref[...]) + _dot(yc_ref[0], wc_ref[...])
    xn = x_ref[0] + mod_ref[0, 2:3, :] * proj
    xn_ref[0] = xn
    h2 = _rms_modulate(xn, g_ref[...], mod_ref[0, 3:4, :], mod_ref[0, 4:5, :]).astype(BF16)
    h2_ref[0] = h2
    logits = _dot(h2, wr_ref[...])
    lane = lax.broadcasted_iota(jnp.int32, logits.shape, 1)
    valid = lane < N_EXPERTS
    logits = jnp.where(valid, logits, NEG_INF)
    e = jnp.exp(logits - jnp.max(logits, axis=-1, keepdims=True))
    e = jnp.where(valid, e, 0.0)
    aff_ref[0] = e / jnp.sum(e, axis=-1, keepdims=True)


def _outproj_call(x3, ya, yb, yc, mod_g, gain, wo_a, wo_b, wo_c, w_router_b):
    g, n, d = x3.shape
    tb = TOKEN_BLOCK
    tok = lambda w: pl.BlockSpec((1, tb, w), lambda i, j: (i, j, 0))
    full = lambda shape: pl.BlockSpec(shape, lambda i, j: (0,) * len(shape))
    return pl.pallas_call(
        _outproj_kernel,
        grid=(g, n // tb),
        in_specs=[tok(d), tok(256), tok(384), tok(384),
                  pl.BlockSpec((1, N_MOD, d), lambda i, j: (i, 0, 0)),
                  full((1, d)), full((256, d)), full((384, d)), full((384, d)), full((d, LANES))],
        out_specs=[tok(d), tok(d), tok(LANES)],
        out_shape=[jax.ShapeDtypeStruct((g, n, d), F32),
                   jax.ShapeDtypeStruct((g, n, d), BF16),
                   jax.ShapeDtypeStruct((g, n, LANES), F32)],
        compiler_params=_params(2),
        name="outproj_router",
    )(x3, ya, yb, yc, mod_g, gain, wo_a, wo_b, wo_c, w_router_b)


def _route_kernel(aff_ref, h2_ref, xs_ref, vals_ref, rank_ref, *, n, cap):
    aff = aff_ref[0]
    aff_t = aff.T
    jj = lax.broadcasted_iota(jnp.int32, (n, n), 0)
    ii = lax.broadcasted_iota(jnp.int32, (n, n), 1)
    earlier = jnp.where(jj < ii, 1.0, 0.0)
    slot = lax.broadcasted_iota(jnp.int32, (cap, n), 0).astype(F32)
    rank_rows = []
    sel = []
    for e in range(N_EXPERTS):
        a_col = aff[:, e:e + 1]
        a_row = aff_t[e:e + 1, :]
        beats = jnp.where(a_col > a_row, 1.0, jnp.where(a_col == a_row, earlier, 0.0))
        rank = jnp.minimum(jnp.sum(beats, axis=0, keepdims=True), float(cap))
        rank_rows.append(rank)
        onehot = jnp.where(rank == slot, 1.0, 0.0)
        vals_ref[e, 0] = jnp.sum(onehot * a_row, axis=1, keepdims=True)
        sel.append(onehot.astype(BF16))
    rank_t = jnp.concatenate(rank_rows + [jnp.full((LANES - N_EXPERTS, n), float(cap), F32)], axis=0)
    rank_ref[0] = rank_t.T
    xs = _dot(jnp.concatenate(sel, axis=0), h2_ref[0])
    xs_ref[:, 0] = xs.reshape(N_EXPERTS, cap, xs.shape[-1]).astype(BF16)


def _route_call(aff, h2):
    b, n, d = h2.shape
    cap = max(1, EC_FACTOR * n // N_EXPERTS)
    return pl.pallas_call(
        functools.partial(_route_kernel, n=n, cap=cap),
        grid=(b,),
        in_specs=[pl.BlockSpec((1, n, LANES), lambda i: (i, 0, 0)),
                  pl.BlockSpec((1, n, d), lambda i: (i, 0, 0))],
        out_specs=[pl.BlockSpec((N_EXPERTS, 1, cap, d), lambda i: (0, i, 0, 0)),
                   pl.BlockSpec((N_EXPERTS, 1, cap, 1), lambda i: (0, i, 0, 0)),
                   pl.BlockSpec((1, n, LANES), lambda i: (i, 0, 0))],
        out_shape=[jax.ShapeDtypeStruct((N_EXPERTS, b, cap, d), BF16),
                   jax.ShapeDtypeStruct((N_EXPERTS, b, cap, 1), F32),
                   jax.ShapeDtypeStruct((b, n, LANES), F32)],
        compiler_params=_params(1),
        name="route_gather",
    )(aff, h2)


def _expert_kernel(xc_ref, xl_ref, vc_ref, vl_ref, wg_ref, wu_ref, wd_ref, yc_ref, yl_ref, acc_c, acc_l):
    f = pl.program_id(1)
    wg = wg_ref[0, 0].astype(BF16)
    wu = wu_ref[0, 0].astype(BF16)
    wd = wd_ref[0, 0].astype(BF16)
    for x_ref, v_ref, y_ref, acc in ((xc_ref, vc_ref, yc_ref, acc_c), (xl_ref, vl_ref, yl_ref, acc_l)):
        rows = x_ref.shape[1]
        chunk = min(EXPERT_ROW_CHUNK, rows)
        for c0 in range(0, rows, chunk):
            sl = slice(c0, c0 + chunk)
            x = x_ref[0, sl, :]
            a = _dot(x, wg)
            u = _dot(x, wu)
            act = ((a * jax.nn.sigmoid(a)) * u).astype(BF16)
            part = _dot(act, wd)

            @pl.when(f == 0)
            def _():
                acc[sl, :] = part

            @pl.when(f == EXPERT_F_STEPS - 1)
            def _():
                y_ref[0, sl, :] = ((acc[sl, :] + part) * v_ref[0, sl, :]).astype(BF16)


def _expert_call(xs_c, xs_l, vals_c, vals_l, w_gate, w_up, w_down, layer):
    e, rc, d = xs_c.shape
    rl = xs_l.shape[1]
    f_total = w_gate.shape[-1]
    assert EXPERT_F_STEPS == 2
    fb = f_total // EXPERT_F_STEPS
    per_e = lambda rows, w: pl.BlockSpec((1, rows, w), lambda i, j: (i, 0, 0))
    return pl.pallas_call(
        _expert_kernel,
        grid=(e, EXPERT_F_STEPS),
        in_specs=[per_e(rc, d), per_e(rl, d), per_e(rc, 1), per_e(rl, 1),
                  pl.BlockSpec((1, 1, d, fb), lambda i, j: (layer, i, 0, j)),
                  pl.BlockSpec((1, 1, d, fb), lambda i, j: (layer, i, 0, j)),
                  pl.BlockSpec((1, 1, fb, d), lambda i, j: (layer, i, j, 0))],
        out_specs=[per_e(rc, d), per_e(rl, d)],
        out_shape=[jax.ShapeDtypeStruct((e, rc, d), BF16),
                   jax.ShapeDtypeStruct((e, rl, d), BF16)],
        scratch_shapes=[pltpu.VMEM((rc, d), F32), pltpu.VMEM((rl, d), F32)],
        compiler_params=_params(2),
        name="experts",
    )(xs_c, xs_l, vals_c, vals_l, w_gate, w_up, w_down)


def _combine_kernel(y_ref, rank_ref, xn_ref, mod_ref, gf_ref, o_ref, *, n, cap, final):
    ec = N_EXPERTS * cap
    d = xn_ref.shape[-1]
    rank = rank_ref[0].astype(BF16)
    ce = lax.broadcasted_iota(jnp.int32, (LANES, ec), 1) // cap
    ee = lax.broadcasted_iota(jnp.int32, (LANES, ec), 0)
    expand = jnp.where(ce == ee, 1.0, 0.0).astype(BF16)
    rexp = _dot(rank, expand)
    slot = (lax.broadcasted_iota(jnp.int32, (n, ec), 1) % cap).astype(F32)
    onehot = jnp.where(rexp == slot, 1.0, 0.0).astype(BF16)
    y_all = y_ref[:, 0].reshape(ec, d)
    moe = _dot(onehot, y_all)
    x = xn_ref[0] + mod_ref[0, 5:6, :] * moe
    if final:
        x = (x * lax.rsqrt(jnp.mean(x * x, axis=-1, keepdims=True) + EPS)) * gf_ref[...]
    o_ref[0] = x


def _combine_call(y4, rank, xn, mod_g, g_final, final):
    e, b, cap, d = y4.shape
    n = xn.shape[1]
    shared_mod = mod_g.shape[0] == 1
    return pl.pallas_call(
        functools.partial(_combine_kernel, n=n, cap=cap, final=final),
        grid=(b,),
        in_specs=[pl.BlockSpec((e, 1, cap, d), lambda i: (0, i, 0, 0)),
                  pl.BlockSpec((1, n, LANES), lambda i: (i, 0, 0)),
                  pl.BlockSpec((1, n, d), lambda i: (i, 0, 0)),
                  pl.BlockSpec((1, N_MOD, d), (lambda i: (0, 0, 0)) if shared_mod else (lambda i: (i, 0, 0))),
                  pl.BlockSpec((1, d), lambda i: (0, 0))],
        out_specs=pl.BlockSpec((1, n, d), lambda i: (i, 0, 0)),
        out_shape=jax.ShapeDtypeStruct((b, n, d), F32),
        compiler_params=_params(1),
        name="combine",
    )(y4, rank, xn, mod_g, g_final)


def _rope_tables(n):
    t = jnp.arange(n, dtype=jnp.int32)
    row = (t // GRID_W).astype(F32)
    col = (t % GRID_W).astype(F32)
    half = HEAD_DIM // 4
    inv = jnp.power(ROPE_BASE, -jnp.arange(half, dtype=F32) / half)
    ang_r = row[:, None] * inv[None, :]
    ang_c = col[:, None] * inv[None, :]
    cos_h = jnp.concatenate([jnp.cos(ang_r)] * 2 + [jnp.cos(ang_c)] * 2, axis=-1)
    sin_h = jnp.concatenate([-jnp.sin(ang_r), jnp.sin(ang_r), -jnp.sin(ang_c), jnp.sin(ang_c)], axis=-1)
    reps = 512 // HEAD_DIM
    return jnp.tile(cos_h, (1, reps)), jnp.tile(sin_h, (1, reps))


def _nbr_bias_tables(rpb, n):
    rows = n // GRID_W
    assert rows >= NA_ROWS
    col = np.arange(GRID_W)
    cs = np.clip(col - NA_COLS // 2, 0, GRID_W - NA_COLS)
    col_mask = (col[None, :] >= cs[:, None]) & (col[None, :] < cs[:, None] + NA_COLS)
    dc_idx = np.clip(col[None, :] - col[:, None], -(NA_COLS - 1), NA_COLS - 1) + (NA_COLS - 1)
    onehot = (dc_idx[None, :, :] == np.arange(2 * NA_COLS - 1)[:, None, None]).astype(np.float32)
    t = jnp.einsum("hdc,cqk->hqdk", rpb.astype(F32), onehot, precision=lax.Precision.HIGHEST)
    t = jnp.where(col_mask[None, :, None, :], t, NEG_INF)
    h = rpb.shape[0]
    return jnp.stack([t[:, :, lo:lo + NA_ROWS, :].reshape(h, GRID_W, NA_ROWS * GRID_W) for lo in range(NA_ROWS)])


def _block_diag(w):
    nb, bw, _ = w.shape
    eye = jnp.eye(nb, dtype=w.dtype)
    return (eye[:, None, :, None] * w[:, :, None, :]).reshape(nb * bw, nb * bw)


def kernel(x_prompt, x_sample, state_lru, cache_k_win, cache_v_win, cache_k_nbr, cache_v_nbr, c, c_ctx, w_mod, b_mod, g_norm1, w_in, conv_w, conv_b, w_gate_r, b_gate_r, w_gate_i, b_gate_i, lru_lambda, sink_logit, nbr_bias, w_out, g_norm2, w_router, w_exp_gate, w_exp_up, w_exp_down, g_final):
    bc, seq, d = x_prompt.shape
    bl, n_lat, _ = x_sample.shape
    depth = w_mod.shape[0]
    assert bl + 1 <= 8 and d == 1024

    cond8 = jnp.zeros((8, d), F32).at[0].set(c_ctx).at[1:1 + bl].set(c)
    mods = _adaln_call(cond8, w_mod, b_mod.reshape(depth, 1, -1)).reshape(depth, 8, N_MOD, d)
    rope_tabs = _rope_tables(n_lat)
    gf = g_final.reshape(1, d)

    xc = x_prompt.reshape(1, bc * seq, d)
    xl = x_sample
    zeros_state = jnp.zeros((bc, 1, 2, 256), F32)
    caches = None
    st = None
    for l in range(depth):
        final = l == depth - 1
        mod_c = mods[l, 0:1]
        mod_l = mods[l, 1:1 + bl]
        g1 = g_norm1[l].reshape(1, d)
        g2 = g_norm2[l].reshape(1, d)
        w_in_b = w_in[l].astype(BF16)
        wo = w_out[l].astype(BF16)
        wo_a, wo_b, wo_c = wo[:256], wo[256:640], wo[640:]
        wr_b = jnp.zeros((d, LANES), BF16).at[:, :N_EXPERTS].set(w_router[l].astype(BF16))
        w_gates = jnp.concatenate([_block_diag(w_gate_r[l, 0]), _block_diag(w_gate_i[l, 0]),
                                   _block_diag(w_gate_r[l, 1]), _block_diag(w_gate_i[l, 1])], axis=1).astype(BF16)
        b_gates = jnp.concatenate([b_gate_r[l, 0], b_gate_i[l, 0], b_gate_r[l, 1], b_gate_i[l, 1]]).reshape(1, -1)
        cb = conv_b[l].reshape(1, -1)
        sink6 = sink_logit[l].reshape(-1)
        bias_tab = _nbr_bias_tables(nbr_bias[l], n_lat)

        ulru_c, qkv_c, *caches = _inproj_call(xc, mod_c, g1, w_in_b, cache_seq=seq, layer=l, depth=depth,
                                              prev_caches=caches)
        ya_c, st = _lru_call(ulru_c.reshape(bc, seq, 512), conv_w[l], cb, w_gates, b_gates, lru_lambda[l],
                             zeros_state, 0, st_layer=l, st_depth=depth, prev_state=st)
        yb_c, yc_c = _ctx_attn_call(sink6, qkv_c.reshape(bc, seq, -1))
        xn_c, h2_c, aff_c = _outproj_call(xc, ya_c.reshape(1, bc * seq, -1), yb_c.reshape(1, bc * seq, -1),
                                          yc_c.reshape(1, bc * seq, -1), mod_c, g2, wo_a, wo_b, wo_c, wr_b)
        xs_c, vals_c, rank_c = _route_call(aff_c.reshape(bc, seq, LANES), h2_c.reshape(bc, seq, d))

        ulru_l, qkv_l = _inproj_call(xl, mod_l, g1, w_in_b, rope_tabs=rope_tabs)
        ya_l, _ = _lru_call(ulru_l, conv_w[l], cb, w_gates, b_gates, lru_lambda[l], state_lru, l)
        yb_l = _win_attn_call(sink6, qkv_l, cache_k_win, cache_v_win, l)
        yc_l = _nbr_attn_call(qkv_l, bias_tab, cache_k_nbr, cache_v_nbr, l)
        xn_l, h2_l, aff_l = _outproj_call(xl, ya_l, yb_l, yc_l, mod_l, g2, wo_a, wo_b, wo_c, wr_b)
        xs_l, vals_l, rank_l = _route_call(aff_l, h2_l)

        cap_c, cap_l = xs_c.shape[2], xs_l.shape[2]
        y_c, y_l = _expert_call(xs_c.reshape(N_EXPERTS, bc * cap_c, d), xs_l.reshape(N_EXPERTS, bl * cap_l, d),
                                vals_c.reshape(N_EXPERTS, bc * cap_c, 1), vals_l.reshape(N_EXPERTS, bl * cap_l, 1),
                                w_exp_gate, w_exp_up, w_exp_down, l)
        xc = _combine_call(y_c.reshape(N_EXPERTS, bc, cap_c, d), rank_c, xn_c.reshape(bc, seq, d), mod_c, gf, final)
        xl = _combine_call(y_l.reshape(N_EXPERTS, bl, cap_l, d), rank_l, xn_l, mod_l, gf, final)
        xc = xc.reshape(1, bc * seq, d)

    y_prompt = xc.reshape(bc, seq, d)
    y_sample = xl
    return (y_prompt, y_sample, st, *caches)
```

```python
import functools

import numpy as np
import jax
import jax.numpy as jnp
from jax import lax
from jax.experimental import pallas as pl
from jax.experimental.pallas import tpu as pltpu

F32 = jnp.float32
BF16 = jnp.bfloat16

HEAD_DIM = 64
GRID_W = 64
LRU_C = 8.0
CONV_W = 4
KV_B = 2
G_B = 3
H_B = KV_B * G_B
H_C = 6
WINDOW = 128
WIN_BLK = 128
NA_ROWS = 8
NA_COLS = 16
ROPE_BASE = 10000.0
N_EXPERTS = 16
EC_FACTOR = 2
N_MOD = 6
EPS = 1e-6
NEG_INF = -1e30
SCALE = HEAD_DIM ** -0.5

LANES = 128
VMEM_LIMIT_BYTES = 56 * 1024 * 1024

TOKEN_BLOCK = 512
MOD_COL_BLOCK = 1536
EXPERT_F_STEPS = 2
EXPERT_ROW_CHUNK = 256
LRU_W = 512
QC0, KC0, VC0, QB0, KB0, VB0, ATT_W = 0, 384, 768, 1152, 1536, 1664, 1792
CTX_ATTN_REQS = 2
NBR_Q_ROWS = 4
NBR_K_ROWS = 12
ROUTE_REQS_CTX = 8
ROUTE_REQS_LAT = 2


def _params(n_axes):
    return pltpu.CompilerParams(dimension_semantics=("arbitrary",) * n_axes,
                                vmem_limit_bytes=VMEM_LIMIT_BYTES)


def _dot(a, b):
    return jnp.dot(a, b, preferred_element_type=F32)


def _dot_nt(a, b):
    return lax.dot_general(a, b, (((1,), (1,)), ((), ())), preferred_element_type=F32)


def _adaln_kernel(cond_ref, w_ref, b_ref, o_ref):
    c = cond_ref[...]
    s = c * jax.nn.sigmoid(c)
    o_ref[0] = _dot(s.astype(BF16), w_ref[0].astype(BF16)) + b_ref[0]


def _adaln_call(cond8, w_mod, b_mod3):
    depth, d, n = w_mod.shape
    nb = MOD_COL_BLOCK
    return pl.pallas_call(
        _adaln_kernel,
        grid=(depth, n // nb),
        in_specs=[pl.BlockSpec((8, d), lambda l, j: (0, 0)),
                  pl.BlockSpec((1, d, nb), lambda l, j: (l, 0, j)),
                  pl.BlockSpec((1, 1, nb), lambda l, j: (l, 0, j))],
        out_specs=pl.BlockSpec((1, 8, nb), lambda l, j: (l, 0, j)),
        out_shape=jax.ShapeDtypeStruct((depth, 8, n), F32),
        compiler_params=_params(2),
        name="adaln",
    )(cond8, w_mod, b_mod3)


def _rms_modulate(x, gain, shift, scale):
    y = x * lax.rsqrt(jnp.mean(x * x, axis=-1, keepdims=True) + EPS)
    return (y * gain) * (1.0 + scale) + shift


def _inproj_kernel(*refs, rope, cache, seq):
    if rope:
        x_ref, mod_ref, g_ref, w_ref, cos_ref, sin_ref, ulru_ref, qkv_ref = refs
    elif cache:
        x_ref, mod_ref, g_ref, w_ref = refs[:4]
        ulru_ref, qkv_ref, kw_ref, vw_ref, kn_ref, vn_ref = refs[-6:]
    else:
        x_ref, mod_ref, g_ref, w_ref, ulru_ref, qkv_ref = refs
    x = x_ref[0]
    h = _rms_modulate(x, g_ref[...], mod_ref[0, 0:1, :], mod_ref[0, 1:2, :])
    u = _dot(h.astype(BF16), w_ref[...])
    tb = u.shape[0]
    ulru_ref[0] = u[:, :LRU_W]
    if rope:
        lo, hi = LRU_W + QB0, LRU_W + VB0
        qk = u[:, lo:hi]
        lane = lax.broadcasted_iota(jnp.int32, (tb, hi - lo), 1)
        first = (lane & 31) < 16
        partner = jnp.where(first, pltpu.roll(qk, hi - lo - 16, 1), pltpu.roll(qk, 16, 1))
        qk = qk * cos_ref[...] + partner * sin_ref[...]
        qkv_ref[0, :, :QB0] = u[:, LRU_W:lo].astype(BF16)
        qkv_ref[0, :, QB0:VB0] = qk.astype(BF16)
        qkv_ref[0, :, VB0:] = u[:, hi:].astype(BF16)
    else:
        qkv_ref[0] = u[:, LRU_W:].astype(BF16)
    if cache:
        for r in range(tb // seq):
            rows = slice(r * seq, (r + 1) * seq)
            for hh in range(KV_B):
                kw_ref[r, 0, hh] = u[rows, LRU_W + KB0 + 64 * hh: LRU_W + KB0 + 64 * (hh + 1)]
                vw_ref[r, 0, hh] = u[rows, LRU_W + VB0 + 64 * hh: LRU_W + VB0 + 64 * (hh + 1)]
            for hh in range(H_C):
                kn_ref[r, 0, hh] = u[rows, LRU_W + KC0 + 64 * hh: LRU_W + KC0 + 64 * (hh + 1)]
                vn_ref[r, 0, hh] = u[rows, LRU_W + VC0 + 64 * hh: LRU_W + VC0 + 64 * (hh + 1)]


def _inproj_call(x3, mod_g, gain, w_in_b, rope_tabs=None, cache_seq=None, layer=0, depth=1, prev_caches=None):
    g, n, d = x3.shape
    d_in = w_in_b.shape[1]
    tb = TOKEN_BLOCK
    rope = rope_tabs is not None
    cache = cache_seq is not None
    in_specs = [pl.BlockSpec((1, tb, d), lambda i, j: (i, j, 0)),
                pl.BlockSpec((1, N_MOD, d), lambda i, j: (i, 0, 0)),
                pl.BlockSpec((1, d), lambda i, j: (0, 0)),
                pl.BlockSpec((d, d_in), lambda i, j: (0, 0))]
    args = [x3, mod_g, gain, w_in_b]
    assert d_in == LRU_W + ATT_W
    out_shape = [jax.ShapeDtypeStruct((g, n, LRU_W), F32),
                 jax.ShapeDtypeStruct((g, n, ATT_W), BF16)]
    out_specs = [pl.BlockSpec((1, tb, LRU_W), lambda i, j: (i, j, 0)),
                 pl.BlockSpec((1, tb, ATT_W), lambda i, j: (i, j, 0))]
    if rope:
        in_specs += [pl.BlockSpec((tb, VB0 - QB0), lambda i, j: (j, 0))] * 2
        args += list(rope_tabs)
    aliases = {}
    if cache:
        assert g == 1 and tb % cache_seq == 0
        rb = tb // cache_seq
        nreq = n // cache_seq
        for heads in (KV_B, KV_B, H_C, H_C):
            out_shape.append(jax.ShapeDtypeStruct((nreq, depth, heads, cache_seq, HEAD_DIM), F32))
            out_specs.append(pl.BlockSpec((rb, 1, heads, cache_seq, HEAD_DIM), lambda i, j: (j, layer, 0, 0, 0)))
        if prev_caches is not None:
            for k, buf in enumerate(prev_caches):
                aliases[len(args)] = 2 + k
                in_specs.append(pl.BlockSpec(memory_space=pl.ANY))
                args.append(buf)
    return pl.pallas_call(
        functools.partial(_inproj_kernel, rope=rope, cache=cache, seq=cache_seq),
        grid=(g, n // tb),
        in_specs=in_specs,
        out_specs=out_specs,
        out_shape=out_shape,
        input_output_aliases=aliases,
        compiler_params=_params(2),
        name="inproj_rope" if rope else "inproj_ctx",
    )(*args)


def _lru_kernel(*refs, n):
    u_ref, cw_ref, cb_ref, wg_ref, bg_ref, lam_ref, h0_ref = refs[:7]
    ya_ref, st_ref, af_s, xf_s, ab_s, xb_s, yf_s, yb_s = refs[-8:]
    c = 256
    nblk = n // 8
    u = u_ref[0]
    xa = u[:, :c]
    ga = u[:, c:]
    t = lax.broadcasted_iota(jnp.int32, (n, c), 0)
    cw = cw_ref[...]
    xc = cw[2:3] * xa + cb_ref[...]
    xc = xc + cw[0:1] * jnp.where(t >= 2, pltpu.roll(xa, 2, 0), 0.0)
    xc = xc + cw[1:2] * jnp.where(t >= 1, pltpu.roll(xa, 1, 0), 0.0)
    xc = xc + cw[3:4] * jnp.where(t < n - 1, pltpu.roll(xa, n - 1, 0), 0.0)
    gates = _dot(xc.astype(BF16), wg_ref[...]) + bg_ref[...]
    lam = lam_ref[...]
    log_sig = jnp.minimum(lam, 0.0) - jnp.log1p(jnp.exp(-jnp.abs(lam)))
    sub = lax.broadcasted_iota(jnp.int32, (nblk, 8, c), 1)
    for d, (a_s, x_s) in enumerate(((af_s, xf_s), (ab_s, xb_s))):
        r = jax.nn.sigmoid(gates[:, 2 * d * c:(2 * d + 1) * c])
        i = jax.nn.sigmoid(gates[:, (2 * d + 1) * c:(2 * d + 2) * c])
        log_a = LRU_C * r * log_sig[d:d + 1]
        a = jnp.exp(log_a)
        xin = jnp.sqrt(jnp.tanh(-log_a) * (1.0 + a * a)) * (i * xc)
        a3 = a.reshape(nblk, 8, c)
        x3 = xin.reshape(nblk, 8, c)
        for s in (1, 2, 4):
            if d == 0:
                m = sub >= s
                sh = s
            else:
                m = sub < 8 - s
                sh = 8 - s
            x3 = jnp.where(m, a3 * pltpu.roll(x3, sh, 1) + x3, x3)
            a3 = jnp.where(m, a3 * pltpu.roll(a3, sh, 1), a3)
        a_s[...] = a3
        x_s[...] = x3

    h0 = h0_ref[0, 0]

    def body(k, carry):
        hf, hb = carry
        blk_f = af_s[k] * hf + xf_s[k]
        yf_s[k] = blk_f
        kb = nblk - 1 - k
        blk_b = ab_s[kb] * hb + xb_s[kb]
        yb_s[kb] = blk_b
        return blk_f[7:8], blk_b[0:1]

    hf, hb = lax.fori_loop(0, nblk, body, (h0[0:1], h0[1:2]))
    y = (yf_s[...] + yb_s[...]).reshape(n, c) * jax.nn.gelu(ga)
    ya_ref[0] = y.astype(BF16)
    st_ref[0, 0, 0:1, :] = hf
    st_ref[0, 0, 1:2, :] = hb


def _lru_call(ulru, conv_w, conv_b, w_gates_b, b_gates, lam, h0, h0_layer, st_layer=0, st_depth=1, prev_state=None):
    b, n, _ = ulru.shape
    c = 256
    full = lambda shape: pl.BlockSpec(shape, lambda i: (0,) * len(shape))
    scratch = [pltpu.VMEM((n // 8, 8, c), F32) for _ in range(6)]
    in_specs = [pl.BlockSpec((1, n, 2 * c), lambda i: (i, 0, 0)),
                full((CONV_W, c)), full((1, c)), full((c, 4 * c)), full((1, 4 * c)), full((2, c)),
                pl.BlockSpec((1, 1, 2, c), lambda i: (i, h0_layer, 0, 0))]
    args = [ulru, conv_w, conv_b, w_gates_b, b_gates, lam, h0]
    aliases = {}
    if prev_state is not None:
        aliases[len(args)] = 1
        in_specs.append(pl.BlockSpec(memory_space=pl.ANY))
        args.append(prev_state)
    return pl.pallas_call(
        functools.partial(_lru_kernel, n=n),
        grid=(b,),
        in_specs=in_specs,
        out_specs=[pl.BlockSpec((1, n, c), lambda i: (i, 0, 0)),
                   pl.BlockSpec((1, 1, 2, c), lambda i: (i, st_layer, 0, 0))],
        out_shape=[jax.ShapeDtypeStruct((b, n, c), BF16),
                   jax.ShapeDtypeStruct((b, st_depth, 2, c), F32)],
        scratch_shapes=scratch,
        input_output_aliases=aliases,
        compiler_params=_params(1),
        name="rglru",
    )(*args)


def _softmax_pv(parts, sink):
    mx = None
    for s, _ in parts:
        cur = jnp.max(s, axis=-1, keepdims=True)
        mx = cur if mx is None else jnp.maximum(mx, cur)
    if sink is not None:
        mx = jnp.maximum(mx, sink)
    den = None
    out = None
    for s, v in parts:
        p = jnp.exp(s - mx)
        cur = jnp.sum(p, axis=-1, keepdims=True)
        den = cur if den is None else den + cur
        o = _dot(p.astype(BF16), v)
        out = o if out is None else out + o
    if sink is not None:
        den = den + jnp.exp(sink - mx)
    return out / den


def _swap_halves(x):
    return jnp.concatenate([x[:, HEAD_DIM:], x[:, :HEAD_DIM]], axis=1)


def _keep_half(x, lo_mask, half):
    zero = jnp.zeros_like(x)
    return jnp.where(lo_mask, x, zero) if half == 0 else jnp.where(lo_mask, zero, x)


def _ctx_attn_kernel(sink_ref, att_ref, ob_ref, oc_ref, *, n, rb):
    lo = lax.broadcasted_iota(jnp.int32, (n, LANES), 1) < HEAD_DIM
    for r in range(rb):
        for p in range(H_C // 2):
            qp = att_ref[r, :, QC0 + LANES * p: QC0 + LANES * (p + 1)]
            kp = att_ref[r, :, KC0 + LANES * p: KC0 + LANES * (p + 1)]
            vp = att_ref[r, :, VC0 + LANES * p: VC0 + LANES * (p + 1)]
            outs = [_softmax_pv([(_dot_nt(qp, _keep_half(kp, lo, half)), vp)], None) for half in range(2)]
            oc_ref[r, :, LANES * p: LANES * (p + 1)] = jnp.where(lo, outs[0], outs[1]).astype(BF16)
        kpair = att_ref[r, :, KB0: KB0 + LANES]
        vpair = att_ref[r, :, VB0: VB0 + LANES]
        kpair_sw = _swap_halves(kpair)
        vpair_sw = _swap_halves(vpair)
        for p in range(H_B // 2):
            qp = att_ref[r, :, QB0 + LANES * p: QB0 + LANES * (p + 1)]
            outs = []
            for half in range(2):
                h = 2 * p + half
                aligned = (h // G_B) == half
                ksrc, vsrc = (kpair, vpair) if aligned else (kpair_sw, vpair_sw)
                outs.append(_softmax_pv([(_dot_nt(qp, _keep_half(ksrc, lo, half)), vsrc)], sink_ref[h]))
            ob_ref[r, :, LANES * p: LANES * (p + 1)] = jnp.where(lo, outs[0], outs[1]).astype(BF16)


def _ctx_attn_call(sink6, att):
    b, n, w = att.shape
    rb = CTX_ATTN_REQS
    assert b % rb == 0 and KV_B == 2
    return pl.pallas_call(
        functools.partial(_ctx_attn_kernel, n=n, rb=rb),
        grid=(b // rb,),
        in_specs=[pl.BlockSpec(memory_space=pltpu.SMEM),
                  pl.BlockSpec((rb, n, w), lambda i: (i, 0, 0))],
        out_specs=[pl.BlockSpec((rb, n, 384), lambda i: (i, 0, 0))] * 2,
        out_shape=[jax.ShapeDtypeStruct((b, n, 384), BF16)] * 2,
        compiler_params=_params(1),
        name="ctx_attn",
    )(sink6, att)


def _win_attn_kernel(sink_ref, qkv_ref, kc_ref, vc_ref, o_ref, *, n):
    j = pl.program_id(1)
    nloc = 3 * WIN_BLK
    q0 = pl.multiple_of(j * WIN_BLK, WIN_BLK)
    ks = pl.multiple_of(jnp.clip((j - 1) * WIN_BLK, 0, n - nloc), WIN_BLK)
    row = lax.broadcasted_iota(jnp.int32, (G_B * WIN_BLK, nloc), 0)
    col = lax.broadcasted_iota(jnp.int32, (G_B * WIN_BLK, nloc), 1)
    qpos = q0 + (row & (WIN_BLK - 1))
    kpos = ks + col
    mask = jnp.abs(qpos - kpos) <= WINDOW
    rowh = lax.broadcasted_iota(jnp.int32, (G_B * WIN_BLK, 1), 0) // WIN_BLK
    for kv in range(KV_B):
        gw = G_B * HEAD_DIM
        qblk = qkv_ref[0, pl.ds(q0, WIN_BLK), QB0 + gw * kv: QB0 + gw * (kv + 1)]
        q3 = jnp.concatenate([qblk[:, 64 * g: 64 * (g + 1)] for g in range(G_B)], axis=0)
        kl = qkv_ref[0, pl.ds(ks, nloc), KB0 + 64 * kv: KB0 + 64 * (kv + 1)]
        vl = qkv_ref[0, pl.ds(ks, nloc), VB0 + 64 * kv: VB0 + 64 * (kv + 1)]
        kc = kc_ref[0, 0, kv].astype(BF16)
        vc = vc_ref[0, 0, kv].astype(BF16)
        s_loc = jnp.where(mask, _dot_nt(q3, kl), NEG_INF)
        s_ctx = _dot_nt(q3, kc)
        sk = jnp.where(rowh == 0, sink_ref[3 * kv],
                       jnp.where(rowh == 1, sink_ref[3 * kv + 1], sink_ref[3 * kv + 2]))
        o = _softmax_pv([(s_loc, vl), (s_ctx, vc)], sk)
        for g in range(G_B):
            h = kv * G_B + g
            o_ref[0, :, 64 * h: 64 * (h + 1)] = o[g * WIN_BLK:(g + 1) * WIN_BLK].astype(BF16)


def _win_attn_call(sink6, qkv, cache_k, cache_v, layer):
    b, n, w = qkv.shape
    past = cache_k.shape[3]
    cache_spec = pl.BlockSpec((1, 1, KV_B, past, HEAD_DIM), lambda i, j: (i, layer, 0, 0, 0))
    return pl.pallas_call(
        functools.partial(_win_attn_kernel, n=n),
        grid=(b, n // WIN_BLK),
        in_specs=[pl.BlockSpec(memory_space=pltpu.SMEM),
                  pl.BlockSpec((1, n, w), lambda i, j: (i, 0, 0)),
                  cache_spec, cache_spec],
        out_specs=pl.BlockSpec((1, WIN_BLK, 384), lambda i, j: (i, j, 0)),
        out_shape=jax.ShapeDtypeStruct((b, n, 384), BF16),
        compiler_params=_params(2),
        name="win_attn",
    )(sink6, qkv, cache_k, cache_v)


def _nbr_window_start(g, rows):
    return jnp.clip(g * NBR_Q_ROWS - NA_ROWS // 2, 0, rows - NBR_K_ROWS)


def _nbr_attn_kernel(q_ref, k_ref, v_ref, bias_ref, kc_ref, vc_ref, o_ref, *, n):
    g = pl.program_id(0)
    nq = NBR_Q_ROWS * GRID_W
    nk = NBR_K_ROWS * GRID_W
    k0 = pl.multiple_of(_nbr_window_start(g, n // GRID_W) * GRID_W, GRID_W)
    lo_k = lax.broadcasted_iota(jnp.int32, (nk, LANES), 1) < HEAD_DIM
    lo_q = lax.broadcasted_iota(jnp.int32, (nq, LANES), 1) < HEAD_DIM
    zpad = jnp.zeros((kc_ref.shape[3], HEAD_DIM), BF16)
    for p in range(H_C // 2):
        qp = q_ref[0, :, LANES * p: LANES * (p + 1)]
        qp_sw = _swap_halves(qp)
        kp = k_ref[0, pl.ds(k0, nk), LANES * p: LANES * (p + 1)]
        vp = v_ref[0, pl.ds(k0, nk), LANES * p: LANES * (p + 1)]
        outs = []
        for half in range(2):
            h = 2 * p + half
            kc = jnp.concatenate([kc_ref[0, 0, h].astype(BF16), zpad], axis=1)
            vc = jnp.concatenate([vc_ref[0, 0, h].astype(BF16), zpad], axis=1)
            s_loc = _dot_nt(qp, _keep_half(kp, lo_k, half)) + bias_ref[0, h]
            s_ctx = _dot_nt(qp if half == 0 else qp_sw, kc)
            mx = jnp.maximum(jnp.max(s_loc, axis=-1, keepdims=True), jnp.max(s_ctx, axis=-1, keepdims=True))
            p_loc = jnp.exp(s_loc - mx)
            p_ctx = jnp.exp(s_ctx - mx)
            den = jnp.sum(p_loc, axis=-1, keepdims=True) + jnp.sum(p_ctx, axis=-1, keepdims=True)
            o_ctx = _dot(p_ctx.astype(BF16), vc)
            if half == 1:
                o_ctx = pltpu.roll(o_ctx, HEAD_DIM, 1)
            outs.append((_dot(p_loc.astype(BF16), vp) + o_ctx) / den)
        o_ref[0, :, LANES * p: LANES * (p + 1)] = jnp.where(lo_q, outs[0], outs[1]).astype(BF16)


def _nbr_attn_call(att, bias_tab, cache_k, cache_v, layer):
    b, n, w = att.shape
    past = cache_k.shape[3]
    nq = NBR_Q_ROWS * GRID_W
    nk = NBR_K_ROWS * GRID_W
    gw = H_C * HEAD_DIM
    assert (QC0, KC0, VC0) == (0, gw, 2 * gw) and n % nq == 0
    cache_spec = pl.BlockSpec((1, 1, H_C, past, HEAD_DIM), lambda g, i: (i, layer, 0, 0, 0))
    return pl.pallas_call(
        functools.partial(_nbr_attn_kernel, n=n),
        grid=(n // nq, b),
        in_specs=[pl.BlockSpec((1, nq, gw), lambda g, i: (i, g, 0)),
                  pl.BlockSpec((1, n, gw), lambda g, i: (i, 0, 1)),
                  pl.BlockSpec((1, n, gw), lambda g, i: (i, 0, 2)),
                  pl.BlockSpec((1, H_C, nq, nk), lambda g, i: (g, 0, 0, 0)),
                  cache_spec, cache_spec],
        out_specs=pl.BlockSpec((1, nq, gw), lambda g, i: (i, g, 0)),
        out_shape=jax.ShapeDtypeStruct((b, n, gw), BF16),
        compiler_params=_params(2),
        name="nbr_attn",
    )(att, att, att, bias_tab, cache_k, cache_v)


def _outproj_kernel(x_ref, ya_ref, yb_ref, yc_ref, mod_ref, g_ref, wa_ref, wb_ref, wc_ref, wr_ref,
                    xn_ref, h2_ref, aff_ref):
    proj = _dot(ya_ref[0], wa_ref[...]) + _dot(yb_ref[0], wb_ref[...]) + _dot(yc_ref[0], wc_ref[...])
    xn = x_ref[0] + mod_ref[0, 2:3, :] * proj
    xn_ref[0] = xn
    h2 = _rms_modulate(xn, g_ref[...], mod_ref[0, 3:4, :], mod_ref[0, 4:5, :]).astype(BF16)
    h2_ref[0] = h2
    logits = _dot(h2, wr_ref[...])
    lane = lax.broadcasted_iota(jnp.int32, logits.shape, 1)
    valid = lane < N_EXPERTS
    logits = jnp.where(valid, logits, NEG_INF)
    e = jnp.exp(logits - jnp.max(logits, axis=-1, keepdims=True))
    e = jnp.where(valid, e, 0.0)
    aff_ref[0] = e / jnp.sum(e, axis=-1, keepdims=True)


def _outproj_call(x3, ya, yb, yc, mod_g, gain, wo_a, wo_b, wo_c, w_router_b):
    g, n, d = x3.shape
    tb = TOKEN_BLOCK
    tok = lambda w: pl.BlockSpec((1, tb, w), lambda i, j: (i, j, 0))
    full = lambda shape: pl.BlockSpec(shape, lambda i, j: (0,) * len(shape))
    return pl.pallas_call(
        _outproj_kernel,
        grid=(g, n // tb),
        in_specs=[tok(d), tok(256), tok(384), tok(384),
                  pl.BlockSpec((1, N_MOD, d), lambda i, j: (i, 0, 0)),
                  full((1, d)), full((256, d)), full((384, d)), full((384, d)), full((d, LANES))],
        out_specs=[tok(d), tok(d), tok(LANES)],
        out_shape=[jax.ShapeDtypeStruct((g, n, d), F32),
                   jax.ShapeDtypeStruct((g, n, d), BF16),
                   jax.ShapeDtypeStruct((g, n, LANES), F32)],
        compiler_params=_params(2),
        name="outproj_router",
    )(x3, ya, yb, yc, mod_g, gain, wo_a, wo_b, wo_c, w_router_b)


def _route_kernel(aff_ref, h2_ref, xs_ref, vals_ref, rank_ref, *, n, cap, rb):
    ne = N_EXPERTS
    nt = n // LANES
    fcap = float(cap)
    jj = lax.broadcasted_iota(jnp.int32, (LANES, LANES), 0)
    ii = lax.broadcasted_iota(jnp.int32, (LANES, LANES), 1)
    earlier = jnp.where(jj < ii, 1.0, 0.0)
    slot = lax.broadcasted_iota(jnp.int32, (cap, n), 0).astype(F32)
    pad = jnp.full((LANES - ne, n), fcap, F32)
    for r in range(rb):
        aff = aff_ref[r]
        aff_t = aff.T[:ne]
        rank_rows = []
        for e in range(ne):
            cols = [jnp.broadcast_to(aff[jb * LANES:(jb + 1) * LANES, e:e + 1], (LANES, LANES)) for jb in range(nt)]
            pieces = []
            for ib in range(nt):
                a_row = aff_t[e:e + 1, ib * LANES:(ib + 1) * LANES]
                acc = None
                for jb in range(nt):
                    if jb < ib:
                        blk = jnp.where(cols[jb] >= a_row, 1.0, 0.0)
                    elif jb > ib:
                        blk = jnp.where(cols[jb] > a_row, 1.0, 0.0)
                    else:
                        blk = jnp.where(cols[jb] > a_row, 1.0, jnp.where(cols[jb] == a_row, earlier, 0.0))
                    acc = blk if acc is None else acc + blk
                pieces.append(jnp.sum(acc, axis=0, keepdims=True))
            rank_rows.append(jnp.concatenate(pieces, axis=1) if nt > 1 else pieces[0])
        rank_r = jnp.minimum(jnp.concatenate(rank_rows, axis=0), fcap)
        rank_ref[r] = jnp.concatenate([rank_r, pad], axis=0).T
        picks = []
        for e in range(ne):
            onehot = jnp.where(rank_r[e:e + 1] == slot, 1.0, 0.0)
            vals_ref[e, r] = jnp.sum(onehot * aff_t[e:e + 1], axis=1, keepdims=True)
            picks.append(onehot.astype(BF16))
        xs = _dot(jnp.concatenate(picks, axis=0), h2_ref[r])
        xs_ref[:, r] = xs.reshape(ne, cap, xs.shape[-1]).astype(BF16)


def _route_call(aff, h2, rb):
    b, n, d = h2.shape
    cap = max(1, EC_FACTOR * n // N_EXPERTS)
    assert b % rb == 0
    return pl.pallas_call(
        functools.partial(_route_kernel, n=n, cap=cap, rb=rb),
        grid=(b // rb,),
        in_specs=[pl.BlockSpec((rb, n, LANES), lambda i: (i, 0, 0)),
                  pl.BlockSpec((rb, n, d), lambda i: (i, 0, 0))],
        out_specs=[pl.BlockSpec((N_EXPERTS, rb, cap, d), lambda i: (0, i, 0, 0)),
                   pl.BlockSpec((N_EXPERTS, rb, cap, 1), lambda i: (0, i, 0, 0)),
                   pl.BlockSpec((rb, n, LANES), lambda i: (i, 0, 0))],
        out_shape=[jax.ShapeDtypeStruct((N_EXPERTS, b, cap, d), BF16),
                   jax.ShapeDtypeStruct((N_EXPERTS, b, cap, 1), F32),
                   jax.ShapeDtypeStruct((b, n, LANES), F32)],
        compiler_params=_params(1),
        name="route_gather",
    )(aff, h2)


def _expert_kernel(xc_ref, xl_ref, vc_ref, vl_ref, wg_ref, wu_ref, wd_ref, yc_ref, yl_ref, acc_c, acc_l):
    f = pl.program_id(1)

    def step(first):
        wg = wg_ref[0, 0].astype(BF16)
        wu = wu_ref[0, 0].astype(BF16)
        wd = wd_ref[0, 0].astype(BF16)
        for x_ref, v_ref, y_ref, acc in ((xc_ref, vc_ref, yc_ref, acc_c), (xl_ref, vl_ref, yl_ref, acc_l)):
            rows = x_ref.shape[1]
            chunk = min(EXPERT_ROW_CHUNK, rows)
            for c0 in range(0, rows, chunk):
                sl = slice(c0, c0 + chunk)
                x = x_ref[0, sl, :]
                a = _dot(x, wg)
                u = _dot(x, wu)
                act = ((a * jax.nn.sigmoid(a)) * u).astype(BF16)
                part = _dot(act, wd)
                if first:
                    acc[sl, :] = part
                else:
                    y_ref[0, sl, :] = ((acc[sl, :] + part) * v_ref[0, sl, :]).astype(BF16)

    pl.when(f == 0)(functools.partial(step, True))
    pl.when(f == EXPERT_F_STEPS - 1)(functools.partial(step, False))


def _expert_call(xs_c, xs_l, vals_c, vals_l, w_gate, w_up, w_down, layer):
    e, rc, d = xs_c.shape
    rl = xs_l.shape[1]
    f_total = w_gate.shape[-1]
    assert EXPERT_F_STEPS == 2
    fb = f_total // EXPERT_F_STEPS
    per_e = lambda rows, w: pl.BlockSpec((1, rows, w), lambda i, j: (i, 0, 0))
    return pl.pallas_call(
        _expert_kernel,
        grid=(e, EXPERT_F_STEPS),
        in_specs=[per_e(rc, d), per_e(rl, d), per_e(rc, 1), per_e(rl, 1),
                  pl.BlockSpec((1, 1, d, fb), lambda i, j: (layer, i, 0, j)),
                  pl.BlockSpec((1, 1, d, fb), lambda i, j: (layer, i, 0, j)),
                  pl.BlockSpec((1, 1, fb, d), lambda i, j: (layer, i, j, 0))],
        out_specs=[per_e(rc, d), per_e(rl, d)],
        out_shape=[jax.ShapeDtypeStruct((e, rc, d), BF16),
                   jax.ShapeDtypeStruct((e, rl, d), BF16)],
        scratch_shapes=[pltpu.VMEM((rc, d), F32), pltpu.VMEM((rl, d), F32)],
        compiler_params=_params(2),
        name="experts",
    )(xs_c, xs_l, vals_c, vals_l, w_gate, w_up, w_down)


def _combine_kernel(y_ref, rank_ref, xn_ref, mod_ref, gf_ref, o_ref, *, n, cap, final):
    ec = N_EXPERTS * cap
    d = xn_ref.shape[-1]
    rank = rank_ref[0].astype(BF16)
    ce = lax.broadcasted_iota(jnp.int32, (LANES, ec), 1) // cap
    ee = lax.broadcasted_iota(jnp.int32, (LANES, ec), 0)
    expand = jnp.where(ce == ee, 1.0, 0.0).astype(BF16)
    rexp = _dot(rank, expand)
    slot = (lax.broadcasted_iota(jnp.int32, (n, ec), 1) % cap).astype(F32)
    onehot = jnp.where(rexp == slot, 1.0, 0.0).astype(BF16)
    y_all = y_ref[:, 0].reshape(ec, d)
    moe = _dot(onehot, y_all)
    x = xn_ref[0] + mod_ref[0, 5:6, :] * moe
    if final:
        x = (x * lax.rsqrt(jnp.mean(x * x, axis=-1, keepdims=True) + EPS)) * gf_ref[...]
    o_ref[0] = x


def _combine_call(y4, rank, xn, mod_g, g_final, final):
    e, b, cap, d = y4.shape
    n = xn.shape[1]
    shared_mod = mod_g.shape[0] == 1
    return pl.pallas_call(
        functools.partial(_combine_kernel, n=n, cap=cap, final=final),
        grid=(b,),
        in_specs=[pl.BlockSpec((e, 1, cap, d), lambda i: (0, i, 0, 0)),
                  pl.BlockSpec((1, n, LANES), lambda i: (i, 0, 0)),
                  pl.BlockSpec((1, n, d), lambda i: (i, 0, 0)),
                  pl.BlockSpec((1, N_MOD, d), (lambda i: (0, 0, 0)) if shared_mod else (lambda i: (i, 0, 0))),
                  pl.BlockSpec((1, d), lambda i: (0, 0))],
        out_specs=pl.BlockSpec((1, n, d), lambda i: (i, 0, 0)),
        out_shape=jax.ShapeDtypeStruct((b, n, d), F32),
        compiler_params=_params(1),
        name="combine",
    )(y4, rank, xn, mod_g, g_final)


def _rope_tables(n):
    t = jnp.arange(n, dtype=jnp.int32)
    row = (t // GRID_W).astype(F32)
    col = (t % GRID_W).astype(F32)
    half = HEAD_DIM // 4
    inv = jnp.power(ROPE_BASE, -jnp.arange(half, dtype=F32) / half)
    ang_r = row[:, None] * inv[None, :]
    ang_c = col[:, None] * inv[None, :]
    cos_h = jnp.concatenate([jnp.cos(ang_r)] * 2 + [jnp.cos(ang_c)] * 2, axis=-1)
    sin_h = jnp.concatenate([-jnp.sin(ang_r), jnp.sin(ang_r), -jnp.sin(ang_c), jnp.sin(ang_c)], axis=-1)
    reps = 512 // HEAD_DIM
    return jnp.tile(cos_h, (1, reps)), jnp.tile(sin_h, (1, reps))


def _nbr_bias_tables(rpb, n):
    rows = n // GRID_W
    assert rows >= NBR_K_ROWS and rows % NBR_Q_ROWS == 0
    col = np.arange(GRID_W)
    cs = np.clip(col - NA_COLS // 2, 0, GRID_W - NA_COLS)
    col_mask = (col[None, :] >= cs[:, None]) & (col[None, :] < cs[:, None] + NA_COLS)
    dc_idx = np.clip(col[None, :] - col[:, None], -(NA_COLS - 1), NA_COLS - 1) + (NA_COLS - 1)
    onehot = (dc_idx[None, :, :] == np.arange(2 * NA_COLS - 1)[:, None, None]).astype(np.float32)
    t = jnp.einsum("hdc,cqk->hqdk", rpb.astype(F32), onehot, precision=lax.Precision.HIGHEST)
    t = jnp.where(col_mask[None, :, None, :], t, NEG_INF)
    neg = jnp.full((rpb.shape[0], GRID_W, GRID_W), NEG_INF, F32)
    groups = []
    for g in range(rows // NBR_Q_ROWS):
        ws = int(np.clip(g * NBR_Q_ROWS - NA_ROWS // 2, 0, rows - NBR_K_ROWS))
        q_blocks = []
        for qr in range(g * NBR_Q_ROWS, (g + 1) * NBR_Q_ROWS):
            rs = int(np.clip(qr - NA_ROWS // 2, 0, rows - NA_ROWS))
            tiles = [t[:, :, kr - qr + NA_ROWS - 1, :] if rs <= kr < rs + NA_ROWS else neg
                     for kr in range(ws, ws + NBR_K_ROWS)]
            q_blocks.append(jnp.concatenate(tiles, axis=-1))
        groups.append(jnp.concatenate(q_blocks, axis=1))
    return jnp.stack(groups)


def _block_diag(w):
    nb, bw, _ = w.shape
    eye = jnp.eye(nb, dtype=w.dtype)
    return (eye[:, None, :, None] * w[:, :, None, :]).reshape(nb * bw, nb * bw)


def kernel(x_prompt, x_sample, state_lru, cache_k_win, cache_v_win, cache_k_nbr, cache_v_nbr, c, c_ctx, w_mod, b_mod, g_norm1, w_in, conv_w, conv_b, w_gate_r, b_gate_r, w_gate_i, b_gate_i, lru_lambda, sink_logit, nbr_bias, w_out, g_norm2, w_router, w_exp_gate, w_exp_up, w_exp_down, g_final):
    bc, seq, d = x_prompt.shape
    bl, n_lat, _ = x_sample.shape
    depth = w_mod.shape[0]
    assert bl + 1 <= 8 and d == 1024

    cond8 = jnp.zeros((8, d), F32).at[0].set(c_ctx).at[1:1 + bl].set(c)
    mods = _adaln_call(cond8, w_mod, b_mod.reshape(depth, 1, -1)).reshape(depth, 8, N_MOD, d)
    rope_tabs = _rope_tables(n_lat)
    gf = g_final.reshape(1, d)
    assert SCALE == 0.125
    new_cols = np.arange(w_in.shape[-1]) - LRU_W
    is_q = ((new_cols >= QC0) & (new_cols < KC0)) | ((new_cols >= QB0) & (new_cols < KB0))
    q_scale = jnp.asarray(np.where(is_q, SCALE, 1.0), F32)

    xc = x_prompt.reshape(1, bc * seq, d)
    xl = x_sample
    zeros_state = jnp.zeros((bc, 1, 2, 256), F32)
    caches = None
    st = None
    for l in range(depth):
        final = l == depth - 1
        mod_c = mods[l, 0:1]
        mod_l = mods[l, 1:1 + bl]
        g1 = g_norm1[l].reshape(1, d)
        g2 = g_norm2[l].reshape(1, d)
        w_l = w_in[l]
        w_in_b = (jnp.concatenate([w_l[:, :512], w_l[:, 1152:], w_l[:, 512:1152]], axis=1) * q_scale).astype(BF16)
        wo = w_out[l].astype(BF16)
        wo_a, wo_b, wo_c = wo[:256], wo[256:640], wo[640:]
        wr_b = jnp.zeros((d, LANES), BF16).at[:, :N_EXPERTS].set(w_router[l].astype(BF16))
        w_gates = jnp.concatenate([_block_diag(w_gate_r[l, 0]), _block_diag(w_gate_i[l, 0]),
                                   _block_diag(w_gate_r[l, 1]), _block_diag(w_gate_i[l, 1])], axis=1).astype(BF16)
        b_gates = jnp.concatenate([b_gate_r[l, 0], b_gate_i[l, 0], b_gate_r[l, 1], b_gate_i[l, 1]]).reshape(1, -1)
        cb = conv_b[l].reshape(1, -1)
        sink6 = sink_logit[l].reshape(-1)
        bias_tab = _nbr_bias_tables(nbr_bias[l], n_lat)

        ulru_c, qkv_c, *caches = _inproj_call(xc, mod_c, g1, w_in_b, cache_seq=seq, layer=l, depth=depth,
                                              prev_caches=caches)
        ya_c, st = _lru_call(ulru_c.reshape(bc, seq, 512), conv_w[l], cb, w_gates, b_gates, lru_lambda[l],
                             zeros_state, 0, st_layer=l, st_depth=depth, prev_state=st)
        yb_c, yc_c = _ctx_attn_call(sink6, qkv_c.reshape(bc, seq, -1))
        xn_c, h2_c, aff_c = _outproj_call(xc, ya_c.reshape(1, bc * seq, -1), yb_c.reshape(1, bc * seq, -1),
                                          yc_c.reshape(1, bc * seq, -1), mod_c, g2, wo_a, wo_b, wo_c, wr_b)
        xs_c, vals_c, rank_c = _route_call(aff_c.reshape(bc, seq, LANES), h2_c.reshape(bc, seq, d), ROUTE_REQS_CTX)

        ulru_l, qkv_l = _inproj_call(xl, mod_l, g1, w_in_b, rope_tabs=rope_tabs)
        ya_l, _ = _lru_call(ulru_l, conv_w[l], cb, w_gates, b_gates, lru_lambda[l], state_lru, l)
        yb_l = _win_attn_call(sink6, qkv_l, cache_k_win, cache_v_win, l)
        yc_l = _nbr_attn_call(qkv_l, bias_tab, cache_k_nbr, cache_v_nbr, l)
        xn_l, h2_l, aff_l = _outproj_call(xl, ya_l, yb_l, yc_l, mod_l, g2, wo_a, wo_b, wo_c, wr_b)
        xs_l, vals_l, rank_l = _route_call(aff_l, h2_l, ROUTE_REQS_LAT)

        cap_c, cap_l = xs_c.shape[2], xs_l.shape[2]
        y_c, y_l = _expert_call(xs_c.reshape(N_EXPERTS, bc * cap_c, d), xs_l.reshape(N_EXPERTS, bl * cap_l, d),
                                vals_c.reshape(N_EXPERTS, bc * cap_c, 1), vals_l.reshape(N_EXPERTS, bl * cap_l, 1),
                                w_exp_gate, w_exp_up, w_exp_down, l)
        xc = _combine_call(y_c.reshape(N_EXPERTS, bc, cap_c, d), rank_c, xn_c.reshape(bc, seq, d), mod_c, gf, final)
        xl = _combine_call(y_l.reshape(N_EXPERTS, bl, cap_l, d), rank_l, xn_l, mod_l, gf, final)
        xc = xc.reshape(1, bc * seq, d)

    y_prompt = xc.reshape(bc, seq, d)
    y_sample = xl
    return (y_prompt, y_sample, st, *caches)
```

```python
import functools

import numpy as np
import jax
import jax.numpy as jnp
from jax import lax
from jax.experimental import pallas as pl
from jax.experimental.pallas import tpu as pltpu

F32 = jnp.float32
BF16 = jnp.bfloat16

HEAD_DIM = 64
GRID_W = 64
LRU_C = 8.0
CONV_W = 4
KV_B = 2
G_B = 3
H_B = KV_B * G_B
H_C = 6
WINDOW = 128
WIN_BLK = 128
NA_ROWS = 8
NA_COLS = 16
ROPE_BASE = 10000.0
N_EXPERTS = 16
EC_FACTOR = 2
N_MOD = 6
EPS = 1e-6
NEG_INF = -1e30
SCALE = HEAD_DIM ** -0.5

LANES = 128
VMEM_LIMIT_BYTES = 56 * 1024 * 1024

TOKEN_BLOCK = 512
MOD_COL_BLOCK = 1536
EXPERT_F_STEPS = 2
EXPERT_ROW_CHUNK = 256
LRU_W = 512
QC0, KC0, VC0, QB0, KB0, VB0, ATT_W = 0, 384, 768, 1152, 1536, 1664, 1792
CTX_ATTN_REQS = 2
NBR_Q_ROWS = 4
NBR_K_ROWS = 12
ROUTE_REQS_CTX = 8
ROUTE_REQS_LAT = 2
COMBINE_REQS_CTX = 4


def _params(n_axes):
    return pltpu.CompilerParams(dimension_semantics=("arbitrary",) * n_axes,
                                vmem_limit_bytes=VMEM_LIMIT_BYTES)


def _dot(a, b):
    return jnp.dot(a, b, preferred_element_type=F32)


def _dot_nt(a, b):
    return lax.dot_general(a, b, (((1,), (1,)), ((), ())), preferred_element_type=F32)


def _adaln_kernel(cond_ref, w_ref, b_ref, o_ref):
    c = cond_ref[...]
    s = c * jax.nn.sigmoid(c)
    o_ref[0] = _dot(s.astype(BF16), w_ref[0].astype(BF16)) + b_ref[0]


def _adaln_call(cond8, w_mod, b_mod3):
    depth, d, n = w_mod.shape
    nb = MOD_COL_BLOCK
    return pl.pallas_call(
        _adaln_kernel,
        grid=(depth, n // nb),
        in_specs=[pl.BlockSpec((8, d), lambda l, j: (0, 0)),
                  pl.BlockSpec((1, d, nb), lambda l, j: (l, 0, j)),
                  pl.BlockSpec((1, 1, nb), lambda l, j: (l, 0, j))],
        out_specs=pl.BlockSpec((1, 8, nb), lambda l, j: (l, 0, j)),
        out_shape=jax.ShapeDtypeStruct((depth, 8, n), F32),
        compiler_params=_params(2),
        name="adaln",
    )(cond8, w_mod, b_mod3)


def _rms_modulate(x, gain, shift, scale):
    y = x * lax.rsqrt(jnp.mean(x * x, axis=-1, keepdims=True) + EPS)
    return (y * gain) * (1.0 + scale) + shift


def _inproj_kernel(*refs, rope, cache, seq):
    if rope:
        x_ref, mod_ref, g_ref, w_ref, cos_ref, sin_ref, ulru_ref, qkv_ref = refs
    elif cache:
        x_ref, mod_ref, g_ref, w_ref = refs[:4]
        ulru_ref, qkv_ref, kw_ref, vw_ref, kn_ref, vn_ref = refs[-6:]
    else:
        x_ref, mod_ref, g_ref, w_ref, ulru_ref, qkv_ref = refs
    x = x_ref[0]
    h = _rms_modulate(x, g_ref[...], mod_ref[0, 0:1, :], mod_ref[0, 1:2, :])
    u = _dot(h.astype(BF16), w_ref[...])
    tb = u.shape[0]
    ulru_ref[0] = u[:, :LRU_W]
    if rope:
        lo, hi = LRU_W + QB0, LRU_W + VB0
        qk = u[:, lo:hi]
        lane = lax.broadcasted_iota(jnp.int32, (tb, hi - lo), 1)
        first = (lane & 31) < 16
        partner = jnp.where(first, pltpu.roll(qk, hi - lo - 16, 1), pltpu.roll(qk, 16, 1))
        qk = qk * cos_ref[...] + partner * sin_ref[...]
        qkv_ref[0, :, :QB0] = u[:, LRU_W:lo].astype(BF16)
        qkv_ref[0, :, QB0:VB0] = qk.astype(BF16)
        qkv_ref[0, :, VB0:] = u[:, hi:].astype(BF16)
    else:
        qkv_ref[0] = u[:, LRU_W:].astype(BF16)
    if cache:
        for r in range(tb // seq):
            rows = slice(r * seq, (r + 1) * seq)
            for ref, c0, heads in ((kw_ref, KB0, KV_B), (vw_ref, VB0, KV_B), (kn_ref, KC0, H_C), (vn_ref, VC0, H_C)):
                for p in range(heads // 2):
                    pair_t = u[rows, LRU_W + c0 + LANES * p: LRU_W + c0 + LANES * (p + 1)].T
                    ref[r, 0, 2 * p] = pair_t[:HEAD_DIM]
                    ref[r, 0, 2 * p + 1] = pair_t[HEAD_DIM:]


def _inproj_call(x3, mod_g, gain, w_in_b, rope_tabs=None, cache_seq=None, layer=0, depth=1, prev_caches=None):
    g, n, d = x3.shape
    d_in = w_in_b.shape[1]
    tb = TOKEN_BLOCK
    rope = rope_tabs is not None
    cache = cache_seq is not None
    in_specs = [pl.BlockSpec((1, tb, d), lambda i, j: (i, j, 0)),
                pl.BlockSpec((1, N_MOD, d), lambda i, j: (i, 0, 0)),
                pl.BlockSpec((1, d), lambda i, j: (0, 0)),
                pl.BlockSpec((d, d_in), lambda i, j: (0, 0))]
    args = [x3, mod_g, gain, w_in_b]
    assert d_in == LRU_W + ATT_W
    out_shape = [jax.ShapeDtypeStruct((g, n, LRU_W), F32),
                 jax.ShapeDtypeStruct((g, n, ATT_W), BF16)]
    out_specs = [pl.BlockSpec((1, tb, LRU_W), lambda i, j: (i, j, 0)),
                 pl.BlockSpec((1, tb, ATT_W), lambda i, j: (i, j, 0))]
    if rope:
        in_specs += [pl.BlockSpec((tb, VB0 - QB0), lambda i, j: (j, 0))] * 2
        args += list(rope_tabs)
    aliases = {}
    if cache:
        assert g == 1 and tb % cache_seq == 0
        rb = tb // cache_seq
        nreq = n // cache_seq
        for heads in (KV_B, KV_B, H_C, H_C):
            out_shape.append(jax.ShapeDtypeStruct((nreq, depth, heads, HEAD_DIM, cache_seq), F32))
            out_specs.append(pl.BlockSpec((rb, 1, heads, HEAD_DIM, cache_seq), lambda i, j: (j, layer, 0, 0, 0)))
        if prev_caches is not None:
            for k, buf in enumerate(prev_caches):
                aliases[len(args)] = 2 + k
                in_specs.append(pl.BlockSpec(memory_space=pl.ANY))
                args.append(buf)
    return pl.pallas_call(
        functools.partial(_inproj_kernel, rope=rope, cache=cache, seq=cache_seq),
        grid=(g, n // tb),
        in_specs=in_specs,
        out_specs=out_specs,
        out_shape=out_shape,
        input_output_aliases=aliases,
        compiler_params=_params(2),
        name="inproj_rope" if rope else "inproj_ctx",
    )(*args)


def _lru_kernel(*refs, n):
    u_ref, cw_ref, cb_ref, wg_ref, bg_ref, lam_ref, h0_ref = refs[:7]
    ya_ref, st_ref, af_s, xf_s, ab_s, xb_s, yf_s, yb_s = refs[-8:]
    c = 256
    nblk = n // 8
    u = u_ref[0]
    xa = u[:, :c]
    ga = u[:, c:]
    t = lax.broadcasted_iota(jnp.int32, (n, c), 0)
    cw = cw_ref[...]
    xc = cw[2:3] * xa + cb_ref[...]
    xc = xc + cw[0:1] * jnp.where(t >= 2, pltpu.roll(xa, 2, 0), 0.0)
    xc = xc + cw[1:2] * jnp.where(t >= 1, pltpu.roll(xa, 1, 0), 0.0)
    xc = xc + cw[3:4] * jnp.where(t < n - 1, pltpu.roll(xa, n - 1, 0), 0.0)
    gates = _dot(xc.astype(BF16), wg_ref[...]) + bg_ref[...]
    lam = lam_ref[...]
    log_sig = jnp.minimum(lam, 0.0) - jnp.log1p(jnp.exp(-jnp.abs(lam)))
    sub = lax.broadcasted_iota(jnp.int32, (nblk, 8, c), 1)
    for d, (a_s, x_s) in enumerate(((af_s, xf_s), (ab_s, xb_s))):
        r = jax.nn.sigmoid(gates[:, 2 * d * c:(2 * d + 1) * c])
        i = jax.nn.sigmoid(gates[:, (2 * d + 1) * c:(2 * d + 2) * c])
        log_a = LRU_C * r * log_sig[d:d + 1]
        a = jnp.exp(log_a)
        t_in = jnp.tanh(-log_a) * (1.0 + a * a)
        xin = jnp.where(t_in > 0.0, t_in * lax.rsqrt(t_in), 0.0) * (i * xc)
        a3 = a.reshape(nblk, 8, c)
        x3 = xin.reshape(nblk, 8, c)
        for s in (1, 2, 4):
            if d == 0:
                m = sub >= s
                sh = s
            else:
                m = sub < 8 - s
                sh = 8 - s
            x3 = jnp.where(m, a3 * pltpu.roll(x3, sh, 1) + x3, x3)
            a3 = jnp.where(m, a3 * pltpu.roll(a3, sh, 1), a3)
        a_s[...] = a3
        x_s[...] = x3

    h0 = h0_ref[0, 0]

    def body(k, carry):
        hf, hb = carry
        blk_f = af_s[k] * hf + xf_s[k]
        yf_s[k] = blk_f
        kb = nblk - 1 - k
        blk_b = ab_s[kb] * hb + xb_s[kb]
        yb_s[kb] = blk_b
        return blk_f[7:8], blk_b[0:1]

    hf, hb = lax.fori_loop(0, nblk, body, (h0[0:1], h0[1:2]))
    y = (yf_s[...] + yb_s[...]).reshape(n, c) * jax.nn.gelu(ga)
    ya_ref[0] = y.astype(BF16)
    st_ref[0, 0, 0:1, :] = hf
    st_ref[0, 0, 1:2, :] = hb


def _lru_call(ulru, conv_w, conv_b, w_gates_b, b_gates, lam, h0, h0_layer, st_layer=0, st_depth=1, prev_state=None):
    b, n, _ = ulru.shape
    c = 256
    full = lambda shape: pl.BlockSpec(shape, lambda i: (0,) * len(shape))
    scratch = [pltpu.VMEM((n // 8, 8, c), F32) for _ in range(6)]
    in_specs = [pl.BlockSpec((1, n, 2 * c), lambda i: (i, 0, 0)),
                full((CONV_W, c)), full((1, c)), full((c, 4 * c)), full((1, 4 * c)), full((2, c)),
                pl.BlockSpec((1, 1, 2, c), lambda i: (i, h0_layer, 0, 0))]
    args = [ulru, conv_w, conv_b, w_gates_b, b_gates, lam, h0]
    aliases = {}
    if prev_state is not None:
        aliases[len(args)] = 1
        in_specs.append(pl.BlockSpec(memory_space=pl.ANY))
        args.append(prev_state)
    return pl.pallas_call(
        functools.partial(_lru_kernel, n=n),
        grid=(b,),
        in_specs=in_specs,
        out_specs=[pl.BlockSpec((1, n, c), lambda i: (i, 0, 0)),
                   pl.BlockSpec((1, 1, 2, c), lambda i: (i, st_layer, 0, 0))],
        out_shape=[jax.ShapeDtypeStruct((b, n, c), BF16),
                   jax.ShapeDtypeStruct((b, st_depth, 2, c), F32)],
        scratch_shapes=scratch,
        input_output_aliases=aliases,
        compiler_params=_params(1),
        name="rglru",
    )(*args)


def _softmax_pv(parts, sink):
    mx = None
    for s, _, _ in parts:
        cur = jnp.max(s, axis=-1, keepdims=True)
        mx = cur if mx is None else jnp.maximum(mx, cur)
    if sink is not None:
        mx = jnp.maximum(mx, sink)
    den = None
    out = None
    for s, v, v_t in parts:
        p = jnp.exp(s - mx)
        cur = jnp.sum(p, axis=-1, keepdims=True)
        den = cur if den is None else den + cur
        o = _dot_nt(p.astype(BF16), v) if v_t else _dot(p.astype(BF16), v)
        out = o if out is None else out + o
    if sink is not None:
        den = den + jnp.exp(sink - mx)
    return out / den


def _swap_halves(x):
    return jnp.concatenate([x[:, HEAD_DIM:], x[:, :HEAD_DIM]], axis=1)


def _keep_half(x, lo_mask, half):
    zero = jnp.zeros_like(x)
    return jnp.where(lo_mask, x, zero) if half == 0 else jnp.where(lo_mask, zero, x)


def _ctx_attn_kernel(sink_ref, att_ref, ob_ref, oc_ref, *, n, rb):
    lo = lax.broadcasted_iota(jnp.int32, (n, LANES), 1) < HEAD_DIM
    for r in range(rb):
        for p in range(H_C // 2):
            qp = att_ref[r, :, QC0 + LANES * p: QC0 + LANES * (p + 1)]
            kp = att_ref[r, :, KC0 + LANES * p: KC0 + LANES * (p + 1)]
            vp = att_ref[r, :, VC0 + LANES * p: VC0 + LANES * (p + 1)]
            outs = [_softmax_pv([(_dot_nt(qp, _keep_half(kp, lo, half)), vp, False)], None) for half in range(2)]
            oc_ref[r, :, LANES * p: LANES * (p + 1)] = jnp.where(lo, outs[0], outs[1]).astype(BF16)
        kpair = att_ref[r, :, KB0: KB0 + LANES]
        vpair = att_ref[r, :, VB0: VB0 + LANES]
        kpair_sw = _swap_halves(kpair)
        vpair_sw = _swap_halves(vpair)
        for p in range(H_B // 2):
            qp = att_ref[r, :, QB0 + LANES * p: QB0 + LANES * (p + 1)]
            outs = []
            for half in range(2):
                h = 2 * p + half
                aligned = (h // G_B) == half
                ksrc, vsrc = (kpair, vpair) if aligned else (kpair_sw, vpair_sw)
                outs.append(_softmax_pv([(_dot_nt(qp, _keep_half(ksrc, lo, half)), vsrc, False)], sink_ref[h]))
            ob_ref[r, :, LANES * p: LANES * (p + 1)] = jnp.where(lo, outs[0], outs[1]).astype(BF16)


def _ctx_attn_call(sink6, att):
    b, n, w = att.shape
    rb = CTX_ATTN_REQS
    assert b % rb == 0 and KV_B == 2
    return pl.pallas_call(
        functools.partial(_ctx_attn_kernel, n=n, rb=rb),
        grid=(b // rb,),
        in_specs=[pl.BlockSpec(memory_space=pltpu.SMEM),
                  pl.BlockSpec((rb, n, w), lambda i: (i, 0, 0))],
        out_specs=[pl.BlockSpec((rb, n, 384), lambda i: (i, 0, 0))] * 2,
        out_shape=[jax.ShapeDtypeStruct((b, n, 384), BF16)] * 2,
        compiler_params=_params(1),
        name="ctx_attn",
    )(sink6, att)


def _win_attn_kernel(sink_ref, qkv_ref, kc_ref, vc_ref, o_ref, *, n):
    j = pl.program_id(1)
    nloc = 3 * WIN_BLK
    q0 = pl.multiple_of(j * WIN_BLK, WIN_BLK)
    ks = pl.multiple_of(jnp.clip((j - 1) * WIN_BLK, 0, n - nloc), WIN_BLK)
    row = lax.broadcasted_iota(jnp.int32, (G_B * WIN_BLK, nloc), 0)
    col = lax.broadcasted_iota(jnp.int32, (G_B * WIN_BLK, nloc), 1)
    qpos = q0 + (row & (WIN_BLK - 1))
    kpos = ks + col
    mask = jnp.abs(qpos - kpos) <= WINDOW
    rowh = lax.broadcasted_iota(jnp.int32, (G_B * WIN_BLK, 1), 0) // WIN_BLK
    for kv in range(KV_B):
        gw = G_B * HEAD_DIM
        qblk = qkv_ref[0, pl.ds(q0, WIN_BLK), QB0 + gw * kv: QB0 + gw * (kv + 1)]
        q3 = jnp.concatenate([qblk[:, 64 * g: 64 * (g + 1)] for g in range(G_B)], axis=0)
        kl = qkv_ref[0, pl.ds(ks, nloc), KB0 + 64 * kv: KB0 + 64 * (kv + 1)]
        vl = qkv_ref[0, pl.ds(ks, nloc), VB0 + 64 * kv: VB0 + 64 * (kv + 1)]
        kc_t = kc_ref[0, 0, kv].astype(BF16)
        vc_t = vc_ref[0, 0, kv].astype(BF16)
        s_loc = jnp.where(mask, _dot_nt(q3, kl), NEG_INF)
        s_ctx = _dot(q3, kc_t)
        sk = jnp.where(rowh == 0, sink_ref[3 * kv],
                       jnp.where(rowh == 1, sink_ref[3 * kv + 1], sink_ref[3 * kv + 2]))
        o = _softmax_pv([(s_loc, vl, False), (s_ctx, vc_t, True)], sk)
        for g in range(G_B):
            h = kv * G_B + g
            o_ref[0, :, 64 * h: 64 * (h + 1)] = o[g * WIN_BLK:(g + 1) * WIN_BLK].astype(BF16)


def _win_attn_call(sink6, qkv, cache_k, cache_v, layer):
    b, n, w = qkv.shape
    past = cache_k.shape[4]
    cache_spec = pl.BlockSpec((1, 1, KV_B, HEAD_DIM, past), lambda i, j: (i, layer, 0, 0, 0))
    return pl.pallas_call(
        functools.partial(_win_attn_kernel, n=n),
        grid=(b, n // WIN_BLK),
        in_specs=[pl.BlockSpec(memory_space=pltpu.SMEM),
                  pl.BlockSpec((1, n, w), lambda i, j: (i, 0, 0)),
                  cache_spec, cache_spec],
        out_specs=pl.BlockSpec((1, WIN_BLK, 384), lambda i, j: (i, j, 0)),
        out_shape=jax.ShapeDtypeStruct((b, n, 384), BF16),
        compiler_params=_params(2),
        name="win_attn",
    )(sink6, qkv, cache_k, cache_v)


def _nbr_window_start(g, rows):
    return jnp.clip(g * NBR_Q_ROWS - NA_ROWS // 2, 0, rows - NBR_K_ROWS)


def _nbr_attn_kernel(q_ref, k_ref, v_ref, bias_ref, kc_ref, vc_ref, o_ref, *, n):
    g = pl.program_id(0)
    nq = NBR_Q_ROWS * GRID_W
    nk = NBR_K_ROWS * GRID_W
    k0 = pl.multiple_of(_nbr_window_start(g, n // GRID_W) * GRID_W, GRID_W)
    lo_k = lax.broadcasted_iota(jnp.int32, (nk, LANES), 1) < HEAD_DIM
    lo_q = lax.broadcasted_iota(jnp.int32, (nq, LANES), 1) < HEAD_DIM
    zpad = jnp.zeros((HEAD_DIM, kc_ref.shape[4]), BF16)
    for p in range(H_C // 2):
        qp = q_ref[0, :, LANES * p: LANES * (p + 1)]
        kp = k_ref[0, pl.ds(k0, nk), LANES * p: LANES * (p + 1)]
        vp = v_ref[0, pl.ds(k0, nk), LANES * p: LANES * (p + 1)]
        outs = []
        for half in range(2):
            h = 2 * p + half
            kc_t = kc_ref[0, 0, h].astype(BF16)
            vc_t = vc_ref[0, 0, h].astype(BF16)
            kc_t = jnp.concatenate([kc_t, zpad] if half == 0 else [zpad, kc_t], axis=0)
            vc_t = jnp.concatenate([vc_t, zpad] if half == 0 else [zpad, vc_t], axis=0)
            s_loc = _dot_nt(qp, _keep_half(kp, lo_k, half)) + bias_ref[0, 0, h]
            s_ctx = _dot(qp, kc_t)
            outs.append(_softmax_pv([(s_loc, vp, False), (s_ctx, vc_t, True)], None))
        o_ref[0, :, LANES * p: LANES * (p + 1)] = jnp.where(lo_q, outs[0], outs[1]).astype(BF16)


def _nbr_attn_call(att, bias_tab, cache_k, cache_v, layer):
    b, n, w = att.shape
    past = cache_k.shape[4]
    nq = NBR_Q_ROWS * GRID_W
    nk = NBR_K_ROWS * GRID_W
    gw = H_C * HEAD_DIM
    assert (QC0, KC0, VC0) == (0, gw, 2 * gw) and n % nq == 0
    cache_spec = pl.BlockSpec((1, 1, H_C, HEAD_DIM, past), lambda g, i: (i, layer, 0, 0, 0))
    return pl.pallas_call(
        functools.partial(_nbr_attn_kernel, n=n),
        grid=(n // nq, b),
        in_specs=[pl.BlockSpec((1, nq, gw), lambda g, i: (i, g, 0)),
                  pl.BlockSpec((1, n, gw), lambda g, i: (i, 0, 1)),
                  pl.BlockSpec((1, n, gw), lambda g, i: (i, 0, 2)),
                  pl.BlockSpec((1, 1, H_C, nq, nk), lambda g, i: (layer, g, 0, 0, 0)),
                  cache_spec, cache_spec],
        out_specs=pl.BlockSpec((1, nq, gw), lambda g, i: (i, g, 0)),
        out_shape=jax.ShapeDtypeStruct((b, n, gw), BF16),
        compiler_params=_params(2),
        name="nbr_attn",
    )(att, att, att, bias_tab, cache_k, cache_v)


def _outproj_kernel(x_ref, ya_ref, yb_ref, yc_ref, mod_ref, g_ref, wa_ref, wb_ref, wc_ref, wr_ref,
                    xn_ref, h2_ref, aff_ref):
    proj = _dot(ya_ref[0], wa_ref[...]) + _dot(yb_ref[0], wb_ref[...]) + _dot(yc_ref[0], wc_ref[...])
    xn = x_ref[0] + mod_ref[0, 2:3, :] * proj
    xn_ref[0] = xn
    h2 = _rms_modulate(xn, g_ref[...], mod_ref[0, 3:4, :], mod_ref[0, 4:5, :]).astype(BF16)
    h2_ref[0] = h2
    logits = _dot(h2, wr_ref[...])
    lane = lax.broadcasted_iota(jnp.int32, logits.shape, 1)
    valid = lane < N_EXPERTS
    logits = jnp.where(valid, logits, NEG_INF)
    e = jnp.exp(logits - jnp.max(logits, axis=-1, keepdims=True))
    e = jnp.where(valid, e, 0.0)
    aff_ref[0] = e / jnp.sum(e, axis=-1, keepdims=True)


def _outproj_call(x3, ya, yb, yc, mod_g, gain, wo_a, wo_b, wo_c, w_router_b):
    g, n, d = x3.shape
    tb = TOKEN_BLOCK
    tok = lambda w: pl.BlockSpec((1, tb, w), lambda i, j: (i, j, 0))
    full = lambda shape: pl.BlockSpec(shape, lambda i, j: (0,) * len(shape))
    return pl.pallas_call(
        _outproj_kernel,
        grid=(g, n // tb),
        in_specs=[tok(d), tok(256), tok(384), tok(384),
                  pl.BlockSpec((1, N_MOD, d), lambda i, j: (i, 0, 0)),
                  full((1, d)), full((256, d)), full((384, d)), full((384, d)), full((d, LANES))],
        out_specs=[tok(d), tok(d), tok(LANES)],
        out_shape=[jax.ShapeDtypeStruct((g, n, d), F32),
                   jax.ShapeDtypeStruct((g, n, d), BF16),
                   jax.ShapeDtypeStruct((g, n, LANES), F32)],
        compiler_params=_params(2),
        name="outproj_router",
    )(x3, ya, yb, yc, mod_g, gain, wo_a, wo_b, wo_c, w_router_b)


def _route_kernel(aff_ref, h2_ref, xs_ref, vals_ref, rank_ref, *, n, cap, rb):
    ne = N_EXPERTS
    nt = n // LANES
    fcap = float(cap)
    jj = lax.broadcasted_iota(jnp.int32, (LANES, LANES), 0)
    ii = lax.broadcasted_iota(jnp.int32, (LANES, LANES), 1)
    earlier = jnp.where(jj < ii, 1.0, 0.0)
    slot = lax.broadcasted_iota(jnp.int32, (cap, n), 0).astype(F32)
    pad = jnp.full((LANES - ne, n), fcap, F32)
    for r in range(rb):
        aff = aff_ref[r]
        aff_t = aff.T[:ne]
        rank_rows = []
        for e in range(ne):
            cols = [jnp.broadcast_to(aff[jb * LANES:(jb + 1) * LANES, e:e + 1], (LANES, LANES)) for jb in range(nt)]
            pieces = []
            for ib in range(nt):
                a_row = aff_t[e:e + 1, ib * LANES:(ib + 1) * LANES]
                acc = None
                for jb in range(nt):
                    if jb < ib:
                        blk = jnp.where(cols[jb] >= a_row, 1.0, 0.0)
                    elif jb > ib:
                        blk = jnp.where(cols[jb] > a_row, 1.0, 0.0)
                    else:
                        blk = jnp.where(cols[jb] > a_row, 1.0, jnp.where(cols[jb] == a_row, earlier, 0.0))
                    acc = blk if acc is None else acc + blk
                pieces.append(jnp.sum(acc, axis=0, keepdims=True))
            rank_rows.append(jnp.concatenate(pieces, axis=1) if nt > 1 else pieces[0])
        rank_r = jnp.minimum(jnp.concatenate(rank_rows, axis=0), fcap)
        rank_ref[r] = jnp.concatenate([rank_r, pad], axis=0).T
        picks = []
        for e in range(ne):
            onehot = jnp.where(rank_r[e:e + 1] == slot, 1.0, 0.0)
            vals_ref[e, r] = jnp.sum(onehot * aff_t[e:e + 1], axis=1, keepdims=True)
            picks.append(onehot.astype(BF16))
        xs = _dot(jnp.concatenate(picks, axis=0), h2_ref[r])
        xs_ref[:, r] = xs.reshape(ne, cap, xs.shape[-1]).astype(BF16)


def _route_call(aff, h2, rb):
    b, n, d = h2.shape
    cap = max(1, EC_FACTOR * n // N_EXPERTS)
    assert b % rb == 0
    return pl.pallas_call(
        functools.partial(_route_kernel, n=n, cap=cap, rb=rb),
        grid=(b // rb,),
        in_specs=[pl.BlockSpec((rb, n, LANES), lambda i: (i, 0, 0)),
                  pl.BlockSpec((rb, n, d), lambda i: (i, 0, 0))],
        out_specs=[pl.BlockSpec((N_EXPERTS, rb, cap, d), lambda i: (0, i, 0, 0)),
                   pl.BlockSpec((N_EXPERTS, rb, cap, 1), lambda i: (0, i, 0, 0)),
                   pl.BlockSpec((rb, n, LANES), lambda i: (i, 0, 0))],
        out_shape=[jax.ShapeDtypeStruct((N_EXPERTS, b, cap, d), BF16),
                   jax.ShapeDtypeStruct((N_EXPERTS, b, cap, 1), F32),
                   jax.ShapeDtypeStruct((b, n, LANES), F32)],
        compiler_params=_params(1),
        name="route_gather",
    )(aff, h2)


def _expert_kernel(xc_ref, xl_ref, vc_ref, vl_ref, wg_ref, wu_ref, wd_ref, yc_ref, yl_ref, acc_c, acc_l):
    f = pl.program_id(1)

    def step(first):
        wg = wg_ref[0, 0].astype(BF16)
        wu = wu_ref[0, 0].astype(BF16)
        wd = wd_ref[0, 0].astype(BF16)
        for x_ref, v_ref, y_ref, acc in ((xc_ref, vc_ref, yc_ref, acc_c), (xl_ref, vl_ref, yl_ref, acc_l)):
            rows = x_ref.shape[1]
            chunk = min(EXPERT_ROW_CHUNK, rows)
            for c0 in range(0, rows, chunk):
                sl = slice(c0, c0 + chunk)
                x = x_ref[0, sl, :]
                a = _dot(x, wg)
                u = _dot(x, wu)
                act = ((a * jax.nn.sigmoid(a)) * u).astype(BF16)
                part = _dot(act, wd)
                if first:
                    acc[sl, :] = part
                else:
                    y_ref[0, sl, :] = ((acc[sl, :] + part) * v_ref[0, sl, :]).astype(BF16)

    pl.when(f == 0)(functools.partial(step, True))
    pl.when(f == EXPERT_F_STEPS - 1)(functools.partial(step, False))


def _expert_call(xs_c, xs_l, vals_c, vals_l, w_gate, w_up, w_down, layer):
    e, rc, d = xs_c.shape
    rl = xs_l.shape[1]
    f_total = w_gate.shape[-1]
    assert EXPERT_F_STEPS == 2
    fb = f_total // EXPERT_F_STEPS
    per_e = lambda rows, w: pl.BlockSpec((1, rows, w), lambda i, j: (i, 0, 0))
    return pl.pallas_call(
        _expert_kernel,
        grid=(e, EXPERT_F_STEPS),
        in_specs=[per_e(rc, d), per_e(rl, d), per_e(rc, 1), per_e(rl, 1),
                  pl.BlockSpec((1, 1, d, fb), lambda i, j: (layer, i, 0, j)),
                  pl.BlockSpec((1, 1, d, fb), lambda i, j: (layer, i, 0, j)),
                  pl.BlockSpec((1, 1, fb, d), lambda i, j: (layer, i, j, 0))],
        out_specs=[per_e(rc, d), per_e(rl, d)],
        out_shape=[jax.ShapeDtypeStruct((e, rc, d), BF16),
                   jax.ShapeDtypeStruct((e, rl, d), BF16)],
        scratch_shapes=[pltpu.VMEM((rc, d), F32), pltpu.VMEM((rl, d), F32)],
        compiler_params=_params(2),
        name="experts",
    )(xs_c, xs_l, vals_c, vals_l, w_gate, w_up, w_down)


def _combine_kernel(y_ref, rank_ref, xn_ref, mod_ref, gf_ref, o_ref, *, n, cap, final, rb):
    ec = N_EXPERTS * cap
    d = xn_ref.shape[-1]
    ce = lax.broadcasted_iota(jnp.int32, (LANES, ec), 1) // cap
    ee = lax.broadcasted_iota(jnp.int32, (LANES, ec), 0)
    expand = jnp.where(ce == ee, 1.0, 0.0).astype(BF16)
    slot = (lax.broadcasted_iota(jnp.int32, (n, ec), 1) % cap).astype(F32)
    for r in range(rb):
        rank = rank_ref[r].astype(BF16)
        rexp = _dot(rank, expand)
        onehot = jnp.where(rexp == slot, 1.0, 0.0).astype(BF16)
        moe = _dot(onehot, y_ref[:, r].reshape(ec, d))
        x = xn_ref[r] + mod_ref[0, 5:6, :] * moe
        if final:
            x = (x * lax.rsqrt(jnp.mean(x * x, axis=-1, keepdims=True) + EPS)) * gf_ref[...]
        o_ref[r] = x


def _combine_call(y4, rank, xn, mod_g, g_final, final):
    e, b, cap, d = y4.shape
    n = xn.shape[1]
    shared_mod = mod_g.shape[0] == 1
    rb = COMBINE_REQS_CTX if shared_mod else 1
    assert b % rb == 0
    return pl.pallas_call(
        functools.partial(_combine_kernel, n=n, cap=cap, final=final, rb=rb),
        grid=(b // rb,),
        in_specs=[pl.BlockSpec((e, rb, cap, d), lambda i: (0, i, 0, 0)),
                  pl.BlockSpec((rb, n, LANES), lambda i: (i, 0, 0)),
                  pl.BlockSpec((rb, n, d), lambda i: (i, 0, 0)),
                  pl.BlockSpec((1, N_MOD, d), (lambda i: (0, 0, 0)) if shared_mod else (lambda i: (i, 0, 0))),
                  pl.BlockSpec((1, d), lambda i: (0, 0))],
        out_specs=pl.BlockSpec((rb, n, d), lambda i: (i, 0, 0)),
        out_shape=jax.ShapeDtypeStruct((b, n, d), F32),
        compiler_params=_params(1),
        name="combine",
    )(y4, rank, xn, mod_g, g_final)


def _rope_tables(n):
    t = jnp.arange(n, dtype=jnp.int32)
    row = (t // GRID_W).astype(F32)
    col = (t % GRID_W).astype(F32)
    half = HEAD_DIM // 4
    inv = jnp.power(ROPE_BASE, -jnp.arange(half, dtype=F32) / half)
    ang_r = row[:, None] * inv[None, :]
    ang_c = col[:, None] * inv[None, :]
    cos_h = jnp.concatenate([jnp.cos(ang_r)] * 2 + [jnp.cos(ang_c)] * 2, axis=-1)
    sin_h = jnp.concatenate([-jnp.sin(ang_r), jnp.sin(ang_r), -jnp.sin(ang_c), jnp.sin(ang_c)], axis=-1)
    reps = 512 // HEAD_DIM
    return jnp.tile(cos_h, (1, reps)), jnp.tile(sin_h, (1, reps))


def _nbr_bias_tables(rpb, n):
    rows = n // GRID_W
    assert rows >= NBR_K_ROWS and rows % NBR_Q_ROWS == 0
    col = np.arange(GRID_W)
    cs = np.clip(col - NA_COLS // 2, 0, GRID_W - NA_COLS)
    col_mask = (col[None, :] >= cs[:, None]) & (col[None, :] < cs[:, None] + NA_COLS)
    dc_idx = np.clip(col[None, :] - col[:, None], -(NA_COLS - 1), NA_COLS - 1) + (NA_COLS - 1)
    sel_col = (dc_idx[None, :, :] == np.arange(2 * NA_COLS - 1)[:, None, None]).astype(np.float32)
    ngrp = rows // NBR_Q_ROWS
    sel_row = np.zeros((ngrp, NBR_Q_ROWS, NBR_K_ROWS, 2 * NA_ROWS - 1), np.float32)
    for g in range(ngrp):
        ws = int(np.clip(g * NBR_Q_ROWS - NA_ROWS // 2, 0, rows - NBR_K_ROWS))
        for i in range(NBR_Q_ROWS):
            qr = g * NBR_Q_ROWS + i
            rs = int(np.clip(qr - NA_ROWS // 2, 0, rows - NA_ROWS))
            for j in range(NBR_K_ROWS):
                if rs <= ws + j < rs + NA_ROWS:
                    sel_row[g, i, j, ws + j - qr + NA_ROWS - 1] = 1.0
    hp = lax.Precision.HIGHEST
    t = jnp.einsum("lhdc,cqk->lhdqk", rpb.astype(F32), sel_col, precision=hp)
    bias = jnp.einsum("gijd,lhdqk->lghiqjk", sel_row, t, precision=hp)
    valid = (sel_row.sum(-1) > 0)[None, :, None, :, None, :, None] & col_mask[None, None, None, None, :, None, :]
    bias = jnp.where(valid, bias, NEG_INF)
    return bias.reshape(rpb.shape[0], ngrp, rpb.shape[1], NBR_Q_ROWS * GRID_W, NBR_K_ROWS * GRID_W)


def _block_diag(w):
    nb, bw, _ = w.shape
    eye = jnp.eye(nb, dtype=w.dtype)
    return (eye[:, None, :, None] * w[:, :, None, :]).reshape(nb * bw, nb * bw)


def kernel(x_prompt, x_sample, state_lru, cache_k_win, cache_v_win, cache_k_nbr, cache_v_nbr, c, c_ctx, w_mod, b_mod, g_norm1, w_in, conv_w, conv_b, w_gate_r, b_gate_r, w_gate_i, b_gate_i, lru_lambda, sink_logit, nbr_bias, w_out, g_norm2, w_router, w_exp_gate, w_exp_up, w_exp_down, g_final):
    bc, seq, d = x_prompt.shape
    bl, n_lat, _ = x_sample.shape
    depth = w_mod.shape[0]
    assert bl + 1 <= 8 and d == 1024

    cond8 = jnp.zeros((8, d), F32).at[0].set(c_ctx).at[1:1 + bl].set(c)
    mods = _adaln_call(cond8, w_mod, b_mod.reshape(depth, 1, -1)).reshape(depth, 8, N_MOD, d)
    rope_tabs = _rope_tables(n_lat)
    bias_tab = _nbr_bias_tables(nbr_bias, n_lat)
    ckw_t, cvw_t, ckn_t, cvn_t = (jnp.swapaxes(a, -1, -2) for a in (cache_k_win, cache_v_win, cache_k_nbr, cache_v_nbr))
    gf = g_final.reshape(1, d)
    assert SCALE == 0.125
    new_cols = np.arange(w_in.shape[-1]) - LRU_W
    is_q = ((new_cols >= QC0) & (new_cols < KC0)) | ((new_cols >= QB0) & (new_cols < KB0))
    q_scale = jnp.asarray(np.where(is_q, SCALE, 1.0), F32)

    xc = x_prompt.reshape(1, bc * seq, d)
    xl = x_sample
    zeros_state = jnp.zeros((bc, 1, 2, 256), F32)
    caches = None
    st = None
    for l in range(depth):
        final = l == depth - 1
        mod_c = mods[l, 0:1]
        mod_l = mods[l, 1:1 + bl]
        g1 = g_norm1[l].reshape(1, d)
        g2 = g_norm2[l].reshape(1, d)
        w_l = w_in[l]
        w_in_b = (jnp.concatenate([w_l[:, :512], w_l[:, 1152:], w_l[:, 512:1152]], axis=1) * q_scale).astype(BF16)
        wo = w_out[l].astype(BF16)
        wo_a, wo_b, wo_c = wo[:256], wo[256:640], wo[640:]
        wr_b = jnp.zeros((d, LANES), BF16).at[:, :N_EXPERTS].set(w_router[l].astype(BF16))
        w_gates = jnp.concatenate([_block_diag(w_gate_r[l, 0]), _block_diag(w_gate_i[l, 0]),
                                   _block_diag(w_gate_r[l, 1]), _block_diag(w_gate_i[l, 1])], axis=1).astype(BF16)
        b_gates = jnp.concatenate([b_gate_r[l, 0], b_gate_i[l, 0], b_gate_r[l, 1], b_gate_i[l, 1]]).reshape(1, -1)
        cb = conv_b[l].reshape(1, -1)
        sink6 = sink_logit[l].reshape(-1)

        ulru_c, qkv_c, *caches = _inproj_call(xc, mod_c, g1, w_in_b, cache_seq=seq, layer=l, depth=depth,
                                              prev_caches=caches)
        ya_c, st = _lru_call(ulru_c.reshape(bc, seq, 512), conv_w[l], cb, w_gates, b_gates, lru_lambda[l],
                             zeros_state, 0, st_layer=l, st_depth=depth, prev_state=st)
        yb_c, yc_c = _ctx_attn_call(sink6, qkv_c.reshape(bc, seq, -1))
        xn_c, h2_c, aff_c = _outproj_call(xc, ya_c.reshape(1, bc * seq, -1), yb_c.reshape(1, bc * seq, -1),
                                          yc_c.reshape(1, bc * seq, -1), mod_c, g2, wo_a, wo_b, wo_c, wr_b)
        xs_c, vals_c, rank_c = _route_call(aff_c.reshape(bc, seq, LANES), h2_c.reshape(bc, seq, d), ROUTE_REQS_CTX)

        ulru_l, qkv_l = _inproj_call(xl, mod_l, g1, w_in_b, rope_tabs=rope_tabs)
        ya_l, _ = _lru_call(ulru_l, conv_w[l], cb, w_gates, b_gates, lru_lambda[l], state_lru, l)
        yb_l = _win_attn_call(sink6, qkv_l, ckw_t, cvw_t, l)
        yc_l = _nbr_attn_call(qkv_l, bias_tab, ckn_t, cvn_t, l)
        xn_l, h2_l, aff_l = _outproj_call(xl, ya_l, yb_l, yc_l, mod_l, g2, wo_a, wo_b, wo_c, wr_b)
        xs_l, vals_l, rank_l = _route_call(aff_l, h2_l, ROUTE_REQS_LAT)

        cap_c, cap_l = xs_c.shape[2], xs_l.shape[2]
        y_c, y_l = _expert_call(xs_c.reshape(N_EXPERTS, bc * cap_c, d), xs_l.reshape(N_EXPERTS, bl * cap_l, d),
                                vals_c.reshape(N_EXPERTS, bc * cap_c, 1), vals_l.reshape(N_EXPERTS, bl * cap_l, 1),
                                w_exp_gate, w_exp_up, w_exp_down, l)
        xc = _combine_call(y_c.reshape(N_EXPERTS, bc, cap_c, d), rank_c, xn_c.reshape(bc, seq, d), mod_c, gf, final)
        xl = _combine_call(y_l.reshape(N_EXPERTS, bl, cap_l, d), rank_l, xn_l, mod_l, gf, final)
        xc = xc.reshape(1, bc * seq, d)

    y_prompt = xc.reshape(bc, seq, d)
    y_sample = xl
    return (y_prompt, y_sample, st, *[jnp.swapaxes(buf, -1, -2) for buf in caches])
```

```python
import functools

import numpy as np
import jax
import jax.numpy as jnp
from jax import lax
from jax.experimental import pallas as pl
from jax.experimental.pallas import tpu as pltpu

F32 = jnp.float32
BF16 = jnp.bfloat16

HEAD_DIM = 64
GRID_W = 64
LRU_C = 8.0
CONV_W = 4
KV_B = 2
G_B = 3
H_B = KV_B * G_B
H_C = 6
WINDOW = 128
WIN_BLK = 128
NA_ROWS = 8
NA_COLS = 16
ROPE_BASE = 10000.0
N_EXPERTS = 16
EC_FACTOR = 2
N_MOD = 6
EPS = 1e-6
NEG_INF = -1e30
SCALE = HEAD_DIM ** -0.5

LANES = 128
VMEM_LIMIT_BYTES = 56 * 1024 * 1024

TOKEN_BLOCK = 512
MOD_COL_BLOCK = 1536
EXPERT_F_STEPS = 2
EXPERT_ROW_CHUNK = 256
LRU_W = 512
QC0, KC0, VC0, QB0, KB0, VB0, ATT_W = 0, 384, 768, 1152, 1536, 1664, 1792
CTX_ATTN_REQS = 2
NBR_Q_ROWS = 4
NBR_K_ROWS = 12
ROUTE_REQS_CTX = 8
ROUTE_REQS_LAT = 2
COMBINE_REQS_CTX = 4


def _params(n_axes):
    return pltpu.CompilerParams(dimension_semantics=("arbitrary",) * n_axes,
                                vmem_limit_bytes=VMEM_LIMIT_BYTES)


def _dot(a, b):
    return jnp.dot(a, b, preferred_element_type=F32)


def _dot_nt(a, b):
    return lax.dot_general(a, b, (((1,), (1,)), ((), ())), preferred_element_type=F32)


def _adaln_kernel(cond_ref, w_ref, b_ref, o_ref):
    c = cond_ref[...]
    s = c * jax.nn.sigmoid(c)
    o_ref[0] = _dot(s.astype(BF16), w_ref[0].astype(BF16)) + b_ref[0]


def _adaln_call(cond8, w_mod, b_mod3):
    depth, d, n = w_mod.shape
    nb = MOD_COL_BLOCK
    return pl.pallas_call(
        _adaln_kernel,
        grid=(depth, n // nb),
        in_specs=[pl.BlockSpec((8, d), lambda l, j: (0, 0)),
                  pl.BlockSpec((1, d, nb), lambda l, j: (l, 0, j)),
                  pl.BlockSpec((1, 1, nb), lambda l, j: (l, 0, j))],
        out_specs=pl.BlockSpec((1, 8, nb), lambda l, j: (l, 0, j)),
        out_shape=jax.ShapeDtypeStruct((depth, 8, n), F32),
        compiler_params=_params(2),
        name="adaln",
    )(cond8, w_mod, b_mod3)


def _rms_modulate(x, gain, shift, scale):
    y = x * lax.rsqrt(jnp.mean(x * x, axis=-1, keepdims=True) + EPS)
    return (y * gain) * (1.0 + scale) + shift


def _inproj_kernel(*refs, rope, cache, seq):
    if rope:
        x_ref, mod_ref, g_ref, w_ref, cos_ref, sin_ref, ulru_ref, qkv_ref = refs
    elif cache:
        x_ref, mod_ref, g_ref, w_ref = refs[:4]
        ulru_ref, qkv_ref, kw_ref, vw_ref, kn_ref, vn_ref = refs[-6:]
    else:
        x_ref, mod_ref, g_ref, w_ref, ulru_ref, qkv_ref = refs
    x = x_ref[0]
    h = _rms_modulate(x, g_ref[...], mod_ref[0, 0:1, :], mod_ref[0, 1:2, :])
    u = _dot(h.astype(BF16), w_ref[...])
    tb = u.shape[0]
    ulru_ref[0] = u[:, :LRU_W]
    if rope:
        lo, hi = LRU_W + QB0, LRU_W + VB0
        qk = u[:, lo:hi]
        lane = lax.broadcasted_iota(jnp.int32, (tb, hi - lo), 1)
        first = (lane & 31) < 16
        partner = jnp.where(first, pltpu.roll(qk, hi - lo - 16, 1), pltpu.roll(qk, 16, 1))
        qk = qk * cos_ref[...] + partner * sin_ref[...]
        qkv_ref[0, :, :QB0] = u[:, LRU_W:lo].astype(BF16)
        qkv_ref[0, :, QB0:VB0] = qk.astype(BF16)
        qkv_ref[0, :, VB0:] = u[:, hi:].astype(BF16)
    else:
        qkv_ref[0] = u[:, LRU_W:].astype(BF16)
    if cache:
        for r in range(tb // seq):
            rows = slice(r * seq, (r + 1) * seq)
            for ref, c0, heads in ((kw_ref, KB0, KV_B), (vw_ref, VB0, KV_B), (kn_ref, KC0, H_C), (vn_ref, VC0, H_C)):
                for p in range(heads // 2):
                    pair_t = u[rows, LRU_W + c0 + LANES * p: LRU_W + c0 + LANES * (p + 1)].T
                    ref[r, 0, 2 * p] = pair_t[:HEAD_DIM]
                    ref[r, 0, 2 * p + 1] = pair_t[HEAD_DIM:]


def _inproj_call(x3, mod_g, gain, w_in_b, rope_tabs=None, cache_seq=None, layer=0, depth=1, prev_caches=None):
    g, n, d = x3.shape
    d_in = w_in_b.shape[1]
    tb = TOKEN_BLOCK
    rope = rope_tabs is not None
    cache = cache_seq is not None
    in_specs = [pl.BlockSpec((1, tb, d), lambda i, j: (i, j, 0)),
                pl.BlockSpec((1, N_MOD, d), lambda i, j: (i, 0, 0)),
                pl.BlockSpec((1, d), lambda i, j: (0, 0)),
                pl.BlockSpec((d, d_in), lambda i, j: (0, 0))]
    args = [x3, mod_g, gain, w_in_b]
    assert d_in == LRU_W + ATT_W
    out_shape = [jax.ShapeDtypeStruct((g, n, LRU_W), F32),
                 jax.ShapeDtypeStruct((g, n, ATT_W), BF16)]
    out_specs = [pl.BlockSpec((1, tb, LRU_W), lambda i, j: (i, j, 0)),
                 pl.BlockSpec((1, tb, ATT_W), lambda i, j: (i, j, 0))]
    if rope:
        in_specs += [pl.BlockSpec((tb, VB0 - QB0), lambda i, j: (j, 0))] * 2
        args += list(rope_tabs)
    aliases = {}
    if cache:
        assert g == 1 and tb % cache_seq == 0
        rb = tb // cache_seq
        nreq = n // cache_seq
        for heads in (KV_B, KV_B, H_C, H_C):
            out_shape.append(jax.ShapeDtypeStruct((nreq, depth, heads, HEAD_DIM, cache_seq), F32))
            out_specs.append(pl.BlockSpec((rb, 1, heads, HEAD_DIM, cache_seq), lambda i, j: (j, layer, 0, 0, 0)))
        if prev_caches is not None:
            for k, buf in enumerate(prev_caches):
                aliases[len(args)] = 2 + k
                in_specs.append(pl.BlockSpec(memory_space=pl.ANY))
                args.append(buf)
    return pl.pallas_call(
        functools.partial(_inproj_kernel, rope=rope, cache=cache, seq=cache_seq),
        grid=(g, n // tb),
        in_specs=in_specs,
        out_specs=out_specs,
        out_shape=out_shape,
        input_output_aliases=aliases,
        compiler_params=_params(2),
        name="inproj_rope" if rope else "inproj_ctx",
    )(*args)


def _lru_kernel(*refs, n):
    u_ref, cw_ref, cb_ref, wg_ref, bg_ref, lam_ref, h0_ref = refs[:7]
    ya_ref, st_ref, af_s, xf_s, ab_s, xb_s, yf_s, yb_s = refs[-8:]
    c = 256
    nblk = n // 8
    u = u_ref[0]
    xa = u[:, :c]
    ga = u[:, c:]
    t = lax.broadcasted_iota(jnp.int32, (n, c), 0)
    cw = cw_ref[...]
    xc = cw[2:3] * xa + cb_ref[...]
    xc = xc + cw[0:1] * jnp.where(t >= 2, pltpu.roll(xa, 2, 0), 0.0)
    xc = xc + cw[1:2] * jnp.where(t >= 1, pltpu.roll(xa, 1, 0), 0.0)
    xc = xc + cw[3:4] * jnp.where(t < n - 1, pltpu.roll(xa, n - 1, 0), 0.0)
    gates = _dot(xc.astype(BF16), wg_ref[...]) + bg_ref[...]
    lam = lam_ref[...]
    log_sig = jnp.minimum(lam, 0.0) - jnp.log1p(jnp.exp(-jnp.abs(lam)))
    sub = lax.broadcasted_iota(jnp.int32, (nblk, 8, c), 1)
    for d, (a_s, x_s) in enumerate(((af_s, xf_s), (ab_s, xb_s))):
        r = jax.nn.sigmoid(gates[:, 2 * d * c:(2 * d + 1) * c])
        i = jax.nn.sigmoid(gates[:, (2 * d + 1) * c:(2 * d + 2) * c])
        log_a = LRU_C * r * log_sig[d:d + 1]
        a = jnp.exp(log_a)
        t_in = jnp.tanh(-log_a) * (1.0 + a * a)
        xin = jnp.where(t_in > 0.0, t_in * lax.rsqrt(t_in), 0.0) * (i * xc)
        a3 = a.reshape(nblk, 8, c)
        x3 = xin.reshape(nblk, 8, c)
        for s in (1, 2, 4):
            if d == 0:
                m = sub >= s
                sh = s
            else:
                m = sub < 8 - s
                sh = 8 - s
            x3 = jnp.where(m, a3 * pltpu.roll(x3, sh, 1) + x3, x3)
            a3 = jnp.where(m, a3 * pltpu.roll(a3, sh, 1), a3)
        a_s[...] = a3
        x_s[...] = x3

    h0 = h0_ref[0, 0]

    def body(k, carry):
        hf, hb = carry
        blk_f = af_s[k] * hf + xf_s[k]
        yf_s[k] = blk_f
        kb = nblk - 1 - k
        blk_b = ab_s[kb] * hb + xb_s[kb]
        yb_s[kb] = blk_b
        return blk_f[7:8], blk_b[0:1]

    hf, hb = lax.fori_loop(0, nblk, body, (h0[0:1], h0[1:2]))
    y = (yf_s[...] + yb_s[...]).reshape(n, c) * jax.nn.gelu(ga)
    ya_ref[0] = y.astype(BF16)
    st_ref[0, 0, 0:1, :] = hf
    st_ref[0, 0, 1:2, :] = hb


def _lru_call(ulru, conv_w, conv_b, w_gates_b, b_gates, lam, h0, h0_layer, st_layer=0, st_depth=1, prev_state=None):
    b, n, _ = ulru.shape
    c = 256
    full = lambda shape: pl.BlockSpec(shape, lambda i: (0,) * len(shape))
    scratch = [pltpu.VMEM((n // 8, 8, c), F32) for _ in range(6)]
    in_specs = [pl.BlockSpec((1, n, 2 * c), lambda i: (i, 0, 0)),
                full((CONV_W, c)), full((1, c)), full((c, 4 * c)), full((1, 4 * c)), full((2, c)),
                pl.BlockSpec((1, 1, 2, c), lambda i: (i, h0_layer, 0, 0))]
    args = [ulru, conv_w, conv_b, w_gates_b, b_gates, lam, h0]
    aliases = {}
    if prev_state is not None:
        aliases[len(args)] = 1
        in_specs.append(pl.BlockSpec(memory_space=pl.ANY))
        args.append(prev_state)
    return pl.pallas_call(
        functools.partial(_lru_kernel, n=n),
        grid=(b,),
        in_specs=in_specs,
        out_specs=[pl.BlockSpec((1, n, c), lambda i: (i, 0, 0)),
                   pl.BlockSpec((1, 1, 2, c), lambda i: (i, st_layer, 0, 0))],
        out_shape=[jax.ShapeDtypeStruct((b, n, c), BF16),
                   jax.ShapeDtypeStruct((b, st_depth, 2, c), F32)],
        scratch_shapes=scratch,
        input_output_aliases=aliases,
        compiler_params=_params(1),
        name="rglru",
    )(*args)


def _softmax_pv(parts, sink):
    mx = None
    for s, _, _ in parts:
        cur = jnp.max(s, axis=-1, keepdims=True)
        mx = cur if mx is None else jnp.maximum(mx, cur)
    if sink is not None:
        mx = jnp.maximum(mx, sink)
    den = None
    out = None
    for s, v, v_t in parts:
        p = jnp.exp(s - mx)
        cur = jnp.sum(p, axis=-1, keepdims=True)
        den = cur if den is None else den + cur
        o = _dot_nt(p.astype(BF16), v) if v_t else _dot(p.astype(BF16), v)
        out = o if out is None else out + o
    if sink is not None:
        den = den + jnp.exp(sink - mx)
    return out / den


def _swap_halves(x):
    return jnp.concatenate([x[:, HEAD_DIM:], x[:, :HEAD_DIM]], axis=1)


def _keep_half(x, lo_mask, half):
    zero = jnp.zeros_like(x)
    return jnp.where(lo_mask, x, zero) if half == 0 else jnp.where(lo_mask, zero, x)


def _ctx_attn_kernel(sink_ref, att_ref, ob_ref, oc_ref, *, n, rb):
    lo = lax.broadcasted_iota(jnp.int32, (n, LANES), 1) < HEAD_DIM
    for r in range(rb):
        for p in range(H_C // 2):
            qp = att_ref[r, :, QC0 + LANES * p: QC0 + LANES * (p + 1)]
            kp = att_ref[r, :, KC0 + LANES * p: KC0 + LANES * (p + 1)]
            vp = att_ref[r, :, VC0 + LANES * p: VC0 + LANES * (p + 1)]
            outs = [_softmax_pv([(_dot_nt(qp, _keep_half(kp, lo, half)), vp, False)], None) for half in range(2)]
            oc_ref[r, :, LANES * p: LANES * (p + 1)] = jnp.where(lo, outs[0], outs[1]).astype(BF16)
        kpair = att_ref[r, :, KB0: KB0 + LANES]
        vpair = att_ref[r, :, VB0: VB0 + LANES]
        kpair_sw = _swap_halves(kpair)
        vpair_sw = _swap_halves(vpair)
        for p in range(H_B // 2):
            qp = att_ref[r, :, QB0 + LANES * p: QB0 + LANES * (p + 1)]
            outs = []
            for half in range(2):
                h = 2 * p + half
                aligned = (h // G_B) == half
                ksrc, vsrc = (kpair, vpair) if aligned else (kpair_sw, vpair_sw)
                outs.append(_softmax_pv([(_dot_nt(qp, _keep_half(ksrc, lo, half)), vsrc, False)], sink_ref[h]))
            ob_ref[r, :, LANES * p: LANES * (p + 1)] = jnp.where(lo, outs[0], outs[1]).astype(BF16)


def _ctx_attn_call(sink6, att):
    b, n, w = att.shape
    rb = CTX_ATTN_REQS
    assert b % rb == 0 and KV_B == 2
    return pl.pallas_call(
        functools.partial(_ctx_attn_kernel, n=n, rb=rb),
        grid=(b // rb,),
        in_specs=[pl.BlockSpec(memory_space=pltpu.SMEM),
                  pl.BlockSpec((rb, n, w), lambda i: (i, 0, 0))],
        out_specs=[pl.BlockSpec((rb, n, 384), lambda i: (i, 0, 0))] * 2,
        out_shape=[jax.ShapeDtypeStruct((b, n, 384), BF16)] * 2,
        compiler_params=_params(1),
        name="ctx_attn",
    )(sink6, att)


def _win_attn_kernel(sink_ref, qkv_ref, kc_ref, vc_ref, o_ref, *, n):
    j = pl.program_id(1)
    nloc = 3 * WIN_BLK
    q0 = pl.multiple_of(j * WIN_BLK, WIN_BLK)
    ks = pl.multiple_of(jnp.clip((j - 1) * WIN_BLK, 0, n - nloc), WIN_BLK)
    row = lax.broadcasted_iota(jnp.int32, (G_B * WIN_BLK, nloc), 0)
    col = lax.broadcasted_iota(jnp.int32, (G_B * WIN_BLK, nloc), 1)
    qpos = q0 + (row & (WIN_BLK - 1))
    kpos = ks + col
    mask = jnp.abs(qpos - kpos) <= WINDOW
    rowh = lax.broadcasted_iota(jnp.int32, (G_B * WIN_BLK, 1), 0) // WIN_BLK
    for kv in range(KV_B):
        gw = G_B * HEAD_DIM
        qblk = qkv_ref[0, pl.ds(q0, WIN_BLK), QB0 + gw * kv: QB0 + gw * (kv + 1)]
        q3 = jnp.concatenate([qblk[:, 64 * g: 64 * (g + 1)] for g in range(G_B)], axis=0)
        kl = qkv_ref[0, pl.ds(ks, nloc), KB0 + 64 * kv: KB0 + 64 * (kv + 1)]
        vl = qkv_ref[0, pl.ds(ks, nloc), VB0 + 64 * kv: VB0 + 64 * (kv + 1)]
        kc_t = kc_ref[0, 0, kv].astype(BF16)
        vc_t = vc_ref[0, 0, kv].astype(BF16)
        s_loc = jnp.where(mask, _dot_nt(q3, kl), NEG_INF)
        s_ctx = _dot(q3, kc_t)
        sk = jnp.where(rowh == 0, sink_ref[3 * kv],
                       jnp.where(rowh == 1, sink_ref[3 * kv + 1], sink_ref[3 * kv + 2]))
        o = _softmax_pv([(s_loc, vl, False), (s_ctx, vc_t, True)], sk)
        for g in range(G_B):
            h = kv * G_B + g
            o_ref[0, :, 64 * h: 64 * (h + 1)] = o[g * WIN_BLK:(g + 1) * WIN_BLK].astype(BF16)


def _win_attn_call(sink6, qkv, cache_k, cache_v, layer):
    b, n, w = qkv.shape
    past = cache_k.shape[4]
    cache_spec = pl.BlockSpec((1, 1, KV_B, HEAD_DIM, past), lambda i, j: (i, layer, 0, 0, 0))
    return pl.pallas_call(
        functools.partial(_win_attn_kernel, n=n),
        grid=(b, n // WIN_BLK),
        in_specs=[pl.BlockSpec(memory_space=pltpu.SMEM),
                  pl.BlockSpec((1, n, w), lambda i, j: (i, 0, 0)),
                  cache_spec, cache_spec],
        out_specs=pl.BlockSpec((1, WIN_BLK, 384), lambda i, j: (i, j, 0)),
        out_shape=jax.ShapeDtypeStruct((b, n, 384), BF16),
        compiler_params=_params(2),
        name="win_attn",
    )(sink6, qkv, cache_k, cache_v)


def _nbr_window_start(g, rows):
    return jnp.clip(g * NBR_Q_ROWS - NA_ROWS // 2, 0, rows - NBR_K_ROWS)


def _nbr_attn_kernel(q_ref, k_ref, v_ref, rel_ref, kc_ref, vc_ref, o_ref, bias_s, *, n):
    g = pl.program_id(0)
    nq = NBR_Q_ROWS * GRID_W
    nk = NBR_K_ROWS * GRID_W
    rows = n // GRID_W
    ws = _nbr_window_start(g, rows)
    k0 = pl.multiple_of(ws * GRID_W, GRID_W)

    @pl.when(pl.program_id(1) == 0)
    def _():
        for i in range(NBR_Q_ROWS):
            qr = g * NBR_Q_ROWS + i
            rs = jnp.clip(qr - NA_ROWS // 2, 0, rows - NA_ROWS)
            for jp in range(NBR_K_ROWS // 2):
                tiles = []
                for j in (2 * jp, 2 * jp + 1):
                    kr = ws + j
                    valid = (kr >= rs) & (kr < rs + NA_ROWS)
                    d = jnp.clip(kr - qr + NA_ROWS - 1, 0, 2 * NA_ROWS - 2)
                    tiles.append(jnp.where(valid, rel_ref[0, :, pl.ds(d, 1)][:, 0], NEG_INF))
                bias_s[:, GRID_W * i: GRID_W * (i + 1), LANES * jp: LANES * (jp + 1)] = jnp.concatenate(tiles, axis=-1)
    lo_k = lax.broadcasted_iota(jnp.int32, (nk, LANES), 1) < HEAD_DIM
    lo_q = lax.broadcasted_iota(jnp.int32, (nq, LANES), 1) < HEAD_DIM
    zpad = jnp.zeros((HEAD_DIM, kc_ref.shape[4]), BF16)
    for p in range(H_C // 2):
        qp = q_ref[0, :, LANES * p: LANES * (p + 1)]
        kp = k_ref[0, pl.ds(k0, nk), LANES * p: LANES * (p + 1)]
        vp = v_ref[0, pl.ds(k0, nk), LANES * p: LANES * (p + 1)]
        outs = []
        for half in range(2):
            h = 2 * p + half
            kc_t = kc_ref[0, 0, h].astype(BF16)
            vc_t = vc_ref[0, 0, h].astype(BF16)
            kc_t = jnp.concatenate([kc_t, zpad] if half == 0 else [zpad, kc_t], axis=0)
            vc_t = jnp.concatenate([vc_t, zpad] if half == 0 else [zpad, vc_t], axis=0)
            s_loc = _dot_nt(qp, _keep_half(kp, lo_k, half)) + bias_s[h]
            s_ctx = _dot(qp, kc_t)
            outs.append(_softmax_pv([(s_loc, vp, False), (s_ctx, vc_t, True)], None))
        o_ref[0, :, LANES * p: LANES * (p + 1)] = jnp.where(lo_q, outs[0], outs[1]).astype(BF16)


def _nbr_attn_call(att, rel_tab, cache_k, cache_v, layer):
    b, n, w = att.shape
    past = cache_k.shape[4]
    nq = NBR_Q_ROWS * GRID_W
    nk = NBR_K_ROWS * GRID_W
    gw = H_C * HEAD_DIM
    rows = n // GRID_W
    assert (QC0, KC0, VC0) == (0, gw, 2 * gw) and n % nq == 0 and rows >= NBR_K_ROWS and NBR_K_ROWS % 2 == 0
    cache_spec = pl.BlockSpec((1, 1, H_C, HEAD_DIM, past), lambda g, i: (i, layer, 0, 0, 0))
    return pl.pallas_call(
        functools.partial(_nbr_attn_kernel, n=n),
        grid=(n // nq, b),
        in_specs=[pl.BlockSpec((1, nq, gw), lambda g, i: (i, g, 0)),
                  pl.BlockSpec((1, n, gw), lambda g, i: (i, 0, 1)),
                  pl.BlockSpec((1, n, gw), lambda g, i: (i, 0, 2)),
                  pl.BlockSpec((1,) + rel_tab.shape[1:], lambda g, i: (layer, 0, 0, 0, 0)),
                  cache_spec, cache_spec],
        out_specs=pl.BlockSpec((1, nq, gw), lambda g, i: (i, g, 0)),
        out_shape=jax.ShapeDtypeStruct((b, n, gw), BF16),
        scratch_shapes=[pltpu.VMEM((H_C, nq, nk), F32)],
        compiler_params=_params(2),
        name="nbr_attn",
    )(att, att, att, rel_tab, cache_k, cache_v)


def _outproj_kernel(x_ref, ya_ref, yb_ref, yc_ref, mod_ref, g_ref, wo_ref, wr_ref,
                    xn_ref, h2_ref, aff_ref):
    mix = jnp.concatenate([ya_ref[0], yb_ref[0], yc_ref[0]], axis=1)
    proj = _dot(mix, wo_ref[...])
    xn = x_ref[0] + mod_ref[0, 2:3, :] * proj
    xn_ref[0] = xn
    h2 = _rms_modulate(xn, g_ref[...], mod_ref[0, 3:4, :], mod_ref[0, 4:5, :]).astype(BF16)
    h2_ref[0] = h2
    logits = _dot(h2, wr_ref[...])
    lane = lax.broadcasted_iota(jnp.int32, logits.shape, 1)
    valid = lane < N_EXPERTS
    logits = jnp.where(valid, logits, NEG_INF)
    e = jnp.exp(logits - jnp.max(logits, axis=-1, keepdims=True))
    e = jnp.where(valid, e, 0.0)
    aff_ref[0] = e / jnp.sum(e, axis=-1, keepdims=True)


def _outproj_call(x3, ya, yb, yc, mod_g, gain, wo_b16, w_router_b):
    g, n, d = x3.shape
    tb = TOKEN_BLOCK
    tok = lambda w: pl.BlockSpec((1, tb, w), lambda i, j: (i, j, 0))
    full = lambda shape: pl.BlockSpec(shape, lambda i, j: (0,) * len(shape))
    return pl.pallas_call(
        _outproj_kernel,
        grid=(g, n // tb),
        in_specs=[tok(d), tok(256), tok(384), tok(384),
                  pl.BlockSpec((1, N_MOD, d), lambda i, j: (i, 0, 0)),
                  full((1, d)), full((256 + 384 + 384, d)), full((d, LANES))],
        out_specs=[tok(d), tok(d), tok(LANES)],
        out_shape=[jax.ShapeDtypeStruct((g, n, d), F32),
                   jax.ShapeDtypeStruct((g, n, d), BF16),
                   jax.ShapeDtypeStruct((g, n, LANES), F32)],
        compiler_params=_params(2),
        name="outproj_router",
    )(x3, ya, yb, yc, mod_g, gain, wo_b16, w_router_b)


def _route_kernel(aff_ref, h2_ref, xs_ref, vals_ref, rank_ref, *, n, cap, rb):
    ne = N_EXPERTS
    nt = n // LANES
    fcap = float(cap)
    jj = lax.broadcasted_iota(jnp.int32, (LANES, LANES), 0)
    ii = lax.broadcasted_iota(jnp.int32, (LANES, LANES), 1)
    earlier = jnp.where(jj < ii, 1.0, 0.0)
    slot = lax.broadcasted_iota(jnp.int32, (cap, n), 0).astype(F32)
    pad = jnp.full((LANES - ne, n), fcap, F32)
    for r in range(rb):
        aff = aff_ref[r]
        aff_t = aff.T[:ne]
        rank_rows = []
        for e in range(ne):
            cols = [jnp.broadcast_to(aff[jb * LANES:(jb + 1) * LANES, e:e + 1], (LANES, LANES)) for jb in range(nt)]
            pieces = []
            for ib in range(nt):
                a_row = aff_t[e:e + 1, ib * LANES:(ib + 1) * LANES]
                acc = None
                for jb in range(nt):
                    if jb < ib:
                        blk = jnp.where(cols[jb] >= a_row, 1.0, 0.0)
                    elif jb > ib:
                        blk = jnp.where(cols[jb] > a_row, 1.0, 0.0)
                    else:
                        blk = jnp.where(cols[jb] > a_row, 1.0, jnp.where(cols[jb] == a_row, earlier, 0.0))
                    acc = blk if acc is None else acc + blk
                pieces.append(jnp.sum(acc, axis=0, keepdims=True))
            rank_rows.append(jnp.concatenate(pieces, axis=1) if nt > 1 else pieces[0])
        rank_r = jnp.minimum(jnp.concatenate(rank_rows, axis=0), fcap)
        rank_ref[r] = jnp.concatenate([rank_r, pad], axis=0).T
        picks = []
        for e in range(ne):
            onehot = jnp.where(rank_r[e:e + 1] == slot, 1.0, 0.0)
            vals_ref[e, r] = jnp.sum(onehot * aff_t[e:e + 1], axis=1, keepdims=True)
            picks.append(onehot.astype(BF16))
        xs = _dot(jnp.concatenate(picks, axis=0), h2_ref[r])
        xs_ref[:, r] = xs.reshape(ne, cap, xs.shape[-1]).astype(BF16)


def _route_call(aff, h2, rb):
    b, n, d = h2.shape
    cap = max(1, EC_FACTOR * n // N_EXPERTS)
    assert b % rb == 0
    return pl.pallas_call(
        functools.partial(_route_kernel, n=n, cap=cap, rb=rb),
        grid=(b // rb,),
        in_specs=[pl.BlockSpec((rb, n, LANES), lambda i: (i, 0, 0)),
                  pl.BlockSpec((rb, n, d), lambda i: (i, 0, 0))],
        out_specs=[pl.BlockSpec((N_EXPERTS, rb, cap, d), lambda i: (0, i, 0, 0)),
                   pl.BlockSpec((N_EXPERTS, rb, cap, 1), lambda i: (0, i, 0, 0)),
                   pl.BlockSpec((rb, n, LANES), lambda i: (i, 0, 0))],
        out_shape=[jax.ShapeDtypeStruct((N_EXPERTS, b, cap, d), BF16),
                   jax.ShapeDtypeStruct((N_EXPERTS, b, cap, 1), F32),
                   jax.ShapeDtypeStruct((b, n, LANES), F32)],
        compiler_params=_params(1),
        name="route_gather",
    )(aff, h2)


def _expert_kernel(xc_ref, xl_ref, vc_ref, vl_ref, wg_ref, wu_ref, wd_ref, yc_ref, yl_ref, acc_c, acc_l):
    f = pl.program_id(1)

    def step(first):
        wg = wg_ref[0, 0].astype(BF16)
        wu = wu_ref[0, 0].astype(BF16)
        wd = wd_ref[0, 0].astype(BF16)
        for x_ref, v_ref, y_ref, acc in ((xc_ref, vc_ref, yc_ref, acc_c), (xl_ref, vl_ref, yl_ref, acc_l)):
            rows = x_ref.shape[1]
            chunk = min(EXPERT_ROW_CHUNK, rows)
            for c0 in range(0, rows, chunk):
                sl = slice(c0, c0 + chunk)
                x = x_ref[0, sl, :]
                a = _dot(x, wg)
                u = _dot(x, wu)
                act = ((a * jax.nn.sigmoid(a)) * u).astype(BF16)
                part = _dot(act, wd)
                if first:
                    acc[sl, :] = part
                else:
                    y_ref[0, sl, :] = ((acc[sl, :] + part) * v_ref[0, sl, :]).astype(BF16)

    pl.when(f == 0)(functools.partial(step, True))
    pl.when(f == EXPERT_F_STEPS - 1)(functools.partial(step, False))


def _expert_call(xs_c, xs_l, vals_c, vals_l, w_gate, w_up, w_down, layer):
    e, rc, d = xs_c.shape
    rl = xs_l.shape[1]
    f_total = w_gate.shape[-1]
    assert EXPERT_F_STEPS == 2
    fb = f_total // EXPERT_F_STEPS
    per_e = lambda rows, w: pl.BlockSpec((1, rows, w), lambda i, j: (i, 0, 0))
    return pl.pallas_call(
        _expert_kernel,
        grid=(e, EXPERT_F_STEPS),
        in_specs=[per_e(rc, d), per_e(rl, d), per_e(rc, 1), per_e(rl, 1),
                  pl.BlockSpec((1, 1, d, fb), lambda i, j: (layer, i, 0, j)),
                  pl.BlockSpec((1, 1, d, fb), lambda i, j: (layer, i, 0, j)),
                  pl.BlockSpec((1, 1, fb, d), lambda i, j: (layer, i, j, 0))],
        out_specs=[per_e(rc, d), per_e(rl, d)],
        out_shape=[jax.ShapeDtypeStruct((e, rc, d), BF16),
                   jax.ShapeDtypeStruct((e, rl, d), BF16)],
        scratch_shapes=[pltpu.VMEM((rc, d), F32), pltpu.VMEM((rl, d), F32)],
        compiler_params=_params(2),
        name="experts",
    )(xs_c, xs_l, vals_c, vals_l, w_gate, w_up, w_down)


def _combine_kernel(y_ref, rank_ref, xn_ref, mod_ref, gf_ref, o_ref, *, n, cap, final, rb):
    ec = N_EXPERTS * cap
    d = xn_ref.shape[-1]
    ce = lax.broadcasted_iota(jnp.int32, (LANES, ec), 1) // cap
    ee = lax.broadcasted_iota(jnp.int32, (LANES, ec), 0)
    expand = jnp.where(ce == ee, 1.0, 0.0).astype(BF16)
    slot = (lax.broadcasted_iota(jnp.int32, (n, ec), 1) % cap).astype(F32)
    for r in range(rb):
        rank = rank_ref[r].astype(BF16)
        rexp = _dot(rank, expand)
        onehot = jnp.where(rexp == slot, 1.0, 0.0).astype(BF16)
        moe = _dot(onehot, y_ref[:, r].reshape(ec, d))
        x = xn_ref[r] + mod_ref[0, 5:6, :] * moe
        if final:
            x = (x * lax.rsqrt(jnp.mean(x * x, axis=-1, keepdims=True) + EPS)) * gf_ref[...]
        o_ref[r] = x


def _combine_call(y4, rank, xn, mod_g, g_final, final):
    e, b, cap, d = y4.shape
    n = xn.shape[1]
    shared_mod = mod_g.shape[0] == 1
    rb = COMBINE_REQS_CTX if shared_mod else 1
    assert b % rb == 0
    return pl.pallas_call(
        functools.partial(_combine_kernel, n=n, cap=cap, final=final, rb=rb),
        grid=(b // rb,),
        in_specs=[pl.BlockSpec((e, rb, cap, d), lambda i: (0, i, 0, 0)),
                  pl.BlockSpec((rb, n, LANES), lambda i: (i, 0, 0)),
                  pl.BlockSpec((rb, n, d), lambda i: (i, 0, 0)),
                  pl.BlockSpec((1, N_MOD, d), (lambda i: (0, 0, 0)) if shared_mod else (lambda i: (i, 0, 0))),
                  pl.BlockSpec((1, d), lambda i: (0, 0))],
        out_specs=pl.BlockSpec((rb, n, d), lambda i: (i, 0, 0)),
        out_shape=jax.ShapeDtypeStruct((b, n, d), F32),
        compiler_params=_params(1),
        name="combine",
    )(y4, rank, xn, mod_g, g_final)


def _rope_tables(n):
    t = jnp.arange(n, dtype=jnp.int32)
    row = (t // GRID_W).astype(F32)
    col = (t % GRID_W).astype(F32)
    half = HEAD_DIM // 4
    inv = jnp.power(ROPE_BASE, -jnp.arange(half, dtype=F32) / half)
    ang_r = row[:, None] * inv[None, :]
    ang_c = col[:, None] * inv[None, :]
    cos_h = jnp.concatenate([jnp.cos(ang_r)] * 2 + [jnp.cos(ang_c)] * 2, axis=-1)
    sin_h = jnp.concatenate([-jnp.sin(ang_r), jnp.sin(ang_r), -jnp.sin(ang_c), jnp.sin(ang_c)], axis=-1)
    reps = 512 // HEAD_DIM
    return jnp.tile(cos_h, (1, reps)), jnp.tile(sin_h, (1, reps))


def _nbr_rel_tables(rpb):
    col = np.arange(GRID_W)
    cs = np.clip(col - NA_COLS // 2, 0, GRID_W - NA_COLS)
    col_mask = (col[None, :] >= cs[:, None]) & (col[None, :] < cs[:, None] + NA_COLS)
    dc_idx = np.clip(col[None, :] - col[:, None], -(NA_COLS - 1), NA_COLS - 1) + (NA_COLS - 1)
    sel_col = (dc_idx[None, :, :] == np.arange(2 * NA_COLS - 1)[:, None, None]).astype(np.float32)
    t = jnp.einsum("lhdc,cqk->lhdqk", rpb.astype(F32), sel_col, precision=lax.Precision.HIGHEST)
    return jnp.where(col_mask, t, NEG_INF)


def _block_diag(w):
    nb, bw, _ = w.shape
    eye = jnp.eye(nb, dtype=w.dtype)
    return (eye[:, None, :, None] * w[:, :, None, :]).reshape(nb * bw, nb * bw)


def kernel(x_prompt, x_sample, state_lru, cache_k_win, cache_v_win, cache_k_nbr, cache_v_nbr, c, c_ctx, w_mod, b_mod, g_norm1, w_in, conv_w, conv_b, w_gate_r, b_gate_r, w_gate_i, b_gate_i, lru_lambda, sink_logit, nbr_bias, w_out, g_norm2, w_router, w_exp_gate, w_exp_up, w_exp_down, g_final):
    bc, seq, d = x_prompt.shape
    bl, n_lat, _ = x_sample.shape
    depth = w_mod.shape[0]
    assert bl + 1 <= 8 and d == 1024

    cond8 = jnp.zeros((8, d), F32).at[0].set(c_ctx).at[1:1 + bl].set(c)
    mods = _adaln_call(cond8, w_mod, b_mod.reshape(depth, 1, -1)).reshape(depth, 8, N_MOD, d)
    rope_tabs = _rope_tables(n_lat)
    bias_tab = _nbr_rel_tables(nbr_bias)
    ckw_t, cvw_t, ckn_t, cvn_t = (jnp.swapaxes(a, -1, -2) for a in (cache_k_win, cache_v_win, cache_k_nbr, cache_v_nbr))
    gf = g_final.reshape(1, d)
    assert SCALE == 0.125
    new_cols = np.arange(w_in.shape[-1]) - LRU_W
    is_q = ((new_cols >= QC0) & (new_cols < KC0)) | ((new_cols >= QB0) & (new_cols < KB0))
    q_scale = jnp.asarray(np.where(is_q, SCALE, 1.0), F32)

    xc = x_prompt.reshape(1, bc * seq, d)
    xl = x_sample
    zeros_state = jnp.zeros((bc, 1, 2, 256), F32)
    caches = None
    st = None
    for l in range(depth):
        final = l == depth - 1
        mod_c = mods[l, 0:1]
        mod_l = mods[l, 1:1 + bl]
        g1 = g_norm1[l].reshape(1, d)
        g2 = g_norm2[l].reshape(1, d)
        w_l = w_in[l]
        w_in_b = (jnp.concatenate([w_l[:, :512], w_l[:, 1152:], w_l[:, 512:1152]], axis=1) * q_scale).astype(BF16)
        wo = w_out[l].astype(BF16)
        wr_b = jnp.zeros((d, LANES), BF16).at[:, :N_EXPERTS].set(w_router[l].astype(BF16))
        w_gates = jnp.concatenate([_block_diag(w_gate_r[l, 0]), _block_diag(w_gate_i[l, 0]),
                                   _block_diag(w_gate_r[l, 1]), _block_diag(w_gate_i[l, 1])], axis=1).astype(BF16)
        b_gates = jnp.concatenate([b_gate_r[l, 0], b_gate_i[l, 0], b_gate_r[l, 1], b_gate_i[l, 1]]).reshape(1, -1)
        cb = conv_b[l].reshape(1, -1)
        sink6 = sink_logit[l].reshape(-1)

        ulru_c, qkv_c, *caches = _inproj_call(xc, mod_c, g1, w_in_b, cache_seq=seq, layer=l, depth=depth,
                                              prev_caches=caches)
        ya_c, st = _lru_call(ulru_c.reshape(bc, seq, 512), conv_w[l], cb, w_gates, b_gates, lru_lambda[l],
                             zeros_state, 0, st_layer=l, st_depth=depth, prev_state=st)
        yb_c, yc_c = _ctx_attn_call(sink6, qkv_c.reshape(bc, seq, -1))
        xn_c, h2_c, aff_c = _outproj_call(xc, ya_c.reshape(1, bc * seq, -1), yb_c.reshape(1, bc * seq, -1),
                                          yc_c.reshape(1, bc * seq, -1), mod_c, g2, wo, wr_b)
        xs_c, vals_c, rank_c = _route_call(aff_c.reshape(bc, seq, LANES), h2_c.reshape(bc, seq, d), ROUTE_REQS_CTX)

        ulru_l, qkv_l = _inproj_call(xl, mod_l, g1, w_in_b, rope_tabs=rope_tabs)
        ya_l, _ = _lru_call(ulru_l, conv_w[l], cb, w_gates, b_gates, lru_lambda[l], state_lru, l)
        yb_l = _win_attn_call(sink6, qkv_l, ckw_t, cvw_t, l)
        yc_l = _nbr_attn_call(qkv_l, bias_tab, ckn_t, cvn_t, l)
        xn_l, h2_l, aff_l = _outproj_call(xl, ya_l, yb_l, yc_l, mod_l, g2, wo, wr_b)
        xs_l, vals_l, rank_l = _route_call(aff_l, h2_l, ROUTE_REQS_LAT)

        cap_c, cap_l = xs_c.shape[2], xs_l.shape[2]
        y_c, y_l = _expert_call(xs_c.reshape(N_EXPERTS, bc * cap_c, d), xs_l.reshape(N_EXPERTS, bl * cap_l, d),
                                vals_c.reshape(N_EXPERTS, bc * cap_c, 1), vals_l.reshape(N_EXPERTS, bl * cap_l, 1),
                                w_exp_gate, w_exp_up, w_exp_down, l)
        xc = _combine_call(y_c.reshape(N_EXPERTS, bc, cap_c, d), rank_c, xn_c.reshape(bc, seq, d), mod_c, gf, final)
        xl = _combine_call(y_l.reshape(N_EXPERTS, bl, cap_l, d), rank_l, xn_l, mod_l, gf, final)
        xc = xc.reshape(1, bc * seq, d)

    y_prompt = xc.reshape(bc, seq, d)
    y_sample = xl
    return (y_prompt, y_sample, st, *[jnp.swapaxes(buf, -1, -2) for buf in caches])
```

```python
import functools

import numpy as np
import jax
import jax.numpy as jnp
from jax import lax
from jax.experimental import pallas as pl
from jax.experimental.pallas import tpu as pltpu

F32 = jnp.float32
BF16 = jnp.bfloat16

HEAD_DIM = 64
GRID_W = 64
LRU_C = 8.0
CONV_W = 4
KV_B = 2
G_B = 3
H_B = KV_B * G_B
H_C = 6
WINDOW = 128
WIN_BLK = 128
NA_ROWS = 8
NA_COLS = 16
ROPE_BASE = 10000.0
N_EXPERTS = 16
EC_FACTOR = 2
N_MOD = 6
EPS = 1e-6
NEG_INF = -1e30
SCALE = HEAD_DIM ** -0.5

LANES = 128
VMEM_LIMIT_BYTES = 56 * 1024 * 1024

TOKEN_BLOCK = 512
MOD_COL_BLOCK = 1536
EXPERT_F_STEPS = 2
EXPERT_ROW_CHUNK = 256
LRU_W = 512
QC0, KC0, VC0, QB0, KB0, VB0, ATT_W = 0, 384, 768, 1152, 1536, 1664, 1792
CTX_ATTN_REQS = 2
NBR_Q_ROWS = 4
NBR_K_ROWS = 12
ROUTE_REQS_CTX = 8
ROUTE_REQS_LAT = 2
COMBINE_REQS_CTX = 4
LRU_SEG = 8


def _params(n_axes):
    return pltpu.CompilerParams(dimension_semantics=("arbitrary",) * n_axes,
                                vmem_limit_bytes=VMEM_LIMIT_BYTES)


def _dot(a, b):
    return jnp.dot(a, b, preferred_element_type=F32)


def _dot_nt(a, b):
    return lax.dot_general(a, b, (((1,), (1,)), ((), ())), preferred_element_type=F32)


def _adaln_kernel(cond_ref, w_ref, b_ref, o_ref):
    c = cond_ref[...]
    s = c * jax.nn.sigmoid(c)
    o_ref[0] = _dot(s.astype(BF16), w_ref[0].astype(BF16)) + b_ref[0]


def _adaln_call(cond8, w_mod, b_mod3):
    depth, d, n = w_mod.shape
    nb = MOD_COL_BLOCK
    return pl.pallas_call(
        _adaln_kernel,
        grid=(depth, n // nb),
        in_specs=[pl.BlockSpec((8, d), lambda l, j: (0, 0)),
                  pl.BlockSpec((1, d, nb), lambda l, j: (l, 0, j)),
                  pl.BlockSpec((1, 1, nb), lambda l, j: (l, 0, j))],
        out_specs=pl.BlockSpec((1, 8, nb), lambda l, j: (l, 0, j)),
        out_shape=jax.ShapeDtypeStruct((depth, 8, n), F32),
        compiler_params=_params(2),
        name="adaln",
    )(cond8, w_mod, b_mod3)


def _rms_modulate(x, gain, shift, scale):
    y = x * lax.rsqrt(jnp.mean(x * x, axis=-1, keepdims=True) + EPS)
    return (y * gain) * (1.0 + scale) + shift


def _inproj_kernel(*refs, rope, cache, seq):
    if rope:
        x_ref, mod_ref, g_ref, w_ref, cos_ref, sin_ref, ulru_ref, qkv_ref = refs
    elif cache:
        x_ref, mod_ref, g_ref, w_ref = refs[:4]
        ulru_ref, qkv_ref, kw_ref, vw_ref, kn_ref, vn_ref = refs[-6:]
    else:
        x_ref, mod_ref, g_ref, w_ref, ulru_ref, qkv_ref = refs
    x = x_ref[0]
    h = _rms_modulate(x, g_ref[...], mod_ref[0, 0:1, :], mod_ref[0, 1:2, :])
    u = _dot(h.astype(BF16), w_ref[...])
    tb = u.shape[0]
    ulru_ref[0] = u[:, :LRU_W]
    if rope:
        lo, hi = LRU_W + QB0, LRU_W + VB0
        qk = u[:, lo:hi]
        lane = lax.broadcasted_iota(jnp.int32, (tb, hi - lo), 1)
        first = (lane & 31) < 16
        partner = jnp.where(first, pltpu.roll(qk, hi - lo - 16, 1), pltpu.roll(qk, 16, 1))
        qk = qk * cos_ref[...] + partner * sin_ref[...]
        qkv_ref[0, :, :QB0] = u[:, LRU_W:lo].astype(BF16)
        qkv_ref[0, :, QB0:VB0] = qk.astype(BF16)
        qkv_ref[0, :, VB0:] = u[:, hi:].astype(BF16)
    else:
        qkv_ref[0] = u[:, LRU_W:].astype(BF16)
    if cache:
        for r in range(tb // seq):
            rows = slice(r * seq, (r + 1) * seq)
            for ref, c0, heads in ((kw_ref, KB0, KV_B), (vw_ref, VB0, KV_B), (kn_ref, KC0, H_C), (vn_ref, VC0, H_C)):
                for p in range(heads // 2):
                    pair_t = u[rows, LRU_W + c0 + LANES * p: LRU_W + c0 + LANES * (p + 1)].T
                    ref[r, 0, 2 * p] = pair_t[:HEAD_DIM]
                    ref[r, 0, 2 * p + 1] = pair_t[HEAD_DIM:]


def _inproj_call(x3, mod_g, gain, w_in_b, rope_tabs=None, cache_seq=None, layer=0, depth=1, prev_caches=None):
    g, n, d = x3.shape
    d_in = w_in_b.shape[1]
    tb = TOKEN_BLOCK
    rope = rope_tabs is not None
    cache = cache_seq is not None
    in_specs = [pl.BlockSpec((1, tb, d), lambda i, j: (i, j, 0)),
                pl.BlockSpec((1, N_MOD, d), lambda i, j: (i, 0, 0)),
                pl.BlockSpec((1, d), lambda i, j: (0, 0)),
                pl.BlockSpec((d, d_in), lambda i, j: (0, 0))]
    args = [x3, mod_g, gain, w_in_b]
    assert d_in == LRU_W + ATT_W
    out_shape = [jax.ShapeDtypeStruct((g, n, LRU_W), F32),
                 jax.ShapeDtypeStruct((g, n, ATT_W), BF16)]
    out_specs = [pl.BlockSpec((1, tb, LRU_W), lambda i, j: (i, j, 0)),
                 pl.BlockSpec((1, tb, ATT_W), lambda i, j: (i, j, 0))]
    if rope:
        in_specs += [pl.BlockSpec((tb, VB0 - QB0), lambda i, j: (j, 0))] * 2
        args += list(rope_tabs)
    aliases = {}
    if cache:
        assert g == 1 and tb % cache_seq == 0
        rb = tb // cache_seq
        nreq = n // cache_seq
        for heads in (KV_B, KV_B, H_C, H_C):
            out_shape.append(jax.ShapeDtypeStruct((nreq, depth, heads, HEAD_DIM, cache_seq), F32))
            out_specs.append(pl.BlockSpec((rb, 1, heads, HEAD_DIM, cache_seq), lambda i, j: (j, layer, 0, 0, 0)))
        if prev_caches is not None:
            for k, buf in enumerate(prev_caches):
                aliases[len(args)] = 2 + k
                in_specs.append(pl.BlockSpec(memory_space=pl.ANY))
                args.append(buf)
    return pl.pallas_call(
        functools.partial(_inproj_kernel, rope=rope, cache=cache, seq=cache_seq),
        grid=(g, n // tb),
        in_specs=in_specs,
        out_specs=out_specs,
        out_shape=out_shape,
        input_output_aliases=aliases,
        compiler_params=_params(2),
        name="inproj_rope" if rope else "inproj_ctx",
    )(*args)


def _lru_kernel(*refs, n):
    u_ref, cw_ref, cb_ref, wg_ref, bg_ref, lam_ref, h0_ref = refs[:7]
    ya_ref, st_ref, a_s, x_s, y_s, hloc_s, ploc_s, hend_s, pend_s, cin_s = refs[-10:]
    c = 256
    nblk = n // LRU_SEG
    nh = c // LANES
    u = u_ref[0]
    xa = u[:, :c]
    ga = u[:, c:]
    t = lax.broadcasted_iota(jnp.int32, (n, c), 0)
    cw = cw_ref[...]
    xc = cw[2:3] * xa + cb_ref[...]
    xc = xc + cw[0:1] * jnp.where(t >= 2, pltpu.roll(xa, 2, 0), 0.0)
    xc = xc + cw[1:2] * jnp.where(t >= 1, pltpu.roll(xa, 1, 0), 0.0)
    xc = xc + cw[3:4] * jnp.where(t < n - 1, pltpu.roll(xa, n - 1, 0), 0.0)
    gates = _dot(xc.astype(BF16), wg_ref[...]) + bg_ref[...]
    lam = lam_ref[...]
    log_sig = jnp.minimum(lam, 0.0) - jnp.log1p(jnp.exp(-jnp.abs(lam)))
    for d in range(2):
        r = 0.5 * jnp.tanh(0.5 * gates[:, 2 * d * c:(2 * d + 1) * c]) + 0.5
        i = 0.5 * jnp.tanh(0.5 * gates[:, (2 * d + 1) * c:(2 * d + 2) * c]) + 0.5
        log_a = LRU_C * r * log_sig[d:d + 1]
        a = jnp.exp(log_a)
        t_in = jnp.tanh(-log_a) * (1.0 + a * a)
        xin = jnp.where(t_in > 0.0, t_in * lax.rsqrt(t_in), 0.0) * (i * xc)
        for hf in range(nh):
            a_s[hf] = a[:, LANES * hf: LANES * (hf + 1)]
            x_s[hf] = xin[:, LANES * hf: LANES * (hf + 1)]
        order = range(LRU_SEG) if d == 0 else range(LRU_SEG - 1, -1, -1)
        for hf in range(nh):
            h_run = p_run = None
            for s in order:
                a_row = a_s[hf, pl.ds(s, nblk, stride=LRU_SEG), :]
                x_row = x_s[hf, pl.ds(s, nblk, stride=LRU_SEG), :]
                h_run = x_row if h_run is None else a_row * h_run + x_row
                p_run = a_row if p_run is None else a_row * p_run
                hloc_s[d, hf, s] = h_run
                ploc_s[d, hf, s] = p_run
            hend_s[d, hf] = h_run
            pend_s[d, hf] = p_run

    h0 = h0_ref[0, 0]
    init = tuple(h0[d:d + 1, LANES * hf: LANES * (hf + 1)] for d in range(2) for hf in range(nh))

    def body(k, carry):
        out = []
        for idx, cur in enumerate(carry):
            d, hf = divmod(idx, nh)
            kk = k if d == 0 else nblk - 1 - k
            cin_s[d, hf, pl.ds(kk, 1), :] = cur
            out.append(pend_s[d, hf, pl.ds(kk, 1), :] * cur + hend_s[d, hf, pl.ds(kk, 1), :])
        return tuple(out)

    final = lax.fori_loop(0, nblk, body, init)

    y = None
    for d in range(2):
        for hf in range(nh):
            cin = cin_s[d, hf]
            for s in range(LRU_SEG):
                full_h = hloc_s[d, hf, s] + ploc_s[d, hf, s] * cin
                if d == 0:
                    y_s[hf, pl.ds(s, nblk, stride=LRU_SEG), :] = full_h
                else:
                    y_s[hf, pl.ds(s, nblk, stride=LRU_SEG), :] += full_h
    y = jnp.concatenate([y_s[hf] for hf in range(nh)], axis=1) * jax.nn.gelu(ga)
    ya_ref[0] = y.astype(BF16)
    st_ref[0, 0, 0:1, :] = jnp.concatenate(final[:nh], axis=1)
    st_ref[0, 0, 1:2, :] = jnp.concatenate(final[nh:], axis=1)


def _lru_call(ulru, conv_w, conv_b, w_gates_b, b_gates, lam, h0, h0_layer, st_layer=0, st_depth=1, prev_state=None):
    b, n, _ = ulru.shape
    c = 256
    full = lambda shape: pl.BlockSpec(shape, lambda i: (0,) * len(shape))
    nh, nblk = c // LANES, n // LRU_SEG
    assert n % LRU_SEG == 0 and nblk % 8 == 0
    scratch = ([pltpu.VMEM((nh, n, LANES), F32)] * 3
               + [pltpu.VMEM((2, nh, LRU_SEG, nblk, LANES), F32)] * 2
               + [pltpu.VMEM((2, nh, nblk, LANES), F32)] * 3)
    in_specs = [pl.BlockSpec((1, n, 2 * c), lambda i: (i, 0, 0)),
                full((CONV_W, c)), full((1, c)), full((c, 4 * c)), full((1, 4 * c)), full((2, c)),
                pl.BlockSpec((1, 1, 2, c), lambda i: (i, h0_layer, 0, 0))]
    args = [ulru, conv_w, conv_b, w_gates_b, b_gates, lam, h0]
    aliases = {}
    if prev_state is not None:
        aliases[len(args)] = 1
        in_specs.append(pl.BlockSpec(memory_space=pl.ANY))
        args.append(prev_state)
    return pl.pallas_call(
        functools.partial(_lru_kernel, n=n),
        grid=(b,),
        in_specs=in_specs,
        out_specs=[pl.BlockSpec((1, n, c), lambda i: (i, 0, 0)),
                   pl.BlockSpec((1, 1, 2, c), lambda i: (i, st_layer, 0, 0))],
        out_shape=[jax.ShapeDtypeStruct((b, n, c), BF16),
                   jax.ShapeDtypeStruct((b, st_depth, 2, c), F32)],
        scratch_shapes=scratch,
        input_output_aliases=aliases,
        compiler_params=_params(1),
        name="rglru",
    )(*args)


def _softmax_pv(parts, sink):
    mx = None
    for s, _, _ in parts:
        cur = jnp.max(s, axis=-1, keepdims=True)
        mx = cur if mx is None else jnp.maximum(mx, cur)
    if sink is not None:
        mx = jnp.maximum(mx, sink)
    den = None
    out = None
    for s, v, v_t in parts:
        p = jnp.exp(s - mx)
        cur = jnp.sum(p, axis=-1, keepdims=True)
        den = cur if den is None else den + cur
        o = _dot_nt(p.astype(BF16), v) if v_t else _dot(p.astype(BF16), v)
        out = o if out is None else out + o
    if sink is not None:
        den = den + jnp.exp(sink - mx)
    return out / den


def _swap_halves(x):
    return jnp.concatenate([x[:, HEAD_DIM:], x[:, :HEAD_DIM]], axis=1)


def _keep_half(x, lo_mask, half):
    zero = jnp.zeros_like(x)
    return jnp.where(lo_mask, x, zero) if half == 0 else jnp.where(lo_mask, zero, x)


def _ctx_attn_kernel(sink_ref, att_ref, ob_ref, oc_ref, *, n, rb):
    lo = lax.broadcasted_iota(jnp.int32, (n, LANES), 1) < HEAD_DIM
    for r in range(rb):
        for p in range(H_C // 2):
            qp = att_ref[r, :, QC0 + LANES * p: QC0 + LANES * (p + 1)]
            kp = att_ref[r, :, KC0 + LANES * p: KC0 + LANES * (p + 1)]
            vp = att_ref[r, :, VC0 + LANES * p: VC0 + LANES * (p + 1)]
            outs = [_softmax_pv([(_dot_nt(qp, _keep_half(kp, lo, half)), vp, False)], None) for half in range(2)]
            oc_ref[r, :, LANES * p: LANES * (p + 1)] = jnp.where(lo, outs[0], outs[1]).astype(BF16)
        kpair = att_ref[r, :, KB0: KB0 + LANES]
        vpair = att_ref[r, :, VB0: VB0 + LANES]
        kpair_sw = _swap_halves(kpair)
        vpair_sw = _swap_halves(vpair)
        for p in range(H_B // 2):
            qp = att_ref[r, :, QB0 + LANES * p: QB0 + LANES * (p + 1)]
            outs = []
            for half in range(2):
                h = 2 * p + half
                aligned = (h // G_B) == half
                ksrc, vsrc = (kpair, vpair) if aligned else (kpair_sw, vpair_sw)
                outs.append(_softmax_pv([(_dot_nt(qp, _keep_half(ksrc, lo, half)), vsrc, False)], sink_ref[h]))
            ob_ref[r, :, LANES * p: LANES * (p + 1)] = jnp.where(lo, outs[0], outs[1]).astype(BF16)


def _ctx_attn_call(sink6, att):
    b, n, w = att.shape
    rb = CTX_ATTN_REQS
    assert b % rb == 0 and KV_B == 2
    return pl.pallas_call(
        functools.partial(_ctx_attn_kernel, n=n, rb=rb),
        grid=(b // rb,),
        in_specs=[pl.BlockSpec(memory_space=pltpu.SMEM),
                  pl.BlockSpec((rb, n, w), lambda i: (i, 0, 0))],
        out_specs=[pl.BlockSpec((rb, n, 384), lambda i: (i, 0, 0))] * 2,
        out_shape=[jax.ShapeDtypeStruct((b, n, 384), BF16)] * 2,
        compiler_params=_params(1),
        name="ctx_attn",
    )(sink6, att)


def _win_attn_kernel(sink_ref, qkv_ref, kc_ref, vc_ref, o_ref, *, n):
    j = pl.program_id(1)
    nloc = 3 * WIN_BLK
    q0 = pl.multiple_of(j * WIN_BLK, WIN_BLK)
    ks = pl.multiple_of(jnp.clip((j - 1) * WIN_BLK, 0, n - nloc), WIN_BLK)
    row = lax.broadcasted_iota(jnp.int32, (WIN_BLK, nloc), 0)
    col = lax.broadcasted_iota(jnp.int32, (WIN_BLK, nloc), 1)
    in_window = jnp.where(jnp.abs((q0 + row) - (ks + col)) <= WINDOW, 0.0, NEG_INF)
    band = jnp.concatenate([in_window] * G_B, axis=0)
    rowh = lax.broadcasted_iota(jnp.int32, (G_B * WIN_BLK, 1), 0) // WIN_BLK
    for kv in range(KV_B):
        gw = G_B * HEAD_DIM
        qblk = qkv_ref[0, pl.ds(q0, WIN_BLK), QB0 + gw * kv: QB0 + gw * (kv + 1)]
        q3 = jnp.concatenate([qblk[:, 64 * g: 64 * (g + 1)] for g in range(G_B)], axis=0)
        kl = qkv_ref[0, pl.ds(ks, nloc), KB0 + 64 * kv: KB0 + 64 * (kv + 1)]
        vl = qkv_ref[0, pl.ds(ks, nloc), VB0 + 64 * kv: VB0 + 64 * (kv + 1)]
        kc_t = kc_ref[0, 0, kv].astype(BF16)
        vc_t = vc_ref[0, 0, kv].astype(BF16)
        s_loc = _dot_nt(q3, kl) + band
        s_ctx = _dot(q3, kc_t)
        sk = jnp.where(rowh == 0, sink_ref[3 * kv],
                       jnp.where(rowh == 1, sink_ref[3 * kv + 1], sink_ref[3 * kv + 2]))
        o = _softmax_pv([(s_loc, vl, False), (s_ctx, vc_t, True)], sk)
        for g in range(G_B):
            h = kv * G_B + g
            o_ref[0, :, 64 * h: 64 * (h + 1)] = o[g * WIN_BLK:(g + 1) * WIN_BLK].astype(BF16)


def _win_attn_call(sink6, qkv, cache_k, cache_v, layer):
    b, n, w = qkv.shape
    past = cache_k.shape[4]
    cache_spec = pl.BlockSpec((1, 1, KV_B, HEAD_DIM, past), lambda i, j: (i, layer, 0, 0, 0))
    return pl.pallas_call(
        functools.partial(_win_attn_kernel, n=n),
        grid=(b, n // WIN_BLK),
        in_specs=[pl.BlockSpec(memory_space=pltpu.SMEM),
                  pl.BlockSpec((1, n, w), lambda i, j: (i, 0, 0)),
                  cache_spec, cache_spec],
        out_specs=pl.BlockSpec((1, WIN_BLK, 384), lambda i, j: (i, j, 0)),
        out_shape=jax.ShapeDtypeStruct((b, n, 384), BF16),
        compiler_params=_params(2),
        name="win_attn",
    )(sink6, qkv, cache_k, cache_v)


def _nbr_window_start(g, rows):
    return jnp.clip(g * NBR_Q_ROWS - NA_ROWS // 2, 0, rows - NBR_K_ROWS)


def _nbr_attn_kernel(q_ref, k_ref, v_ref, rel_ref, kc_ref, vc_ref, o_ref, bias_s, *, n):
    g = pl.program_id(0)
    nq = NBR_Q_ROWS * GRID_W
    nk = NBR_K_ROWS * GRID_W
    rows = n // GRID_W
    ws = _nbr_window_start(g, rows)
    k0 = pl.multiple_of(ws * GRID_W, GRID_W)

    @pl.when(pl.program_id(1) == 0)
    def _():
        for i in range(NBR_Q_ROWS):
            qr = g * NBR_Q_ROWS + i
            rs = jnp.clip(qr - NA_ROWS // 2, 0, rows - NA_ROWS)
            for jp in range(NBR_K_ROWS // 2):
                tiles = []
                for j in (2 * jp, 2 * jp + 1):
                    kr = ws + j
                    valid = (kr >= rs) & (kr < rs + NA_ROWS)
                    d = jnp.clip(kr - qr + NA_ROWS - 1, 0, 2 * NA_ROWS - 2)
                    tiles.append(jnp.where(valid, rel_ref[0, :, pl.ds(d, 1)][:, 0], NEG_INF))
                bias_s[:, GRID_W * i: GRID_W * (i + 1), LANES * jp: LANES * (jp + 1)] = jnp.concatenate(tiles, axis=-1)
    lo_k = lax.broadcasted_iota(jnp.int32, (nk, LANES), 1) < HEAD_DIM
    lo_q = lax.broadcasted_iota(jnp.int32, (nq, LANES), 1) < HEAD_DIM
    zpad = jnp.zeros((HEAD_DIM, kc_ref.shape[4]), BF16)
    for p in range(H_C // 2):
        qp = q_ref[0, :, LANES * p: LANES * (p + 1)]
        kp = k_ref[0, pl.ds(k0, nk), LANES * p: LANES * (p + 1)]
        vp = v_ref[0, pl.ds(k0, nk), LANES * p: LANES * (p + 1)]
        outs = []
        for half in range(2):
            h = 2 * p + half
            kc_t = kc_ref[0, 0, h].astype(BF16)
            vc_t = vc_ref[0, 0, h].astype(BF16)
            kc_t = jnp.concatenate([kc_t, zpad] if half == 0 else [zpad, kc_t], axis=0)
            vc_t = jnp.concatenate([vc_t, zpad] if half == 0 else [zpad, vc_t], axis=0)
            s_loc = _dot_nt(qp, _keep_half(kp, lo_k, half)) + bias_s[h]
            s_ctx = _dot(qp, kc_t)
            outs.append(_softmax_pv([(s_loc, vp, False), (s_ctx, vc_t, True)], None))
        o_ref[0, :, LANES * p: LANES * (p + 1)] = jnp.where(lo_q, outs[0], outs[1]).astype(BF16)


def _nbr_attn_call(att, rel_tab, cache_k, cache_v, layer):
    b, n, w = att.shape
    past = cache_k.shape[4]
    nq = NBR_Q_ROWS * GRID_W
    nk = NBR_K_ROWS * GRID_W
    gw = H_C * HEAD_DIM
    rows = n // GRID_W
    assert (QC0, KC0, VC0) == (0, gw, 2 * gw) and n % nq == 0 and rows >= NBR_K_ROWS and NBR_K_ROWS % 2 == 0
    cache_spec = pl.BlockSpec((1, 1, H_C, HEAD_DIM, past), lambda g, i: (i, layer, 0, 0, 0))
    return pl.pallas_call(
        functools.partial(_nbr_attn_kernel, n=n),
        grid=(n // nq, b),
        in_specs=[pl.BlockSpec((1, nq, gw), lambda g, i: (i, g, 0)),
                  pl.BlockSpec((1, n, gw), lambda g, i: (i, 0, 1)),
                  pl.BlockSpec((1, n, gw), lambda g, i: (i, 0, 2)),
                  pl.BlockSpec((1,) + rel_tab.shape[1:], lambda g, i: (layer, 0, 0, 0, 0)),
                  cache_spec, cache_spec],
        out_specs=pl.BlockSpec((1, nq, gw), lambda g, i: (i, g, 0)),
        out_shape=jax.ShapeDtypeStruct((b, n, gw), BF16),
        scratch_shapes=[pltpu.VMEM((H_C, nq, nk), F32)],
        compiler_params=_params(2),
        name="nbr_attn",
    )(att, att, att, rel_tab, cache_k, cache_v)


def _outproj_kernel(x_ref, ya_ref, yb_ref, yc_ref, mod_ref, g_ref, wo_ref, wr_ref,
                    xn_ref, h2_ref, aff_ref):
    mix = jnp.concatenate([ya_ref[0], yb_ref[0], yc_ref[0]], axis=1)
    proj = _dot(mix, wo_ref[...])
    xn = x_ref[0] + mod_ref[0, 2:3, :] * proj
    xn_ref[0] = xn
    h2 = _rms_modulate(xn, g_ref[...], mod_ref[0, 3:4, :], mod_ref[0, 4:5, :]).astype(BF16)
    h2_ref[0] = h2
    logits = _dot(h2, wr_ref[...])
    lane = lax.broadcasted_iota(jnp.int32, logits.shape, 1)
    valid = lane < N_EXPERTS
    logits = jnp.where(valid, logits, NEG_INF)
    e = jnp.exp(logits - jnp.max(logits, axis=-1, keepdims=True))
    e = jnp.where(valid, e, 0.0)
    aff_ref[0] = e / jnp.sum(e, axis=-1, keepdims=True)


def _outproj_call(x3, ya, yb, yc, mod_g, gain, wo_b16, w_router_b):
    g, n, d = x3.shape
    tb = TOKEN_BLOCK
    tok = lambda w: pl.BlockSpec((1, tb, w), lambda i, j: (i, j, 0))
    full = lambda shape: pl.BlockSpec(shape, lambda i, j: (0,) * len(shape))
    return pl.pallas_call(
        _outproj_kernel,
        grid=(g, n // tb),
        in_specs=[tok(d), tok(256), tok(384), tok(384),
                  pl.BlockSpec((1, N_MOD, d), lambda i, j: (i, 0, 0)),
                  full((1, d)), full((256 + 384 + 384, d)), full((d, LANES))],
        out_specs=[tok(d), tok(d), tok(LANES)],
        out_shape=[jax.ShapeDtypeStruct((g, n, d), F32),
                   jax.ShapeDtypeStruct((g, n, d), BF16),
                   jax.ShapeDtypeStruct((g, n, LANES), F32)],
        compiler_params=_params(2),
        name="outproj_router",
    )(x3, ya, yb, yc, mod_g, gain, wo_b16, w_router_b)


def _route_kernel(aff_ref, h2_ref, xs_ref, vals_ref, rank_ref, *, n, cap, rb):
    ne = N_EXPERTS
    nt = n // LANES
    fcap = float(cap)
    jj = lax.broadcasted_iota(jnp.int32, (LANES, LANES), 0)
    ii = lax.broadcasted_iota(jnp.int32, (LANES, LANES), 1)
    earlier = jnp.where(jj < ii, 1.0, 0.0)
    slot = lax.broadcasted_iota(jnp.int32, (cap, n), 0).astype(F32)
    pad = jnp.full((LANES - ne, n), fcap, F32)
    for r in range(rb):
        aff = aff_ref[r]
        aff_t = aff.T[:ne]
        rank_rows = []
        for e in range(ne):
            cols = [jnp.broadcast_to(aff[jb * LANES:(jb + 1) * LANES, e:e + 1], (LANES, LANES)) for jb in range(nt)]
            pieces = []
            for ib in range(nt):
                a_row = aff_t[e:e + 1, ib * LANES:(ib + 1) * LANES]
                acc = None
                for jb in range(nt):
                    if jb < ib:
                        blk = jnp.where(cols[jb] >= a_row, 1.0, 0.0)
                    elif jb > ib:
                        blk = jnp.where(cols[jb] > a_row, 1.0, 0.0)
                    else:
                        blk = jnp.where(cols[jb] > a_row, 1.0, jnp.where(cols[jb] == a_row, earlier, 0.0))
                    acc = blk if acc is None else acc + blk
                pieces.append(jnp.sum(acc, axis=0, keepdims=True))
            rank_rows.append(jnp.concatenate(pieces, axis=1) if nt > 1 else pieces[0])
        rank_r = jnp.minimum(jnp.concatenate(rank_rows, axis=0), fcap)
        rank_ref[r] = jnp.concatenate([rank_r, pad], axis=0).T
        picks = []
        for e in range(ne):
            onehot = jnp.where(rank_r[e:e + 1] == slot, 1.0, 0.0)
            vals_ref[e, r] = jnp.sum(onehot * aff_t[e:e + 1], axis=1, keepdims=True)
            picks.append(onehot.astype(BF16))
        xs = _dot(jnp.concatenate(picks, axis=0), h2_ref[r])
        xs_ref[:, r] = xs.reshape(ne, cap, xs.shape[-1]).astype(BF16)


def _route_call(aff, h2, rb):
    b, n, d = h2.shape
    cap = max(1, EC_FACTOR * n // N_EXPERTS)
    assert b % rb == 0
    return pl.pallas_call(
        functools.partial(_route_kernel, n=n, cap=cap, rb=rb),
        grid=(b // rb,),
        in_specs=[pl.BlockSpec((rb, n, LANES), lambda i: (i, 0, 0)),
                  pl.BlockSpec((rb, n, d), lambda i: (i, 0, 0))],
        out_specs=[pl.BlockSpec((N_EXPERTS, rb, cap, d), lambda i: (0, i, 0, 0)),
                   pl.BlockSpec((N_EXPERTS, rb, cap, 1), lambda i: (0, i, 0, 0)),
                   pl.BlockSpec((rb, n, LANES), lambda i: (i, 0, 0))],
        out_shape=[jax.ShapeDtypeStruct((N_EXPERTS, b, cap, d), BF16),
                   jax.ShapeDtypeStruct((N_EXPERTS, b, cap, 1), F32),
                   jax.ShapeDtypeStruct((b, n, LANES), F32)],
        compiler_params=_params(1),
        name="route_gather",
    )(aff, h2)


def _expert_kernel(xc_ref, xl_ref, vc_ref, vl_ref, wg_ref, wu_ref, wd_ref, yc_ref, yl_ref, acc_c, acc_l):
    f = pl.program_id(1)

    def step(first):
        wg = wg_ref[0, 0].astype(BF16)
        wu = wu_ref[0, 0].astype(BF16)
        wd = wd_ref[0, 0].astype(BF16)
        for x_ref, v_ref, y_ref, acc in ((xc_ref, vc_ref, yc_ref, acc_c), (xl_ref, vl_ref, yl_ref, acc_l)):
            rows = x_ref.shape[1]
            chunk = min(EXPERT_ROW_CHUNK, rows)
            for c0 in range(0, rows, chunk):
                sl = slice(c0, c0 + chunk)
                x = x_ref[0, sl, :]
                a = _dot(x, wg)
                u = _dot(x, wu)
                act = ((a * jax.nn.sigmoid(a)) * u).astype(BF16)
                part = _dot(act, wd)
                if first:
                    acc[sl, :] = part
                else:
                    y_ref[0, sl, :] = ((acc[sl, :] + part) * v_ref[0, sl, :]).astype(BF16)

    pl.when(f == 0)(functools.partial(step, True))
    pl.when(f == EXPERT_F_STEPS - 1)(functools.partial(step, False))


def _expert_call(xs_c, xs_l, vals_c, vals_l, w_gate, w_up, w_down, layer):
    e, rc, d = xs_c.shape
    rl = xs_l.shape[1]
    f_total = w_gate.shape[-1]
    assert EXPERT_F_STEPS == 2
    fb = f_total // EXPERT_F_STEPS
    per_e = lambda rows, w: pl.BlockSpec((1, rows, w), lambda i, j: (i, 0, 0))
    return pl.pallas_call(
        _expert_kernel,
        grid=(e, EXPERT_F_STEPS),
        in_specs=[per_e(rc, d), per_e(rl, d), per_e(rc, 1), per_e(rl, 1),
                  pl.BlockSpec((1, 1, d, fb), lambda i, j: (layer, i, 0, j)),
                  pl.BlockSpec((1, 1, d, fb), lambda i, j: (layer, i, 0, j)),
                  pl.BlockSpec((1, 1, fb, d), lambda i, j: (layer, i, j, 0))],
        out_specs=[per_e(rc, d), per_e(rl, d)],
        out_shape=[jax.ShapeDtypeStruct((e, rc, d), BF16),
                   jax.ShapeDtypeStruct((e, rl, d), BF16)],
        scratch_shapes=[pltpu.VMEM((rc, d), F32), pltpu.VMEM((rl, d), F32)],
        compiler_params=_params(2),
        name="experts",
    )(xs_c, xs_l, vals_c, vals_l, w_gate, w_up, w_down)


def _combine_kernel(y_ref, rank_ref, xn_ref, mod_ref, gf_ref, o_ref, *, n, cap, final, rb):
    ec = N_EXPERTS * cap
    d = xn_ref.shape[-1]
    ce = lax.broadcasted_iota(jnp.int32, (LANES, ec), 1) // cap
    ee = lax.broadcasted_iota(jnp.int32, (LANES, ec), 0)
    expand = jnp.where(ce == ee, 1.0, 0.0).astype(BF16)
    slot = (lax.broadcasted_iota(jnp.int32, (n, ec), 1) % cap).astype(F32)
    for r in range(rb):
        rank = rank_ref[r].astype(BF16)
        rexp = _dot(rank, expand)
        onehot = jnp.where(rexp == slot, 1.0, 0.0).astype(BF16)
        moe = _dot(onehot, y_ref[:, r].reshape(ec, d))
        x = xn_ref[r] + mod_ref[0, 5:6, :] * moe
        if final:
            x = (x * lax.rsqrt(jnp.mean(x * x, axis=-1, keepdims=True) + EPS)) * gf_ref[...]
        o_ref[r] = x


def _combine_call(y4, rank, xn, mod_g, g_final, final):
    e, b, cap, d = y4.shape
    n = xn.shape[1]
    shared_mod = mod_g.shape[0] == 1
    rb = COMBINE_REQS_CTX if shared_mod else 1
    assert b % rb == 0
    return pl.pallas_call(
        functools.partial(_combine_kernel, n=n, cap=cap, final=final, rb=rb),
        grid=(b // rb,),
        in_specs=[pl.BlockSpec((e, rb, cap, d), lambda i: (0, i, 0, 0)),
                  pl.BlockSpec((rb, n, LANES), lambda i: (i, 0, 0)),
                  pl.BlockSpec((rb, n, d), lambda i: (i, 0, 0)),
                  pl.BlockSpec((1, N_MOD, d), (lambda i: (0, 0, 0)) if shared_mod else (lambda i: (i, 0, 0))),
                  pl.BlockSpec((1, d), lambda i: (0, 0))],
        out_specs=pl.BlockSpec((rb, n, d), lambda i: (i, 0, 0)),
        out_shape=jax.ShapeDtypeStruct((b, n, d), F32),
        compiler_params=_params(1),
        name="combine",
    )(y4, rank, xn, mod_g, g_final)


def _rope_tables(n):
    t = jnp.arange(n, dtype=jnp.int32)
    row = (t // GRID_W).astype(F32)
    col = (t % GRID_W).astype(F32)
    half = HEAD_DIM // 4
    inv = jnp.power(ROPE_BASE, -jnp.arange(half, dtype=F32) / half)
    ang_r = row[:, None] * inv[None, :]
    ang_c = col[:, None] * inv[None, :]
    cos_h = jnp.concatenate([jnp.cos(ang_r)] * 2 + [jnp.cos(ang_c)] * 2, axis=-1)
    sin_h = jnp.concatenate([-jnp.sin(ang_r), jnp.sin(ang_r), -jnp.sin(ang_c), jnp.sin(ang_c)], axis=-1)
    reps = 512 // HEAD_DIM
    return jnp.tile(cos_h, (1, reps)), jnp.tile(sin_h, (1, reps))


def _nbr_rel_tables(rpb):
    col = np.arange(GRID_W)
    cs = np.clip(col - NA_COLS // 2, 0, GRID_W - NA_COLS)
    col_mask = (col[None, :] >= cs[:, None]) & (col[None, :] < cs[:, None] + NA_COLS)
    dc_idx = np.clip(col[None, :] - col[:, None], -(NA_COLS - 1), NA_COLS - 1) + (NA_COLS - 1)
    sel_col = (dc_idx[None, :, :] == np.arange(2 * NA_COLS - 1)[:, None, None]).astype(np.float32)
    t = jnp.einsum("lhdc,cqk->lhdqk", rpb.astype(F32), sel_col, precision=lax.Precision.HIGHEST)
    return jnp.where(col_mask, t, NEG_INF)


def _block_diag(w):
    nb, bw, _ = w.shape
    eye = jnp.eye(nb, dtype=w.dtype)
    return (eye[:, None, :, None] * w[:, :, None, :]).reshape(nb * bw, nb * bw)


def kernel(x_prompt, x_sample, state_lru, cache_k_win, cache_v_win, cache_k_nbr, cache_v_nbr, c, c_ctx, w_mod, b_mod, g_norm1, w_in, conv_w, conv_b, w_gate_r, b_gate_r, w_gate_i, b_gate_i, lru_lambda, sink_logit, nbr_bias, w_out, g_norm2, w_router, w_exp_gate, w_exp_up, w_exp_down, g_final):
    bc, seq, d = x_prompt.shape
    bl, n_lat, _ = x_sample.shape
    depth = w_mod.shape[0]
    assert bl + 1 <= 8 and d == 1024

    cond8 = jnp.zeros((8, d), F32).at[0].set(c_ctx).at[1:1 + bl].set(c)
    mods = _adaln_call(cond8, w_mod, b_mod.reshape(depth, 1, -1)).reshape(depth, 8, N_MOD, d)
    rope_tabs = _rope_tables(n_lat)
    bias_tab = _nbr_rel_tables(nbr_bias)
    ckw_t, cvw_t, ckn_t, cvn_t = (jnp.swapaxes(a, -1, -2) for a in (cache_k_win, cache_v_win, cache_k_nbr, cache_v_nbr))
    gf = g_final.reshape(1, d)
    assert SCALE == 0.125
    new_cols = np.arange(w_in.shape[-1]) - LRU_W
    is_q = ((new_cols >= QC0) & (new_cols < KC0)) | ((new_cols >= QB0) & (new_cols < KB0))
    q_scale = jnp.asarray(np.where(is_q, SCALE, 1.0), F32)

    xc = x_prompt.reshape(1, bc * seq, d)
    xl = x_sample
    zeros_state = jnp.zeros((bc, 1, 2, 256), F32)
    caches = None
    st = None
    for l in range(depth):
        final = l == depth - 1
        mod_c = mods[l, 0:1]
        mod_l = mods[l, 1:1 + bl]
        g1 = g_norm1[l].reshape(1, d)
        g2 = g_norm2[l].reshape(1, d)
        w_l = w_in[l]
        w_in_b = (jnp.concatenate([w_l[:, :512], w_l[:, 1152:], w_l[:, 512:1152]], axis=1) * q_scale).astype(BF16)
        wo = w_out[l].astype(BF16)
        wr_b = jnp.zeros((d, LANES), BF16).at[:, :N_EXPERTS].set(w_router[l].astype(BF16))
        w_gates = jnp.concatenate([_block_diag(w_gate_r[l, 0]), _block_diag(w_gate_i[l, 0]),
                                   _block_diag(w_gate_r[l, 1]), _block_diag(w_gate_i[l, 1])], axis=1).astype(BF16)
        b_gates = jnp.concatenate([b_gate_r[l, 0], b_gate_i[l, 0], b_gate_r[l, 1], b_gate_i[l, 1]]).reshape(1, -1)
        cb = conv_b[l].reshape(1, -1)
        sink6 = sink_logit[l].reshape(-1)

        ulru_c, qkv_c, *caches = _inproj_call(xc, mod_c, g1, w_in_b, cache_seq=seq, layer=l, depth=depth,
                                              prev_caches=caches)
        ya_c, st = _lru_call(ulru_c.reshape(bc, seq, 512), conv_w[l], cb, w_gates, b_gates, lru_lambda[l],
                             zeros_state, 0, st_layer=l, st_depth=depth, prev_state=st)
        yb_c, yc_c = _ctx_attn_call(sink6, qkv_c.reshape(bc, seq, -1))
        xn_c, h2_c, aff_c = _outproj_call(xc, ya_c.reshape(1, bc * seq, -1), yb_c.reshape(1, bc * seq, -1),
                                          yc_c.reshape(1, bc * seq, -1), mod_c, g2, wo, wr_b)
        xs_c, vals_c, rank_c = _route_call(aff_c.reshape(bc, seq, LANES), h2_c.reshape(bc, seq, d), ROUTE_REQS_CTX)

        ulru_l, qkv_l = _inproj_call(xl, mod_l, g1, w_in_b, rope_tabs=rope_tabs)
        ya_l, _ = _lru_call(ulru_l, conv_w[l], cb, w_gates, b_gates, lru_lambda[l], state_lru, l)
        yb_l = _win_attn_call(sink6, qkv_l, ckw_t, cvw_t, l)
        yc_l = _nbr_attn_call(qkv_l, bias_tab, ckn_t, cvn_t, l)
        xn_l, h2_l, aff_l = _outproj_call(xl, ya_l, yb_l, yc_l, mod_l, g2, wo, wr_b)
        xs_l, vals_l, rank_l = _route_call(aff_l, h2_l, ROUTE_REQS_LAT)

        cap_c, cap_l = xs_c.shape[2], xs_l.shape[2]
        y_c, y_l = _expert_call(xs_c.reshape(N_EXPERTS, bc * cap_c, d), xs_l.reshape(N_EXPERTS, bl * cap_l, d),
                                vals_c.reshape(N_EXPERTS, bc * cap_c, 1), vals_l.reshape(N_EXPERTS, bl * cap_l, 1),
                                w_exp_gate, w_exp_up, w_exp_down, l)
        xc = _combine_call(y_c.reshape(N_EXPERTS, bc, cap_c, d), rank_c, xn_c.reshape(bc, seq, d), mod_c, gf, final)
        xl = _combine_call(y_l.reshape(N_EXPERTS, bl, cap_l, d), rank_l, xn_l, mod_l, gf, final)
        xc = xc.reshape(1, bc * seq, d)

    y_prompt = xc.reshape(bc, seq, d)
    y_sample = xl
    return (y_prompt, y_sample, st, *[jnp.swapaxes(buf, -1, -2) for buf in caches])
```

```python
import functools

import numpy as np
import jax
import jax.numpy as jnp
from jax import lax
from jax.experimental import pallas as pl
from jax.experimental.pallas import tpu as pltpu

F32 = jnp.float32
BF16 = jnp.bfloat16

HEAD_DIM = 64
GRID_W = 64
LRU_C = 8.0
CONV_W = 4
KV_B = 2
G_B = 3
H_B = KV_B * G_B
H_C = 6
WINDOW = 128
WIN_BLK = 128
NA_ROWS = 8
NA_COLS = 16
ROPE_BASE = 10000.0
N_EXPERTS = 16
EC_FACTOR = 2
N_MOD = 6
EPS = 1e-6
NEG_INF = -1e30
SCALE = HEAD_DIM ** -0.5

LANES = 128
VMEM_LIMIT_BYTES = 56 * 1024 * 1024

TOKEN_BLOCK = 512
MOD_COL_BLOCK = 1536
EXPERT_F_STEPS = 2
EXPERT_ROW_CHUNK = 256
LRU_W = 512
QC0, KC0, VC0, QB0, KB0, VB0, ATT_W = 0, 384, 768, 1152, 1536, 1664, 1792
CTX_ATTN_REQS = 2
NBR_Q_ROWS = 4
NBR_K_ROWS = 12
ROUTE_REQS_CTX = 8
ROUTE_REQS_LAT = 2
ROUTE_SORT_ROWS = 256
COMBINE_REQS_CTX = 4
LRU_SEG = 8


def _params(n_axes):
    return pltpu.CompilerParams(dimension_semantics=("arbitrary",) * n_axes,
                                vmem_limit_bytes=VMEM_LIMIT_BYTES)


def _dot(a, b):
    return jnp.dot(a, b, preferred_element_type=F32)


def _dot_nt(a, b):
    return lax.dot_general(a, b, (((1,), (1,)), ((), ())), preferred_element_type=F32)


def _adaln_kernel(cond_ref, w_ref, b_ref, o_ref):
    c = cond_ref[...]
    s = c * jax.nn.sigmoid(c)
    o_ref[0] = _dot(s.astype(BF16), w_ref[0].astype(BF16)) + b_ref[0]


def _adaln_call(cond8, w_mod, b_mod3):
    depth, d, n = w_mod.shape
    nb = MOD_COL_BLOCK
    return pl.pallas_call(
        _adaln_kernel,
        grid=(depth, n // nb),
        in_specs=[pl.BlockSpec((8, d), lambda l, j: (0, 0)),
                  pl.BlockSpec((1, d, nb), lambda l, j: (l, 0, j)),
                  pl.BlockSpec((1, 1, nb), lambda l, j: (l, 0, j))],
        out_specs=pl.BlockSpec((1, 8, nb), lambda l, j: (l, 0, j)),
        out_shape=jax.ShapeDtypeStruct((depth, 8, n), F32),
        compiler_params=_params(2),
        name="adaln",
    )(cond8, w_mod, b_mod3)


def _rms_modulate(x, gain, shift, scale):
    y = x * lax.rsqrt(jnp.mean(x * x, axis=-1, keepdims=True) + EPS)
    return (y * gain) * (1.0 + scale) + shift


def _inproj_kernel(*refs, rope, cache, seq):
    if rope:
        x_ref, mod_ref, g_ref, w_ref, cos_ref, sin_ref, ulru_ref, qkv_ref = refs
    elif cache:
        x_ref, mod_ref, g_ref, w_ref = refs[:4]
        ulru_ref, qkv_ref, kw_ref, vw_ref, kn_ref, vn_ref = refs[-6:]
    else:
        x_ref, mod_ref, g_ref, w_ref, ulru_ref, qkv_ref = refs
    x = x_ref[0]
    h = _rms_modulate(x, g_ref[...], mod_ref[0, 0:1, :], mod_ref[0, 1:2, :])
    u = _dot(h.astype(BF16), w_ref[...])
    tb = u.shape[0]
    ulru_ref[0] = u[:, :LRU_W]
    if rope:
        lo, hi = LRU_W + QB0, LRU_W + VB0
        qk = u[:, lo:hi]
        lane = lax.broadcasted_iota(jnp.int32, (tb, hi - lo), 1)
        first = (lane & 31) < 16
        partner = jnp.where(first, pltpu.roll(qk, hi - lo - 16, 1), pltpu.roll(qk, 16, 1))
        qk = qk * cos_ref[...] + partner * sin_ref[...]
        qkv_ref[0, :, :QB0] = u[:, LRU_W:lo].astype(BF16)
        qkv_ref[0, :, QB0:VB0] = qk.astype(BF16)
        qkv_ref[0, :, VB0:] = u[:, hi:].astype(BF16)
    else:
        qkv_ref[0] = u[:, LRU_W:].astype(BF16)
    if cache:
        for r in range(tb // seq):
            rows = slice(r * seq, (r + 1) * seq)
            for ref, c0, heads in ((kw_ref, KB0, KV_B), (vw_ref, VB0, KV_B), (kn_ref, KC0, H_C), (vn_ref, VC0, H_C)):
                for p in range(heads // 2):
                    pair_t = u[rows, LRU_W + c0 + LANES * p: LRU_W + c0 + LANES * (p + 1)].T
                    ref[r, 0, 2 * p] = pair_t[:HEAD_DIM]
                    ref[r, 0, 2 * p + 1] = pair_t[HEAD_DIM:]


def _inproj_call(x3, mod_g, gain, w_in_b, rope_tabs=None, cache_seq=None, layer=0, depth=1, prev_caches=None):
    g, n, d = x3.shape
    d_in = w_in_b.shape[1]
    tb = TOKEN_BLOCK
    rope = rope_tabs is not None
    cache = cache_seq is not None
    in_specs = [pl.BlockSpec((1, tb, d), lambda i, j: (i, j, 0)),
                pl.BlockSpec((1, N_MOD, d), lambda i, j: (i, 0, 0)),
                pl.BlockSpec((1, d), lambda i, j: (0, 0)),
                pl.BlockSpec((d, d_in), lambda i, j: (0, 0))]
    args = [x3, mod_g, gain, w_in_b]
    assert d_in == LRU_W + ATT_W
    out_shape = [jax.ShapeDtypeStruct((g, n, LRU_W), F32),
                 jax.ShapeDtypeStruct((g, n, ATT_W), BF16)]
    out_specs = [pl.BlockSpec((1, tb, LRU_W), lambda i, j: (i, j, 0)),
                 pl.BlockSpec((1, tb, ATT_W), lambda i, j: (i, j, 0))]
    if rope:
        in_specs += [pl.BlockSpec((tb, VB0 - QB0), lambda i, j: (j, 0))] * 2
        args += list(rope_tabs)
    aliases = {}
    if cache:
        assert g == 1 and tb % cache_seq == 0
        rb = tb // cache_seq
        nreq = n // cache_seq
        for heads in (KV_B, KV_B, H_C, H_C):
            out_shape.append(jax.ShapeDtypeStruct((nreq, depth, heads, HEAD_DIM, cache_seq), F32))
            out_specs.append(pl.BlockSpec((rb, 1, heads, HEAD_DIM, cache_seq), lambda i, j: (j, layer, 0, 0, 0)))
        if prev_caches is not None:
            for k, buf in enumerate(prev_caches):
                aliases[len(args)] = 2 + k
                in_specs.append(pl.BlockSpec(memory_space=pl.ANY))
                args.append(buf)
    return pl.pallas_call(
        functools.partial(_inproj_kernel, rope=rope, cache=cache, seq=cache_seq),
        grid=(g, n // tb),
        in_specs=in_specs,
        out_specs=out_specs,
        out_shape=out_shape,
        input_output_aliases=aliases,
        compiler_params=_params(2),
        name="inproj_rope" if rope else "inproj_ctx",
    )(*args)


def _lru_kernel(*refs, n):
    u_ref, cw_ref, cb_ref, wg_ref, bg_ref, lam_ref, h0_ref = refs[:7]
    ya_ref, st_ref, a_s, x_s, y_s, hloc_s, ploc_s, hend_s, pend_s, cin_s = refs[-10:]
    c = 256
    nblk = n // LRU_SEG
    nh = c // LANES
    u = u_ref[0]
    xa = u[:, :c]
    ga = u[:, c:]
    t = lax.broadcasted_iota(jnp.int32, (n, c), 0)
    cw = cw_ref[...]
    xc = cw[2:3] * xa + cb_ref[...]
    xc = xc + cw[0:1] * jnp.where(t >= 2, pltpu.roll(xa, 2, 0), 0.0)
    xc = xc + cw[1:2] * jnp.where(t >= 1, pltpu.roll(xa, 1, 0), 0.0)
    xc = xc + cw[3:4] * jnp.where(t < n - 1, pltpu.roll(xa, n - 1, 0), 0.0)
    gates = _dot(xc.astype(BF16), wg_ref[...]) + bg_ref[...]
    lam = lam_ref[...]
    log_sig = jnp.minimum(lam, 0.0) - jnp.log1p(jnp.exp(-jnp.abs(lam)))
    for d in range(2):
        r = 0.5 * jnp.tanh(0.5 * gates[:, 2 * d * c:(2 * d + 1) * c]) + 0.5
        i = 0.5 * jnp.tanh(0.5 * gates[:, (2 * d + 1) * c:(2 * d + 2) * c]) + 0.5
        log_a = LRU_C * r * log_sig[d:d + 1]
        a = jnp.exp(log_a)
        t_in = jnp.tanh(-log_a) * (1.0 + a * a)
        xin = jnp.where(t_in > 0.0, t_in * lax.rsqrt(t_in), 0.0) * (i * xc)
        for hf in range(nh):
            a_s[hf] = a[:, LANES * hf: LANES * (hf + 1)]
            x_s[hf] = xin[:, LANES * hf: LANES * (hf + 1)]
        order = range(LRU_SEG) if d == 0 else range(LRU_SEG - 1, -1, -1)
        for hf in range(nh):
            h_run = p_run = None
            for s in order:
                a_row = a_s[hf, pl.ds(s, nblk, stride=LRU_SEG), :]
                x_row = x_s[hf, pl.ds(s, nblk, stride=LRU_SEG), :]
                h_run = x_row if h_run is None else a_row * h_run + x_row
                p_run = a_row if p_run is None else a_row * p_run
                hloc_s[d, hf, s] = h_run
                ploc_s[d, hf, s] = p_run
            hend_s[d, hf] = h_run
            pend_s[d, hf] = p_run

    h0 = h0_ref[0, 0]
    init = tuple(h0[d:d + 1, LANES * hf: LANES * (hf + 1)] for d in range(2) for hf in range(nh))

    def body(k, carry):
        out = []
        for idx, cur in enumerate(carry):
            d, hf = divmod(idx, nh)
            kk = k if d == 0 else nblk - 1 - k
            cin_s[d, hf, pl.ds(kk, 1), :] = cur
            out.append(pend_s[d, hf, pl.ds(kk, 1), :] * cur + hend_s[d, hf, pl.ds(kk, 1), :])
        return tuple(out)

    final = lax.fori_loop(0, nblk, body, init)

    y = None
    for d in range(2):
        for hf in range(nh):
            cin = cin_s[d, hf]
            for s in range(LRU_SEG):
                full_h = hloc_s[d, hf, s] + ploc_s[d, hf, s] * cin
                if d == 0:
                    y_s[hf, pl.ds(s, nblk, stride=LRU_SEG), :] = full_h
                else:
                    y_s[hf, pl.ds(s, nblk, stride=LRU_SEG), :] += full_h
    y = jnp.concatenate([y_s[hf] for hf in range(nh)], axis=1) * jax.nn.gelu(ga)
    ya_ref[0] = y.astype(BF16)
    st_ref[0, 0, 0:1, :] = jnp.concatenate(final[:nh], axis=1)
    st_ref[0, 0, 1:2, :] = jnp.concatenate(final[nh:], axis=1)


def _lru_call(ulru, conv_w, conv_b, w_gates_b, b_gates, lam, h0, h0_layer, st_layer=0, st_depth=1, prev_state=None):
    b, n, _ = ulru.shape
    c = 256
    full = lambda shape: pl.BlockSpec(shape, lambda i: (0,) * len(shape))
    nh, nblk = c // LANES, n // LRU_SEG
    assert n % LRU_SEG == 0 and nblk % 8 == 0
    scratch = ([pltpu.VMEM((nh, n, LANES), F32)] * 3
               + [pltpu.VMEM((2, nh, LRU_SEG, nblk, LANES), F32)] * 2
               + [pltpu.VMEM((2, nh, nblk, LANES), F32)] * 3)
    in_specs = [pl.BlockSpec((1, n, 2 * c), lambda i: (i, 0, 0)),
                full((CONV_W, c)), full((1, c)), full((c, 4 * c)), full((1, 4 * c)), full((2, c)),
                pl.BlockSpec((1, 1, 2, c), lambda i: (i, h0_layer, 0, 0))]
    args = [ulru, conv_w, conv_b, w_gates_b, b_gates, lam, h0]
    aliases = {}
    if prev_state is not None:
        aliases[len(args)] = 1
        in_specs.append(pl.BlockSpec(memory_space=pl.ANY))
        args.append(prev_state)
    return pl.pallas_call(
        functools.partial(_lru_kernel, n=n),
        grid=(b,),
        in_specs=in_specs,
        out_specs=[pl.BlockSpec((1, n, c), lambda i: (i, 0, 0)),
                   pl.BlockSpec((1, 1, 2, c), lambda i: (i, st_layer, 0, 0))],
        out_shape=[jax.ShapeDtypeStruct((b, n, c), BF16),
                   jax.ShapeDtypeStruct((b, st_depth, 2, c), F32)],
        scratch_shapes=scratch,
        input_output_aliases=aliases,
        compiler_params=_params(1),
        name="rglru",
    )(*args)


def _softmax_pv(parts, sink):
    mx = None
    for s, _, _ in parts:
        cur = jnp.max(s, axis=-1, keepdims=True)
        mx = cur if mx is None else jnp.maximum(mx, cur)
    if sink is not None:
        mx = jnp.maximum(mx, sink)
    den = None
    out = None
    for s, v, v_t in parts:
        p = jnp.exp(s - mx)
        cur = jnp.sum(p, axis=-1, keepdims=True)
        den = cur if den is None else den + cur
        o = _dot_nt(p.astype(BF16), v) if v_t else _dot(p.astype(BF16), v)
        out = o if out is None else out + o
    if sink is not None:
        den = den + jnp.exp(sink - mx)
    return out / den


def _swap_halves(x):
    return jnp.concatenate([x[:, HEAD_DIM:], x[:, :HEAD_DIM]], axis=1)


def _keep_half(x, lo_mask, half):
    zero = jnp.zeros_like(x)
    return jnp.where(lo_mask, x, zero) if half == 0 else jnp.where(lo_mask, zero, x)


def _ctx_attn_kernel(sink_ref, att_ref, ob_ref, oc_ref, *, n, rb):
    lo = lax.broadcasted_iota(jnp.int32, (n, LANES), 1) < HEAD_DIM
    for r in range(rb):
        for p in range(H_C // 2):
            qp = att_ref[r, :, QC0 + LANES * p: QC0 + LANES * (p + 1)]
            kp = att_ref[r, :, KC0 + LANES * p: KC0 + LANES * (p + 1)]
            vp = att_ref[r, :, VC0 + LANES * p: VC0 + LANES * (p + 1)]
            outs = [_softmax_pv([(_dot_nt(qp, _keep_half(kp, lo, half)), vp, False)], None) for half in range(2)]
            oc_ref[r, :, LANES * p: LANES * (p + 1)] = jnp.where(lo, outs[0], outs[1]).astype(BF16)
        kpair = att_ref[r, :, KB0: KB0 + LANES]
        vpair = att_ref[r, :, VB0: VB0 + LANES]
        kpair_sw = _swap_halves(kpair)
        vpair_sw = _swap_halves(vpair)
        for p in range(H_B // 2):
            qp = att_ref[r, :, QB0 + LANES * p: QB0 + LANES * (p + 1)]
            outs = []
            for half in range(2):
                h = 2 * p + half
                aligned = (h // G_B) == half
                ksrc, vsrc = (kpair, vpair) if aligned else (kpair_sw, vpair_sw)
                outs.append(_softmax_pv([(_dot_nt(qp, _keep_half(ksrc, lo, half)), vsrc, False)], sink_ref[h]))
            ob_ref[r, :, LANES * p: LANES * (p + 1)] = jnp.where(lo, outs[0], outs[1]).astype(BF16)


def _ctx_attn_call(sink6, att):
    b, n, w = att.shape
    rb = CTX_ATTN_REQS
    assert b % rb == 0 and KV_B == 2
    return pl.pallas_call(
        functools.partial(_ctx_attn_kernel, n=n, rb=rb),
        grid=(b // rb,),
        in_specs=[pl.BlockSpec(memory_space=pltpu.SMEM),
                  pl.BlockSpec((rb, n, w), lambda i: (i, 0, 0))],
        out_specs=[pl.BlockSpec((rb, n, 384), lambda i: (i, 0, 0))] * 2,
        out_shape=[jax.ShapeDtypeStruct((b, n, 384), BF16)] * 2,
        compiler_params=_params(1),
        name="ctx_attn",
    )(sink6, att)


def _win_attn_kernel(sink_ref, qkv_ref, kc_ref, vc_ref, o_ref, *, n):
    j = pl.program_id(1)
    nloc = 3 * WIN_BLK
    q0 = pl.multiple_of(j * WIN_BLK, WIN_BLK)
    ks = pl.multiple_of(jnp.clip((j - 1) * WIN_BLK, 0, n - nloc), WIN_BLK)
    row = lax.broadcasted_iota(jnp.int32, (WIN_BLK, nloc), 0)
    col = lax.broadcasted_iota(jnp.int32, (WIN_BLK, nloc), 1)
    in_window = jnp.where(jnp.abs((q0 + row) - (ks + col)) <= WINDOW, 0.0, NEG_INF)
    band = jnp.concatenate([in_window] * G_B, axis=0)
    rowh = lax.broadcasted_iota(jnp.int32, (G_B * WIN_BLK, 1), 0) // WIN_BLK
    for kv in range(KV_B):
        gw = G_B * HEAD_DIM
        qblk = qkv_ref[0, pl.ds(q0, WIN_BLK), QB0 + gw * kv: QB0 + gw * (kv + 1)]
        q3 = jnp.concatenate([qblk[:, 64 * g: 64 * (g + 1)] for g in range(G_B)], axis=0)
        kl = qkv_ref[0, pl.ds(ks, nloc), KB0 + 64 * kv: KB0 + 64 * (kv + 1)]
        vl = qkv_ref[0, pl.ds(ks, nloc), VB0 + 64 * kv: VB0 + 64 * (kv + 1)]
        kc_t = kc_ref[0, 0, kv].astype(BF16)
        vc_t = vc_ref[0, 0, kv].astype(BF16)
        s_loc = _dot_nt(q3, kl) + band
        s_ctx = _dot(q3, kc_t)
        sk = jnp.where(rowh == 0, sink_ref[3 * kv],
                       jnp.where(rowh == 1, sink_ref[3 * kv + 1], sink_ref[3 * kv + 2]))
        o = _softmax_pv([(s_loc, vl, False), (s_ctx, vc_t, True)], sk)
        for g in range(G_B):
            h = kv * G_B + g
            o_ref[0, :, 64 * h: 64 * (h + 1)] = o[g * WIN_BLK:(g + 1) * WIN_BLK].astype(BF16)


def _win_attn_call(sink6, qkv, cache_k, cache_v, layer):
    b, n, w = qkv.shape
    past = cache_k.shape[4]
    cache_spec = pl.BlockSpec((1, 1, KV_B, HEAD_DIM, past), lambda i, j: (i, layer, 0, 0, 0))
    return pl.pallas_call(
        functools.partial(_win_attn_kernel, n=n),
        grid=(b, n // WIN_BLK),
        in_specs=[pl.BlockSpec(memory_space=pltpu.SMEM),
                  pl.BlockSpec((1, n, w), lambda i, j: (i, 0, 0)),
                  cache_spec, cache_spec],
        out_specs=pl.BlockSpec((1, WIN_BLK, 384), lambda i, j: (i, j, 0)),
        out_shape=jax.ShapeDtypeStruct((b, n, 384), BF16),
        compiler_params=_params(2),
        name="win_attn",
    )(sink6, qkv, cache_k, cache_v)


def _nbr_window_start(g, rows):
    return jnp.clip(g * NBR_Q_ROWS - NA_ROWS // 2, 0, rows - NBR_K_ROWS)


def _nbr_attn_kernel(q_ref, k_ref, v_ref, rel_ref, kc_ref, vc_ref, o_ref, bias_s, *, n):
    g = pl.program_id(0)
    nq = NBR_Q_ROWS * GRID_W
    nk = NBR_K_ROWS * GRID_W
    rows = n // GRID_W
    ws = _nbr_window_start(g, rows)
    k0 = pl.multiple_of(ws * GRID_W, GRID_W)

    @pl.when(pl.program_id(1) == 0)
    def _():
        for i in range(NBR_Q_ROWS):
            qr = g * NBR_Q_ROWS + i
            rs = jnp.clip(qr - NA_ROWS // 2, 0, rows - NA_ROWS)
            for jp in range(NBR_K_ROWS // 2):
                tiles = []
                for j in (2 * jp, 2 * jp + 1):
                    kr = ws + j
                    valid = (kr >= rs) & (kr < rs + NA_ROWS)
                    d = jnp.clip(kr - qr + NA_ROWS - 1, 0, 2 * NA_ROWS - 2)
                    tiles.append(jnp.where(valid, rel_ref[0, :, pl.ds(d, 1)][:, 0], NEG_INF))
                bias_s[:, GRID_W * i: GRID_W * (i + 1), LANES * jp: LANES * (jp + 1)] = jnp.concatenate(tiles, axis=-1)
    lo_k = lax.broadcasted_iota(jnp.int32, (nk, LANES), 1) < HEAD_DIM
    lo_q = lax.broadcasted_iota(jnp.int32, (nq, LANES), 1) < HEAD_DIM
    zpad = jnp.zeros((HEAD_DIM, kc_ref.shape[4]), BF16)
    for p in range(H_C // 2):
        qp = q_ref[0, :, LANES * p: LANES * (p + 1)]
        kp = k_ref[0, pl.ds(k0, nk), LANES * p: LANES * (p + 1)]
        vp = v_ref[0, pl.ds(k0, nk), LANES * p: LANES * (p + 1)]
        outs = []
        for half in range(2):
            h = 2 * p + half
            kc_t = kc_ref[0, 0, h].astype(BF16)
            vc_t = vc_ref[0, 0, h].astype(BF16)
            kc_t = jnp.concatenate([kc_t, zpad] if half == 0 else [zpad, kc_t], axis=0)
            vc_t = jnp.concatenate([vc_t, zpad] if half == 0 else [zpad, vc_t], axis=0)
            s_loc = _dot_nt(qp, _keep_half(kp, lo_k, half)) + bias_s[h]
            s_ctx = _dot(qp, kc_t)
            outs.append(_softmax_pv([(s_loc, vp, False), (s_ctx, vc_t, True)], None))
        o_ref[0, :, LANES * p: LANES * (p + 1)] = jnp.where(lo_q, outs[0], outs[1]).astype(BF16)


def _nbr_attn_call(att, rel_tab, cache_k, cache_v, layer):
    b, n, w = att.shape
    past = cache_k.shape[4]
    nq = NBR_Q_ROWS * GRID_W
    nk = NBR_K_ROWS * GRID_W
    gw = H_C * HEAD_DIM
    rows = n // GRID_W
    assert (QC0, KC0, VC0) == (0, gw, 2 * gw) and n % nq == 0 and rows >= NBR_K_ROWS and NBR_K_ROWS % 2 == 0
    cache_spec = pl.BlockSpec((1, 1, H_C, HEAD_DIM, past), lambda g, i: (i, layer, 0, 0, 0))
    return pl.pallas_call(
        functools.partial(_nbr_attn_kernel, n=n),
        grid=(n // nq, b),
        in_specs=[pl.BlockSpec((1, nq, gw), lambda g, i: (i, g, 0)),
                  pl.BlockSpec((1, n, gw), lambda g, i: (i, 0, 1)),
                  pl.BlockSpec((1, n, gw), lambda g, i: (i, 0, 2)),
                  pl.BlockSpec((1,) + rel_tab.shape[1:], lambda g, i: (layer, 0, 0, 0, 0)),
                  cache_spec, cache_spec],
        out_specs=pl.BlockSpec((1, nq, gw), lambda g, i: (i, g, 0)),
        out_shape=jax.ShapeDtypeStruct((b, n, gw), BF16),
        scratch_shapes=[pltpu.VMEM((H_C, nq, nk), F32)],
        compiler_params=_params(2),
        name="nbr_attn",
    )(att, att, att, rel_tab, cache_k, cache_v)


def _outproj_kernel(x_ref, ya_ref, yb_ref, yc_ref, mod_ref, g_ref, wo_ref, wr_ref,
                    xn_ref, h2_ref, aff_ref):
    mix = jnp.concatenate([ya_ref[0], yb_ref[0], yc_ref[0]], axis=1)
    proj = _dot(mix, wo_ref[...])
    xn = x_ref[0] + mod_ref[0, 2:3, :] * proj
    xn_ref[0] = xn
    h2 = _rms_modulate(xn, g_ref[...], mod_ref[0, 3:4, :], mod_ref[0, 4:5, :]).astype(BF16)
    h2_ref[0] = h2
    logits = _dot(h2, wr_ref[...])
    lane = lax.broadcasted_iota(jnp.int32, logits.shape, 1)
    valid = lane < N_EXPERTS
    logits = jnp.where(valid, logits, NEG_INF)
    e = jnp.exp(logits - jnp.max(logits, axis=-1, keepdims=True))
    e = jnp.where(valid, e, 0.0)
    aff_ref[0] = e / jnp.sum(e, axis=-1, keepdims=True)


def _outproj_call(x3, ya, yb, yc, mod_g, gain, wo_b16, w_router_b):
    g, n, d = x3.shape
    tb = TOKEN_BLOCK
    tok = lambda w: pl.BlockSpec((1, tb, w), lambda i, j: (i, j, 0))
    full = lambda shape: pl.BlockSpec(shape, lambda i, j: (0,) * len(shape))
    return pl.pallas_call(
        _outproj_kernel,
        grid=(g, n // tb),
        in_specs=[tok(d), tok(256), tok(384), tok(384),
                  pl.BlockSpec((1, N_MOD, d), lambda i, j: (i, 0, 0)),
                  full((1, d)), full((256 + 384 + 384, d)), full((d, LANES))],
        out_specs=[tok(d), tok(d), tok(LANES)],
        out_shape=[jax.ShapeDtypeStruct((g, n, d), F32),
                   jax.ShapeDtypeStruct((g, n, d), BF16),
                   jax.ShapeDtypeStruct((g, n, LANES), F32)],
        compiler_params=_params(2),
        name="outproj_router",
    )(x3, ya, yb, yc, mod_g, gain, wo_b16, w_router_b)


def _sort_by_affinity(keys, idx, pos, levels, seg_lanes):
    rows = keys.shape[0]
    sign = [jnp.where(((pos >> m) & 1) == 0, 1.0, -1.0) for m in range(levels)]

    def partner(x, j):
        d = 1 << j
        low = sign[j] > 0.0
        if d >= rows:
            sh = (d // rows) * seg_lanes
            return jnp.where(low, pltpu.roll(x, LANES - sh, 1), pltpu.roll(x, sh, 1))
        if d >= 8:
            x4 = x.reshape(rows // (2 * d), 2, d, LANES)
            return jnp.concatenate([x4[:, 1:2], x4[:, 0:1]], axis=1).reshape(rows, LANES)
        x3 = x.reshape(rows // 8, 8, LANES)
        low3 = sign[j].reshape(rows // 8, 8, LANES) > 0.0
        return jnp.where(low3, pltpu.roll(x3, 8 - d, 1), pltpu.roll(x3, d, 1)).reshape(rows, LANES)

    for k in range(1, levels + 1):
        for j in range(k - 1, -1, -1):
            pk, pi = partner(keys, j), partner(idx, j)
            before = jnp.where(keys == pk, jnp.where(idx < pi, 1.0, -1.0), jnp.where(keys > pk, 1.0, -1.0))
            want_first = sign[j] * sign[k] if k < levels else sign[j]
            keep = before * want_first > 0.0
            keys = jnp.where(keep, keys, pk)
            idx = jnp.where(keep, idx, pi)
    return keys, idx


def _route_kernel(aff_ref, h2_ref, xs_ref, vals_ref, rank_ref, *, n, cap, rb):
    ne = N_EXPERTS
    rows = ROUTE_SORT_ROWS
    nseg = n // rows
    seg_lanes = rb * ne
    levels = n.bit_length() - 1
    fcap = float(cap)
    row = lax.broadcasted_iota(jnp.int32, (rows, LANES), 0)
    lane = lax.broadcasted_iota(jnp.int32, (rows, LANES), 1)
    pos = (lane // seg_lanes) * rows + row
    keys = jnp.full((rows, LANES), -1.0, F32)
    for r in range(rb):
        for seg in range(nseg):
            off = seg * seg_lanes + r * ne
            piece = aff_ref[r, seg * rows:(seg + 1) * rows, :]
            if off:
                piece = pltpu.roll(piece, off, 1)
            keys = jnp.where((lane >= off) & (lane < off + ne), piece, keys)
    keys, idx = _sort_by_affinity(keys, pos.astype(F32), pos, levels, seg_lanes)
    top_keys = keys[:cap]
    top_idx = idx[:cap]

    tok = lax.broadcasted_iota(jnp.int32, (cap, n), 1).astype(F32)
    weight = fcap - lax.broadcasted_iota(jnp.int32, (cap, n), 0).astype(F32)
    pad = jnp.full((LANES - ne, n), fcap, F32)
    for r in range(rb):
        picks, rank_rows = [], []
        for e in range(ne):
            col = r * ne + e
            onehot = jnp.where(top_idx[:, col:col + 1] == tok, 1.0, 0.0)
            vals_ref[e, r] = top_keys[:, col:col + 1]
            rank_rows.append(fcap - jnp.sum(onehot * weight, axis=0, keepdims=True))
            picks.append(onehot.astype(BF16))
        rank_ref[r] = jnp.concatenate(rank_rows + [pad], axis=0).T
        xs = _dot(jnp.concatenate(picks, axis=0), h2_ref[r])
        xs_ref[:, r] = xs.reshape(ne, cap, xs.shape[-1]).astype(BF16)


def _route_call(aff, h2, rb):
    b, n, d = h2.shape
    cap = max(1, EC_FACTOR * n // N_EXPERTS)
    nseg = n // ROUTE_SORT_ROWS
    assert b % rb == 0 and n == nseg * ROUTE_SORT_ROWS and n & (n - 1) == 0
    assert nseg * rb * N_EXPERTS <= LANES and cap <= ROUTE_SORT_ROWS
    return pl.pallas_call(
        functools.partial(_route_kernel, n=n, cap=cap, rb=rb),
        grid=(b // rb,),
        in_specs=[pl.BlockSpec((rb, n, LANES), lambda i: (i, 0, 0)),
                  pl.BlockSpec((rb, n, d), lambda i: (i, 0, 0))],
        out_specs=[pl.BlockSpec((N_EXPERTS, rb, cap, d), lambda i: (0, i, 0, 0)),
                   pl.BlockSpec((N_EXPERTS, rb, cap, 1), lambda i: (0, i, 0, 0)),
                   pl.BlockSpec((rb, n, LANES), lambda i: (i, 0, 0))],
        out_shape=[jax.ShapeDtypeStruct((N_EXPERTS, b, cap, d), BF16),
                   jax.ShapeDtypeStruct((N_EXPERTS, b, cap, 1), F32),
                   jax.ShapeDtypeStruct((b, n, LANES), F32)],
        compiler_params=_params(1),
        name="route_gather",
    )(aff, h2)


def _expert_kernel(xc_ref, xl_ref, vc_ref, vl_ref, wg_ref, wu_ref, wd_ref, yc_ref, yl_ref, acc_c, acc_l):
    f = pl.program_id(1)

    def step(first):
        wg = wg_ref[0, 0].astype(BF16)
        wu = wu_ref[0, 0].astype(BF16)
        wd = wd_ref[0, 0].astype(BF16)
        for x_ref, v_ref, y_ref, acc in ((xc_ref, vc_ref, yc_ref, acc_c), (xl_ref, vl_ref, yl_ref, acc_l)):
            rows = x_ref.shape[1]
            chunk = min(EXPERT_ROW_CHUNK, rows)
            for c0 in range(0, rows, chunk):
                sl = slice(c0, c0 + chunk)
                x = x_ref[0, sl, :]
                a = _dot(x, wg)
                u = _dot(x, wu)
                act = ((a * jax.nn.sigmoid(a)) * u).astype(BF16)
                part = _dot(act, wd)
                if first:
                    acc[sl, :] = part
                else:
                    y_ref[0, sl, :] = ((acc[sl, :] + part) * v_ref[0, sl, :]).astype(BF16)

    pl.when(f == 0)(functools.partial(step, True))
    pl.when(f == EXPERT_F_STEPS - 1)(functools.partial(step, False))


def _expert_call(xs_c, xs_l, vals_c, vals_l, w_gate, w_up, w_down, layer):
    e, rc, d = xs_c.shape
    rl = xs_l.shape[1]
    f_total = w_gate.shape[-1]
    assert EXPERT_F_STEPS == 2
    fb = f_total // EXPERT_F_STEPS
    per_e = lambda rows, w: pl.BlockSpec((1, rows, w), lambda i, j: (i, 0, 0))
    return pl.pallas_call(
        _expert_kernel,
        grid=(e, EXPERT_F_STEPS),
        in_specs=[per_e(rc, d), per_e(rl, d), per_e(rc, 1), per_e(rl, 1),
                  pl.BlockSpec((1, 1, d, fb), lambda i, j: (layer, i, 0, j)),
                  pl.BlockSpec((1, 1, d, fb), lambda i, j: (layer, i, 0, j)),
                  pl.BlockSpec((1, 1, fb, d), lambda i, j: (layer, i, j, 0))],
        out_specs=[per_e(rc, d), per_e(rl, d)],
        out_shape=[jax.ShapeDtypeStruct((e, rc, d), BF16),
                   jax.ShapeDtypeStruct((e, rl, d), BF16)],
        scratch_shapes=[pltpu.VMEM((rc, d), F32), pltpu.VMEM((rl, d), F32)],
        compiler_params=_params(2),
        name="experts",
    )(xs_c, xs_l, vals_c, vals_l, w_gate, w_up, w_down)


def _combine_kernel(y_ref, rank_ref, xn_ref, mod_ref, gf_ref, o_ref, *, n, cap, final, rb):
    ec = N_EXPERTS * cap
    d = xn_ref.shape[-1]
    ce = lax.broadcasted_iota(jnp.int32, (LANES, ec), 1) // cap
    ee = lax.broadcasted_iota(jnp.int32, (LANES, ec), 0)
    expand = jnp.where(ce == ee, 1.0, 0.0).astype(BF16)
    slot = (lax.broadcasted_iota(jnp.int32, (n, ec), 1) % cap).astype(F32)
    for r in range(rb):
        rank = rank_ref[r].astype(BF16)
        rexp = _dot(rank, expand)
        onehot = jnp.where(rexp == slot, 1.0, 0.0).astype(BF16)
        moe = _dot(onehot, y_ref[:, r].reshape(ec, d))
        x = xn_ref[r] + mod_ref[0, 5:6, :] * moe
        if final:
            x = (x * lax.rsqrt(jnp.mean(x * x, axis=-1, keepdims=True) + EPS)) * gf_ref[...]
        o_ref[r] = x


def _combine_call(y4, rank, xn, mod_g, g_final, final):
    e, b, cap, d = y4.shape
    n = xn.shape[1]
    shared_mod = mod_g.shape[0] == 1
    rb = COMBINE_REQS_CTX if shared_mod else 1
    assert b % rb == 0
    return pl.pallas_call(
        functools.partial(_combine_kernel, n=n, cap=cap, final=final, rb=rb),
        grid=(b // rb,),
        in_specs=[pl.BlockSpec((e, rb, cap, d), lambda i: (0, i, 0, 0)),
                  pl.BlockSpec((rb, n, LANES), lambda i: (i, 0, 0)),
                  pl.BlockSpec((rb, n, d), lambda i: (i, 0, 0)),
                  pl.BlockSpec((1, N_MOD, d), (lambda i: (0, 0, 0)) if shared_mod else (lambda i: (i, 0, 0))),
                  pl.BlockSpec((1, d), lambda i: (0, 0))],
        out_specs=pl.BlockSpec((rb, n, d), lambda i: (i, 0, 0)),
        out_shape=jax.ShapeDtypeStruct((b, n, d), F32),
        compiler_params=_params(1),
        name="combine",
    )(y4, rank, xn, mod_g, g_final)


def _rope_tables(n):
    t = jnp.arange(n, dtype=jnp.int32)
    row = (t // GRID_W).astype(F32)
    col = (t % GRID_W).astype(F32)
    half = HEAD_DIM // 4
    inv = jnp.power(ROPE_BASE, -jnp.arange(half, dtype=F32) / half)
    ang_r = row[:, None] * inv[None, :]
    ang_c = col[:, None] * inv[None, :]
    cos_h = jnp.concatenate([jnp.cos(ang_r)] * 2 + [jnp.cos(ang_c)] * 2, axis=-1)
    sin_h = jnp.concatenate([-jnp.sin(ang_r), jnp.sin(ang_r), -jnp.sin(ang_c), jnp.sin(ang_c)], axis=-1)
    reps = 512 // HEAD_DIM
    return jnp.tile(cos_h, (1, reps)), jnp.tile(sin_h, (1, reps))


def _nbr_rel_tables(rpb):
    col = np.arange(GRID_W)
    cs = np.clip(col - NA_COLS // 2, 0, GRID_W - NA_COLS)
    col_mask = (col[None, :] >= cs[:, None]) & (col[None, :] < cs[:, None] + NA_COLS)
    dc_idx = np.clip(col[None, :] - col[:, None], -(NA_COLS - 1), NA_COLS - 1) + (NA_COLS - 1)
    sel_col = (dc_idx[None, :, :] == np.arange(2 * NA_COLS - 1)[:, None, None]).astype(np.float32)
    t = jnp.einsum("lhdc,cqk->lhdqk", rpb.astype(F32), sel_col, precision=lax.Precision.HIGHEST)
    return jnp.where(col_mask, t, NEG_INF)


def _block_diag(w):
    nb, bw, _ = w.shape
    eye = jnp.eye(nb, dtype=w.dtype)
    return (eye[:, None, :, None] * w[:, :, None, :]).reshape(nb * bw, nb * bw)


def kernel(x_prompt, x_sample, state_lru, cache_k_win, cache_v_win, cache_k_nbr, cache_v_nbr, c, c_ctx, w_mod, b_mod, g_norm1, w_in, conv_w, conv_b, w_gate_r, b_gate_r, w_gate_i, b_gate_i, lru_lambda, sink_logit, nbr_bias, w_out, g_norm2, w_router, w_exp_gate, w_exp_up, w_exp_down, g_final):
    bc, seq, d = x_prompt.shape
    bl, n_lat, _ = x_sample.shape
    depth = w_mod.shape[0]
    assert bl + 1 <= 8 and d == 1024

    cond8 = jnp.zeros((8, d), F32).at[0].set(c_ctx).at[1:1 + bl].set(c)
    mods = _adaln_call(cond8, w_mod, b_mod.reshape(depth, 1, -1)).reshape(depth, 8, N_MOD, d)
    rope_tabs = _rope_tables(n_lat)
    bias_tab = _nbr_rel_tables(nbr_bias)
    ckw_t, cvw_t, ckn_t, cvn_t = (jnp.swapaxes(a, -1, -2) for a in (cache_k_win, cache_v_win, cache_k_nbr, cache_v_nbr))
    gf = g_final.reshape(1, d)
    assert SCALE == 0.125
    new_cols = np.arange(w_in.shape[-1]) - LRU_W
    is_q = ((new_cols >= QC0) & (new_cols < KC0)) | ((new_cols >= QB0) & (new_cols < KB0))
    q_scale = jnp.asarray(np.where(is_q, SCALE, 1.0), F32)

    xc = x_prompt.reshape(1, bc * seq, d)
    xl = x_sample
    zeros_state = jnp.zeros((bc, 1, 2, 256), F32)
    caches = None
    st = None
    for l in range(depth):
        final = l == depth - 1
        mod_c = mods[l, 0:1]
        mod_l = mods[l, 1:1 + bl]
        g1 = g_norm1[l].reshape(1, d)
        g2 = g_norm2[l].reshape(1, d)
        w_l = w_in[l]
        w_in_b = (jnp.concatenate([w_l[:, :512], w_l[:, 1152:], w_l[:, 512:1152]], axis=1) * q_scale).astype(BF16)
        wo = w_out[l].astype(BF16)
        wr_b = jnp.zeros((d, LANES), BF16).at[:, :N_EXPERTS].set(w_router[l].astype(BF16))
        w_gates = jnp.concatenate([_block_diag(w_gate_r[l, 0]), _block_diag(w_gate_i[l, 0]),
                                   _block_diag(w_gate_r[l, 1]), _block_diag(w_gate_i[l, 1])], axis=1).astype(BF16)
        b_gates = jnp.concatenate([b_gate_r[l, 0], b_gate_i[l, 0], b_gate_r[l, 1], b_gate_i[l, 1]]).reshape(1, -1)
        cb = conv_b[l].reshape(1, -1)
        sink6 = sink_logit[l].reshape(-1)

        ulru_c, qkv_c, *caches = _inproj_call(xc, mod_c, g1, w_in_b, cache_seq=seq, layer=l, depth=depth,
                                              prev_caches=caches)
        ya_c, st = _lru_call(ulru_c.reshape(bc, seq, 512), conv_w[l], cb, w_gates, b_gates, lru_lambda[l],
                             zeros_state, 0, st_layer=l, st_depth=depth, prev_state=st)
        yb_c, yc_c = _ctx_attn_call(sink6, qkv_c.reshape(bc, seq, -1))
        xn_c, h2_c, aff_c = _outproj_call(xc, ya_c.reshape(1, bc * seq, -1), yb_c.reshape(1, bc * seq, -1),
                                          yc_c.reshape(1, bc * seq, -1), mod_c, g2, wo, wr_b)
        xs_c, vals_c, rank_c = _route_call(aff_c.reshape(bc, seq, LANES), h2_c.reshape(bc, seq, d), ROUTE_REQS_CTX)

        ulru_l, qkv_l = _inproj_call(xl, mod_l, g1, w_in_b, rope_tabs=rope_tabs)
        ya_l, _ = _lru_call(ulru_l, conv_w[l], cb, w_gates, b_gates, lru_lambda[l], state_lru, l)
        yb_l = _win_attn_call(sink6, qkv_l, ckw_t, cvw_t, l)
        yc_l = _nbr_attn_call(qkv_l, bias_tab, ckn_t, cvn_t, l)
        xn_l, h2_l, aff_l = _outproj_call(xl, ya_l, yb_l, yc_l, mod_l, g2, wo, wr_b)
        xs_l, vals_l, rank_l = _route_call(aff_l, h2_l, ROUTE_REQS_LAT)

        cap_c, cap_l = xs_c.shape[2], xs_l.shape[2]
        y_c, y_l = _expert_call(xs_c.reshape(N_EXPERTS, bc * cap_c, d), xs_l.reshape(N_EXPERTS, bl * cap_l, d),
                                vals_c.reshape(N_EXPERTS, bc * cap_c, 1), vals_l.reshape(N_EXPERTS, bl * cap_l, 1),
                                w_exp_gate, w_exp_up, w_exp_down, l)
        xc = _combine_call(y_c.reshape(N_EXPERTS, bc, cap_c, d), rank_c, xn_c.reshape(bc, seq, d), mod_c, gf, final)
        xl = _combine_call(y_l.reshape(N_EXPERTS, bl, cap_l, d), rank_l, xn_l, mod_l, gf, final)
        xc = xc.reshape(1, bc * seq, d)

    y_prompt = xc.reshape(bc, seq, d)
    y_sample = xl
    return (y_prompt, y_sample, st, *[jnp.swapaxes(buf, -1, -2) for buf in caches])
```

```python
import functools

import numpy as np
import jax
import jax.numpy as jnp
from jax import lax
from jax.experimental import pallas as pl
from jax.experimental.pallas import tpu as pltpu

F32 = jnp.float32
BF16 = jnp.bfloat16

HEAD_DIM = 64
GRID_W = 64
LRU_C = 8.0
CONV_W = 4
KV_B = 2
G_B = 3
H_B = KV_B * G_B
H_C = 6
WINDOW = 128
WIN_BLK = 128
NA_ROWS = 8
NA_COLS = 16
ROPE_BASE = 10000.0
N_EXPERTS = 16
EC_FACTOR = 2
N_MOD = 6
EPS = 1e-6
NEG_INF = -1e30
SCALE = HEAD_DIM ** -0.5

LANES = 128
VMEM_LIMIT_BYTES = 56 * 1024 * 1024

TOKEN_BLOCK = 1024
MOD_COL_BLOCK = 1536
EXPERT_F_STEPS = 2
EXPERT_ROW_CHUNK = 256
LRU_W = 512
QC0, KC0, VC0, QB0, KB0, VB0, ATT_W = 0, 384, 768, 1152, 1536, 1664, 1792
CTX_ATTN_REQS = 2
NBR_Q_ROWS = 4
NBR_K_ROWS = 12
ROUTE_REQS_CTX = 8
ROUTE_REQS_LAT = 2
ROUTE_SORT_ROWS = 256
COMBINE_REQS_CTX = 4
LRU_SEG = 8


def _params(n_axes):
    return pltpu.CompilerParams(dimension_semantics=("arbitrary",) * n_axes,
                                vmem_limit_bytes=VMEM_LIMIT_BYTES)


def _dot(a, b):
    return jnp.dot(a, b, preferred_element_type=F32)


def _dot_nt(a, b):
    return lax.dot_general(a, b, (((1,), (1,)), ((), ())), preferred_element_type=F32)


def _adaln_kernel(cond_ref, w_ref, b_ref, o_ref):
    c = cond_ref[...]
    s = c * jax.nn.sigmoid(c)
    o_ref[0] = _dot(s.astype(BF16), w_ref[0].astype(BF16)) + b_ref[0]


def _adaln_call(cond8, w_mod, b_mod3):
    depth, d, n = w_mod.shape
    nb = MOD_COL_BLOCK
    return pl.pallas_call(
        _adaln_kernel,
        grid=(depth, n // nb),
        in_specs=[pl.BlockSpec((8, d), lambda l, j: (0, 0)),
                  pl.BlockSpec((1, d, nb), lambda l, j: (l, 0, j)),
                  pl.BlockSpec((1, 1, nb), lambda l, j: (l, 0, j))],
        out_specs=pl.BlockSpec((1, 8, nb), lambda l, j: (l, 0, j)),
        out_shape=jax.ShapeDtypeStruct((depth, 8, n), F32),
        compiler_params=_params(2),
        name="adaln",
    )(cond8, w_mod, b_mod3)


def _rms_modulate(x, gain, shift, scale):
    y = x * lax.rsqrt(jnp.mean(x * x, axis=-1, keepdims=True) + EPS)
    return (y * gain) * (1.0 + scale) + shift


def _inproj_kernel(*refs, rope, cache, seq):
    if rope:
        x_ref, mod_ref, g_ref, w_ref, cos_ref, sin_ref, ulru_ref, qkv_ref = refs
    elif cache:
        x_ref, mod_ref, g_ref, w_ref = refs[:4]
        ulru_ref, qkv_ref, kw_ref, vw_ref, kn_ref, vn_ref = refs[-6:]
    else:
        x_ref, mod_ref, g_ref, w_ref, ulru_ref, qkv_ref = refs
    x = x_ref[0]
    h = _rms_modulate(x, g_ref[...], mod_ref[0, 0:1, :], mod_ref[0, 1:2, :])
    u = _dot(h.astype(BF16), w_ref[...])
    tb = u.shape[0]
    ulru_ref[0] = u[:, :LRU_W]
    if rope:
        lo, hi = LRU_W + QB0, LRU_W + VB0
        qk = u[:, lo:hi]
        lane = lax.broadcasted_iota(jnp.int32, (tb, hi - lo), 1)
        first = (lane & 31) < 16
        partner = jnp.where(first, pltpu.roll(qk, hi - lo - 16, 1), pltpu.roll(qk, 16, 1))
        qk = qk * cos_ref[...] + partner * sin_ref[...]
        qkv_ref[0, :, :QB0] = u[:, LRU_W:lo].astype(BF16)
        qkv_ref[0, :, QB0:VB0] = qk.astype(BF16)
        qkv_ref[0, :, VB0:] = u[:, hi:].astype(BF16)
    else:
        qkv_ref[0] = u[:, LRU_W:].astype(BF16)
    if cache:
        for r in range(tb // seq):
            rows = slice(r * seq, (r + 1) * seq)
            for ref, c0, heads in ((kw_ref, KB0, KV_B), (vw_ref, VB0, KV_B), (kn_ref, KC0, H_C), (vn_ref, VC0, H_C)):
                for p in range(heads // 2):
                    pair_t = u[rows, LRU_W + c0 + LANES * p: LRU_W + c0 + LANES * (p + 1)].T
                    ref[r, 0, 2 * p] = pair_t[:HEAD_DIM]
                    ref[r, 0, 2 * p + 1] = pair_t[HEAD_DIM:]


def _inproj_call(x3, mod_g, gain, w_in_b, rope_tabs=None, cache_seq=None, layer=0, depth=1, prev_caches=None):
    g, n, d = x3.shape
    d_in = w_in_b.shape[1]
    tb = TOKEN_BLOCK
    rope = rope_tabs is not None
    cache = cache_seq is not None
    in_specs = [pl.BlockSpec((1, tb, d), lambda i, j: (i, j, 0)),
                pl.BlockSpec((1, N_MOD, d), lambda i, j: (i, 0, 0)),
                pl.BlockSpec((1, d), lambda i, j: (0, 0)),
                pl.BlockSpec((d, d_in), lambda i, j: (0, 0))]
    args = [x3, mod_g, gain, w_in_b]
    assert d_in == LRU_W + ATT_W
    out_shape = [jax.ShapeDtypeStruct((g, n, LRU_W), F32),
                 jax.ShapeDtypeStruct((g, n, ATT_W), BF16)]
    out_specs = [pl.BlockSpec((1, tb, LRU_W), lambda i, j: (i, j, 0)),
                 pl.BlockSpec((1, tb, ATT_W), lambda i, j: (i, j, 0))]
    if rope:
        in_specs += [pl.BlockSpec((tb, VB0 - QB0), lambda i, j: (j, 0))] * 2
        args += list(rope_tabs)
    aliases = {}
    if cache:
        assert g == 1 and tb % cache_seq == 0
        rb = tb // cache_seq
        nreq = n // cache_seq
        for heads in (KV_B, KV_B, H_C, H_C):
            out_shape.append(jax.ShapeDtypeStruct((nreq, depth, heads, HEAD_DIM, cache_seq), F32))
            out_specs.append(pl.BlockSpec((rb, 1, heads, HEAD_DIM, cache_seq), lambda i, j: (j, layer, 0, 0, 0)))
        if prev_caches is not None:
            for k, buf in enumerate(prev_caches):
                aliases[len(args)] = 2 + k
                in_specs.append(pl.BlockSpec(memory_space=pl.ANY))
                args.append(buf)
    return pl.pallas_call(
        functools.partial(_inproj_kernel, rope=rope, cache=cache, seq=cache_seq),
        grid=(g, n // tb),
        in_specs=in_specs,
        out_specs=out_specs,
        out_shape=out_shape,
        input_output_aliases=aliases,
        compiler_params=_params(2),
        name="inproj_rope" if rope else "inproj_ctx",
    )(*args)


def _lru_kernel(*refs, n):
    u_ref, cw_ref, cb_ref, wg_ref, bg_ref, lam_ref, h0_ref = refs[:7]
    ya_ref, st_ref, a_s, x_s, y_s, hloc_s, ploc_s, hend_s, pend_s, cin_s = refs[-10:]
    c = 256
    nblk = n // LRU_SEG
    nh = c // LANES
    u = u_ref[0]
    xa = u[:, :c]
    ga = u[:, c:]
    t = lax.broadcasted_iota(jnp.int32, (n, c), 0)
    cw = cw_ref[...]
    xc = cw[2:3] * xa + cb_ref[...]
    xc = xc + cw[0:1] * jnp.where(t >= 2, pltpu.roll(xa, 2, 0), 0.0)
    xc = xc + cw[1:2] * jnp.where(t >= 1, pltpu.roll(xa, 1, 0), 0.0)
    xc = xc + cw[3:4] * jnp.where(t < n - 1, pltpu.roll(xa, n - 1, 0), 0.0)
    gates = _dot(xc.astype(BF16), wg_ref[...]) + bg_ref[...]
    lam = lam_ref[...]
    log_sig = jnp.minimum(lam, 0.0) - jnp.log1p(jnp.exp(-jnp.abs(lam)))
    for d in range(2):
        r = 0.5 * jnp.tanh(0.5 * gates[:, 2 * d * c:(2 * d + 1) * c]) + 0.5
        i = 0.5 * jnp.tanh(0.5 * gates[:, (2 * d + 1) * c:(2 * d + 2) * c]) + 0.5
        log_a = LRU_C * r * log_sig[d:d + 1]
        a = jnp.exp(log_a)
        t_in = jnp.tanh(-log_a) * (1.0 + a * a)
        xin = jnp.where(t_in > 0.0, t_in * lax.rsqrt(t_in), 0.0) * (i * xc)
        for hf in range(nh):
            a_s[hf] = a[:, LANES * hf: LANES * (hf + 1)]
            x_s[hf] = xin[:, LANES * hf: LANES * (hf + 1)]
        order = range(LRU_SEG) if d == 0 else range(LRU_SEG - 1, -1, -1)
        for hf in range(nh):
            h_run = p_run = None
            for s in order:
                a_row = a_s[hf, pl.ds(s, nblk, stride=LRU_SEG), :]
                x_row = x_s[hf, pl.ds(s, nblk, stride=LRU_SEG), :]
                h_run = x_row if h_run is None else a_row * h_run + x_row
                p_run = a_row if p_run is None else a_row * p_run
                hloc_s[d, hf, s] = h_run
                ploc_s[d, hf, s] = p_run
            hend_s[d, hf] = h_run
            pend_s[d, hf] = p_run

    h0 = h0_ref[0, 0]
    init = tuple(h0[d:d + 1, LANES * hf: LANES * (hf + 1)] for d in range(2) for hf in range(nh))

    def body(k, carry):
        out = []
        for idx, cur in enumerate(carry):
            d, hf = divmod(idx, nh)
            kk = k if d == 0 else nblk - 1 - k
            cin_s[d, hf, pl.ds(kk, 1), :] = cur
            out.append(pend_s[d, hf, pl.ds(kk, 1), :] * cur + hend_s[d, hf, pl.ds(kk, 1), :])
        return tuple(out)

    final = lax.fori_loop(0, nblk, body, init)

    y = None
    for d in range(2):
        for hf in range(nh):
            cin = cin_s[d, hf]
            for s in range(LRU_SEG):
                full_h = hloc_s[d, hf, s] + ploc_s[d, hf, s] * cin
                if d == 0:
                    y_s[hf, pl.ds(s, nblk, stride=LRU_SEG), :] = full_h
                else:
                    y_s[hf, pl.ds(s, nblk, stride=LRU_SEG), :] += full_h
    y = jnp.concatenate([y_s[hf] for hf in range(nh)], axis=1) * jax.nn.gelu(ga)
    ya_ref[0] = y.astype(BF16)
    st_ref[0, 0, 0:1, :] = jnp.concatenate(final[:nh], axis=1)
    st_ref[0, 0, 1:2, :] = jnp.concatenate(final[nh:], axis=1)


def _lru_call(ulru, conv_w, conv_b, w_gates_b, b_gates, lam, h0, h0_layer, st_layer=0, st_depth=1, prev_state=None):
    b, n, _ = ulru.shape
    c = 256
    full = lambda shape: pl.BlockSpec(shape, lambda i: (0,) * len(shape))
    nh, nblk = c // LANES, n // LRU_SEG
    assert n % LRU_SEG == 0 and nblk % 8 == 0
    scratch = ([pltpu.VMEM((nh, n, LANES), F32)] * 3
               + [pltpu.VMEM((2, nh, LRU_SEG, nblk, LANES), F32)] * 2
               + [pltpu.VMEM((2, nh, nblk, LANES), F32)] * 3)
    in_specs = [pl.BlockSpec((1, n, 2 * c), lambda i: (i, 0, 0)),
                full((CONV_W, c)), full((1, c)), full((c, 4 * c)), full((1, 4 * c)), full((2, c)),
                pl.BlockSpec((1, 1, 2, c), lambda i: (i, h0_layer, 0, 0))]
    args = [ulru, conv_w, conv_b, w_gates_b, b_gates, lam, h0]
    aliases = {}
    if prev_state is not None:
        aliases[len(args)] = 1
        in_specs.append(pl.BlockSpec(memory_space=pl.ANY))
        args.append(prev_state)
    return pl.pallas_call(
        functools.partial(_lru_kernel, n=n),
        grid=(b,),
        in_specs=in_specs,
        out_specs=[pl.BlockSpec((1, n, c), lambda i: (i, 0, 0)),
                   pl.BlockSpec((1, 1, 2, c), lambda i: (i, st_layer, 0, 0))],
        out_shape=[jax.ShapeDtypeStruct((b, n, c), BF16),
                   jax.ShapeDtypeStruct((b, st_depth, 2, c), F32)],
        scratch_shapes=scratch,
        input_output_aliases=aliases,
        compiler_params=_params(1),
        name="rglru",
    )(*args)


def _softmax_pv(parts, sink):
    mx = None
    for s, _, _ in parts:
        cur = jnp.max(s, axis=-1, keepdims=True)
        mx = cur if mx is None else jnp.maximum(mx, cur)
    if sink is not None:
        mx = jnp.maximum(mx, sink)
    den = None
    out = None
    for s, v, v_t in parts:
        p = jnp.exp(s - mx)
        cur = jnp.sum(p, axis=-1, keepdims=True)
        den = cur if den is None else den + cur
        o = _dot_nt(p.astype(BF16), v) if v_t else _dot(p.astype(BF16), v)
        out = o if out is None else out + o
    if sink is not None:
        den = den + jnp.exp(sink - mx)
    return out / den


def _swap_halves(x):
    return jnp.concatenate([x[:, HEAD_DIM:], x[:, :HEAD_DIM]], axis=1)


def _keep_half(x, lo_mask, half):
    zero = jnp.zeros_like(x)
    return jnp.where(lo_mask, x, zero) if half == 0 else jnp.where(lo_mask, zero, x)


def _ctx_attn_kernel(sink_ref, att_ref, ob_ref, oc_ref, *, n, rb):
    lo = lax.broadcasted_iota(jnp.int32, (n, LANES), 1) < HEAD_DIM
    for r in range(rb):
        for p in range(H_C // 2):
            qp = att_ref[r, :, QC0 + LANES * p: QC0 + LANES * (p + 1)]
            kp = att_ref[r, :, KC0 + LANES * p: KC0 + LANES * (p + 1)]
            vp = att_ref[r, :, VC0 + LANES * p: VC0 + LANES * (p + 1)]
            outs = [_softmax_pv([(_dot_nt(qp, _keep_half(kp, lo, half)), vp, False)], None) for half in range(2)]
            oc_ref[r, :, LANES * p: LANES * (p + 1)] = jnp.where(lo, outs[0], outs[1]).astype(BF16)
        kpair = att_ref[r, :, KB0: KB0 + LANES]
        vpair = att_ref[r, :, VB0: VB0 + LANES]
        kpair_sw = _swap_halves(kpair)
        vpair_sw = _swap_halves(vpair)
        for p in range(H_B // 2):
            qp = att_ref[r, :, QB0 + LANES * p: QB0 + LANES * (p + 1)]
            outs = []
            for half in range(2):
                h = 2 * p + half
                aligned = (h // G_B) == half
                ksrc, vsrc = (kpair, vpair) if aligned else (kpair_sw, vpair_sw)
                outs.append(_softmax_pv([(_dot_nt(qp, _keep_half(ksrc, lo, half)), vsrc, False)], sink_ref[h]))
            ob_ref[r, :, LANES * p: LANES * (p + 1)] = jnp.where(lo, outs[0], outs[1]).astype(BF16)


def _ctx_attn_call(sink6, att):
    b, n, w = att.shape
    rb = CTX_ATTN_REQS
    assert b % rb == 0 and KV_B == 2
    return pl.pallas_call(
        functools.partial(_ctx_attn_kernel, n=n, rb=rb),
        grid=(b // rb,),
        in_specs=[pl.BlockSpec(memory_space=pltpu.SMEM),
                  pl.BlockSpec((rb, n, w), lambda i: (i, 0, 0))],
        out_specs=[pl.BlockSpec((rb, n, 384), lambda i: (i, 0, 0))] * 2,
        out_shape=[jax.ShapeDtypeStruct((b, n, 384), BF16)] * 2,
        compiler_params=_params(1),
        name="ctx_attn",
    )(sink6, att)


def _win_attn_kernel(sink_ref, qkv_ref, kc_ref, vc_ref, o_ref, *, n):
    j = pl.program_id(1)
    nloc = 3 * WIN_BLK
    q0 = pl.multiple_of(j * WIN_BLK, WIN_BLK)
    ks = pl.multiple_of(jnp.clip((j - 1) * WIN_BLK, 0, n - nloc), WIN_BLK)
    row = lax.broadcasted_iota(jnp.int32, (WIN_BLK, nloc), 0)
    col = lax.broadcasted_iota(jnp.int32, (WIN_BLK, nloc), 1)
    in_window = jnp.where(jnp.abs((q0 + row) - (ks + col)) <= WINDOW, 0.0, NEG_INF)
    band = jnp.concatenate([in_window] * G_B, axis=0)
    rowh = lax.broadcasted_iota(jnp.int32, (G_B * WIN_BLK, 1), 0) // WIN_BLK
    for kv in range(KV_B):
        gw = G_B * HEAD_DIM
        qblk = qkv_ref[0, pl.ds(q0, WIN_BLK), QB0 + gw * kv: QB0 + gw * (kv + 1)]
        q3 = jnp.concatenate([qblk[:, 64 * g: 64 * (g + 1)] for g in range(G_B)], axis=0)
        kl = qkv_ref[0, pl.ds(ks, nloc), KB0 + 64 * kv: KB0 + 64 * (kv + 1)]
        vl = qkv_ref[0, pl.ds(ks, nloc), VB0 + 64 * kv: VB0 + 64 * (kv + 1)]
        kc_t = kc_ref[0, 0, kv].astype(BF16)
        vc_t = vc_ref[0, 0, kv].astype(BF16)
        s_loc = _dot_nt(q3, kl) + band
        s_ctx = _dot(q3, kc_t)
        sk = jnp.where(rowh == 0, sink_ref[3 * kv],
                       jnp.where(rowh == 1, sink_ref[3 * kv + 1], sink_ref[3 * kv + 2]))
        o = _softmax_pv([(s_loc, vl, False), (s_ctx, vc_t, True)], sk)
        for g in range(G_B):
            h = kv * G_B + g
            o_ref[0, :, 64 * h: 64 * (h + 1)] = o[g * WIN_BLK:(g + 1) * WIN_BLK].astype(BF16)


def _win_attn_call(sink6, qkv, cache_k, cache_v, layer):
    b, n, w = qkv.shape
    past = cache_k.shape[4]
    cache_spec = pl.BlockSpec((1, 1, KV_B, HEAD_DIM, past), lambda i, j: (i, layer, 0, 0, 0))
    return pl.pallas_call(
        functools.partial(_win_attn_kernel, n=n),
        grid=(b, n // WIN_BLK),
        in_specs=[pl.BlockSpec(memory_space=pltpu.SMEM),
                  pl.BlockSpec((1, n, w), lambda i, j: (i, 0, 0)),
                  cache_spec, cache_spec],
        out_specs=pl.BlockSpec((1, WIN_BLK, 384), lambda i, j: (i, j, 0)),
        out_shape=jax.ShapeDtypeStruct((b, n, 384), BF16),
        compiler_params=_params(2),
        name="win_attn",
    )(sink6, qkv, cache_k, cache_v)


def _nbr_window_start(g, rows):
    return jnp.clip(g * NBR_Q_ROWS - NA_ROWS // 2, 0, rows - NBR_K_ROWS)


def _nbr_attn_kernel(q_ref, k_ref, v_ref, rel_ref, kc_ref, vc_ref, o_ref, bias_s, *, n):
    g = pl.program_id(0)
    nq = NBR_Q_ROWS * GRID_W
    nk = NBR_K_ROWS * GRID_W
    rows = n // GRID_W
    ws = _nbr_window_start(g, rows)
    k0 = pl.multiple_of(ws * GRID_W, GRID_W)

    @pl.when(pl.program_id(1) == 0)
    def _():
        for i in range(NBR_Q_ROWS):
            qr = g * NBR_Q_ROWS + i
            rs = jnp.clip(qr - NA_ROWS // 2, 0, rows - NA_ROWS)
            for jp in range(NBR_K_ROWS // 2):
                tiles = []
                for j in (2 * jp, 2 * jp + 1):
                    kr = ws + j
                    valid = (kr >= rs) & (kr < rs + NA_ROWS)
                    d = jnp.clip(kr - qr + NA_ROWS - 1, 0, 2 * NA_ROWS - 2)
                    tiles.append(jnp.where(valid, rel_ref[0, :, pl.ds(d, 1)][:, 0], NEG_INF))
                bias_s[:, GRID_W * i: GRID_W * (i + 1), LANES * jp: LANES * (jp + 1)] = jnp.concatenate(tiles, axis=-1)
    lo_k = lax.broadcasted_iota(jnp.int32, (nk, LANES), 1) < HEAD_DIM
    lo_q = lax.broadcasted_iota(jnp.int32, (nq, LANES), 1) < HEAD_DIM
    zpad = jnp.zeros((HEAD_DIM, kc_ref.shape[4]), BF16)
    for p in range(H_C // 2):
        qp = q_ref[0, :, LANES * p: LANES * (p + 1)]
        kp = k_ref[0, pl.ds(k0, nk), LANES * p: LANES * (p + 1)]
        vp = v_ref[0, pl.ds(k0, nk), LANES * p: LANES * (p + 1)]
        outs = []
        for half in range(2):
            h = 2 * p + half
            kc_t = kc_ref[0, 0, h].astype(BF16)
            vc_t = vc_ref[0, 0, h].astype(BF16)
            kc_t = jnp.concatenate([kc_t, zpad] if half == 0 else [zpad, kc_t], axis=0)
            vc_t = jnp.concatenate([vc_t, zpad] if half == 0 else [zpad, vc_t], axis=0)
            s_loc = _dot_nt(qp, _keep_half(kp, lo_k, half)) + bias_s[h]
            s_ctx = _dot(qp, kc_t)
            outs.append(_softmax_pv([(s_loc, vp, False), (s_ctx, vc_t, True)], None))
        o_ref[0, :, LANES * p: LANES * (p + 1)] = jnp.where(lo_q, outs[0], outs[1]).astype(BF16)


def _nbr_attn_call(att, rel_tab, cache_k, cache_v, layer):
    b, n, w = att.shape
    past = cache_k.shape[4]
    nq = NBR_Q_ROWS * GRID_W
    nk = NBR_K_ROWS * GRID_W
    gw = H_C * HEAD_DIM
    rows = n // GRID_W
    assert (QC0, KC0, VC0) == (0, gw, 2 * gw) and n % nq == 0 and rows >= NBR_K_ROWS and NBR_K_ROWS % 2 == 0
    cache_spec = pl.BlockSpec((1, 1, H_C, HEAD_DIM, past), lambda g, i: (i, layer, 0, 0, 0))
    return pl.pallas_call(
        functools.partial(_nbr_attn_kernel, n=n),
        grid=(n // nq, b),
        in_specs=[pl.BlockSpec((1, nq, gw), lambda g, i: (i, g, 0)),
                  pl.BlockSpec((1, n, gw), lambda g, i: (i, 0, 1)),
                  pl.BlockSpec((1, n, gw), lambda g, i: (i, 0, 2)),
                  pl.BlockSpec((1,) + rel_tab.shape[1:], lambda g, i: (layer, 0, 0, 0, 0)),
                  cache_spec, cache_spec],
        out_specs=pl.BlockSpec((1, nq, gw), lambda g, i: (i, g, 0)),
        out_shape=jax.ShapeDtypeStruct((b, n, gw), BF16),
        scratch_shapes=[pltpu.VMEM((H_C, nq, nk), F32)],
        compiler_params=_params(2),
        name="nbr_attn",
    )(att, att, att, rel_tab, cache_k, cache_v)


def _outproj_kernel(x_ref, ya_ref, yb_ref, yc_ref, mod_ref, g_ref, wo_ref, wr_ref,
                    xn_ref, h2_ref, aff_ref):
    mix = jnp.concatenate([ya_ref[0], yb_ref[0], yc_ref[0]], axis=1)
    proj = _dot(mix, wo_ref[...])
    xn = x_ref[0] + mod_ref[0, 2:3, :] * proj
    xn_ref[0] = xn
    h2 = _rms_modulate(xn, g_ref[...], mod_ref[0, 3:4, :], mod_ref[0, 4:5, :]).astype(BF16)
    h2_ref[0] = h2
    logits = _dot(h2, wr_ref[...])
    lane = lax.broadcasted_iota(jnp.int32, logits.shape, 1)
    valid = lane < N_EXPERTS
    logits = jnp.where(valid, logits, NEG_INF)
    e = jnp.exp(logits - jnp.max(logits, axis=-1, keepdims=True))
    e = jnp.where(valid, e, 0.0)
    aff_ref[0] = e / jnp.sum(e, axis=-1, keepdims=True)


def _outproj_call(x3, ya, yb, yc, mod_g, gain, wo_b16, w_router_b):
    g, n, d = x3.shape
    tb = TOKEN_BLOCK
    tok = lambda w: pl.BlockSpec((1, tb, w), lambda i, j: (i, j, 0))
    full = lambda shape: pl.BlockSpec(shape, lambda i, j: (0,) * len(shape))
    return pl.pallas_call(
        _outproj_kernel,
        grid=(g, n // tb),
        in_specs=[tok(d), tok(256), tok(384), tok(384),
                  pl.BlockSpec((1, N_MOD, d), lambda i, j: (i, 0, 0)),
                  full((1, d)), full((256 + 384 + 384, d)), full((d, LANES))],
        out_specs=[tok(d), tok(d), tok(LANES)],
        out_shape=[jax.ShapeDtypeStruct((g, n, d), F32),
                   jax.ShapeDtypeStruct((g, n, d), BF16),
                   jax.ShapeDtypeStruct((g, n, LANES), F32)],
        compiler_params=_params(2),
        name="outproj_router",
    )(x3, ya, yb, yc, mod_g, gain, wo_b16, w_router_b)


def _sort_by_affinity(keys, idx, pos, levels, seg_lanes):
    rows = keys.shape[0]
    sign = [jnp.where(((pos >> m) & 1) == 0, 1.0, -1.0) for m in range(levels)]

    def partner(x, j):
        d = 1 << j
        low = sign[j] > 0.0
        if d >= rows:
            sh = (d // rows) * seg_lanes
            return jnp.where(low, pltpu.roll(x, LANES - sh, 1), pltpu.roll(x, sh, 1))
        if d >= 8:
            x4 = x.reshape(rows // (2 * d), 2, d, LANES)
            return jnp.concatenate([x4[:, 1:2], x4[:, 0:1]], axis=1).reshape(rows, LANES)
        x3 = x.reshape(rows // 8, 8, LANES)
        low3 = sign[j].reshape(rows // 8, 8, LANES) > 0.0
        return jnp.where(low3, pltpu.roll(x3, 8 - d, 1), pltpu.roll(x3, d, 1)).reshape(rows, LANES)

    for k in range(1, levels + 1):
        for j in range(k - 1, -1, -1):
            pk, pi = partner(keys, j), partner(idx, j)
            before = jnp.where(keys == pk, jnp.where(idx < pi, 1.0, -1.0), jnp.where(keys > pk, 1.0, -1.0))
            want_first = sign[j] * sign[k] if k < levels else sign[j]
            keep = before * want_first > 0.0
            keys = jnp.where(keep, keys, pk)
            idx = jnp.where(keep, idx, pi)
    return keys, idx


def _route_kernel(aff_ref, h2_ref, xs_ref, vals_ref, rank_ref, *, n, cap, rb):
    ne = N_EXPERTS
    rows = ROUTE_SORT_ROWS
    nseg = n // rows
    seg_lanes = rb * ne
    levels = n.bit_length() - 1
    fcap = float(cap)
    row = lax.broadcasted_iota(jnp.int32, (rows, LANES), 0)
    lane = lax.broadcasted_iota(jnp.int32, (rows, LANES), 1)
    pos = (lane // seg_lanes) * rows + row
    keys = jnp.full((rows, LANES), -1.0, F32)
    for r in range(rb):
        for seg in range(nseg):
            off = seg * seg_lanes + r * ne
            piece = aff_ref[r, seg * rows:(seg + 1) * rows, :]
            if off:
                piece = pltpu.roll(piece, off, 1)
            keys = jnp.where((lane >= off) & (lane < off + ne), piece, keys)
    keys, idx = _sort_by_affinity(keys, pos.astype(F32), pos, levels, seg_lanes)
    top_keys = keys[:cap]
    top_idx = idx[:cap]

    tok = lax.broadcasted_iota(jnp.int32, (cap, n), 1).astype(F32)
    weight = fcap - lax.broadcasted_iota(jnp.int32, (cap, n), 0).astype(F32)
    pad = jnp.full((LANES - ne, n), fcap, F32)
    for r in range(rb):
        picks, rank_rows = [], []
        for e in range(ne):
            col = r * ne + e
            onehot = jnp.where(top_idx[:, col:col + 1] == tok, 1.0, 0.0)
            vals_ref[e, r] = top_keys[:, col:col + 1]
            rank_rows.append(fcap - jnp.sum(onehot * weight, axis=0, keepdims=True))
            picks.append(onehot.astype(BF16))
        rank_ref[r] = jnp.concatenate(rank_rows + [pad], axis=0).T
        xs = _dot(jnp.concatenate(picks, axis=0), h2_ref[r])
        xs_ref[:, r] = xs.reshape(ne, cap, xs.shape[-1]).astype(BF16)


def _route_call(aff, h2, rb):
    b, n, d = h2.shape
    cap = max(1, EC_FACTOR * n // N_EXPERTS)
    nseg = n // ROUTE_SORT_ROWS
    assert b % rb == 0 and n == nseg * ROUTE_SORT_ROWS and n & (n - 1) == 0
    assert nseg * rb * N_EXPERTS <= LANES and cap <= ROUTE_SORT_ROWS
    return pl.pallas_call(
        functools.partial(_route_kernel, n=n, cap=cap, rb=rb),
        grid=(b // rb,),
        in_specs=[pl.BlockSpec((rb, n, LANES), lambda i: (i, 0, 0)),
                  pl.BlockSpec((rb, n, d), lambda i: (i, 0, 0))],
        out_specs=[pl.BlockSpec((N_EXPERTS, rb, cap, d), lambda i: (0, i, 0, 0)),
                   pl.BlockSpec((N_EXPERTS, rb, cap, 1), lambda i: (0, i, 0, 0)),
                   pl.BlockSpec((rb, n, LANES), lambda i: (i, 0, 0))],
        out_shape=[jax.ShapeDtypeStruct((N_EXPERTS, b, cap, d), BF16),
                   jax.ShapeDtypeStruct((N_EXPERTS, b, cap, 1), F32),
                   jax.ShapeDtypeStruct((b, n, LANES), F32)],
        compiler_params=_params(1),
        name="route_gather",
    )(aff, h2)


def _expert_kernel(xc_ref, xl_ref, vc_ref, vl_ref, wg_ref, wu_ref, wd_ref, yc_ref, yl_ref, acc_c, acc_l):
    f = pl.program_id(1)

    def step(first):
        wg = wg_ref[0, 0].astype(BF16)
        wu = wu_ref[0, 0].astype(BF16)
        wd = wd_ref[0, 0].astype(BF16)
        for x_ref, v_ref, y_ref, acc in ((xc_ref, vc_ref, yc_ref, acc_c), (xl_ref, vl_ref, yl_ref, acc_l)):
            rows = x_ref.shape[1]
            chunk = min(EXPERT_ROW_CHUNK, rows)
            for c0 in range(0, rows, chunk):
                sl = slice(c0, c0 + chunk)
                x = x_ref[0, sl, :]
                a = _dot(x, wg)
                u = _dot(x, wu)
                act = ((a * jax.nn.sigmoid(a)) * u).astype(BF16)
                part = _dot(act, wd)
                if first:
                    acc[sl, :] = part
                else:
                    y_ref[0, sl, :] = ((acc[sl, :] + part) * v_ref[0, sl, :]).astype(BF16)

    pl.when(f == 0)(functools.partial(step, True))
    pl.when(f == EXPERT_F_STEPS - 1)(functools.partial(step, False))


def _expert_call(xs_c, xs_l, vals_c, vals_l, w_gate, w_up, w_down, layer):
    e, rc, d = xs_c.shape
    rl = xs_l.shape[1]
    f_total = w_gate.shape[-1]
    assert EXPERT_F_STEPS == 2
    fb = f_total // EXPERT_F_STEPS
    per_e = lambda rows, w: pl.BlockSpec((1, rows, w), lambda i, j: (i, 0, 0))
    return pl.pallas_call(
        _expert_kernel,
        grid=(e, EXPERT_F_STEPS),
        in_specs=[per_e(rc, d), per_e(rl, d), per_e(rc, 1), per_e(rl, 1),
                  pl.BlockSpec((1, 1, d, fb), lambda i, j: (layer, i, 0, j)),
                  pl.BlockSpec((1, 1, d, fb), lambda i, j: (layer, i, 0, j)),
                  pl.BlockSpec((1, 1, fb, d), lambda i, j: (layer, i, j, 0))],
        out_specs=[per_e(rc, d), per_e(rl, d)],
        out_shape=[jax.ShapeDtypeStruct((e, rc, d), BF16),
                   jax.ShapeDtypeStruct((e, rl, d), BF16)],
        scratch_shapes=[pltpu.VMEM((rc, d), F32), pltpu.VMEM((rl, d), F32)],
        compiler_params=_params(2),
        name="experts",
    )(xs_c, xs_l, vals_c, vals_l, w_gate, w_up, w_down)


def _combine_kernel(y_ref, rank_ref, xn_ref, mod_ref, gf_ref, o_ref, *, n, cap, final, rb):
    ec = N_EXPERTS * cap
    d = xn_ref.shape[-1]
    ce = lax.broadcasted_iota(jnp.int32, (LANES, ec), 1) // cap
    ee = lax.broadcasted_iota(jnp.int32, (LANES, ec), 0)
    expand = jnp.where(ce == ee, 1.0, 0.0).astype(BF16)
    slot = (lax.broadcasted_iota(jnp.int32, (n, ec), 1) % cap).astype(F32)
    for r in range(rb):
        rank = rank_ref[r].astype(BF16)
        rexp = _dot(rank, expand)
        onehot = jnp.where(rexp == slot, 1.0, 0.0).astype(BF16)
        moe = _dot(onehot, y_ref[:, r].reshape(ec, d))
        x = xn_ref[r] + mod_ref[0, 5:6, :] * moe
        if final:
            x = (x * lax.rsqrt(jnp.mean(x * x, axis=-1, keepdims=True) + EPS)) * gf_ref[...]
        o_ref[r] = x


def _combine_call(y4, rank, xn, mod_g, g_final, final):
    e, b, cap, d = y4.shape
    n = xn.shape[1]
    shared_mod = mod_g.shape[0] == 1
    rb = COMBINE_REQS_CTX if shared_mod else 1
    assert b % rb == 0
    return pl.pallas_call(
        functools.partial(_combine_kernel, n=n, cap=cap, final=final, rb=rb),
        grid=(b // rb,),
        in_specs=[pl.BlockSpec((e, rb, cap, d), lambda i: (0, i, 0, 0)),
                  pl.BlockSpec((rb, n, LANES), lambda i: (i, 0, 0)),
                  pl.BlockSpec((rb, n, d), lambda i: (i, 0, 0)),
                  pl.BlockSpec((1, N_MOD, d), (lambda i: (0, 0, 0)) if shared_mod else (lambda i: (i, 0, 0))),
                  pl.BlockSpec((1, d), lambda i: (0, 0))],
        out_specs=pl.BlockSpec((rb, n, d), lambda i: (i, 0, 0)),
        out_shape=jax.ShapeDtypeStruct((b, n, d), F32),
        compiler_params=_params(1),
        name="combine",
    )(y4, rank, xn, mod_g, g_final)


def _rope_tables(n):
    t = jnp.arange(n, dtype=jnp.int32)
    row = (t // GRID_W).astype(F32)
    col = (t % GRID_W).astype(F32)
    half = HEAD_DIM // 4
    inv = jnp.power(ROPE_BASE, -jnp.arange(half, dtype=F32) / half)
    ang_r = row[:, None] * inv[None, :]
    ang_c = col[:, None] * inv[None, :]
    cos_h = jnp.concatenate([jnp.cos(ang_r)] * 2 + [jnp.cos(ang_c)] * 2, axis=-1)
    sin_h = jnp.concatenate([-jnp.sin(ang_r), jnp.sin(ang_r), -jnp.sin(ang_c), jnp.sin(ang_c)], axis=-1)
    reps = 512 // HEAD_DIM
    return jnp.tile(cos_h, (1, reps)), jnp.tile(sin_h, (1, reps))


def _nbr_rel_tables(rpb):
    col = np.arange(GRID_W)
    cs = np.clip(col - NA_COLS // 2, 0, GRID_W - NA_COLS)
    col_mask = (col[None, :] >= cs[:, None]) & (col[None, :] < cs[:, None] + NA_COLS)
    dc_idx = np.clip(col[None, :] - col[:, None], -(NA_COLS - 1), NA_COLS - 1) + (NA_COLS - 1)
    sel_col = (dc_idx[None, :, :] == np.arange(2 * NA_COLS - 1)[:, None, None]).astype(np.float32)
    t = jnp.einsum("lhdc,cqk->lhdqk", rpb.astype(F32), sel_col, precision=lax.Precision.HIGHEST)
    return jnp.where(col_mask, t, NEG_INF)


def _block_diag(w):
    nb, bw, _ = w.shape
    eye = jnp.eye(nb, dtype=w.dtype)
    return (eye[:, None, :, None] * w[:, :, None, :]).reshape(nb * bw, nb * bw)


def kernel(x_prompt, x_sample, state_lru, cache_k_win, cache_v_win, cache_k_nbr, cache_v_nbr, c, c_ctx, w_mod, b_mod, g_norm1, w_in, conv_w, conv_b, w_gate_r, b_gate_r, w_gate_i, b_gate_i, lru_lambda, sink_logit, nbr_bias, w_out, g_norm2, w_router, w_exp_gate, w_exp_up, w_exp_down, g_final):
    bc, seq, d = x_prompt.shape
    bl, n_lat, _ = x_sample.shape
    depth = w_mod.shape[0]
    assert bl + 1 <= 8 and d == 1024

    cond8 = jnp.zeros((8, d), F32).at[0].set(c_ctx).at[1:1 + bl].set(c)
    mods = _adaln_call(cond8, w_mod, b_mod.reshape(depth, 1, -1)).reshape(depth, 8, N_MOD, d)
    rope_tabs = _rope_tables(n_lat)
    bias_tab = _nbr_rel_tables(nbr_bias)
    ckw_t, cvw_t, ckn_t, cvn_t = (jnp.swapaxes(a, -1, -2) for a in (cache_k_win, cache_v_win, cache_k_nbr, cache_v_nbr))
    gf = g_final.reshape(1, d)
    assert SCALE == 0.125
    new_cols = np.arange(w_in.shape[-1]) - LRU_W
    is_q = ((new_cols >= QC0) & (new_cols < KC0)) | ((new_cols >= QB0) & (new_cols < KB0))
    q_scale = jnp.asarray(np.where(is_q, SCALE, 1.0), F32)

    xc = x_prompt.reshape(1, bc * seq, d)
    xl = x_sample
    zeros_state = jnp.zeros((bc, 1, 2, 256), F32)
    caches = None
    st = None
    for l in range(depth):
        final = l == depth - 1
        mod_c = mods[l, 0:1]
        mod_l = mods[l, 1:1 + bl]
        g1 = g_norm1[l].reshape(1, d)
        g2 = g_norm2[l].reshape(1, d)
        w_l = w_in[l]
        w_in_b = (jnp.concatenate([w_l[:, :512], w_l[:, 1152:], w_l[:, 512:1152]], axis=1) * q_scale).astype(BF16)
        wo = w_out[l].astype(BF16)
        wr_b = jnp.zeros((d, LANES), BF16).at[:, :N_EXPERTS].set(w_router[l].astype(BF16))
        w_gates = jnp.concatenate([_block_diag(w_gate_r[l, 0]), _block_diag(w_gate_i[l, 0]),
                                   _block_diag(w_gate_r[l, 1]), _block_diag(w_gate_i[l, 1])], axis=1).astype(BF16)
        b_gates = jnp.concatenate([b_gate_r[l, 0], b_gate_i[l, 0], b_gate_r[l, 1], b_gate_i[l, 1]]).reshape(1, -1)
        cb = conv_b[l].reshape(1, -1)
        sink6 = sink_logit[l].reshape(-1)

        ulru_c, qkv_c, *caches = _inproj_call(xc, mod_c, g1, w_in_b, cache_seq=seq, layer=l, depth=depth,
                                              prev_caches=caches)
        ya_c, st = _lru_call(ulru_c.reshape(bc, seq, 512), conv_w[l], cb, w_gates, b_gates, lru_lambda[l],
                             zeros_state, 0, st_layer=l, st_depth=depth, prev_state=st)
        yb_c, yc_c = _ctx_attn_call(sink6, qkv_c.reshape(bc, seq, -1))
        xn_c, h2_c, aff_c = _outproj_call(xc, ya_c.reshape(1, bc * seq, -1), yb_c.reshape(1, bc * seq, -1),
                                          yc_c.reshape(1, bc * seq, -1), mod_c, g2, wo, wr_b)
        xs_c, vals_c, rank_c = _route_call(aff_c.reshape(bc, seq, LANES), h2_c.reshape(bc, seq, d), ROUTE_REQS_CTX)

        ulru_l, qkv_l = _inproj_call(xl, mod_l, g1, w_in_b, rope_tabs=rope_tabs)
        ya_l, _ = _lru_call(ulru_l, conv_w[l], cb, w_gates, b_gates, lru_lambda[l], state_lru, l)
        yb_l = _win_attn_call(sink6, qkv_l, ckw_t, cvw_t, l)
        yc_l = _nbr_attn_call(qkv_l, bias_tab, ckn_t, cvn_t, l)
        xn_l, h2_l, aff_l = _outproj_call(xl, ya_l, yb_l, yc_l, mod_l, g2, wo, wr_b)
        xs_l, vals_l, rank_l = _route_call(aff_l, h2_l, ROUTE_REQS_LAT)

        cap_c, cap_l = xs_c.shape[2], xs_l.shape[2]
        y_c, y_l = _expert_call(xs_c.reshape(N_EXPERTS, bc * cap_c, d), xs_l.reshape(N_EXPERTS, bl * cap_l, d),
                                vals_c.reshape(N_EXPERTS, bc * cap_c, 1), vals_l.reshape(N_EXPERTS, bl * cap_l, 1),
                                w_exp_gate, w_exp_up, w_exp_down, l)
        xc = _combine_call(y_c.reshape(N_EXPERTS, bc, cap_c, d), rank_c, xn_c.reshape(bc, seq, d), mod_c, gf, final)
        xl = _combine_call(y_l.reshape(N_EXPERTS, bl, cap_l, d), rank_l, xn_l, mod_l, gf, final)
        xc = xc.reshape(1, bc * seq, d)

    y_prompt = xc.reshape(bc, seq, d)
    y_sample = xl
    return (y_prompt, y_sample, st, *[jnp.swapaxes(buf, -1, -2) for buf in caches])
```

```python
import functools

import numpy as np
import jax
import jax.numpy as jnp
from jax import lax
from jax.experimental import pallas as pl
from jax.experimental.pallas import tpu as pltpu

F32 = jnp.float32
BF16 = jnp.bfloat16

HEAD_DIM = 64
GRID_W = 64
LRU_C = 8.0
CONV_W = 4
KV_B = 2
G_B = 3
H_B = KV_B * G_B
H_C = 6
WINDOW = 128
WIN_BLK = 128
NA_ROWS = 8
NA_COLS = 16
ROPE_BASE = 10000.0
N_EXPERTS = 16
EC_FACTOR = 2
N_MOD = 6
EPS = 1e-6
NEG_INF = -1e30
SCALE = HEAD_DIM ** -0.5

LANES = 128
VMEM_LIMIT_BYTES = 56 * 1024 * 1024

TOKEN_BLOCK = 1024
MOD_COL_BLOCK = 1536
EXPERT_F_STEPS = 2
EXPERT_ROW_CHUNK = 256
LRU_W = 512
QC0, KC0, VC0, QB0, KB0, VB0, ATT_W = 0, 384, 768, 1152, 1536, 1664, 1792
CTX_ATTN_REQS = 2
NBR_Q_ROWS = 4
NBR_K_ROWS = 12
ROUTE_REQS_CTX = 8
ROUTE_REQS_LAT = 2
ROUTE_SORT_ROWS = 256
COMBINE_REQS_CTX = 4
LRU_SEG = 8


def _params(n_axes):
    return pltpu.CompilerParams(dimension_semantics=("arbitrary",) * n_axes,
                                vmem_limit_bytes=VMEM_LIMIT_BYTES)


def _dot(a, b):
    return jnp.dot(a, b, preferred_element_type=F32)


def _dot_nt(a, b):
    return lax.dot_general(a, b, (((1,), (1,)), ((), ())), preferred_element_type=F32)


def _adaln_kernel(cond_ref, w_ref, b_ref, o_ref):
    c = cond_ref[...]
    s = c * jax.nn.sigmoid(c)
    o_ref[0] = _dot(s.astype(BF16), w_ref[0].astype(BF16)) + b_ref[0]


def _adaln_call(cond8, w_mod, b_mod3):
    depth, d, n = w_mod.shape
    nb = MOD_COL_BLOCK
    return pl.pallas_call(
        _adaln_kernel,
        grid=(depth, n // nb),
        in_specs=[pl.BlockSpec((8, d), lambda l, j: (0, 0)),
                  pl.BlockSpec((1, d, nb), lambda l, j: (l, 0, j)),
                  pl.BlockSpec((1, 1, nb), lambda l, j: (l, 0, j))],
        out_specs=pl.BlockSpec((1, 8, nb), lambda l, j: (l, 0, j)),
        out_shape=jax.ShapeDtypeStruct((depth, 8, n), F32),
        compiler_params=_params(2),
        name="adaln",
    )(cond8, w_mod, b_mod3)


def _rms_modulate(x, gain, shift, scale):
    y = x * lax.rsqrt(jnp.mean(x * x, axis=-1, keepdims=True) + EPS)
    return (y * gain) * (1.0 + scale) + shift


def _inproj_kernel(*refs, rope, cache, seq):
    if rope:
        x_ref, mod_ref, g_ref, w_ref, cos_ref, sin_ref, ulru_ref, qkv_ref = refs
    elif cache:
        x_ref, mod_ref, g_ref, w_ref = refs[:4]
        ulru_ref, qkv_ref, kw_ref, vw_ref, kn_ref, vn_ref = refs[-6:]
    else:
        x_ref, mod_ref, g_ref, w_ref, ulru_ref, qkv_ref = refs
    x = x_ref[0]
    h = _rms_modulate(x, g_ref[0], mod_ref[0, 0, 0:1, :], mod_ref[0, 0, 1:2, :])
    u = _dot(h.astype(BF16), w_ref[0])
    tb = u.shape[0]
    ulru_ref[0] = u[:, :LRU_W]
    if rope:
        lo, hi = LRU_W + QB0, LRU_W + VB0
        qk = u[:, lo:hi]
        lane = lax.broadcasted_iota(jnp.int32, (tb, hi - lo), 1)
        first = (lane & 31) < 16
        partner = jnp.where(first, pltpu.roll(qk, hi - lo - 16, 1), pltpu.roll(qk, 16, 1))
        qk = qk * cos_ref[...] + partner * sin_ref[...]
        qkv_ref[0, :, :QB0] = u[:, LRU_W:lo].astype(BF16)
        qkv_ref[0, :, QB0:VB0] = qk.astype(BF16)
        qkv_ref[0, :, VB0:] = u[:, hi:].astype(BF16)
    else:
        qkv_ref[0] = u[:, LRU_W:].astype(BF16)
    if cache:
        for r in range(tb // seq):
            rows = slice(r * seq, (r + 1) * seq)
            for ref, c0, heads in ((kw_ref, KB0, KV_B), (vw_ref, VB0, KV_B), (kn_ref, KC0, H_C), (vn_ref, VC0, H_C)):
                for p in range(heads // 2):
                    pair_t = u[rows, LRU_W + c0 + LANES * p: LRU_W + c0 + LANES * (p + 1)].T
                    ref[r, 0, 2 * p] = pair_t[:HEAD_DIM]
                    ref[r, 0, 2 * p + 1] = pair_t[HEAD_DIM:]


def _inproj_call(x3, mods, mod_row0, gains, w_in_b, layer, rope_tabs=None, cache_seq=None, depth=1, prev_caches=None):
    g, n, d = x3.shape
    d_in = w_in_b.shape[2]
    tb = TOKEN_BLOCK
    rope = rope_tabs is not None
    cache = cache_seq is not None
    in_specs = [pl.BlockSpec((1, tb, d), lambda i, j: (i, j, 0)),
                pl.BlockSpec((1, 1, N_MOD, d), lambda i, j: (layer, mod_row0 + i, 0, 0)),
                pl.BlockSpec((1, 1, d), lambda i, j: (layer, 0, 0)),
                pl.BlockSpec((1, d, d_in), lambda i, j: (layer, 0, 0))]
    args = [x3, mods, gains, w_in_b]
    assert d_in == LRU_W + ATT_W
    out_shape = [jax.ShapeDtypeStruct((g, n, LRU_W), F32),
                 jax.ShapeDtypeStruct((g, n, ATT_W), BF16)]
    out_specs = [pl.BlockSpec((1, tb, LRU_W), lambda i, j: (i, j, 0)),
                 pl.BlockSpec((1, tb, ATT_W), lambda i, j: (i, j, 0))]
    if rope:
        in_specs += [pl.BlockSpec((tb, VB0 - QB0), lambda i, j: (j, 0))] * 2
        args += list(rope_tabs)
    aliases = {}
    if cache:
        assert g == 1 and tb % cache_seq == 0
        rb = tb // cache_seq
        nreq = n // cache_seq
        for heads in (KV_B, KV_B, H_C, H_C):
            out_shape.append(jax.ShapeDtypeStruct((nreq, depth, heads, HEAD_DIM, cache_seq), F32))
            out_specs.append(pl.BlockSpec((rb, 1, heads, HEAD_DIM, cache_seq), lambda i, j: (j, layer, 0, 0, 0)))
        if prev_caches is not None:
            for k, buf in enumerate(prev_caches):
                aliases[len(args)] = 2 + k
                in_specs.append(pl.BlockSpec(memory_space=pl.ANY))
                args.append(buf)
    return pl.pallas_call(
        functools.partial(_inproj_kernel, rope=rope, cache=cache, seq=cache_seq),
        grid=(g, n // tb),
        in_specs=in_specs,
        out_specs=out_specs,
        out_shape=out_shape,
        input_output_aliases=aliases,
        compiler_params=_params(2),
        name="inproj_rope" if rope else "inproj_ctx",
    )(*args)


def _lru_kernel(*refs, n):
    u_ref, cw_ref, cb_ref, wg_ref, bg_ref, lam_ref, h0_ref = refs[:7]
    ya_ref, st_ref, a_s, x_s, y_s, hloc_s, ploc_s, hend_s, pend_s, cin_s = refs[-10:]
    c = 256
    nblk = n // LRU_SEG
    nh = c // LANES
    u = u_ref[0]
    xa = u[:, :c]
    ga = u[:, c:]
    t = lax.broadcasted_iota(jnp.int32, (n, c), 0)
    cw = cw_ref[0]
    xc = cw[2:3] * xa + cb_ref[0]
    xc = xc + cw[0:1] * jnp.where(t >= 2, pltpu.roll(xa, 2, 0), 0.0)
    xc = xc + cw[1:2] * jnp.where(t >= 1, pltpu.roll(xa, 1, 0), 0.0)
    xc = xc + cw[3:4] * jnp.where(t < n - 1, pltpu.roll(xa, n - 1, 0), 0.0)
    gates = _dot(xc.astype(BF16), wg_ref[0]) + bg_ref[0]
    lam = lam_ref[0]
    log_sig = jnp.minimum(lam, 0.0) - jnp.log1p(jnp.exp(-jnp.abs(lam)))
    for d in range(2):
        r = 0.5 * jnp.tanh(0.5 * gates[:, 2 * d * c:(2 * d + 1) * c]) + 0.5
        i = 0.5 * jnp.tanh(0.5 * gates[:, (2 * d + 1) * c:(2 * d + 2) * c]) + 0.5
        log_a = LRU_C * r * log_sig[d:d + 1]
        a = jnp.exp(log_a)
        t_in = jnp.tanh(-log_a) * (1.0 + a * a)
        xin = jnp.where(t_in > 0.0, t_in * lax.rsqrt(t_in), 0.0) * (i * xc)
        for hf in range(nh):
            a_s[hf] = a[:, LANES * hf: LANES * (hf + 1)]
            x_s[hf] = xin[:, LANES * hf: LANES * (hf + 1)]
        order = range(LRU_SEG) if d == 0 else range(LRU_SEG - 1, -1, -1)
        for hf in range(nh):
            h_run = p_run = None
            for s in order:
                a_row = a_s[hf, pl.ds(s, nblk, stride=LRU_SEG), :]
                x_row = x_s[hf, pl.ds(s, nblk, stride=LRU_SEG), :]
                h_run = x_row if h_run is None else a_row * h_run + x_row
                p_run = a_row if p_run is None else a_row * p_run
                hloc_s[d, hf, s] = h_run
                ploc_s[d, hf, s] = p_run
            hend_s[d, hf] = h_run
            pend_s[d, hf] = p_run

    h0 = h0_ref[0, 0]
    init = tuple(h0[d:d + 1, LANES * hf: LANES * (hf + 1)] for d in range(2) for hf in range(nh))

    def body(k, carry):
        out = []
        for idx, cur in enumerate(carry):
            d, hf = divmod(idx, nh)
            kk = k if d == 0 else nblk - 1 - k
            cin_s[d, hf, pl.ds(kk, 1), :] = cur
            out.append(pend_s[d, hf, pl.ds(kk, 1), :] * cur + hend_s[d, hf, pl.ds(kk, 1), :])
        return tuple(out)

    final = lax.fori_loop(0, nblk, body, init)

    y = None
    for d in range(2):
        for hf in range(nh):
            cin = cin_s[d, hf]
            for s in range(LRU_SEG):
                full_h = hloc_s[d, hf, s] + ploc_s[d, hf, s] * cin
                if d == 0:
                    y_s[hf, pl.ds(s, nblk, stride=LRU_SEG), :] = full_h
                else:
                    y_s[hf, pl.ds(s, nblk, stride=LRU_SEG), :] += full_h
    y = jnp.concatenate([y_s[hf] for hf in range(nh)], axis=1) * jax.nn.gelu(ga)
    ya_ref[0] = y.astype(BF16)
    st_ref[0, 0, 0:1, :] = jnp.concatenate(final[:nh], axis=1)
    st_ref[0, 0, 1:2, :] = jnp.concatenate(final[nh:], axis=1)


def _lru_call(ulru, conv_w, conv_b, w_gates_b, b_gates, lam, layer, h0, h0_layer, st_layer=0, st_depth=1,
              prev_state=None):
    b, n, _ = ulru.shape
    c = 256
    per_layer = lambda shape: pl.BlockSpec((1,) + shape, lambda i: (layer,) + (0,) * len(shape))
    nh, nblk = c // LANES, n // LRU_SEG
    assert n % LRU_SEG == 0 and nblk % 8 == 0
    scratch = ([pltpu.VMEM((nh, n, LANES), F32)] * 3
               + [pltpu.VMEM((2, nh, LRU_SEG, nblk, LANES), F32)] * 2
               + [pltpu.VMEM((2, nh, nblk, LANES), F32)] * 3)
    in_specs = [pl.BlockSpec((1, n, 2 * c), lambda i: (i, 0, 0)),
                per_layer((CONV_W, c)), per_layer((1, c)), per_layer((c, 4 * c)), per_layer((1, 4 * c)),
                per_layer((2, c)),
                pl.BlockSpec((1, 1, 2, c), lambda i: (i, h0_layer, 0, 0))]
    args = [ulru, conv_w, conv_b, w_gates_b, b_gates, lam, h0]
    aliases = {}
    if prev_state is not None:
        aliases[len(args)] = 1
        in_specs.append(pl.BlockSpec(memory_space=pl.ANY))
        args.append(prev_state)
    return pl.pallas_call(
        functools.partial(_lru_kernel, n=n),
        grid=(b,),
        in_specs=in_specs,
        out_specs=[pl.BlockSpec((1, n, c), lambda i: (i, 0, 0)),
                   pl.BlockSpec((1, 1, 2, c), lambda i: (i, st_layer, 0, 0))],
        out_shape=[jax.ShapeDtypeStruct((b, n, c), BF16),
                   jax.ShapeDtypeStruct((b, st_depth, 2, c), F32)],
        scratch_shapes=scratch,
        input_output_aliases=aliases,
        compiler_params=_params(1),
        name="rglru",
    )(*args)


def _softmax_pv(parts, sink):
    mx = None
    for s, _, _ in parts:
        cur = jnp.max(s, axis=-1, keepdims=True)
        mx = cur if mx is None else jnp.maximum(mx, cur)
    if sink is not None:
        mx = jnp.maximum(mx, sink)
    den = None
    out = None
    for s, v, v_t in parts:
        p = jnp.exp(s - mx)
        cur = jnp.sum(p, axis=-1, keepdims=True)
        den = cur if den is None else den + cur
        o = _dot_nt(p.astype(BF16), v) if v_t else _dot(p.astype(BF16), v)
        out = o if out is None else out + o
    if sink is not None:
        den = den + jnp.exp(sink - mx)
    return out / den


def _swap_halves(x):
    return jnp.concatenate([x[:, HEAD_DIM:], x[:, :HEAD_DIM]], axis=1)


def _keep_half(x, lo_mask, half):
    zero = jnp.zeros_like(x)
    return jnp.where(lo_mask, x, zero) if half == 0 else jnp.where(lo_mask, zero, x)


def _ctx_attn_kernel(sink_ref, att_ref, ob_ref, oc_ref, *, n, rb, sink0):
    lo = lax.broadcasted_iota(jnp.int32, (n, LANES), 1) < HEAD_DIM
    for r in range(rb):
        for p in range(H_C // 2):
            qp = att_ref[r, :, QC0 + LANES * p: QC0 + LANES * (p + 1)]
            kp = att_ref[r, :, KC0 + LANES * p: KC0 + LANES * (p + 1)]
            vp = att_ref[r, :, VC0 + LANES * p: VC0 + LANES * (p + 1)]
            outs = [_softmax_pv([(_dot_nt(qp, _keep_half(kp, lo, half)), vp, False)], None) for half in range(2)]
            oc_ref[r, :, LANES * p: LANES * (p + 1)] = jnp.where(lo, outs[0], outs[1]).astype(BF16)
        kpair = att_ref[r, :, KB0: KB0 + LANES]
        vpair = att_ref[r, :, VB0: VB0 + LANES]
        kpair_sw = _swap_halves(kpair)
        vpair_sw = _swap_halves(vpair)
        for p in range(H_B // 2):
            qp = att_ref[r, :, QB0 + LANES * p: QB0 + LANES * (p + 1)]
            outs = []
            for half in range(2):
                h = 2 * p + half
                aligned = (h // G_B) == half
                ksrc, vsrc = (kpair, vpair) if aligned else (kpair_sw, vpair_sw)
                outs.append(_softmax_pv([(_dot_nt(qp, _keep_half(ksrc, lo, half)), vsrc, False)], sink_ref[sink0 + h]))
            ob_ref[r, :, LANES * p: LANES * (p + 1)] = jnp.where(lo, outs[0], outs[1]).astype(BF16)


def _ctx_attn_call(sinks, att, layer):
    b, n, w = att.shape
    rb = CTX_ATTN_REQS
    assert b % rb == 0 and KV_B == 2
    return pl.pallas_call(
        functools.partial(_ctx_attn_kernel, n=n, rb=rb, sink0=layer * H_B),
        grid=(b // rb,),
        in_specs=[pl.BlockSpec(memory_space=pltpu.SMEM),
                  pl.BlockSpec((rb, n, w), lambda i: (i, 0, 0))],
        out_specs=[pl.BlockSpec((rb, n, 384), lambda i: (i, 0, 0))] * 2,
        out_shape=[jax.ShapeDtypeStruct((b, n, 384), BF16)] * 2,
        compiler_params=_params(1),
        name="ctx_attn",
    )(sinks, att)


def _win_attn_kernel(sink_ref, qkv_ref, kc_ref, vc_ref, o_ref, *, n, sink0):
    j = pl.program_id(1)
    nloc = 3 * WIN_BLK
    q0 = pl.multiple_of(j * WIN_BLK, WIN_BLK)
    ks = pl.multiple_of(jnp.clip((j - 1) * WIN_BLK, 0, n - nloc), WIN_BLK)
    row = lax.broadcasted_iota(jnp.int32, (WIN_BLK, nloc), 0)
    col = lax.broadcasted_iota(jnp.int32, (WIN_BLK, nloc), 1)
    in_window = jnp.where(jnp.abs((q0 + row) - (ks + col)) <= WINDOW, 0.0, NEG_INF)
    band = jnp.concatenate([in_window] * G_B, axis=0)
    rowh = lax.broadcasted_iota(jnp.int32, (G_B * WIN_BLK, 1), 0) // WIN_BLK
    for kv in range(KV_B):
        gw = G_B * HEAD_DIM
        qblk = qkv_ref[0, pl.ds(q0, WIN_BLK), QB0 + gw * kv: QB0 + gw * (kv + 1)]
        q3 = jnp.concatenate([qblk[:, 64 * g: 64 * (g + 1)] for g in range(G_B)], axis=0)
        kl = qkv_ref[0, pl.ds(ks, nloc), KB0 + 64 * kv: KB0 + 64 * (kv + 1)]
        vl = qkv_ref[0, pl.ds(ks, nloc), VB0 + 64 * kv: VB0 + 64 * (kv + 1)]
        kc_t = kc_ref[0, 0, kv].astype(BF16)
        vc_t = vc_ref[0, 0, kv].astype(BF16)
        s_loc = _dot_nt(q3, kl) + band
        s_ctx = _dot(q3, kc_t)
        s0 = sink0 + G_B * kv
        sk = jnp.where(rowh == 0, sink_ref[s0], jnp.where(rowh == 1, sink_ref[s0 + 1], sink_ref[s0 + 2]))
        o = _softmax_pv([(s_loc, vl, False), (s_ctx, vc_t, True)], sk)
        for g in range(G_B):
            h = kv * G_B + g
            o_ref[0, :, 64 * h: 64 * (h + 1)] = o[g * WIN_BLK:(g + 1) * WIN_BLK].astype(BF16)


def _win_attn_call(sinks, qkv, cache_k, cache_v, layer):
    b, n, w = qkv.shape
    past = cache_k.shape[4]
    cache_spec = pl.BlockSpec((1, 1, KV_B, HEAD_DIM, past), lambda i, j: (i, layer, 0, 0, 0))
    return pl.pallas_call(
        functools.partial(_win_attn_kernel, n=n, sink0=layer * H_B),
        grid=(b, n // WIN_BLK),
        in_specs=[pl.BlockSpec(memory_space=pltpu.SMEM),
                  pl.BlockSpec((1, n, w), lambda i, j: (i, 0, 0)),
                  cache_spec, cache_spec],
        out_specs=pl.BlockSpec((1, WIN_BLK, 384), lambda i, j: (i, j, 0)),
        out_shape=jax.ShapeDtypeStruct((b, n, 384), BF16),
        compiler_params=_params(2),
        name="win_attn",
    )(sinks, qkv, cache_k, cache_v)


def _nbr_window_start(g, rows):
    return jnp.clip(g * NBR_Q_ROWS - NA_ROWS // 2, 0, rows - NBR_K_ROWS)


def _nbr_attn_kernel(q_ref, k_ref, v_ref, rel_ref, kc_ref, vc_ref, o_ref, bias_s, *, n):
    g = pl.program_id(0)
    nq = NBR_Q_ROWS * GRID_W
    nk = NBR_K_ROWS * GRID_W
    rows = n // GRID_W
    ws = _nbr_window_start(g, rows)
    k0 = pl.multiple_of(ws * GRID_W, GRID_W)

    @pl.when(pl.program_id(1) == 0)
    def _():
        for i in range(NBR_Q_ROWS):
            qr = g * NBR_Q_ROWS + i
            rs = jnp.clip(qr - NA_ROWS // 2, 0, rows - NA_ROWS)
            for jp in range(NBR_K_ROWS // 2):
                tiles = []
                for j in (2 * jp, 2 * jp + 1):
                    kr = ws + j
                    valid = (kr >= rs) & (kr < rs + NA_ROWS)
                    d = jnp.clip(kr - qr + NA_ROWS - 1, 0, 2 * NA_ROWS - 2)
                    tiles.append(jnp.where(valid, rel_ref[0, :, pl.ds(d, 1)][:, 0], NEG_INF))
                bias_s[:, GRID_W * i: GRID_W * (i + 1), LANES * jp: LANES * (jp + 1)] = jnp.concatenate(tiles, axis=-1)
    lo_k = lax.broadcasted_iota(jnp.int32, (nk, LANES), 1) < HEAD_DIM
    lo_q = lax.broadcasted_iota(jnp.int32, (nq, LANES), 1) < HEAD_DIM
    zpad = jnp.zeros((HEAD_DIM, kc_ref.shape[4]), BF16)
    for p in range(H_C // 2):
        qp = q_ref[0, :, LANES * p: LANES * (p + 1)]
        kp = k_ref[0, pl.ds(k0, nk), LANES * p: LANES * (p + 1)]
        vp = v_ref[0, pl.ds(k0, nk), LANES * p: LANES * (p + 1)]
        outs = []
        for half in range(2):
            h = 2 * p + half
            kc_t = kc_ref[0, 0, h].astype(BF16)
            vc_t = vc_ref[0, 0, h].astype(BF16)
            kc_t = jnp.concatenate([kc_t, zpad] if half == 0 else [zpad, kc_t], axis=0)
            vc_t = jnp.concatenate([vc_t, zpad] if half == 0 else [zpad, vc_t], axis=0)
            s_loc = _dot_nt(qp, _keep_half(kp, lo_k, half)) + bias_s[h]
            s_ctx = _dot(qp, kc_t)
            outs.append(_softmax_pv([(s_loc, vp, False), (s_ctx, vc_t, True)], None))
        o_ref[0, :, LANES * p: LANES * (p + 1)] = jnp.where(lo_q, outs[0], outs[1]).astype(BF16)


def _nbr_attn_call(att, rel_tab, cache_k, cache_v, layer):
    b, n, w = att.shape
    past = cache_k.shape[4]
    nq = NBR_Q_ROWS * GRID_W
    nk = NBR_K_ROWS * GRID_W
    gw = H_C * HEAD_DIM
    rows = n // GRID_W
    assert (QC0, KC0, VC0) == (0, gw, 2 * gw) and n % nq == 0 and rows >= NBR_K_ROWS and NBR_K_ROWS % 2 == 0
    cache_spec = pl.BlockSpec((1, 1, H_C, HEAD_DIM, past), lambda g, i: (i, layer, 0, 0, 0))
    return pl.pallas_call(
        functools.partial(_nbr_attn_kernel, n=n),
        grid=(n // nq, b),
        in_specs=[pl.BlockSpec((1, nq, gw), lambda g, i: (i, g, 0)),
                  pl.BlockSpec((1, n, gw), lambda g, i: (i, 0, 1)),
                  pl.BlockSpec((1, n, gw), lambda g, i: (i, 0, 2)),
                  pl.BlockSpec((1,) + rel_tab.shape[1:], lambda g, i: (layer, 0, 0, 0, 0)),
                  cache_spec, cache_spec],
        out_specs=pl.BlockSpec((1, nq, gw), lambda g, i: (i, g, 0)),
        out_shape=jax.ShapeDtypeStruct((b, n, gw), BF16),
        scratch_shapes=[pltpu.VMEM((H_C, nq, nk), F32)],
        compiler_params=_params(2),
        name="nbr_attn",
    )(att, att, att, rel_tab, cache_k, cache_v)


def _outproj_kernel(x_ref, ya_ref, yb_ref, yc_ref, mod_ref, g_ref, wo_ref, wr_ref,
                    xn_ref, h2_ref, aff_ref):
    mix = jnp.concatenate([ya_ref[0], yb_ref[0], yc_ref[0]], axis=1)
    proj = _dot(mix, wo_ref[0])
    xn = x_ref[0] + mod_ref[0, 0, 2:3, :] * proj
    xn_ref[0] = xn
    h2 = _rms_modulate(xn, g_ref[0], mod_ref[0, 0, 3:4, :], mod_ref[0, 0, 4:5, :]).astype(BF16)
    h2_ref[0] = h2
    logits = _dot(h2, wr_ref[0])
    lane = lax.broadcasted_iota(jnp.int32, logits.shape, 1)
    valid = lane < N_EXPERTS
    logits = jnp.where(valid, logits, NEG_INF)
    e = jnp.exp(logits - jnp.max(logits, axis=-1, keepdims=True))
    e = jnp.where(valid, e, 0.0)
    aff_ref[0] = e / jnp.sum(e, axis=-1, keepdims=True)


def _outproj_call(x3, ya, yb, yc, mods, mod_row0, gains, wo_b16, w_router_b, layer):
    g, n, d = x3.shape
    tb = TOKEN_BLOCK
    tok = lambda w: pl.BlockSpec((1, tb, w), lambda i, j: (i, j, 0))
    per_layer = lambda shape: pl.BlockSpec((1,) + shape, lambda i, j: (layer,) + (0,) * len(shape))
    return pl.pallas_call(
        _outproj_kernel,
        grid=(g, n // tb),
        in_specs=[tok(d), tok(256), tok(384), tok(384),
                  pl.BlockSpec((1, 1, N_MOD, d), lambda i, j: (layer, mod_row0 + i, 0, 0)),
                  per_layer((1, d)), per_layer((256 + 384 + 384, d)), per_layer((d, LANES))],
        out_specs=[tok(d), tok(d), tok(LANES)],
        out_shape=[jax.ShapeDtypeStruct((g, n, d), F32),
                   jax.ShapeDtypeStruct((g, n, d), BF16),
                   jax.ShapeDtypeStruct((g, n, LANES), F32)],
        compiler_params=_params(2),
        name="outproj_router",
    )(x3, ya, yb, yc, mods, gains, wo_b16, w_router_b)


def _sort_by_affinity(keys, idx, pos, levels, seg_lanes):
    rows = keys.shape[0]
    sign = [jnp.where(((pos >> m) & 1) == 0, 1.0, -1.0) for m in range(levels)]

    def partner(x, j):
        d = 1 << j
        low = sign[j] > 0.0
        if d >= rows:
            sh = (d // rows) * seg_lanes
            return jnp.where(low, pltpu.roll(x, LANES - sh, 1), pltpu.roll(x, sh, 1))
        if d >= 8:
            x4 = x.reshape(rows // (2 * d), 2, d, LANES)
            return jnp.concatenate([x4[:, 1:2], x4[:, 0:1]], axis=1).reshape(rows, LANES)
        x3 = x.reshape(rows // 8, 8, LANES)
        low3 = sign[j].reshape(rows // 8, 8, LANES) > 0.0
        return jnp.where(low3, pltpu.roll(x3, 8 - d, 1), pltpu.roll(x3, d, 1)).reshape(rows, LANES)

    for k in range(1, levels + 1):
        for j in range(k - 1, -1, -1):
            pk, pi = partner(keys, j), partner(idx, j)
            before = jnp.where(keys == pk, jnp.where(idx < pi, 1.0, -1.0), jnp.where(keys > pk, 1.0, -1.0))
            want_first = sign[j] * sign[k] if k < levels else sign[j]
            keep = before * want_first > 0.0
            keys = jnp.where(keep, keys, pk)
            idx = jnp.where(keep, idx, pi)
    return keys, idx


def _route_kernel(aff_ref, h2_ref, xs_ref, vals_ref, rank_ref, *, n, cap, rb):
    ne = N_EXPERTS
    rows = ROUTE_SORT_ROWS
    nseg = n // rows
    seg_lanes = rb * ne
    levels = n.bit_length() - 1
    fcap = float(cap)
    row = lax.broadcasted_iota(jnp.int32, (rows, LANES), 0)
    lane = lax.broadcasted_iota(jnp.int32, (rows, LANES), 1)
    pos = (lane // seg_lanes) * rows + row
    keys = jnp.full((rows, LANES), -1.0, F32)
    for r in range(rb):
        for seg in range(nseg):
            off = seg * seg_lanes + r * ne
            piece = aff_ref[r, seg * rows:(seg + 1) * rows, :]
            if off:
                piece = pltpu.roll(piece, off, 1)
            keys = jnp.where((lane >= off) & (lane < off + ne), piece, keys)
    keys, idx = _sort_by_affinity(keys, pos.astype(F32), pos, levels, seg_lanes)
    top_keys = keys[:cap]
    top_idx = idx[:cap]

    tok = lax.broadcasted_iota(jnp.int32, (cap, n), 1).astype(F32)
    weight = fcap - lax.broadcasted_iota(jnp.int32, (cap, n), 0).astype(F32)
    pad = jnp.full((LANES - ne, n), fcap, F32)
    for r in range(rb):
        picks, rank_rows = [], []
        for e in range(ne):
            col = r * ne + e
            onehot = jnp.where(top_idx[:, col:col + 1] == tok, 1.0, 0.0)
            vals_ref[e, r] = top_keys[:, col:col + 1]
            rank_rows.append(fcap - jnp.sum(onehot * weight, axis=0, keepdims=True))
            picks.append(onehot.astype(BF16))
        rank_ref[r] = jnp.concatenate(rank_rows + [pad], axis=0).T
        xs = _dot(jnp.concatenate(picks, axis=0), h2_ref[r])
        xs_ref[:, r] = xs.reshape(ne, cap, xs.shape[-1]).astype(BF16)


def _route_call(aff, h2, rb):
    b, n, d = h2.shape
    cap = max(1, EC_FACTOR * n // N_EXPERTS)
    nseg = n // ROUTE_SORT_ROWS
    assert b % rb == 0 and n == nseg * ROUTE_SORT_ROWS and n & (n - 1) == 0
    assert nseg * rb * N_EXPERTS <= LANES and cap <= ROUTE_SORT_ROWS
    return pl.pallas_call(
        functools.partial(_route_kernel, n=n, cap=cap, rb=rb),
        grid=(b // rb,),
        in_specs=[pl.BlockSpec((rb, n, LANES), lambda i: (i, 0, 0)),
                  pl.BlockSpec((rb, n, d), lambda i: (i, 0, 0))],
        out_specs=[pl.BlockSpec((N_EXPERTS, rb, cap, d), lambda i: (0, i, 0, 0)),
                   pl.BlockSpec((N_EXPERTS, rb, cap, 1), lambda i: (0, i, 0, 0)),
                   pl.BlockSpec((rb, n, LANES), lambda i: (i, 0, 0))],
        out_shape=[jax.ShapeDtypeStruct((N_EXPERTS, b, cap, d), BF16),
                   jax.ShapeDtypeStruct((N_EXPERTS, b, cap, 1), F32),
                   jax.ShapeDtypeStruct((b, n, LANES), F32)],
        compiler_params=_params(1),
        name="route_gather",
    )(aff, h2)


def _expert_kernel(xc_ref, xl_ref, vc_ref, vl_ref, wg_ref, wu_ref, wd_ref, yc_ref, yl_ref, acc_c, acc_l):
    f = pl.program_id(1)

    def step(first):
        wg = wg_ref[0, 0].astype(BF16)
        wu = wu_ref[0, 0].astype(BF16)
        wd = wd_ref[0, 0].astype(BF16)
        for x_ref, v_ref, y_ref, acc in ((xc_ref, vc_ref, yc_ref, acc_c), (xl_ref, vl_ref, yl_ref, acc_l)):
            rows = x_ref.shape[1]
            chunk = min(EXPERT_ROW_CHUNK, rows)
            for c0 in range(0, rows, chunk):
                sl = slice(c0, c0 + chunk)
                x = x_ref[0, sl, :]
                a = _dot(x, wg)
                u = _dot(x, wu)
                act = ((a * jax.nn.sigmoid(a)) * u).astype(BF16)
                part = _dot(act, wd)
                if first:
                    acc[sl, :] = part
                else:
                    y_ref[0, sl, :] = ((acc[sl, :] + part) * v_ref[0, sl, :]).astype(BF16)

    pl.when(f == 0)(functools.partial(step, True))
    pl.when(f == EXPERT_F_STEPS - 1)(functools.partial(step, False))


def _expert_call(xs_c, xs_l, vals_c, vals_l, w_gate, w_up, w_down, layer):
    e, rc, d = xs_c.shape
    rl = xs_l.shape[1]
    f_total = w_gate.shape[-1]
    assert EXPERT_F_STEPS == 2
    fb = f_total // EXPERT_F_STEPS
    per_e = lambda rows, w: pl.BlockSpec((1, rows, w), lambda i, j: (i, 0, 0))
    return pl.pallas_call(
        _expert_kernel,
        grid=(e, EXPERT_F_STEPS),
        in_specs=[per_e(rc, d), per_e(rl, d), per_e(rc, 1), per_e(rl, 1),
                  pl.BlockSpec((1, 1, d, fb), lambda i, j: (layer, i, 0, j)),
                  pl.BlockSpec((1, 1, d, fb), lambda i, j: (layer, i, 0, j)),
                  pl.BlockSpec((1, 1, fb, d), lambda i, j: (layer, i, j, 0))],
        out_specs=[per_e(rc, d), per_e(rl, d)],
        out_shape=[jax.ShapeDtypeStruct((e, rc, d), BF16),
                   jax.ShapeDtypeStruct((e, rl, d), BF16)],
        scratch_shapes=[pltpu.VMEM((rc, d), F32), pltpu.VMEM((rl, d), F32)],
        compiler_params=_params(2),
        name="experts",
    )(xs_c, xs_l, vals_c, vals_l, w_gate, w_up, w_down)


def _combine_kernel(y_ref, rank_ref, xn_ref, mod_ref, gf_ref, o_ref, *, n, cap, final, rb):
    ec = N_EXPERTS * cap
    d = xn_ref.shape[-1]
    ce = lax.broadcasted_iota(jnp.int32, (LANES, ec), 1) // cap
    ee = lax.broadcasted_iota(jnp.int32, (LANES, ec), 0)
    expand = jnp.where(ce == ee, 1.0, 0.0).astype(BF16)
    slot = (lax.broadcasted_iota(jnp.int32, (n, ec), 1) % cap).astype(F32)
    for r in range(rb):
        rank = rank_ref[r].astype(BF16)
        rexp = _dot(rank, expand)
        onehot = jnp.where(rexp == slot, 1.0, 0.0).astype(BF16)
        moe = _dot(onehot, y_ref[:, r].reshape(ec, d))
        x = xn_ref[r] + mod_ref[0, 0, 5:6, :] * moe
        if final:
            x = (x * lax.rsqrt(jnp.mean(x * x, axis=-1, keepdims=True) + EPS)) * gf_ref[...]
        o_ref[r] = x


def _combine_call(y4, rank, xn, mods, mod_row0, shared_mod, layer, g_final, final):
    e, b, cap, d = y4.shape
    n = xn.shape[1]
    rb = COMBINE_REQS_CTX if shared_mod else 1
    assert b % rb == 0
    return pl.pallas_call(
        functools.partial(_combine_kernel, n=n, cap=cap, final=final, rb=rb),
        grid=(b // rb,),
        in_specs=[pl.BlockSpec((e, rb, cap, d), lambda i: (0, i, 0, 0)),
                  pl.BlockSpec((rb, n, LANES), lambda i: (i, 0, 0)),
                  pl.BlockSpec((rb, n, d), lambda i: (i, 0, 0)),
                  pl.BlockSpec((1, 1, N_MOD, d), (lambda i: (layer, mod_row0, 0, 0)) if shared_mod
                               else (lambda i: (layer, mod_row0 + i, 0, 0))),
                  pl.BlockSpec((1, d), lambda i: (0, 0))],
        out_specs=pl.BlockSpec((rb, n, d), lambda i: (i, 0, 0)),
        out_shape=jax.ShapeDtypeStruct((b, n, d), F32),
        compiler_params=_params(1),
        name="combine",
    )(y4, rank, xn, mods, g_final)


def _rope_tables(n):
    t = jnp.arange(n, dtype=jnp.int32)
    row = (t // GRID_W).astype(F32)
    col = (t % GRID_W).astype(F32)
    half = HEAD_DIM // 4
    inv = jnp.power(ROPE_BASE, -jnp.arange(half, dtype=F32) / half)
    ang_r = row[:, None] * inv[None, :]
    ang_c = col[:, None] * inv[None, :]
    cos_h = jnp.concatenate([jnp.cos(ang_r)] * 2 + [jnp.cos(ang_c)] * 2, axis=-1)
    sin_h = jnp.concatenate([-jnp.sin(ang_r), jnp.sin(ang_r), -jnp.sin(ang_c), jnp.sin(ang_c)], axis=-1)
    reps = 512 // HEAD_DIM
    return jnp.tile(cos_h, (1, reps)), jnp.tile(sin_h, (1, reps))


def _nbr_rel_tables(rpb):
    col = np.arange(GRID_W)
    cs = np.clip(col - NA_COLS // 2, 0, GRID_W - NA_COLS)
    col_mask = (col[None, :] >= cs[:, None]) & (col[None, :] < cs[:, None] + NA_COLS)
    dc_idx = np.clip(col[None, :] - col[:, None], -(NA_COLS - 1), NA_COLS - 1) + (NA_COLS - 1)
    sel_col = (dc_idx[None, :, :] == np.arange(2 * NA_COLS - 1)[:, None, None]).astype(np.float32)
    t = jnp.einsum("lhdc,cqk->lhdqk", rpb.astype(F32), sel_col, precision=lax.Precision.HIGHEST)
    return jnp.where(col_mask, t, NEG_INF)


def _block_diag(w):
    nb, bw = w.shape[-3], w.shape[-1]
    eye = jnp.eye(nb, dtype=w.dtype)
    return (eye[:, None, :, None] * w[..., :, :, None, :]).reshape(w.shape[:-3] + (nb * bw, nb * bw))


def kernel(x_prompt, x_sample, state_lru, cache_k_win, cache_v_win, cache_k_nbr, cache_v_nbr, c, c_ctx, w_mod, b_mod, g_norm1, w_in, conv_w, conv_b, w_gate_r, b_gate_r, w_gate_i, b_gate_i, lru_lambda, sink_logit, nbr_bias, w_out, g_norm2, w_router, w_exp_gate, w_exp_up, w_exp_down, g_final):
    bc, seq, d = x_prompt.shape
    bl, n_lat, _ = x_sample.shape
    depth = w_mod.shape[0]
    assert bl + 1 <= 8 and d == 1024

    cond8 = jnp.zeros((8, d), F32).at[0].set(c_ctx).at[1:1 + bl].set(c)
    mods = _adaln_call(cond8, w_mod, b_mod.reshape(depth, 1, -1)).reshape(depth, 8, N_MOD, d)
    rope_tabs = _rope_tables(n_lat)
    bias_tab = _nbr_rel_tables(nbr_bias)
    ckw_t, cvw_t, ckn_t, cvn_t = (jnp.swapaxes(a, -1, -2) for a in (cache_k_win, cache_v_win, cache_k_nbr, cache_v_nbr))
    gf = g_final.reshape(1, d)
    assert SCALE == 0.125
    new_cols = np.arange(w_in.shape[-1]) - LRU_W
    is_q = ((new_cols >= QC0) & (new_cols < KC0)) | ((new_cols >= QB0) & (new_cols < KB0))
    q_scale = jnp.asarray(np.where(is_q, SCALE, 1.0), F32)

    w_in_b = (jnp.concatenate([w_in[..., :512], w_in[..., 1152:], w_in[..., 512:1152]], axis=-1) * q_scale).astype(BF16)
    wo = w_out.astype(BF16)
    wr_b = jnp.pad(w_router.astype(BF16), ((0, 0), (0, 0), (0, LANES - N_EXPERTS)))
    gate_r, gate_i = _block_diag(w_gate_r), _block_diag(w_gate_i)
    w_gates = jnp.concatenate([gate_r[:, 0], gate_i[:, 0], gate_r[:, 1], gate_i[:, 1]], axis=-1).astype(BF16)
    b_gates = jnp.concatenate([b_gate_r[:, 0], b_gate_i[:, 0], b_gate_r[:, 1], b_gate_i[:, 1]], axis=-1)[:, None, :]
    cb = conv_b[:, None, :]
    g1 = g_norm1[:, None, :]
    g2 = g_norm2[:, None, :]
    sinks = sink_logit.reshape(-1)
    ctx_row, lat_row = 0, 1

    xc = x_prompt.reshape(1, bc * seq, d)
    xl = x_sample
    zeros_state = jnp.zeros((bc, 1, 2, 256), F32)
    caches = None
    st = None
    for l in range(depth):
        final = l == depth - 1
        ulru_c, qkv_c, *caches = _inproj_call(xc, mods, ctx_row, g1, w_in_b, l, cache_seq=seq, depth=depth,
                                              prev_caches=caches)
        ya_c, st = _lru_call(ulru_c.reshape(bc, seq, 512), conv_w, cb, w_gates, b_gates, lru_lambda, l,
                             zeros_state, 0, st_layer=l, st_depth=depth, prev_state=st)
        yb_c, yc_c = _ctx_attn_call(sinks, qkv_c.reshape(bc, seq, -1), l)
        xn_c, h2_c, aff_c = _outproj_call(xc, ya_c.reshape(1, bc * seq, -1), yb_c.reshape(1, bc * seq, -1),
                                          yc_c.reshape(1, bc * seq, -1), mods, ctx_row, g2, wo, wr_b, l)
        xs_c, vals_c, rank_c = _route_call(aff_c.reshape(bc, seq, LANES), h2_c.reshape(bc, seq, d), ROUTE_REQS_CTX)

        ulru_l, qkv_l = _inproj_call(xl, mods, lat_row, g1, w_in_b, l, rope_tabs=rope_tabs)
        ya_l, _ = _lru_call(ulru_l, conv_w, cb, w_gates, b_gates, lru_lambda, l, state_lru, l)
        yb_l = _win_attn_call(sinks, qkv_l, ckw_t, cvw_t, l)
        yc_l = _nbr_attn_call(qkv_l, bias_tab, ckn_t, cvn_t, l)
        xn_l, h2_l, aff_l = _outproj_call(xl, ya_l, yb_l, yc_l, mods, lat_row, g2, wo, wr_b, l)
        xs_l, vals_l, rank_l = _route_call(aff_l, h2_l, ROUTE_REQS_LAT)

        cap_c, cap_l = xs_c.shape[2], xs_l.shape[2]
        y_c, y_l = _expert_call(xs_c.reshape(N_EXPERTS, bc * cap_c, d), xs_l.reshape(N_EXPERTS, bl * cap_l, d),
                                vals_c.reshape(N_EXPERTS, bc * cap_c, 1), vals_l.reshape(N_EXPERTS, bl * cap_l, 1),
                                w_exp_gate, w_exp_up, w_exp_down, l)
        xc = _combine_call(y_c.reshape(N_EXPERTS, bc, cap_c, d), rank_c, xn_c.reshape(bc, seq, d), mods, ctx_row, True,
                           l, gf, final)
        xl = _combine_call(y_l.reshape(N_EXPERTS, bl, cap_l, d), rank_l, xn_l, mods, lat_row, False, l, gf, final)
        xc = xc.reshape(1, bc * seq, d)

    y_prompt = xc.reshape(bc, seq, d)
    y_sample = xl
    return (y_prompt, y_sample, st, *[jnp.swapaxes(buf, -1, -2) for buf in caches])
```

```python
import functools

import numpy as np
import jax
import jax.numpy as jnp
from jax import lax
from jax.experimental import pallas as pl
from jax.experimental.pallas import tpu as pltpu

F32 = jnp.float32
BF16 = jnp.bfloat16

HEAD_DIM = 64
GRID_W = 64
LRU_C = 8.0
CONV_W = 4
KV_B = 2
G_B = 3
H_B = KV_B * G_B
H_C = 6
WINDOW = 128
WIN_BLK = 128
NA_ROWS = 8
NA_COLS = 16
ROPE_BASE = 10000.0
N_EXPERTS = 16
EC_FACTOR = 2
N_MOD = 6
EPS = 1e-6
NEG_INF = -1e30
SCALE = HEAD_DIM ** -0.5

LANES = 128
VMEM_LIMIT_BYTES = 56 * 1024 * 1024

TOKEN_BLOCK = 1024
MOD_COL_BLOCK = 1536
EXPERT_F_STEPS = 2
EXPERT_ROW_CHUNK = 256
LRU_W = 512
QC0, KC0, VC0, QB0, KB0, VB0, ATT_W = 0, 384, 768, 1152, 1536, 1664, 1792
CTX_ATTN_REQS = 2
WIN_Q_BLOCKS = 2
NBR_Q_ROWS = 4
NBR_K_ROWS = 12
ROUTE_REQS_CTX = 8
ROUTE_REQS_LAT = 2
ROUTE_SORT_ROWS = 256
COMBINE_REQS_CTX = 4
LRU_SEG = 8


def _params(n_axes):
    return pltpu.CompilerParams(dimension_semantics=("arbitrary",) * n_axes,
                                vmem_limit_bytes=VMEM_LIMIT_BYTES)


def _dot(a, b):
    return jnp.dot(a, b, preferred_element_type=F32)


def _dot_nt(a, b):
    return lax.dot_general(a, b, (((1,), (1,)), ((), ())), preferred_element_type=F32)


def _adaln_kernel(cond_ref, w_ref, b_ref, o_ref):
    c = cond_ref[...]
    s = c * jax.nn.sigmoid(c)
    o_ref[0] = _dot(s.astype(BF16), w_ref[0].astype(BF16)) + b_ref[0]


def _adaln_call(cond8, w_mod, b_mod3):
    depth, d, n = w_mod.shape
    nb = MOD_COL_BLOCK
    return pl.pallas_call(
        _adaln_kernel,
        grid=(depth, n // nb),
        in_specs=[pl.BlockSpec((8, d), lambda l, j: (0, 0)),
                  pl.BlockSpec((1, d, nb), lambda l, j: (l, 0, j)),
                  pl.BlockSpec((1, 1, nb), lambda l, j: (l, 0, j))],
        out_specs=pl.BlockSpec((1, 8, nb), lambda l, j: (l, 0, j)),
        out_shape=jax.ShapeDtypeStruct((depth, 8, n), F32),
        compiler_params=_params(2),
        name="adaln",
    )(cond8, w_mod, b_mod3)


def _rms_modulate(x, gain, shift, scale):
    y = x * lax.rsqrt(jnp.mean(x * x, axis=-1, keepdims=True) + EPS)
    return (y * gain) * (1.0 + scale) + shift


def _inproj_kernel(*refs, rope, cache, seq):
    if rope:
        x_ref, mod_ref, g_ref, w_ref, cos_ref, sin_ref, ulru_ref, qkv_ref = refs
    elif cache:
        x_ref, mod_ref, g_ref, w_ref = refs[:4]
        ulru_ref, qkv_ref, kw_ref, vw_ref, kn_ref, vn_ref = refs[-6:]
    else:
        x_ref, mod_ref, g_ref, w_ref, ulru_ref, qkv_ref = refs
    x = x_ref[0]
    h = _rms_modulate(x, g_ref[0], mod_ref[0, 0, 0:1, :], mod_ref[0, 0, 1:2, :])
    u = _dot(h.astype(BF16), w_ref[0])
    tb = u.shape[0]
    ulru_ref[0] = u[:, :LRU_W]
    if rope:
        lo, hi = LRU_W + QB0, LRU_W + VB0
        qk = u[:, lo:hi]
        lane = lax.broadcasted_iota(jnp.int32, (tb, hi - lo), 1)
        first = (lane & 31) < 16
        partner = jnp.where(first, pltpu.roll(qk, hi - lo - 16, 1), pltpu.roll(qk, 16, 1))
        qk = qk * cos_ref[...] + partner * sin_ref[...]
        qkv_ref[0, :, :QB0] = u[:, LRU_W:lo].astype(BF16)
        qkv_ref[0, :, QB0:VB0] = qk.astype(BF16)
        qkv_ref[0, :, VB0:] = u[:, hi:].astype(BF16)
    else:
        qkv_ref[0] = u[:, LRU_W:].astype(BF16)
    if cache:
        for r in range(tb // seq):
            rows = slice(r * seq, (r + 1) * seq)
            for ref, c0, heads in ((kw_ref, KB0, KV_B), (vw_ref, VB0, KV_B), (kn_ref, KC0, H_C), (vn_ref, VC0, H_C)):
                for p in range(heads // 2):
                    pair_t = u[rows, LRU_W + c0 + LANES * p: LRU_W + c0 + LANES * (p + 1)].T
                    ref[r, 0, 2 * p] = pair_t[:HEAD_DIM]
                    ref[r, 0, 2 * p + 1] = pair_t[HEAD_DIM:]


def _inproj_call(x3, mods, mod_row0, gains, w_in_b, layer, rope_tabs=None, cache_seq=None, depth=1, prev_caches=None):
    g, n, d = x3.shape
    d_in = w_in_b.shape[2]
    tb = TOKEN_BLOCK
    rope = rope_tabs is not None
    cache = cache_seq is not None
    in_specs = [pl.BlockSpec((1, tb, d), lambda i, j: (i, j, 0)),
                pl.BlockSpec((1, 1, N_MOD, d), lambda i, j: (layer, mod_row0 + i, 0, 0)),
                pl.BlockSpec((1, 1, d), lambda i, j: (layer, 0, 0)),
                pl.BlockSpec((1, d, d_in), lambda i, j: (layer, 0, 0))]
    args = [x3, mods, gains, w_in_b]
    assert d_in == LRU_W + ATT_W
    out_shape = [jax.ShapeDtypeStruct((g, n, LRU_W), F32),
                 jax.ShapeDtypeStruct((g, n, ATT_W), BF16)]
    out_specs = [pl.BlockSpec((1, tb, LRU_W), lambda i, j: (i, j, 0)),
                 pl.BlockSpec((1, tb, ATT_W), lambda i, j: (i, j, 0))]
    if rope:
        in_specs += [pl.BlockSpec((tb, VB0 - QB0), lambda i, j: (j, 0))] * 2
        args += list(rope_tabs)
    aliases = {}
    if cache:
        assert g == 1 and tb % cache_seq == 0
        rb = tb // cache_seq
        nreq = n // cache_seq
        for heads in (KV_B, KV_B, H_C, H_C):
            out_shape.append(jax.ShapeDtypeStruct((nreq, depth, heads, HEAD_DIM, cache_seq), F32))
            out_specs.append(pl.BlockSpec((rb, 1, heads, HEAD_DIM, cache_seq), lambda i, j: (j, layer, 0, 0, 0)))
        if prev_caches is not None:
            for k, buf in enumerate(prev_caches):
                aliases[len(args)] = 2 + k
                in_specs.append(pl.BlockSpec(memory_space=pl.ANY))
                args.append(buf)
    return pl.pallas_call(
        functools.partial(_inproj_kernel, rope=rope, cache=cache, seq=cache_seq),
        grid=(g, n // tb),
        in_specs=in_specs,
        out_specs=out_specs,
        out_shape=out_shape,
        input_output_aliases=aliases,
        compiler_params=_params(2),
        name="inproj_rope" if rope else "inproj_ctx",
    )(*args)


def _lru_kernel(*refs, n):
    u_ref, cw_ref, cb_ref, wg_ref, bg_ref, lam_ref, h0_ref = refs[:7]
    ya_ref, st_ref, a_s, x_s, y_s, hloc_s, ploc_s, hend_s, pend_s, cin_s = refs[-10:]
    c = 256
    nblk = n // LRU_SEG
    nh = c // LANES
    u = u_ref[0]
    xa = u[:, :c]
    ga = u[:, c:]
    t = lax.broadcasted_iota(jnp.int32, (n, c), 0)
    cw = cw_ref[0]
    xc = cw[2:3] * xa + cb_ref[0]
    xc = xc + cw[0:1] * jnp.where(t >= 2, pltpu.roll(xa, 2, 0), 0.0)
    xc = xc + cw[1:2] * jnp.where(t >= 1, pltpu.roll(xa, 1, 0), 0.0)
    xc = xc + cw[3:4] * jnp.where(t < n - 1, pltpu.roll(xa, n - 1, 0), 0.0)
    gates = _dot(xc.astype(BF16), wg_ref[0]) + bg_ref[0]
    lam = lam_ref[0]
    log_sig = jnp.minimum(lam, 0.0) - jnp.log1p(jnp.exp(-jnp.abs(lam)))
    for d in range(2):
        r = 0.5 * jnp.tanh(0.5 * gates[:, 2 * d * c:(2 * d + 1) * c]) + 0.5
        i = 0.5 * jnp.tanh(0.5 * gates[:, (2 * d + 1) * c:(2 * d + 2) * c]) + 0.5
        log_a = LRU_C * r * log_sig[d:d + 1]
        a = jnp.exp(log_a)
        t_in = jnp.tanh(-log_a) * (1.0 + a * a)
        xin = jnp.where(t_in > 0.0, t_in * lax.rsqrt(t_in), 0.0) * (i * xc)
        for hf in range(nh):
            a_s[hf] = a[:, LANES * hf: LANES * (hf + 1)]
            x_s[hf] = xin[:, LANES * hf: LANES * (hf + 1)]
        order = range(LRU_SEG) if d == 0 else range(LRU_SEG - 1, -1, -1)
        for hf in range(nh):
            h_run = p_run = None
            for s in order:
                a_row = a_s[hf, pl.ds(s, nblk, stride=LRU_SEG), :]
                x_row = x_s[hf, pl.ds(s, nblk, stride=LRU_SEG), :]
                h_run = x_row if h_run is None else a_row * h_run + x_row
                p_run = a_row if p_run is None else a_row * p_run
                hloc_s[d, hf, s] = h_run
                ploc_s[d, hf, s] = p_run
            hend_s[d, hf] = h_run
            pend_s[d, hf] = p_run

    h0 = h0_ref[0, 0]
    init = tuple(h0[d:d + 1, LANES * hf: LANES * (hf + 1)] for d in range(2) for hf in range(nh))

    def body(k, carry):
        out = []
        for idx, cur in enumerate(carry):
            d, hf = divmod(idx, nh)
            kk = k if d == 0 else nblk - 1 - k
            cin_s[d, hf, pl.ds(kk, 1), :] = cur
            out.append(pend_s[d, hf, pl.ds(kk, 1), :] * cur + hend_s[d, hf, pl.ds(kk, 1), :])
        return tuple(out)

    final = lax.fori_loop(0, nblk, body, init)

    y = None
    for d in range(2):
        for hf in range(nh):
            cin = cin_s[d, hf]
            for s in range(LRU_SEG):
                full_h = hloc_s[d, hf, s] + ploc_s[d, hf, s] * cin
                if d == 0:
                    y_s[hf, pl.ds(s, nblk, stride=LRU_SEG), :] = full_h
                else:
                    y_s[hf, pl.ds(s, nblk, stride=LRU_SEG), :] += full_h
    y = jnp.concatenate([y_s[hf] for hf in range(nh)], axis=1) * jax.nn.gelu(ga)
    ya_ref[0] = y.astype(BF16)
    st_ref[0, 0, 0:1, :] = jnp.concatenate(final[:nh], axis=1)
    st_ref[0, 0, 1:2, :] = jnp.concatenate(final[nh:], axis=1)


def _lru_call(ulru, conv_w, conv_b, w_gates_b, b_gates, lam, layer, h0, h0_layer, st_layer=0, st_depth=1,
              prev_state=None):
    b, n, _ = ulru.shape
    c = 256
    per_layer = lambda shape: pl.BlockSpec((1,) + shape, lambda i: (layer,) + (0,) * len(shape))
    nh, nblk = c // LANES, n // LRU_SEG
    assert n % LRU_SEG == 0 and nblk % 8 == 0
    scratch = ([pltpu.VMEM((nh, n, LANES), F32)] * 3
               + [pltpu.VMEM((2, nh, LRU_SEG, nblk, LANES), F32)] * 2
               + [pltpu.VMEM((2, nh, nblk, LANES), F32)] * 3)
    in_specs = [pl.BlockSpec((1, n, 2 * c), lambda i: (i, 0, 0)),
                per_layer((CONV_W, c)), per_layer((1, c)), per_layer((c, 4 * c)), per_layer((1, 4 * c)),
                per_layer((2, c)),
                pl.BlockSpec((1, 1, 2, c), lambda i: (i, h0_layer, 0, 0))]
    args = [ulru, conv_w, conv_b, w_gates_b, b_gates, lam, h0]
    aliases = {}
    if prev_state is not None:
        aliases[len(args)] = 1
        in_specs.append(pl.BlockSpec(memory_space=pl.ANY))
        args.append(prev_state)
    return pl.pallas_call(
        functools.partial(_lru_kernel, n=n),
        grid=(b,),
        in_specs=in_specs,
        out_specs=[pl.BlockSpec((1, n, c), lambda i: (i, 0, 0)),
                   pl.BlockSpec((1, 1, 2, c), lambda i: (i, st_layer, 0, 0))],
        out_shape=[jax.ShapeDtypeStruct((b, n, c), BF16),
                   jax.ShapeDtypeStruct((b, st_depth, 2, c), F32)],
        scratch_shapes=scratch,
        input_output_aliases=aliases,
        compiler_params=_params(1),
        name="rglru",
    )(*args)


def _softmax_pv(parts, sink):
    mx = None
    for s, _, _ in parts:
        cur = jnp.max(s, axis=-1, keepdims=True)
        mx = cur if mx is None else jnp.maximum(mx, cur)
    if sink is not None:
        mx = jnp.maximum(mx, sink)
    den = None
    out = None
    for s, v, v_t in parts:
        p = jnp.exp(s - mx)
        cur = jnp.sum(p, axis=-1, keepdims=True)
        den = cur if den is None else den + cur
        o = _dot_nt(p.astype(BF16), v) if v_t else _dot(p.astype(BF16), v)
        out = o if out is None else out + o
    if sink is not None:
        den = den + jnp.exp(sink - mx)
    return out / den


def _swap_halves(x):
    return jnp.concatenate([x[:, HEAD_DIM:], x[:, :HEAD_DIM]], axis=1)


def _keep_half(x, lo_mask, half):
    zero = jnp.zeros_like(x)
    return jnp.where(lo_mask, x, zero) if half == 0 else jnp.where(lo_mask, zero, x)


def _ctx_attn_kernel(sink_ref, att_ref, ob_ref, oc_ref, *, n, rb, sink0):
    lo = lax.broadcasted_iota(jnp.int32, (n, LANES), 1) < HEAD_DIM
    for r in range(rb):
        for p in range(H_C // 2):
            qp = att_ref[r, :, QC0 + LANES * p: QC0 + LANES * (p + 1)]
            kp = att_ref[r, :, KC0 + LANES * p: KC0 + LANES * (p + 1)]
            vp = att_ref[r, :, VC0 + LANES * p: VC0 + LANES * (p + 1)]
            outs = [_softmax_pv([(_dot_nt(qp, _keep_half(kp, lo, half)), vp, False)], None) for half in range(2)]
            oc_ref[r, :, LANES * p: LANES * (p + 1)] = jnp.where(lo, outs[0], outs[1]).astype(BF16)
        kpair = att_ref[r, :, KB0: KB0 + LANES]
        vpair = att_ref[r, :, VB0: VB0 + LANES]
        kpair_sw = _swap_halves(kpair)
        vpair_sw = _swap_halves(vpair)
        for p in range(H_B // 2):
            qp = att_ref[r, :, QB0 + LANES * p: QB0 + LANES * (p + 1)]
            outs = []
            for half in range(2):
                h = 2 * p + half
                aligned = (h // G_B) == half
                ksrc, vsrc = (kpair, vpair) if aligned else (kpair_sw, vpair_sw)
                outs.append(_softmax_pv([(_dot_nt(qp, _keep_half(ksrc, lo, half)), vsrc, False)], sink_ref[sink0 + h]))
            ob_ref[r, :, LANES * p: LANES * (p + 1)] = jnp.where(lo, outs[0], outs[1]).astype(BF16)


def _ctx_attn_call(sinks, att, layer):
    b, n, w = att.shape
    rb = CTX_ATTN_REQS
    assert b % rb == 0 and KV_B == 2
    return pl.pallas_call(
        functools.partial(_ctx_attn_kernel, n=n, rb=rb, sink0=layer * H_B),
        grid=(b // rb,),
        in_specs=[pl.BlockSpec(memory_space=pltpu.SMEM),
                  pl.BlockSpec((rb, n, w), lambda i: (i, 0, 0))],
        out_specs=[pl.BlockSpec((rb, n, 384), lambda i: (i, 0, 0))] * 2,
        out_shape=[jax.ShapeDtypeStruct((b, n, 384), BF16)] * 2,
        compiler_params=_params(1),
        name="ctx_attn",
    )(sinks, att)


def _win_attn_kernel(sink_ref, qkv_ref, kc_ref, vc_ref, o_ref, *, n, sink0):
    j = pl.program_id(1)
    nq = WIN_Q_BLOCKS * WIN_BLK
    nloc = nq + 2 * WIN_BLK
    q0 = pl.multiple_of(j * nq, WIN_BLK)
    ks = pl.multiple_of(jnp.clip(j * nq - WIN_BLK, 0, n - nloc), WIN_BLK)
    row = lax.broadcasted_iota(jnp.int32, (nq, nloc), 0)
    col = lax.broadcasted_iota(jnp.int32, (nq, nloc), 1)
    in_window = jnp.where(jnp.abs((q0 + row) - (ks + col)) <= WINDOW, 0.0, NEG_INF)
    band = jnp.concatenate([in_window] * G_B, axis=0)
    rowh = lax.broadcasted_iota(jnp.int32, (G_B * nq, 1), 0) // nq
    for kv in range(KV_B):
        gw = G_B * HEAD_DIM
        qblk = qkv_ref[0, pl.ds(q0, nq), QB0 + gw * kv: QB0 + gw * (kv + 1)]
        q3 = jnp.concatenate([qblk[:, 64 * g: 64 * (g + 1)] for g in range(G_B)], axis=0)
        kl = qkv_ref[0, pl.ds(ks, nloc), KB0 + 64 * kv: KB0 + 64 * (kv + 1)]
        vl = qkv_ref[0, pl.ds(ks, nloc), VB0 + 64 * kv: VB0 + 64 * (kv + 1)]
        kc_t = kc_ref[0, 0, kv].astype(BF16)
        vc_t = vc_ref[0, 0, kv].astype(BF16)
        s_loc = _dot_nt(q3, kl) + band
        s_ctx = _dot(q3, kc_t)
        s0 = sink0 + G_B * kv
        sk = jnp.where(rowh == 0, sink_ref[s0], jnp.where(rowh == 1, sink_ref[s0 + 1], sink_ref[s0 + 2]))
        o = _softmax_pv([(s_loc, vl, False), (s_ctx, vc_t, True)], sk)
        for g in range(G_B):
            h = kv * G_B + g
            o_ref[0, :, 64 * h: 64 * (h + 1)] = o[g * nq:(g + 1) * nq].astype(BF16)


def _win_attn_call(sinks, qkv, cache_k, cache_v, layer):
    b, n, w = qkv.shape
    past = cache_k.shape[4]
    cache_spec = pl.BlockSpec((1, 1, KV_B, HEAD_DIM, past), lambda i, j: (i, layer, 0, 0, 0))
    nq = WIN_Q_BLOCKS * WIN_BLK
    assert n % nq == 0 and n >= nq + 2 * WIN_BLK and WINDOW == WIN_BLK
    return pl.pallas_call(
        functools.partial(_win_attn_kernel, n=n, sink0=layer * H_B),
        grid=(b, n // nq),
        in_specs=[pl.BlockSpec(memory_space=pltpu.SMEM),
                  pl.BlockSpec((1, n, w), lambda i, j: (i, 0, 0)),
                  cache_spec, cache_spec],
        out_specs=pl.BlockSpec((1, nq, 384), lambda i, j: (i, j, 0)),
        out_shape=jax.ShapeDtypeStruct((b, n, 384), BF16),
        compiler_params=_params(2),
        name="win_attn",
    )(sinks, qkv, cache_k, cache_v)


def _nbr_window_start(g, rows):
    return jnp.clip(g * NBR_Q_ROWS - NA_ROWS // 2, 0, rows - NBR_K_ROWS)


def _nbr_attn_kernel(q_ref, k_ref, v_ref, rel_ref, kc_ref, vc_ref, o_ref, bias_s, *, n):
    g = pl.program_id(0)
    nq = NBR_Q_ROWS * GRID_W
    nk = NBR_K_ROWS * GRID_W
    rows = n // GRID_W
    ws = _nbr_window_start(g, rows)
    k0 = pl.multiple_of(ws * GRID_W, GRID_W)

    @pl.when(pl.program_id(1) == 0)
    def _():
        for i in range(NBR_Q_ROWS):
            qr = g * NBR_Q_ROWS + i
            rs = jnp.clip(qr - NA_ROWS // 2, 0, rows - NA_ROWS)
            for jp in range(NBR_K_ROWS // 2):
                tiles = []
                for j in (2 * jp, 2 * jp + 1):
                    kr = ws + j
                    valid = (kr >= rs) & (kr < rs + NA_ROWS)
                    d = jnp.clip(kr - qr + NA_ROWS - 1, 0, 2 * NA_ROWS - 2)
                    tiles.append(jnp.where(valid, rel_ref[0, :, pl.ds(d, 1)][:, 0], NEG_INF))
                bias_s[:, GRID_W * i: GRID_W * (i + 1), LANES * jp: LANES * (jp + 1)] = jnp.concatenate(tiles, axis=-1)
    lo_k = lax.broadcasted_iota(jnp.int32, (nk, LANES), 1) < HEAD_DIM
    lo_q = lax.broadcasted_iota(jnp.int32, (nq, LANES), 1) < HEAD_DIM
    zpad = jnp.zeros((HEAD_DIM, kc_ref.shape[4]), BF16)
    for p in range(H_C // 2):
        qp = q_ref[0, :, LANES * p: LANES * (p + 1)]
        kp = k_ref[0, pl.ds(k0, nk), LANES * p: LANES * (p + 1)]
        vp = v_ref[0, pl.ds(k0, nk), LANES * p: LANES * (p + 1)]
        outs = []
        for half in range(2):
            h = 2 * p + half
            kc_t = kc_ref[0, 0, h].astype(BF16)
            vc_t = vc_ref[0, 0, h].astype(BF16)
            kc_t = jnp.concatenate([kc_t, zpad] if half == 0 else [zpad, kc_t], axis=0)
            vc_t = jnp.concatenate([vc_t, zpad] if half == 0 else [zpad, vc_t], axis=0)
            s_loc = _dot_nt(qp, _keep_half(kp, lo_k, half)) + bias_s[h]
            s_ctx = _dot(qp, kc_t)
            outs.append(_softmax_pv([(s_loc, vp, False), (s_ctx, vc_t, True)], None))
        o_ref[0, :, LANES * p: LANES * (p + 1)] = jnp.where(lo_q, outs[0], outs[1]).astype(BF16)


def _nbr_attn_call(att, rel_tab, cache_k, cache_v, layer):
    b, n, w = att.shape
    past = cache_k.shape[4]
    nq = NBR_Q_ROWS * GRID_W
    nk = NBR_K_ROWS * GRID_W
    gw = H_C * HEAD_DIM
    rows = n // GRID_W
    assert (QC0, KC0, VC0) == (0, gw, 2 * gw) and n % nq == 0 and rows >= NBR_K_ROWS and NBR_K_ROWS % 2 == 0
    cache_spec = pl.BlockSpec((1, 1, H_C, HEAD_DIM, past), lambda g, i: (i, layer, 0, 0, 0))
    return pl.pallas_call(
        functools.partial(_nbr_attn_kernel, n=n),
        grid=(n // nq, b),
        in_specs=[pl.BlockSpec((1, nq, gw), lambda g, i: (i, g, 0)),
                  pl.BlockSpec((1, n, gw), lambda g, i: (i, 0, 1)),
                  pl.BlockSpec((1, n, gw), lambda g, i: (i, 0, 2)),
                  pl.BlockSpec((1,) + rel_tab.shape[1:], lambda g, i: (layer, 0, 0, 0, 0)),
                  cache_spec, cache_spec],
        out_specs=pl.BlockSpec((1, nq, gw), lambda g, i: (i, g, 0)),
        out_shape=jax.ShapeDtypeStruct((b, n, gw), BF16),
        scratch_shapes=[pltpu.VMEM((H_C, nq, nk), F32)],
        compiler_params=_params(2),
        name="nbr_attn",
    )(att, att, att, rel_tab, cache_k, cache_v)


def _outproj_kernel(x_ref, ya_ref, yb_ref, yc_ref, mod_ref, g_ref, wo_ref, wr_ref,
                    xn_ref, h2_ref, aff_ref):
    mix = jnp.concatenate([ya_ref[0], yb_ref[0], yc_ref[0]], axis=1)
    proj = _dot(mix, wo_ref[0])
    xn = x_ref[0] + mod_ref[0, 0, 2:3, :] * proj
    xn_ref[0] = xn
    h2 = _rms_modulate(xn, g_ref[0], mod_ref[0, 0, 3:4, :], mod_ref[0, 0, 4:5, :]).astype(BF16)
    h2_ref[0] = h2
    logits = _dot(h2, wr_ref[0])
    lane = lax.broadcasted_iota(jnp.int32, logits.shape, 1)
    valid = lane < N_EXPERTS
    logits = jnp.where(valid, logits, NEG_INF)
    e = jnp.exp(logits - jnp.max(logits, axis=-1, keepdims=True))
    e = jnp.where(valid, e, 0.0)
    aff_ref[0] = e / jnp.sum(e, axis=-1, keepdims=True)


def _outproj_call(x3, ya, yb, yc, mods, mod_row0, gains, wo_b16, w_router_b, layer):
    g, n, d = x3.shape
    tb = TOKEN_BLOCK
    tok = lambda w: pl.BlockSpec((1, tb, w), lambda i, j: (i, j, 0))
    per_layer = lambda shape: pl.BlockSpec((1,) + shape, lambda i, j: (layer,) + (0,) * len(shape))
    return pl.pallas_call(
        _outproj_kernel,
        grid=(g, n // tb),
        in_specs=[tok(d), tok(256), tok(384), tok(384),
                  pl.BlockSpec((1, 1, N_MOD, d), lambda i, j: (layer, mod_row0 + i, 0, 0)),
                  per_layer((1, d)), per_layer((256 + 384 + 384, d)), per_layer((d, LANES))],
        out_specs=[tok(d), tok(d), tok(LANES)],
        out_shape=[jax.ShapeDtypeStruct((g, n, d), F32),
                   jax.ShapeDtypeStruct((g, n, d), BF16),
                   jax.ShapeDtypeStruct((g, n, LANES), F32)],
        compiler_params=_params(2),
        name="outproj_router",
    )(x3, ya, yb, yc, mods, gains, wo_b16, w_router_b)


def _sort_by_affinity(keys, idx, pos, levels, seg_lanes):
    rows = keys.shape[0]
    sign = [jnp.where(((pos >> m) & 1) == 0, 1.0, -1.0) for m in range(levels)]

    def partner(x, j):
        d = 1 << j
        low = sign[j] > 0.0
        if d >= rows:
            sh = (d // rows) * seg_lanes
            return jnp.where(low, pltpu.roll(x, LANES - sh, 1), pltpu.roll(x, sh, 1))
        if d >= 8:
            x4 = x.reshape(rows // (2 * d), 2, d, LANES)
            return jnp.concatenate([x4[:, 1:2], x4[:, 0:1]], axis=1).reshape(rows, LANES)
        x3 = x.reshape(rows // 8, 8, LANES)
        low3 = sign[j].reshape(rows // 8, 8, LANES) > 0.0
        return jnp.where(low3, pltpu.roll(x3, 8 - d, 1), pltpu.roll(x3, d, 1)).reshape(rows, LANES)

    for k in range(1, levels + 1):
        for j in range(k - 1, -1, -1):
            pk, pi = partner(keys, j), partner(idx, j)
            before = jnp.where(keys == pk, jnp.where(idx < pi, 1.0, -1.0), jnp.where(keys > pk, 1.0, -1.0))
            want_first = sign[j] * sign[k] if k < levels else sign[j]
            keep = before * want_first > 0.0
            keys = jnp.where(keep, keys, pk)
            idx = jnp.where(keep, idx, pi)
    return keys, idx


def _route_kernel(aff_ref, h2_ref, xs_ref, vals_ref, rank_ref, *, n, cap, rb):
    ne = N_EXPERTS
    rows = ROUTE_SORT_ROWS
    nseg = n // rows
    seg_lanes = rb * ne
    levels = n.bit_length() - 1
    fcap = float(cap)
    row = lax.broadcasted_iota(jnp.int32, (rows, LANES), 0)
    lane = lax.broadcasted_iota(jnp.int32, (rows, LANES), 1)
    pos = (lane // seg_lanes) * rows + row
    keys = jnp.full((rows, LANES), -1.0, F32)
    for r in range(rb):
        for seg in range(nseg):
            off = seg * seg_lanes + r * ne
            piece = aff_ref[r, seg * rows:(seg + 1) * rows, :]
            if off:
                piece = pltpu.roll(piece, off, 1)
            keys = jnp.where((lane >= off) & (lane < off + ne), piece, keys)
    keys, idx = _sort_by_affinity(keys, pos.astype(F32), pos, levels, seg_lanes)
    top_keys = keys[:cap]
    top_idx = idx[:cap]

    tok = lax.broadcasted_iota(jnp.int32, (cap, n), 1).astype(F32)
    weight = fcap - lax.broadcasted_iota(jnp.int32, (cap, n), 0).astype(F32)
    pad = jnp.full((LANES - ne, n), fcap, F32)
    for r in range(rb):
        picks, rank_rows = [], []
        for e in range(ne):
            col = r * ne + e
            onehot = jnp.where(top_idx[:, col:col + 1] == tok, 1.0, 0.0)
            vals_ref[e, r] = top_keys[:, col:col + 1]
            rank_rows.append(fcap - jnp.sum(onehot * weight, axis=0, keepdims=True))
            picks.append(onehot.astype(BF16))
        rank_ref[r] = jnp.concatenate(rank_rows + [pad], axis=0).T
        xs = _dot(jnp.concatenate(picks, axis=0), h2_ref[r])
        xs_ref[:, r] = xs.reshape(ne, cap, xs.shape[-1]).astype(BF16)


def _route_call(aff, h2, rb):
    b, n, d = h2.shape
    cap = max(1, EC_FACTOR * n // N_EXPERTS)
    nseg = n // ROUTE_SORT_ROWS
    assert b % rb == 0 and n == nseg * ROUTE_SORT_ROWS and n & (n - 1) == 0
    assert nseg * rb * N_EXPERTS <= LANES and cap <= ROUTE_SORT_ROWS
    return pl.pallas_call(
        functools.partial(_route_kernel, n=n, cap=cap, rb=rb),
        grid=(b // rb,),
        in_specs=[pl.BlockSpec((rb, n, LANES), lambda i: (i, 0, 0)),
                  pl.BlockSpec((rb, n, d), lambda i: (i, 0, 0))],
        out_specs=[pl.BlockSpec((N_EXPERTS, rb, cap, d), lambda i: (0, i, 0, 0)),
                   pl.BlockSpec((N_EXPERTS, rb, cap, 1), lambda i: (0, i, 0, 0)),
                   pl.BlockSpec((rb, n, LANES), lambda i: (i, 0, 0))],
        out_shape=[jax.ShapeDtypeStruct((N_EXPERTS, b, cap, d), BF16),
                   jax.ShapeDtypeStruct((N_EXPERTS, b, cap, 1), F32),
                   jax.ShapeDtypeStruct((b, n, LANES), F32)],
        compiler_params=_params(1),
        name="route_gather",
    )(aff, h2)


def _expert_kernel(xc_ref, xl_ref, vc_ref, vl_ref, wg_ref, wu_ref, wd_ref, yc_ref, yl_ref, acc_c, acc_l):
    f = pl.program_id(1)

    def step(first):
        wg = wg_ref[0, 0].astype(BF16)
        wu = wu_ref[0, 0].astype(BF16)
        wd = wd_ref[0, 0].astype(BF16)
        for x_ref, v_ref, y_ref, acc in ((xc_ref, vc_ref, yc_ref, acc_c), (xl_ref, vl_ref, yl_ref, acc_l)):
            rows = x_ref.shape[1]
            chunk = min(EXPERT_ROW_CHUNK, rows)
            for c0 in range(0, rows, chunk):
                sl = slice(c0, c0 + chunk)
                x = x_ref[0, sl, :]
                a = _dot(x, wg)
                u = _dot(x, wu)
                act = ((a * jax.nn.sigmoid(a)) * u).astype(BF16)
                part = _dot(act, wd)
                if first:
                    acc[sl, :] = part
                else:
                    y_ref[0, sl, :] = ((acc[sl, :] + part) * v_ref[0, sl, :]).astype(BF16)

    pl.when(f == 0)(functools.partial(step, True))
    pl.when(f == EXPERT_F_STEPS - 1)(functools.partial(step, False))


def _expert_call(xs_c, xs_l, vals_c, vals_l, w_gate, w_up, w_down, layer):
    e, rc, d = xs_c.shape
    rl = xs_l.shape[1]
    f_total = w_gate.shape[-1]
    assert EXPERT_F_STEPS == 2
    fb = f_total // EXPERT_F_STEPS
    per_e = lambda rows, w: pl.BlockSpec((1, rows, w), lambda i, j: (i, 0, 0))
    return pl.pallas_call(
        _expert_kernel,
        grid=(e, EXPERT_F_STEPS),
        in_specs=[per_e(rc, d), per_e(rl, d), per_e(rc, 1), per_e(rl, 1),
                  pl.BlockSpec((1, 1, d, fb), lambda i, j: (layer, i, 0, j)),
                  pl.BlockSpec((1, 1, d, fb), lambda i, j: (layer, i, 0, j)),
                  pl.BlockSpec((1, 1, fb, d), lambda i, j: (layer, i, j, 0))],
        out_specs=[per_e(rc, d), per_e(rl, d)],
        out_shape=[jax.ShapeDtypeStruct((e, rc, d), BF16),
                   jax.ShapeDtypeStruct((e, rl, d), BF16)],
        scratch_shapes=[pltpu.VMEM((rc, d), F32), pltpu.VMEM((rl, d), F32)],
        compiler_params=_params(2),
        name="experts",
    )(xs_c, xs_l, vals_c, vals_l, w_gate, w_up, w_down)


def _combine_kernel(y_ref, rank_ref, xn_ref, mod_ref, gf_ref, o_ref, *, n, cap, final, rb):
    ec = N_EXPERTS * cap
    d = xn_ref.shape[-1]
    ce = lax.broadcasted_iota(jnp.int32, (LANES, ec), 1) // cap
    ee = lax.broadcasted_iota(jnp.int32, (LANES, ec), 0)
    expand = jnp.where(ce == ee, 1.0, 0.0).astype(BF16)
    slot = (lax.broadcasted_iota(jnp.int32, (n, ec), 1) % cap).astype(F32)
    for r in range(rb):
        rank = rank_ref[r].astype(BF16)
        rexp = _dot(rank, expand)
        onehot = jnp.where(rexp == slot, 1.0, 0.0).astype(BF16)
        moe = _dot(onehot, y_ref[:, r].reshape(ec, d))
        x = xn_ref[r] + mod_ref[0, 0, 5:6, :] * moe
        if final:
            x = (x * lax.rsqrt(jnp.mean(x * x, axis=-1, keepdims=True) + EPS)) * gf_ref[...]
        o_ref[r] = x


def _combine_call(y4, rank, xn, mods, mod_row0, shared_mod, layer, g_final, final):
    e, b, cap, d = y4.shape
    n = xn.shape[1]
    rb = COMBINE_REQS_CTX if shared_mod else 1
    assert b % rb == 0
    return pl.pallas_call(
        functools.partial(_combine_kernel, n=n, cap=cap, final=final, rb=rb),
        grid=(b // rb,),
        in_specs=[pl.BlockSpec((e, rb, cap, d), lambda i: (0, i, 0, 0)),
                  pl.BlockSpec((rb, n, LANES), lambda i: (i, 0, 0)),
                  pl.BlockSpec((rb, n, d), lambda i: (i, 0, 0)),
                  pl.BlockSpec((1, 1, N_MOD, d), (lambda i: (layer, mod_row0, 0, 0)) if shared_mod
                               else (lambda i: (layer, mod_row0 + i, 0, 0))),
                  pl.BlockSpec((1, d), lambda i: (0, 0))],
        out_specs=pl.BlockSpec((rb, n, d), lambda i: (i, 0, 0)),
        out_shape=jax.ShapeDtypeStruct((b, n, d), F32),
        compiler_params=_params(1),
        name="combine",
    )(y4, rank, xn, mods, g_final)


def _rope_tables(n):
    t = jnp.arange(n, dtype=jnp.int32)
    row = (t // GRID_W).astype(F32)
    col = (t % GRID_W).astype(F32)
    half = HEAD_DIM // 4
    inv = jnp.power(ROPE_BASE, -jnp.arange(half, dtype=F32) / half)
    ang_r = row[:, None] * inv[None, :]
    ang_c = col[:, None] * inv[None, :]
    cos_h = jnp.concatenate([jnp.cos(ang_r)] * 2 + [jnp.cos(ang_c)] * 2, axis=-1)
    sin_h = jnp.concatenate([-jnp.sin(ang_r), jnp.sin(ang_r), -jnp.sin(ang_c), jnp.sin(ang_c)], axis=-1)
    reps = 512 // HEAD_DIM
    return jnp.tile(cos_h, (1, reps)), jnp.tile(sin_h, (1, reps))


def _nbr_rel_tables(rpb):
    col = np.arange(GRID_W)
    cs = np.clip(col - NA_COLS // 2, 0, GRID_W - NA_COLS)
    col_mask = (col[None, :] >= cs[:, None]) & (col[None, :] < cs[:, None] + NA_COLS)
    dc_idx = np.clip(col[None, :] - col[:, None], -(NA_COLS - 1), NA_COLS - 1) + (NA_COLS - 1)
    sel_col = (dc_idx[None, :, :] == np.arange(2 * NA_COLS - 1)[:, None, None]).astype(np.float32)
    t = jnp.einsum("lhdc,cqk->lhdqk", rpb.astype(F32), sel_col, precision=lax.Precision.HIGHEST)
    return jnp.where(col_mask, t, NEG_INF)


def _block_diag(w):
    nb, bw = w.shape[-3], w.shape[-1]
    eye = jnp.eye(nb, dtype=w.dtype)
    return (eye[:, None, :, None] * w[..., :, :, None, :]).reshape(w.shape[:-3] + (nb * bw, nb * bw))


def kernel(x_prompt, x_sample, state_lru, cache_k_win, cache_v_win, cache_k_nbr, cache_v_nbr, c, c_ctx, w_mod, b_mod, g_norm1, w_in, conv_w, conv_b, w_gate_r, b_gate_r, w_gate_i, b_gate_i, lru_lambda, sink_logit, nbr_bias, w_out, g_norm2, w_router, w_exp_gate, w_exp_up, w_exp_down, g_final):
    bc, seq, d = x_prompt.shape
    bl, n_lat, _ = x_sample.shape
    depth = w_mod.shape[0]
    assert bl + 1 <= 8 and d == 1024

    cond8 = jnp.zeros((8, d), F32).at[0].set(c_ctx).at[1:1 + bl].set(c)
    mods = _adaln_call(cond8, w_mod, b_mod.reshape(depth, 1, -1)).reshape(depth, 8, N_MOD, d)
    rope_tabs = _rope_tables(n_lat)
    bias_tab = _nbr_rel_tables(nbr_bias)
    ckw_t, cvw_t, ckn_t, cvn_t = (jnp.swapaxes(a, -1, -2) for a in (cache_k_win, cache_v_win, cache_k_nbr, cache_v_nbr))
    gf = g_final.reshape(1, d)
    assert SCALE == 0.125
    new_cols = np.arange(w_in.shape[-1]) - LRU_W
    is_q = ((new_cols >= QC0) & (new_cols < KC0)) | ((new_cols >= QB0) & (new_cols < KB0))
    q_scale = jnp.asarray(np.where(is_q, SCALE, 1.0), F32)

    w_in_b = (jnp.concatenate([w_in[..., :512], w_in[..., 1152:], w_in[..., 512:1152]], axis=-1) * q_scale).astype(BF16)
    wo = w_out.astype(BF16)
    wr_b = jnp.pad(w_router.astype(BF16), ((0, 0), (0, 0), (0, LANES - N_EXPERTS)))
    gate_r, gate_i = _block_diag(w_gate_r), _block_diag(w_gate_i)
    w_gates = jnp.concatenate([gate_r[:, 0], gate_i[:, 0], gate_r[:, 1], gate_i[:, 1]], axis=-1).astype(BF16)
    b_gates = jnp.concatenate([b_gate_r[:, 0], b_gate_i[:, 0], b_gate_r[:, 1], b_gate_i[:, 1]], axis=-1)[:, None, :]
    cb = conv_b[:, None, :]
    g1 = g_norm1[:, None, :]
    g2 = g_norm2[:, None, :]
    sinks = sink_logit.reshape(-1)
    ctx_row, lat_row = 0, 1

    xc = x_prompt.reshape(1, bc * seq, d)
    xl = x_sample
    zeros_state = jnp.zeros((bc, 1, 2, 256), F32)
    caches = None
    st = None
    for l in range(depth):
        final = l == depth - 1
        ulru_c, qkv_c, *caches = _inproj_call(xc, mods, ctx_row, g1, w_in_b, l, cache_seq=seq, depth=depth,
                                              prev_caches=caches)
        ya_c, st = _lru_call(ulru_c.reshape(bc, seq, 512), conv_w, cb, w_gates, b_gates, lru_lambda, l,
                             zeros_state, 0, st_layer=l, st_depth=depth, prev_state=st)
        yb_c, yc_c = _ctx_attn_call(sinks, qkv_c.reshape(bc, seq, -1), l)
        xn_c, h2_c, aff_c = _outproj_call(xc, ya_c.reshape(1, bc * seq, -1), yb_c.reshape(1, bc * seq, -1),
                                          yc_c.reshape(1, bc * seq, -1), mods, ctx_row, g2, wo, wr_b, l)
        xs_c, vals_c, rank_c = _route_call(aff_c.reshape(bc, seq, LANES), h2_c.reshape(bc, seq, d), ROUTE_REQS_CTX)

        ulru_l, qkv_l = _inproj_call(xl, mods, lat_row, g1, w_in_b, l, rope_tabs=rope_tabs)
        ya_l, _ = _lru_call(ulru_l, conv_w, cb, w_gates, b_gates, lru_lambda, l, state_lru, l)
        yb_l = _win_attn_call(sinks, qkv_l, ckw_t, cvw_t, l)
        yc_l = _nbr_attn_call(qkv_l, bias_tab, ckn_t, cvn_t, l)
        xn_l, h2_l, aff_l = _outproj_call(xl, ya_l, yb_l, yc_l, mods, lat_row, g2, wo, wr_b, l)
        xs_l, vals_l, rank_l = _route_call(aff_l, h2_l, ROUTE_REQS_LAT)

        cap_c, cap_l = xs_c.shape[2], xs_l.shape[2]
        y_c, y_l = _expert_call(xs_c.reshape(N_EXPERTS, bc * cap_c, d), xs_l.reshape(N_EXPERTS, bl * cap_l, d),
                                vals_c.reshape(N_EXPERTS, bc * cap_c, 1), vals_l.reshape(N_EXPERTS, bl * cap_l, 1),
                                w_exp_gate, w_exp_up, w_exp_down, l)
        xc = _combine_call(y_c.reshape(N_EXPERTS, bc, cap_c, d), rank_c, xn_c.reshape(bc, seq, d), mods, ctx_row, True,
                           l, gf, final)
        xl = _combine_call(y_l.reshape(N_EXPERTS, bl, cap_l, d), rank_l, xn_l, mods, lat_row, False, l, gf, final)
        xc = xc.reshape(1, bc * seq, d)

    y_prompt = xc.reshape(bc, seq, d)
    y_sample = xl
    return (y_prompt, y_sample, st, *[jnp.swapaxes(buf, -1, -2) for buf in caches])
```

```python
import functools

import numpy as np
import jax
import jax.numpy as jnp
from jax import lax
from jax.experimental import pallas as pl
from jax.experimental.pallas import tpu as pltpu

F32 = jnp.float32
BF16 = jnp.bfloat16

HEAD_DIM = 64
GRID_W = 64
LRU_C = 8.0
CONV_W = 4
KV_B = 2
G_B = 3
H_B = KV_B * G_B
H_C = 6
WINDOW = 128
WIN_BLK = 128
NA_ROWS = 8
NA_COLS = 16
ROPE_BASE = 10000.0
N_EXPERTS = 16
EC_FACTOR = 2
N_MOD = 6
EPS = 1e-6
NEG_INF = -1e30
SCALE = HEAD_DIM ** -0.5

LANES = 128
VMEM_LIMIT_BYTES = 56 * 1024 * 1024

TOKEN_BLOCK = 1024
MOD_COL_BLOCK = 1536
EXPERT_F_STEPS = 2
EXPERT_ROW_CHUNK = 256
D_A = 256
D_B = H_B * HEAD_DIM
D_C = H_C * HEAD_DIM
KV_W = KV_B * HEAD_DIM
LRU_W = 2 * D_A
QC0, KC0, VC0 = 0, D_C, 2 * D_C
QB0 = 3 * D_C
KB0, VB0, ATT_W = QB0 + D_B, QB0 + D_B + KV_W, QB0 + D_B + 2 * KV_W
CTX_ATTN_REQS = 4
WIN_Q_BLOCKS = 2
NBR_Q_ROWS = 4
NBR_K_ROWS = 12
ROUTE_REQS_CTX = 8
ROUTE_REQS_LAT = 2
ROUTE_SORT_ROWS = 256
COMBINE_REQS_CTX = 4
LRU_SEG = 8


def _params(n_axes):
    return pltpu.CompilerParams(dimension_semantics=("arbitrary",) * n_axes,
                                vmem_limit_bytes=VMEM_LIMIT_BYTES)


def _dot(a, b):
    return jnp.dot(a, b, preferred_element_type=F32)


def _dot_nt(a, b):
    return lax.dot_general(a, b, (((1,), (1,)), ((), ())), preferred_element_type=F32)


def _adaln_kernel(cond_ref, w_ref, b_ref, o_ref):
    c = cond_ref[...]
    s = c * jax.nn.sigmoid(c)
    o_ref[0] = _dot(s.astype(BF16), w_ref[0].astype(BF16)) + b_ref[0]


def _adaln_call(cond8, w_mod, b_mod3):
    depth, d, n = w_mod.shape
    nb = MOD_COL_BLOCK
    return pl.pallas_call(
        _adaln_kernel,
        grid=(depth, n // nb),
        in_specs=[pl.BlockSpec((8, d), lambda l, j: (0, 0)),
                  pl.BlockSpec((1, d, nb), lambda l, j: (l, 0, j)),
                  pl.BlockSpec((1, 1, nb), lambda l, j: (l, 0, j))],
        out_specs=pl.BlockSpec((1, 8, nb), lambda l, j: (l, 0, j)),
        out_shape=jax.ShapeDtypeStruct((depth, 8, n), F32),
        compiler_params=_params(2),
        name="adaln",
    )(cond8, w_mod, b_mod3)


def _rms_modulate(x, gain, shift, scale):
    y = x * lax.rsqrt(jnp.mean(x * x, axis=-1, keepdims=True) + EPS)
    return (y * gain) * (1.0 + scale) + shift


def _inproj_kernel(*refs, rope, cache, seq):
    if rope:
        x_ref, mod_ref, g_ref, w_ref, cos_ref, sin_ref, ulru_ref, qkv_ref = refs
    elif cache:
        x_ref, mod_ref, g_ref, w_ref = refs[:4]
        ulru_ref, qkv_ref, kw_ref, vw_ref, kn_ref, vn_ref = refs[-6:]
    else:
        x_ref, mod_ref, g_ref, w_ref, ulru_ref, qkv_ref = refs
    x = x_ref[0]
    h = _rms_modulate(x, g_ref[0], mod_ref[0, 0, 0:1, :], mod_ref[0, 0, 1:2, :])
    u = _dot(h.astype(BF16), w_ref[0])
    tb = u.shape[0]
    ulru_ref[0] = u[:, :LRU_W]
    if rope:
        lo, hi = LRU_W + QB0, LRU_W + VB0
        qk = u[:, lo:hi]
        lane = lax.broadcasted_iota(jnp.int32, (tb, hi - lo), 1)
        first = (lane & 31) < 16
        partner = jnp.where(first, pltpu.roll(qk, hi - lo - 16, 1), pltpu.roll(qk, 16, 1))
        qk = qk * cos_ref[...] + partner * sin_ref[...]
        qkv_ref[0, :, :QB0] = u[:, LRU_W:lo].astype(BF16)
        qkv_ref[0, :, QB0:VB0] = qk.astype(BF16)
        qkv_ref[0, :, VB0:] = u[:, hi:].astype(BF16)
    else:
        qkv_ref[0] = u[:, LRU_W:].astype(BF16)
    if cache:
        for r in range(tb // seq):
            rows = slice(r * seq, (r + 1) * seq)
            for ref, c0, heads in ((kw_ref, KB0, KV_B), (vw_ref, VB0, KV_B), (kn_ref, KC0, H_C), (vn_ref, VC0, H_C)):
                for p in range(heads // 2):
                    pair_t = u[rows, LRU_W + c0 + LANES * p: LRU_W + c0 + LANES * (p + 1)].T
                    ref[r, 0, 2 * p] = pair_t[:HEAD_DIM]
                    ref[r, 0, 2 * p + 1] = pair_t[HEAD_DIM:]


def _inproj_call(x3, mods, mod_row0, gains, w_in_b, layer, rope_tabs=None, cache_seq=None, depth=1, prev_caches=None):
    g, n, d = x3.shape
    d_in = w_in_b.shape[2]
    tb = TOKEN_BLOCK
    rope = rope_tabs is not None
    cache = cache_seq is not None
    in_specs = [pl.BlockSpec((1, tb, d), lambda i, j: (i, j, 0)),
                pl.BlockSpec((1, 1, N_MOD, d), lambda i, j: (layer, mod_row0 + i, 0, 0)),
                pl.BlockSpec((1, 1, d), lambda i, j: (layer, 0, 0)),
                pl.BlockSpec((1, d, d_in), lambda i, j: (layer, 0, 0))]
    args = [x3, mods, gains, w_in_b]
    assert d_in == LRU_W + ATT_W
    out_shape = [jax.ShapeDtypeStruct((g, n, LRU_W), F32),
                 jax.ShapeDtypeStruct((g, n, ATT_W), BF16)]
    out_specs = [pl.BlockSpec((1, tb, LRU_W), lambda i, j: (i, j, 0)),
                 pl.BlockSpec((1, tb, ATT_W), lambda i, j: (i, j, 0))]
    if rope:
        in_specs += [pl.BlockSpec((tb, VB0 - QB0), lambda i, j: (j, 0))] * 2
        args += list(rope_tabs)
    aliases = {}
    if cache:
        assert g == 1 and tb % cache_seq == 0
        rb = tb // cache_seq
        nreq = n // cache_seq
        for heads in (KV_B, KV_B, H_C, H_C):
            out_shape.append(jax.ShapeDtypeStruct((nreq, depth, heads, HEAD_DIM, cache_seq), F32))
            out_specs.append(pl.BlockSpec((rb, 1, heads, HEAD_DIM, cache_seq), lambda i, j: (j, layer, 0, 0, 0)))
        if prev_caches is not None:
            for k, buf in enumerate(prev_caches):
                aliases[len(args)] = 2 + k
                in_specs.append(pl.BlockSpec(memory_space=pl.ANY))
                args.append(buf)
    return pl.pallas_call(
        functools.partial(_inproj_kernel, rope=rope, cache=cache, seq=cache_seq),
        grid=(g, n // tb),
        in_specs=in_specs,
        out_specs=out_specs,
        out_shape=out_shape,
        input_output_aliases=aliases,
        compiler_params=_params(2),
        name="inproj_rope" if rope else "inproj_ctx",
    )(*args)


def _lru_kernel(*refs, n):
    u_ref, cw_ref, cb_ref, wg_ref, bg_ref, lam_ref, h0_ref = refs[:7]
    ya_ref, st_ref, a_s, x_s, y_s, hloc_s, ploc_s, hend_s, pend_s, cin_s = refs[-10:]
    c = D_A
    nblk = n // LRU_SEG
    nh = c // LANES
    u = u_ref[0]
    xa = u[:, :c]
    ga = u[:, c:]
    t = lax.broadcasted_iota(jnp.int32, (n, c), 0)
    cw = cw_ref[0]
    xc = cw[2:3] * xa + cb_ref[0]
    xc = xc + cw[0:1] * jnp.where(t >= 2, pltpu.roll(xa, 2, 0), 0.0)
    xc = xc + cw[1:2] * jnp.where(t >= 1, pltpu.roll(xa, 1, 0), 0.0)
    xc = xc + cw[3:4] * jnp.where(t < n - 1, pltpu.roll(xa, n - 1, 0), 0.0)
    gates = _dot(xc.astype(BF16), wg_ref[0]) + bg_ref[0]
    lam = lam_ref[0]
    log_sig = jnp.minimum(lam, 0.0) - jnp.log1p(jnp.exp(-jnp.abs(lam)))
    for d in range(2):
        r = 0.5 * jnp.tanh(0.5 * gates[:, 2 * d * c:(2 * d + 1) * c]) + 0.5
        i = 0.5 * jnp.tanh(0.5 * gates[:, (2 * d + 1) * c:(2 * d + 2) * c]) + 0.5
        log_a = LRU_C * r * log_sig[d:d + 1]
        a = jnp.exp(log_a)
        t_in = jnp.tanh(-log_a) * (1.0 + a * a)
        xin = jnp.where(t_in > 0.0, t_in * lax.rsqrt(t_in), 0.0) * (i * xc)
        for hf in range(nh):
            a_s[hf] = a[:, LANES * hf: LANES * (hf + 1)]
            x_s[hf] = xin[:, LANES * hf: LANES * (hf + 1)]
        order = range(LRU_SEG) if d == 0 else range(LRU_SEG - 1, -1, -1)
        for hf in range(nh):
            h_run = p_run = None
            for s in order:
                a_row = a_s[hf, pl.ds(s, nblk, stride=LRU_SEG), :]
                x_row = x_s[hf, pl.ds(s, nblk, stride=LRU_SEG), :]
                h_run = x_row if h_run is None else a_row * h_run + x_row
                p_run = a_row if p_run is None else a_row * p_run
                hloc_s[d, hf, s] = h_run
                ploc_s[d, hf, s] = p_run
            hend_s[d, hf] = h_run
            pend_s[d, hf] = p_run

    h0 = h0_ref[0, 0]
    init = tuple(h0[d:d + 1, LANES * hf: LANES * (hf + 1)] for d in range(2) for hf in range(nh))

    def body(k, carry):
        out = []
        for idx, cur in enumerate(carry):
            d, hf = divmod(idx, nh)
            kk = k if d == 0 else nblk - 1 - k
            cin_s[d, hf, pl.ds(kk, 1), :] = cur
            out.append(pend_s[d, hf, pl.ds(kk, 1), :] * cur + hend_s[d, hf, pl.ds(kk, 1), :])
        return tuple(out)

    final = lax.fori_loop(0, nblk, body, init)

    y = None
    for d in range(2):
        for hf in range(nh):
            cin = cin_s[d, hf]
            for s in range(LRU_SEG):
                full_h = hloc_s[d, hf, s] + ploc_s[d, hf, s] * cin
                if d == 0:
                    y_s[hf, pl.ds(s, nblk, stride=LRU_SEG), :] = full_h
                else:
                    y_s[hf, pl.ds(s, nblk, stride=LRU_SEG), :] += full_h
    y = jnp.concatenate([y_s[hf] for hf in range(nh)], axis=1) * jax.nn.gelu(ga)
    ya_ref[0] = y.astype(BF16)
    st_ref[0, 0, 0:1, :] = jnp.concatenate(final[:nh], axis=1)
    st_ref[0, 0, 1:2, :] = jnp.concatenate(final[nh:], axis=1)


def _lru_call(ulru, conv_w, conv_b, w_gates_b, b_gates, lam, layer, h0, h0_layer, st_layer=0, st_depth=1,
              prev_state=None):
    b, n, _ = ulru.shape
    c = D_A
    per_layer = lambda shape: pl.BlockSpec((1,) + shape, lambda i: (layer,) + (0,) * len(shape))
    nh, nblk = c // LANES, n // LRU_SEG
    assert n % LRU_SEG == 0 and nblk % 8 == 0
    scratch = ([pltpu.VMEM((nh, n, LANES), F32)] * 3
               + [pltpu.VMEM((2, nh, LRU_SEG, nblk, LANES), F32)] * 2
               + [pltpu.VMEM((2, nh, nblk, LANES), F32)] * 3)
    in_specs = [pl.BlockSpec((1, n, 2 * c), lambda i: (i, 0, 0)),
                per_layer((CONV_W, c)), per_layer((1, c)), per_layer((c, 4 * c)), per_layer((1, 4 * c)),
                per_layer((2, c)),
                pl.BlockSpec((1, 1, 2, c), lambda i: (i, h0_layer, 0, 0))]
    args = [ulru, conv_w, conv_b, w_gates_b, b_gates, lam, h0]
    aliases = {}
    if prev_state is not None:
        aliases[len(args)] = 1
        in_specs.append(pl.BlockSpec(memory_space=pl.ANY))
        args.append(prev_state)
    return pl.pallas_call(
        functools.partial(_lru_kernel, n=n),
        grid=(b,),
        in_specs=in_specs,
        out_specs=[pl.BlockSpec((1, n, c), lambda i: (i, 0, 0)),
                   pl.BlockSpec((1, 1, 2, c), lambda i: (i, st_layer, 0, 0))],
        out_shape=[jax.ShapeDtypeStruct((b, n, c), BF16),
                   jax.ShapeDtypeStruct((b, st_depth, 2, c), F32)],
        scratch_shapes=scratch,
        input_output_aliases=aliases,
        compiler_params=_params(1),
        name="rglru",
    )(*args)


def _softmax_pv(parts, sink):
    mx = None
    for s, _, _ in parts:
        cur = jnp.max(s, axis=-1, keepdims=True)
        mx = cur if mx is None else jnp.maximum(mx, cur)
    if sink is not None:
        mx = jnp.maximum(mx, sink)
    den = None
    out = None
    for s, v, v_t in parts:
        p = jnp.exp(s - mx)
        cur = jnp.sum(p, axis=-1, keepdims=True)
        den = cur if den is None else den + cur
        o = _dot_nt(p.astype(BF16), v) if v_t else _dot(p.astype(BF16), v)
        out = o if out is None else out + o
    if sink is not None:
        den = den + jnp.exp(sink - mx)
    return out / den


def _swap_halves(x):
    return jnp.concatenate([x[:, HEAD_DIM:], x[:, :HEAD_DIM]], axis=1)


def _keep_half(x, lo_mask, half):
    zero = jnp.zeros_like(x)
    return jnp.where(lo_mask, x, zero) if half == 0 else jnp.where(lo_mask, zero, x)


def _ctx_attn_kernel(sink_ref, att_ref, ob_ref, oc_ref, *, n, rb, sink0):
    lo = lax.broadcasted_iota(jnp.int32, (n, LANES), 1) < HEAD_DIM
    for r in range(rb):
        for p in range(H_C // 2):
            qp = att_ref[r, :, QC0 + LANES * p: QC0 + LANES * (p + 1)]
            kp = att_ref[r, :, KC0 + LANES * p: KC0 + LANES * (p + 1)]
            vp = att_ref[r, :, VC0 + LANES * p: VC0 + LANES * (p + 1)]
            outs = [_softmax_pv([(_dot_nt(qp, _keep_half(kp, lo, half)), vp, False)], None) for half in range(2)]
            oc_ref[r, :, LANES * p: LANES * (p + 1)] = jnp.where(lo, outs[0], outs[1]).astype(BF16)
        kpair = att_ref[r, :, KB0: KB0 + LANES]
        vpair = att_ref[r, :, VB0: VB0 + LANES]
        kpair_sw = _swap_halves(kpair)
        vpair_sw = _swap_halves(vpair)
        for p in range(H_B // 2):
            qp = att_ref[r, :, QB0 + LANES * p: QB0 + LANES * (p + 1)]
            outs = []
            for half in range(2):
                h = 2 * p + half
                aligned = (h // G_B) == half
                ksrc, vsrc = (kpair, vpair) if aligned else (kpair_sw, vpair_sw)
                outs.append(_softmax_pv([(_dot_nt(qp, _keep_half(ksrc, lo, half)), vsrc, False)], sink_ref[sink0 + h]))
            ob_ref[r, :, LANES * p: LANES * (p + 1)] = jnp.where(lo, outs[0], outs[1]).astype(BF16)


def _ctx_attn_call(sinks, att, layer):
    b, n, w = att.shape
    rb = CTX_ATTN_REQS
    assert b % rb == 0 and KV_B == 2
    return pl.pallas_call(
        functools.partial(_ctx_attn_kernel, n=n, rb=rb, sink0=layer * H_B),
        grid=(b // rb,),
        in_specs=[pl.BlockSpec(memory_space=pltpu.SMEM),
                  pl.BlockSpec((rb, n, w), lambda i: (i, 0, 0))],
        out_specs=[pl.BlockSpec((rb, n, D_B), lambda i: (i, 0, 0)), pl.BlockSpec((rb, n, D_C), lambda i: (i, 0, 0))],
        out_shape=[jax.ShapeDtypeStruct((b, n, D_B), BF16), jax.ShapeDtypeStruct((b, n, D_C), BF16)],
        compiler_params=_params(1),
        name="ctx_attn",
    )(sinks, att)


def _win_attn_kernel(sink_ref, qkv_ref, kc_ref, vc_ref, o_ref, *, n, sink0):
    j = pl.program_id(1)
    nq = WIN_Q_BLOCKS * WIN_BLK
    nloc = nq + 2 * WIN_BLK
    q0 = pl.multiple_of(j * nq, WIN_BLK)
    ks = pl.multiple_of(jnp.clip(j * nq - WIN_BLK, 0, n - nloc), WIN_BLK)
    row = lax.broadcasted_iota(jnp.int32, (nq, nloc), 0)
    col = lax.broadcasted_iota(jnp.int32, (nq, nloc), 1)
    in_window = jnp.where(jnp.abs((q0 + row) - (ks + col)) <= WINDOW, 0.0, NEG_INF)
    band = jnp.concatenate([in_window] * G_B, axis=0)
    rowh = lax.broadcasted_iota(jnp.int32, (G_B * nq, 1), 0) // nq
    for kv in range(KV_B):
        gw = G_B * HEAD_DIM
        qblk = qkv_ref[0, pl.ds(q0, nq), QB0 + gw * kv: QB0 + gw * (kv + 1)]
        q3 = jnp.concatenate([qblk[:, 64 * g: 64 * (g + 1)] for g in range(G_B)], axis=0)
        kl = qkv_ref[0, pl.ds(ks, nloc), KB0 + 64 * kv: KB0 + 64 * (kv + 1)]
        vl = qkv_ref[0, pl.ds(ks, nloc), VB0 + 64 * kv: VB0 + 64 * (kv + 1)]
        kc_t = kc_ref[0, 0, kv].astype(BF16)
        vc_t = vc_ref[0, 0, kv].astype(BF16)
        s_loc = _dot_nt(q3, kl) + band
        s_ctx = _dot(q3, kc_t)
        s0 = sink0 + G_B * kv
        sk = jnp.where(rowh == 0, sink_ref[s0], jnp.where(rowh == 1, sink_ref[s0 + 1], sink_ref[s0 + 2]))
        o = _softmax_pv([(s_loc, vl, False), (s_ctx, vc_t, True)], sk)
        for g in range(G_B):
            h = kv * G_B + g
            o_ref[0, :, 64 * h: 64 * (h + 1)] = o[g * nq:(g + 1) * nq].astype(BF16)


def _win_attn_call(sinks, qkv, cache_k, cache_v, layer):
    b, n, w = qkv.shape
    past = cache_k.shape[4]
    cache_spec = pl.BlockSpec((1, 1, KV_B, HEAD_DIM, past), lambda i, j: (i, layer, 0, 0, 0))
    nq = WIN_Q_BLOCKS * WIN_BLK
    assert n % nq == 0 and n >= nq + 2 * WIN_BLK and WINDOW == WIN_BLK
    return pl.pallas_call(
        functools.partial(_win_attn_kernel, n=n, sink0=layer * H_B),
        grid=(b, n // nq),
        in_specs=[pl.BlockSpec(memory_space=pltpu.SMEM),
                  pl.BlockSpec((1, n, w), lambda i, j: (i, 0, 0)),
                  cache_spec, cache_spec],
        out_specs=pl.BlockSpec((1, nq, D_B), lambda i, j: (i, j, 0)),
        out_shape=jax.ShapeDtypeStruct((b, n, D_B), BF16),
        compiler_params=_params(2),
        name="win_attn",
    )(sinks, qkv, cache_k, cache_v)


def _nbr_window_start(g, rows):
    return jnp.clip(g * NBR_Q_ROWS - NA_ROWS // 2, 0, rows - NBR_K_ROWS)


def _nbr_attn_kernel(q_ref, k_ref, v_ref, rel_ref, kc_ref, vc_ref, o_ref, bias_s, *, n):
    g = pl.program_id(0)
    nq = NBR_Q_ROWS * GRID_W
    nk = NBR_K_ROWS * GRID_W
    rows = n // GRID_W
    ws = _nbr_window_start(g, rows)
    k0 = pl.multiple_of(ws * GRID_W, GRID_W)

    @pl.when(pl.program_id(1) == 0)
    def _():
        for i in range(NBR_Q_ROWS):
            qr = g * NBR_Q_ROWS + i
            rs = jnp.clip(qr - NA_ROWS // 2, 0, rows - NA_ROWS)
            for jp in range(NBR_K_ROWS // 2):
                tiles = []
                for j in (2 * jp, 2 * jp + 1):
                    kr = ws + j
                    valid = (kr >= rs) & (kr < rs + NA_ROWS)
                    d = jnp.clip(kr - qr + NA_ROWS - 1, 0, 2 * NA_ROWS - 2)
                    tiles.append(jnp.where(valid, rel_ref[0, :, pl.ds(d, 1)][:, 0], NEG_INF))
                bias_s[:, GRID_W * i: GRID_W * (i + 1), LANES * jp: LANES * (jp + 1)] = jnp.concatenate(tiles, axis=-1)
    lo_k = lax.broadcasted_iota(jnp.int32, (nk, LANES), 1) < HEAD_DIM
    lo_q = lax.broadcasted_iota(jnp.int32, (nq, LANES), 1) < HEAD_DIM
    zpad = jnp.zeros((HEAD_DIM, kc_ref.shape[4]), BF16)
    for p in range(H_C // 2):
        qp = q_ref[0, :, LANES * p: LANES * (p + 1)]
        kp = k_ref[0, pl.ds(k0, nk), LANES * p: LANES * (p + 1)]
        vp = v_ref[0, pl.ds(k0, nk), LANES * p: LANES * (p + 1)]
        outs = []
        for half in range(2):
            h = 2 * p + half
            kc_t = kc_ref[0, 0, h].astype(BF16)
            vc_t = vc_ref[0, 0, h].astype(BF16)
            kc_t = jnp.concatenate([kc_t, zpad] if half == 0 else [zpad, kc_t], axis=0)
            vc_t = jnp.concatenate([vc_t, zpad] if half == 0 else [zpad, vc_t], axis=0)
            s_loc = _dot_nt(qp, _keep_half(kp, lo_k, half)) + bias_s[h]
            s_ctx = _dot(qp, kc_t)
            outs.append(_softmax_pv([(s_loc, vp, False), (s_ctx, vc_t, True)], None))
        o_ref[0, :, LANES * p: LANES * (p + 1)] = jnp.where(lo_q, outs[0], outs[1]).astype(BF16)


def _nbr_attn_call(att, rel_tab, cache_k, cache_v, layer):
    b, n, w = att.shape
    past = cache_k.shape[4]
    nq = NBR_Q_ROWS * GRID_W
    nk = NBR_K_ROWS * GRID_W
    gw = H_C * HEAD_DIM
    rows = n // GRID_W
    assert (QC0, KC0, VC0) == (0, gw, 2 * gw) and n % nq == 0 and rows >= NBR_K_ROWS and NBR_K_ROWS % 2 == 0
    cache_spec = pl.BlockSpec((1, 1, H_C, HEAD_DIM, past), lambda g, i: (i, layer, 0, 0, 0))
    return pl.pallas_call(
        functools.partial(_nbr_attn_kernel, n=n),
        grid=(n // nq, b),
        in_specs=[pl.BlockSpec((1, nq, gw), lambda g, i: (i, g, 0)),
                  pl.BlockSpec((1, n, gw), lambda g, i: (i, 0, 1)),
                  pl.BlockSpec((1, n, gw), lambda g, i: (i, 0, 2)),
                  pl.BlockSpec((1,) + rel_tab.shape[1:], lambda g, i: (layer, 0, 0, 0, 0)),
                  cache_spec, cache_spec],
        out_specs=pl.BlockSpec((1, nq, gw), lambda g, i: (i, g, 0)),
        out_shape=jax.ShapeDtypeStruct((b, n, gw), BF16),
        scratch_shapes=[pltpu.VMEM((H_C, nq, nk), F32)],
        compiler_params=_params(2),
        name="nbr_attn",
    )(att, att, att, rel_tab, cache_k, cache_v)


def _outproj_kernel(x_ref, ya_ref, yb_ref, yc_ref, mod_ref, g_ref, wo_ref, wr_ref,
                    xn_ref, h2_ref, aff_ref):
    mix = jnp.concatenate([ya_ref[0], yb_ref[0], yc_ref[0]], axis=1)
    proj = _dot(mix, wo_ref[0])
    xn = x_ref[0] + mod_ref[0, 0, 2:3, :] * proj
    xn_ref[0] = xn
    h2 = _rms_modulate(xn, g_ref[0], mod_ref[0, 0, 3:4, :], mod_ref[0, 0, 4:5, :]).astype(BF16)
    h2_ref[0] = h2
    logits = _dot(h2, wr_ref[0])
    lane = lax.broadcasted_iota(jnp.int32, logits.shape, 1)
    valid = lane < N_EXPERTS
    logits = jnp.where(valid, logits, NEG_INF)
    e = jnp.exp(logits - jnp.max(logits, axis=-1, keepdims=True))
    e = jnp.where(valid, e, 0.0)
    aff_ref[0] = e / jnp.sum(e, axis=-1, keepdims=True)


def _outproj_call(x3, ya, yb, yc, mods, mod_row0, gains, wo_b16, w_router_b, layer):
    g, n, d = x3.shape
    tb = TOKEN_BLOCK
    tok = lambda w: pl.BlockSpec((1, tb, w), lambda i, j: (i, j, 0))
    per_layer = lambda shape: pl.BlockSpec((1,) + shape, lambda i, j: (layer,) + (0,) * len(shape))
    return pl.pallas_call(
        _outproj_kernel,
        grid=(g, n // tb),
        in_specs=[tok(d), tok(D_A), tok(D_B), tok(D_C),
                  pl.BlockSpec((1, 1, N_MOD, d), lambda i, j: (layer, mod_row0 + i, 0, 0)),
                  per_layer((1, d)), per_layer((D_A + D_B + D_C, d)), per_layer((d, LANES))],
        out_specs=[tok(d), tok(d), tok(LANES)],
        out_shape=[jax.ShapeDtypeStruct((g, n, d), F32),
                   jax.ShapeDtypeStruct((g, n, d), BF16),
                   jax.ShapeDtypeStruct((g, n, LANES), F32)],
        compiler_params=_params(2),
        name="outproj_router",
    )(x3, ya, yb, yc, mods, gains, wo_b16, w_router_b)


def _sort_by_affinity(keys, idx, pos, levels, seg_lanes):
    rows = keys.shape[0]
    sign = [jnp.where(((pos >> m) & 1) == 0, 1.0, -1.0) for m in range(levels)]

    def partner(x, j):
        d = 1 << j
        low = sign[j] > 0.0
        if d >= rows:
            sh = (d // rows) * seg_lanes
            return jnp.where(low, pltpu.roll(x, LANES - sh, 1), pltpu.roll(x, sh, 1))
        if d >= 8:
            x4 = x.reshape(rows // (2 * d), 2, d, LANES)
            return jnp.concatenate([x4[:, 1:2], x4[:, 0:1]], axis=1).reshape(rows, LANES)
        x3 = x.reshape(rows // 8, 8, LANES)
        low3 = sign[j].reshape(rows // 8, 8, LANES) > 0.0
        return jnp.where(low3, pltpu.roll(x3, 8 - d, 1), pltpu.roll(x3, d, 1)).reshape(rows, LANES)

    for k in range(1, levels + 1):
        for j in range(k - 1, -1, -1):
            pk, pi = partner(keys, j), partner(idx, j)
            before = jnp.where(keys == pk, jnp.where(idx < pi, 1.0, -1.0), jnp.where(keys > pk, 1.0, -1.0))
            want_first = sign[j] * sign[k] if k < levels else sign[j]
            keep = before * want_first > 0.0
            keys = jnp.where(keep, keys, pk)
            idx = jnp.where(keep, idx, pi)
    return keys, idx


def _route_kernel(aff_ref, h2_ref, xs_ref, vals_ref, rank_ref, *, n, cap, rb):
    ne = N_EXPERTS
    rows = ROUTE_SORT_ROWS
    nseg = n // rows
    seg_lanes = rb * ne
    levels = n.bit_length() - 1
    fcap = float(cap)
    row = lax.broadcasted_iota(jnp.int32, (rows, LANES), 0)
    lane = lax.broadcasted_iota(jnp.int32, (rows, LANES), 1)
    pos = (lane // seg_lanes) * rows + row
    keys = jnp.full((rows, LANES), -1.0, F32)
    for r in range(rb):
        for seg in range(nseg):
            off = seg * seg_lanes + r * ne
            piece = aff_ref[r, seg * rows:(seg + 1) * rows, :]
            if off:
                piece = pltpu.roll(piece, off, 1)
            keys = jnp.where((lane >= off) & (lane < off + ne), piece, keys)
    keys, idx = _sort_by_affinity(keys, pos.astype(F32), pos, levels, seg_lanes)
    top_keys = keys[:cap]
    top_idx = idx[:cap]

    tok = lax.broadcasted_iota(jnp.int32, (cap, n), 1).astype(F32)
    weight = fcap - lax.broadcasted_iota(jnp.int32, (cap, n), 0).astype(F32)
    pad = jnp.full((LANES - ne, n), fcap, F32)
    for r in range(rb):
        picks, rank_rows = [], []
        for e in range(ne):
            col = r * ne + e
            onehot = jnp.where(top_idx[:, col:col + 1] == tok, 1.0, 0.0)
            vals_ref[e, r] = top_keys[:, col:col + 1]
            rank_rows.append(fcap - jnp.sum(onehot * weight, axis=0, keepdims=True))
            picks.append(onehot.astype(BF16))
        rank_ref[r] = jnp.concatenate(rank_rows + [pad], axis=0).T
        xs = _dot(jnp.concatenate(picks, axis=0), h2_ref[r])
        xs_ref[:, r] = xs.reshape(ne, cap, xs.shape[-1]).astype(BF16)


def _route_call(aff, h2, rb):
    b, n, d = h2.shape
    cap = max(1, EC_FACTOR * n // N_EXPERTS)
    nseg = n // ROUTE_SORT_ROWS
    assert b % rb == 0 and n == nseg * ROUTE_SORT_ROWS and n & (n - 1) == 0
    assert nseg * rb * N_EXPERTS <= LANES and cap <= ROUTE_SORT_ROWS
    return pl.pallas_call(
        functools.partial(_route_kernel, n=n, cap=cap, rb=rb),
        grid=(b // rb,),
        in_specs=[pl.BlockSpec((rb, n, LANES), lambda i: (i, 0, 0)),
                  pl.BlockSpec((rb, n, d), lambda i: (i, 0, 0))],
        out_specs=[pl.BlockSpec((N_EXPERTS, rb, cap, d), lambda i: (0, i, 0, 0)),
                   pl.BlockSpec((N_EXPERTS, rb, cap, 1), lambda i: (0, i, 0, 0)),
                   pl.BlockSpec((rb, n, LANES), lambda i: (i, 0, 0))],
        out_shape=[jax.ShapeDtypeStruct((N_EXPERTS, b, cap, d), BF16),
                   jax.ShapeDtypeStruct((N_EXPERTS, b, cap, 1), F32),
                   jax.ShapeDtypeStruct((b, n, LANES), F32)],
        compiler_params=_params(1),
        name="route_gather",
    )(aff, h2)


def _expert_kernel(xc_ref, xl_ref, vc_ref, vl_ref, wg_ref, wu_ref, wd_ref, yc_ref, yl_ref, acc_c, acc_l):
    f = pl.program_id(1)

    def step(first):
        wg = wg_ref[0, 0].astype(BF16)
        wu = wu_ref[0, 0].astype(BF16)
        wd = wd_ref[0, 0].astype(BF16)
        for x_ref, v_ref, y_ref, acc in ((xc_ref, vc_ref, yc_ref, acc_c), (xl_ref, vl_ref, yl_ref, acc_l)):
            rows = x_ref.shape[1]
            chunk = min(EXPERT_ROW_CHUNK, rows)
            for c0 in range(0, rows, chunk):
                sl = slice(c0, c0 + chunk)
                x = x_ref[0, sl, :]
                a = _dot(x, wg)
                u = _dot(x, wu)
                act = ((a * jax.nn.sigmoid(a)) * u).astype(BF16)
                part = _dot(act, wd)
                if first:
                    acc[sl, :] = part
                else:
                    y_ref[0, sl, :] = ((acc[sl, :] + part) * v_ref[0, sl, :]).astype(BF16)

    pl.when(f == 0)(functools.partial(step, True))
    pl.when(f == EXPERT_F_STEPS - 1)(functools.partial(step, False))


def _expert_call(xs_c, xs_l, vals_c, vals_l, w_gate, w_up, w_down, layer):
    e, rc, d = xs_c.shape
    rl = xs_l.shape[1]
    f_total = w_gate.shape[-1]
    assert EXPERT_F_STEPS == 2
    fb = f_total // EXPERT_F_STEPS
    per_e = lambda rows, w: pl.BlockSpec((1, rows, w), lambda i, j: (i, 0, 0))
    return pl.pallas_call(
        _expert_kernel,
        grid=(e, EXPERT_F_STEPS),
        in_specs=[per_e(rc, d), per_e(rl, d), per_e(rc, 1), per_e(rl, 1),
                  pl.BlockSpec((1, 1, d, fb), lambda i, j: (layer, i, 0, j)),
                  pl.BlockSpec((1, 1, d, fb), lambda i, j: (layer, i, 0, j)),
                  pl.BlockSpec((1, 1, fb, d), lambda i, j: (layer, i, j, 0))],
        out_specs=[per_e(rc, d), per_e(rl, d)],
        out_shape=[jax.ShapeDtypeStruct((e, rc, d), BF16),
                   jax.ShapeDtypeStruct((e, rl, d), BF16)],
        scratch_shapes=[pltpu.VMEM((rc, d), F32), pltpu.VMEM((rl, d), F32)],
        compiler_params=_params(2),
        name="experts",
    )(xs_c, xs_l, vals_c, vals_l, w_gate, w_up, w_down)


def _combine_kernel(y_ref, rank_ref, xn_ref, mod_ref, gf_ref, o_ref, *, n, cap, final, rb):
    ec = N_EXPERTS * cap
    d = xn_ref.shape[-1]
    ce = lax.broadcasted_iota(jnp.int32, (LANES, ec), 1) // cap
    ee = lax.broadcasted_iota(jnp.int32, (LANES, ec), 0)
    expand = jnp.where(ce == ee, 1.0, 0.0).astype(BF16)
    slot = (lax.broadcasted_iota(jnp.int32, (n, ec), 1) % cap).astype(F32)
    for r in range(rb):
        rank = rank_ref[r].astype(BF16)
        rexp = _dot(rank, expand)
        onehot = jnp.where(rexp == slot, 1.0, 0.0).astype(BF16)
        moe = _dot(onehot, y_ref[:, r].reshape(ec, d))
        x = xn_ref[r] + mod_ref[0, 0, 5:6, :] * moe
        if final:
            x = (x * lax.rsqrt(jnp.mean(x * x, axis=-1, keepdims=True) + EPS)) * gf_ref[...]
        o_ref[r] = x


def _combine_call(y4, rank, xn, mods, mod_row0, shared_mod, layer, g_final, final):
    e, b, cap, d = y4.shape
    n = xn.shape[1]
    rb = COMBINE_REQS_CTX if shared_mod else 1
    assert b % rb == 0
    return pl.pallas_call(
        functools.partial(_combine_kernel, n=n, cap=cap, final=final, rb=rb),
        grid=(b // rb,),
        in_specs=[pl.BlockSpec((e, rb, cap, d), lambda i: (0, i, 0, 0)),
                  pl.BlockSpec((rb, n, LANES), lambda i: (i, 0, 0)),
                  pl.BlockSpec((rb, n, d), lambda i: (i, 0, 0)),
                  pl.BlockSpec((1, 1, N_MOD, d), (lambda i: (layer, mod_row0, 0, 0)) if shared_mod
                               else (lambda i: (layer, mod_row0 + i, 0, 0))),
                  pl.BlockSpec((1, d), lambda i: (0, 0))],
        out_specs=pl.BlockSpec((rb, n, d), lambda i: (i, 0, 0)),
        out_shape=jax.ShapeDtypeStruct((b, n, d), F32),
        compiler_params=_params(1),
        name="combine",
    )(y4, rank, xn, mods, g_final)


def _rope_tables(n):
    t = jnp.arange(n, dtype=jnp.int32)
    row = (t // GRID_W).astype(F32)
    col = (t % GRID_W).astype(F32)
    half = HEAD_DIM // 4
    inv = jnp.power(ROPE_BASE, -jnp.arange(half, dtype=F32) / half)
    ang_r = row[:, None] * inv[None, :]
    ang_c = col[:, None] * inv[None, :]
    cos_h = jnp.concatenate([jnp.cos(ang_r)] * 2 + [jnp.cos(ang_c)] * 2, axis=-1)
    sin_h = jnp.concatenate([-jnp.sin(ang_r), jnp.sin(ang_r), -jnp.sin(ang_c), jnp.sin(ang_c)], axis=-1)
    reps = H_B + KV_B
    return jnp.tile(cos_h, (1, reps)), jnp.tile(sin_h, (1, reps))


def _nbr_rel_tables(rpb):
    col = np.arange(GRID_W)
    cs = np.clip(col - NA_COLS // 2, 0, GRID_W - NA_COLS)
    col_mask = (col[None, :] >= cs[:, None]) & (col[None, :] < cs[:, None] + NA_COLS)
    dc_idx = np.clip(col[None, :] - col[:, None], -(NA_COLS - 1), NA_COLS - 1) + (NA_COLS - 1)
    sel_col = (dc_idx[None, :, :] == np.arange(2 * NA_COLS - 1)[:, None, None]).astype(np.float32)
    t = jnp.einsum("lhdc,cqk->lhdqk", rpb.astype(F32), sel_col, precision=lax.Precision.HIGHEST)
    return jnp.where(col_mask, t, NEG_INF)


def _block_diag(w):
    nb, bw = w.shape[-3], w.shape[-1]
    eye = jnp.eye(nb, dtype=w.dtype)
    return (eye[:, None, :, None] * w[..., :, :, None, :]).reshape(w.shape[:-3] + (nb * bw, nb * bw))


def kernel(x_prompt, x_sample, state_lru, cache_k_win, cache_v_win, cache_k_nbr, cache_v_nbr, c, c_ctx, w_mod, b_mod, g_norm1, w_in, conv_w, conv_b, w_gate_r, b_gate_r, w_gate_i, b_gate_i, lru_lambda, sink_logit, nbr_bias, w_out, g_norm2, w_router, w_exp_gate, w_exp_up, w_exp_down, g_final):
    bc, seq, d = x_prompt.shape
    bl, n_lat, _ = x_sample.shape
    depth = w_mod.shape[0]
    assert bl + 1 <= 8 and d == 1024

    cond8 = jnp.zeros((8, d), F32).at[0].set(c_ctx).at[1:1 + bl].set(c)
    mods = _adaln_call(cond8, w_mod, b_mod.reshape(depth, 1, -1)).reshape(depth, 8, N_MOD, d)
    rope_tabs = _rope_tables(n_lat)
    bias_tab = _nbr_rel_tables(nbr_bias)
    ckw_t, cvw_t, ckn_t, cvn_t = (jnp.swapaxes(a, -1, -2) for a in (cache_k_win, cache_v_win, cache_k_nbr, cache_v_nbr))
    gf = g_final.reshape(1, d)
    assert SCALE == 0.125
    new_cols = np.arange(w_in.shape[-1]) - LRU_W
    is_q = ((new_cols >= QC0) & (new_cols < KC0)) | ((new_cols >= QB0) & (new_cols < KB0))
    q_scale = jnp.asarray(np.where(is_q, SCALE, 1.0), F32)

    nbr0 = LRU_W + D_B + 2 * KV_W
    w_in_b = (jnp.concatenate([w_in[..., :LRU_W], w_in[..., nbr0:], w_in[..., LRU_W:nbr0]], axis=-1) * q_scale).astype(BF16)
    wo = w_out.astype(BF16)
    wr_b = jnp.pad(w_router.astype(BF16), ((0, 0), (0, 0), (0, LANES - N_EXPERTS)))
    gate_r, gate_i = _block_diag(w_gate_r), _block_diag(w_gate_i)
    w_gates = jnp.concatenate([gate_r[:, 0], gate_i[:, 0], gate_r[:, 1], gate_i[:, 1]], axis=-1).astype(BF16)
    b_gates = jnp.concatenate([b_gate_r[:, 0], b_gate_i[:, 0], b_gate_r[:, 1], b_gate_i[:, 1]], axis=-1)[:, None, :]
    cb = conv_b[:, None, :]
    g1 = g_norm1[:, None, :]
    g2 = g_norm2[:, None, :]
    sinks = sink_logit.reshape(-1)
    ctx_row, lat_row = 0, 1

    xc = x_prompt.reshape(1, bc * seq, d)
    xl = x_sample
    zeros_state = jnp.zeros((bc, 1, 2, D_A), F32)
    caches = None
    st = None
    for l in range(depth):
        final = l == depth - 1
        ulru_c, qkv_c, *caches = _inproj_call(xc, mods, ctx_row, g1, w_in_b, l, cache_seq=seq, depth=depth,
                                              prev_caches=caches)
        ya_c, st = _lru_call(ulru_c.reshape(bc, seq, LRU_W), conv_w, cb, w_gates, b_gates, lru_lambda, l,
                             zeros_state, 0, st_layer=l, st_depth=depth, prev_state=st)
        yb_c, yc_c = _ctx_attn_call(sinks, qkv_c.reshape(bc, seq, -1), l)
        xn_c, h2_c, aff_c = _outproj_call(xc, ya_c.reshape(1, bc * seq, -1), yb_c.reshape(1, bc * seq, -1),
                                          yc_c.reshape(1, bc * seq, -1), mods, ctx_row, g2, wo, wr_b, l)
        xs_c, vals_c, rank_c = _route_call(aff_c.reshape(bc, seq, LANES), h2_c.reshape(bc, seq, d), ROUTE_REQS_CTX)

        ulru_l, qkv_l = _inproj_call(xl, mods, lat_row, g1, w_in_b, l, rope_tabs=rope_tabs)
        ya_l, _ = _lru_call(ulru_l, conv_w, cb, w_gates, b_gates, lru_lambda, l, state_lru, l)
        yb_l = _win_attn_call(sinks, qkv_l, ckw_t, cvw_t, l)
        yc_l = _nbr_attn_call(qkv_l, bias_tab, ckn_t, cvn_t, l)
        xn_l, h2_l, aff_l = _outproj_call(xl, ya_l, yb_l, yc_l, mods, lat_row, g2, wo, wr_b, l)
        xs_l, vals_l, rank_l = _route_call(aff_l, h2_l, ROUTE_REQS_LAT)

        cap_c, cap_l = xs_c.shape[2], xs_l.shape[2]
        y_c, y_l = _expert_call(xs_c.reshape(N_EXPERTS, bc * cap_c, d), xs_l.reshape(N_EXPERTS, bl * cap_l, d),
                                vals_c.reshape(N_EXPERTS, bc * cap_c, 1), vals_l.reshape(N_EXPERTS, bl * cap_l, 1),
                                w_exp_gate, w_exp_up, w_exp_down, l)
        xc = _combine_call(y_c.reshape(N_EXPERTS, bc, cap_c, d), rank_c, xn_c.reshape(bc, seq, d), mods, ctx_row, True,
                           l, gf, final)
        xl = _combine_call(y_l.reshape(N_EXPERTS, bl, cap_l, d), rank_l, xn_l, mods, lat_row, False, l, gf, final)
        xc = xc.reshape(1, bc * seq, d)

    y_prompt = xc.reshape(bc, seq, d)
    y_sample = xl
    return (y_prompt, y_sample, st, *[jnp.swapaxes(buf, -1, -2) for buf in caches])
```

```python
import functools

import numpy as np
import jax
import jax.numpy as jnp
from jax import lax
from jax.experimental import pallas as pl
from jax.experimental.pallas import tpu as pltpu

F32 = jnp.float32
BF16 = jnp.bfloat16

HEAD_DIM = 64
GRID_W = 64
LRU_C = 8.0
CONV_W = 4
KV_B = 2
G_B = 3
H_B = KV_B * G_B
H_C = 6
WINDOW = 128
WIN_BLK = 128
NA_ROWS = 8
NA_COLS = 16
ROPE_BASE = 10000.0
N_EXPERTS = 16
EC_FACTOR = 2
N_MOD = 6
EPS = 1e-6
NEG_INF = -1e30
SCALE = HEAD_DIM ** -0.5

LANES = 128
VMEM_LIMIT_BYTES = 56 * 1024 * 1024

TOKEN_BLOCK = 1024
MOD_COL_BLOCK = 3072
EXPERT_F_STEPS = 2
EXPERT_ROW_CHUNK = 256
D_A = 256
D_B = H_B * HEAD_DIM
D_C = H_C * HEAD_DIM
KV_W = KV_B * HEAD_DIM
LRU_W = 2 * D_A
QC0, KC0, VC0 = 0, D_C, 2 * D_C
QB0 = 3 * D_C
KB0, VB0, ATT_W = QB0 + D_B, QB0 + D_B + KV_W, QB0 + D_B + 2 * KV_W
CTX_ATTN_REQS = 4
WIN_Q_BLOCKS = 2
NBR_Q_ROWS = 4
NBR_K_ROWS = 12
ROUTE_REQS_CTX = 8
ROUTE_REQS_LAT = 2
ROUTE_SORT_ROWS = 256
COMBINE_REQS_CTX = 4
LRU_SEG = 8


def _params(n_axes):
    return pltpu.CompilerParams(dimension_semantics=("arbitrary",) * n_axes,
                                vmem_limit_bytes=VMEM_LIMIT_BYTES)


def _dot(a, b):
    return jnp.dot(a, b, preferred_element_type=F32)


def _dot_nt(a, b):
    return lax.dot_general(a, b, (((1,), (1,)), ((), ())), preferred_element_type=F32)


def _adaln_kernel(cond_ref, w_ref, b_ref, o_ref):
    c = cond_ref[...]
    s = c * jax.nn.sigmoid(c)
    o_ref[0] = _dot(s.astype(BF16), w_ref[0].astype(BF16)) + b_ref[0]


def _adaln_call(cond8, w_mod, b_mod3):
    depth, d, n = w_mod.shape
    nb = MOD_COL_BLOCK
    return pl.pallas_call(
        _adaln_kernel,
        grid=(depth, n // nb),
        in_specs=[pl.BlockSpec((8, d), lambda l, j: (0, 0)),
                  pl.BlockSpec((1, d, nb), lambda l, j: (l, 0, j)),
                  pl.BlockSpec((1, 1, nb), lambda l, j: (l, 0, j))],
        out_specs=pl.BlockSpec((1, 8, nb), lambda l, j: (l, 0, j)),
        out_shape=jax.ShapeDtypeStruct((depth, 8, n), F32),
        compiler_params=_params(2),
        name="adaln",
    )(cond8, w_mod, b_mod3)


def _rms_modulate(x, gain, shift, scale):
    y = x * lax.rsqrt(jnp.mean(x * x, axis=-1, keepdims=True) + EPS)
    return (y * gain) * (1.0 + scale) + shift


def _inproj_kernel(*refs, rope, cache, seq):
    if rope:
        x_ref, mod_ref, g_ref, w_ref, cos_ref, sin_ref, ulru_ref, qkv_ref = refs
    elif cache:
        x_ref, mod_ref, g_ref, w_ref = refs[:4]
        ulru_ref, qkv_ref, kw_ref, vw_ref, kn_ref, vn_ref = refs[-6:]
    else:
        x_ref, mod_ref, g_ref, w_ref, ulru_ref, qkv_ref = refs
    x = x_ref[0]
    h = _rms_modulate(x, g_ref[0], mod_ref[0, 0, 0:1, :], mod_ref[0, 0, 1:2, :])
    u = _dot(h.astype(BF16), w_ref[0])
    tb = u.shape[0]
    ulru_ref[0] = u[:, :LRU_W]
    if rope:
        lo, hi = LRU_W + QB0, LRU_W + VB0
        qk = u[:, lo:hi]
        lane = lax.broadcasted_iota(jnp.int32, (tb, hi - lo), 1)
        first = (lane & 31) < 16
        partner = jnp.where(first, pltpu.roll(qk, hi - lo - 16, 1), pltpu.roll(qk, 16, 1))
        qk = qk * cos_ref[...] + partner * sin_ref[...]
        qkv_ref[0, :, :QB0] = u[:, LRU_W:lo].astype(BF16)
        qkv_ref[0, :, QB0:VB0] = qk.astype(BF16)
        qkv_ref[0, :, VB0:] = u[:, hi:].astype(BF16)
    else:
        qkv_ref[0] = u[:, LRU_W:].astype(BF16)
    if cache:
        for r in range(tb // seq):
            rows = slice(r * seq, (r + 1) * seq)
            for ref, c0, heads in ((kw_ref, KB0, KV_B), (vw_ref, VB0, KV_B), (kn_ref, KC0, H_C), (vn_ref, VC0, H_C)):
                for p in range(heads // 2):
                    pair_t = u[rows, LRU_W + c0 + LANES * p: LRU_W + c0 + LANES * (p + 1)].T
                    ref[r, 0, 2 * p] = pair_t[:HEAD_DIM]
                    ref[r, 0, 2 * p + 1] = pair_t[HEAD_DIM:]


def _inproj_call(x3, mods, mod_row0, gains, w_in_b, layer, rope_tabs=None, cache_seq=None, depth=1, prev_caches=None):
    g, n, d = x3.shape
    d_in = w_in_b.shape[2]
    tb = TOKEN_BLOCK
    rope = rope_tabs is not None
    cache = cache_seq is not None
    in_specs = [pl.BlockSpec((1, tb, d), lambda j, i: (i, j, 0)),
                pl.BlockSpec((1, 1, N_MOD, d), lambda j, i: (layer, mod_row0 + i, 0, 0)),
                pl.BlockSpec((1, 1, d), lambda j, i: (layer, 0, 0)),
                pl.BlockSpec((1, d, d_in), lambda j, i: (layer, 0, 0))]
    args = [x3, mods, gains, w_in_b]
    assert d_in == LRU_W + ATT_W
    out_shape = [jax.ShapeDtypeStruct((g, n, LRU_W), F32),
                 jax.ShapeDtypeStruct((g, n, ATT_W), BF16)]
    out_specs = [pl.BlockSpec((1, tb, LRU_W), lambda j, i: (i, j, 0)),
                 pl.BlockSpec((1, tb, ATT_W), lambda j, i: (i, j, 0))]
    if rope:
        in_specs += [pl.BlockSpec((tb, VB0 - QB0), lambda j, i: (j, 0))] * 2
        args += list(rope_tabs)
    aliases = {}
    if cache:
        assert g == 1 and tb % cache_seq == 0
        rb = tb // cache_seq
        nreq = n // cache_seq
        for heads in (KV_B, KV_B, H_C, H_C):
            out_shape.append(jax.ShapeDtypeStruct((nreq, depth, heads, HEAD_DIM, cache_seq), F32))
            out_specs.append(pl.BlockSpec((rb, 1, heads, HEAD_DIM, cache_seq), lambda j, i: (j, layer, 0, 0, 0)))
        if prev_caches is not None:
            for k, buf in enumerate(prev_caches):
                aliases[len(args)] = 2 + k
                in_specs.append(pl.BlockSpec(memory_space=pl.ANY))
                args.append(buf)
    return pl.pallas_call(
        functools.partial(_inproj_kernel, rope=rope, cache=cache, seq=cache_seq),
        grid=(n // tb, g),
        in_specs=in_specs,
        out_specs=out_specs,
        out_shape=out_shape,
        input_output_aliases=aliases,
        compiler_params=_params(2),
        name="inproj_rope" if rope else "inproj_ctx",
    )(*args)


def _lru_kernel(*refs, n):
    u_ref, cw_ref, cb_ref, wg_ref, bg_ref, lam_ref, h0_ref = refs[:7]
    ya_ref, st_ref, a_s, x_s, y_s, hloc_s, ploc_s, hend_s, pend_s, cin_s = refs[-10:]
    c = D_A
    nblk = n // LRU_SEG
    nh = c // LANES
    u = u_ref[0]
    xa = u[:, :c]
    ga = u[:, c:]
    t = lax.broadcasted_iota(jnp.int32, (n, c), 0)
    cw = cw_ref[0]
    xc = cw[2:3] * xa + cb_ref[0]
    xc = xc + cw[0:1] * jnp.where(t >= 2, pltpu.roll(xa, 2, 0), 0.0)
    xc = xc + cw[1:2] * jnp.where(t >= 1, pltpu.roll(xa, 1, 0), 0.0)
    xc = xc + cw[3:4] * jnp.where(t < n - 1, pltpu.roll(xa, n - 1, 0), 0.0)
    gates = _dot(xc.astype(BF16), wg_ref[0]) + bg_ref[0]
    lam = lam_ref[0]
    log_sig = jnp.minimum(lam, 0.0) - jnp.log1p(jnp.exp(-jnp.abs(lam)))
    for d in range(2):
        r = 0.5 * jnp.tanh(0.5 * gates[:, 2 * d * c:(2 * d + 1) * c]) + 0.5
        i = 0.5 * jnp.tanh(0.5 * gates[:, (2 * d + 1) * c:(2 * d + 2) * c]) + 0.5
        log_a = LRU_C * r * log_sig[d:d + 1]
        a = jnp.exp(log_a)
        t_in = jnp.tanh(-log_a) * (1.0 + a * a)
        xin = jnp.where(t_in > 0.0, t_in * lax.rsqrt(t_in), 0.0) * (i * xc)
        for hf in range(nh):
            a_s[hf] = a[:, LANES * hf: LANES * (hf + 1)]
            x_s[hf] = xin[:, LANES * hf: LANES * (hf + 1)]
        order = range(LRU_SEG) if d == 0 else range(LRU_SEG - 1, -1, -1)
        for hf in range(nh):
            h_run = p_run = None
            for s in order:
                a_row = a_s[hf, pl.ds(s, nblk, stride=LRU_SEG), :]
                x_row = x_s[hf, pl.ds(s, nblk, stride=LRU_SEG), :]
                h_run = x_row if h_run is None else a_row * h_run + x_row
                p_run = a_row if p_run is None else a_row * p_run
                hloc_s[d, hf, s] = h_run
                ploc_s[d, hf, s] = p_run
            hend_s[d, hf] = h_run
            pend_s[d, hf] = p_run

    h0 = h0_ref[0, 0]
    init = tuple(h0[d:d + 1, LANES * hf: LANES * (hf + 1)] for d in range(2) for hf in range(nh))

    def body(k, carry):
        out = []
        for idx, cur in enumerate(carry):
            d, hf = divmod(idx, nh)
            kk = k if d == 0 else nblk - 1 - k
            cin_s[d, hf, pl.ds(kk, 1), :] = cur
            out.append(pend_s[d, hf, pl.ds(kk, 1), :] * cur + hend_s[d, hf, pl.ds(kk, 1), :])
        return tuple(out)

    final = lax.fori_loop(0, nblk, body, init)

    y = None
    for d in range(2):
        for hf in range(nh):
            cin = cin_s[d, hf]
            for s in range(LRU_SEG):
                full_h = hloc_s[d, hf, s] + ploc_s[d, hf, s] * cin
                if d == 0:
                    y_s[hf, pl.ds(s, nblk, stride=LRU_SEG), :] = full_h
                else:
                    y_s[hf, pl.ds(s, nblk, stride=LRU_SEG), :] += full_h
    y = jnp.concatenate([y_s[hf] for hf in range(nh)], axis=1) * jax.nn.gelu(ga)
    ya_ref[0] = y.astype(BF16)
    st_ref[0, 0, 0:1, :] = jnp.concatenate(final[:nh], axis=1)
    st_ref[0, 0, 1:2, :] = jnp.concatenate(final[nh:], axis=1)


def _lru_call(ulru, conv_w, conv_b, w_gates_b, b_gates, lam, layer, h0, h0_layer, st_layer=0, st_depth=1,
              prev_state=None):
    b, n, _ = ulru.shape
    c = D_A
    per_layer = lambda shape: pl.BlockSpec((1,) + shape, lambda i: (layer,) + (0,) * len(shape))
    nh, nblk = c // LANES, n // LRU_SEG
    assert n % LRU_SEG == 0 and nblk % 8 == 0
    scratch = ([pltpu.VMEM((nh, n, LANES), F32)] * 3
               + [pltpu.VMEM((2, nh, LRU_SEG, nblk, LANES), F32)] * 2
               + [pltpu.VMEM((2, nh, nblk, LANES), F32)] * 3)
    in_specs = [pl.BlockSpec((1, n, 2 * c), lambda i: (i, 0, 0)),
                per_layer((CONV_W, c)), per_layer((1, c)), per_layer((c, 4 * c)), per_layer((1, 4 * c)),
                per_layer((2, c)),
                pl.BlockSpec((1, 1, 2, c), lambda i: (i, h0_layer, 0, 0))]
    args = [ulru, conv_w, conv_b, w_gates_b, b_gates, lam, h0]
    aliases = {}
    if prev_state is not None:
        aliases[len(args)] = 1
        in_specs.append(pl.BlockSpec(memory_space=pl.ANY))
        args.append(prev_state)
    return pl.pallas_call(
        functools.partial(_lru_kernel, n=n),
        grid=(b,),
        in_specs=in_specs,
        out_specs=[pl.BlockSpec((1, n, c), lambda i: (i, 0, 0)),
                   pl.BlockSpec((1, 1, 2, c), lambda i: (i, st_layer, 0, 0))],
        out_shape=[jax.ShapeDtypeStruct((b, n, c), BF16),
                   jax.ShapeDtypeStruct((b, st_depth, 2, c), F32)],
        scratch_shapes=scratch,
        input_output_aliases=aliases,
        compiler_params=_params(1),
        name="rglru",
    )(*args)


def _softmax_pv(parts, sink):
    mx = None
    for s, _, _ in parts:
        cur = jnp.max(s, axis=-1, keepdims=True)
        mx = cur if mx is None else jnp.maximum(mx, cur)
    if sink is not None:
        mx = jnp.maximum(mx, sink)
    den = None
    out = None
    for s, v, v_t in parts:
        p = jnp.exp(s - mx)
        cur = jnp.sum(p, axis=-1, keepdims=True)
        den = cur if den is None else den + cur
        o = _dot_nt(p.astype(BF16), v) if v_t else _dot(p.astype(BF16), v)
        out = o if out is None else out + o
    if sink is not None:
        den = den + jnp.exp(sink - mx)
    return out / den


def _swap_halves(x):
    return jnp.concatenate([x[:, HEAD_DIM:], x[:, :HEAD_DIM]], axis=1)


def _keep_half(x, lo_mask, half):
    zero = jnp.zeros_like(x)
    return jnp.where(lo_mask, x, zero) if half == 0 else jnp.where(lo_mask, zero, x)


def _ctx_attn_kernel(sink_ref, att_ref, ob_ref, oc_ref, *, n, rb, sink0):
    lo = lax.broadcasted_iota(jnp.int32, (n, LANES), 1) < HEAD_DIM
    for r in range(rb):
        for p in range(H_C // 2):
            qp = att_ref[r, :, QC0 + LANES * p: QC0 + LANES * (p + 1)]
            kp = att_ref[r, :, KC0 + LANES * p: KC0 + LANES * (p + 1)]
            vp = att_ref[r, :, VC0 + LANES * p: VC0 + LANES * (p + 1)]
            outs = [_softmax_pv([(_dot_nt(qp, _keep_half(kp, lo, half)), vp, False)], None) for half in range(2)]
            oc_ref[r, :, LANES * p: LANES * (p + 1)] = jnp.where(lo, outs[0], outs[1]).astype(BF16)
        kpair = att_ref[r, :, KB0: KB0 + LANES]
        vpair = att_ref[r, :, VB0: VB0 + LANES]
        kpair_sw = _swap_halves(kpair)
        vpair_sw = _swap_halves(vpair)
        for p in range(H_B // 2):
            qp = att_ref[r, :, QB0 + LANES * p: QB0 + LANES * (p + 1)]
            outs = []
            for half in range(2):
                h = 2 * p + half
                aligned = (h // G_B) == half
                ksrc, vsrc = (kpair, vpair) if aligned else (kpair_sw, vpair_sw)
                outs.append(_softmax_pv([(_dot_nt(qp, _keep_half(ksrc, lo, half)), vsrc, False)], sink_ref[sink0 + h]))
            ob_ref[r, :, LANES * p: LANES * (p + 1)] = jnp.where(lo, outs[0], outs[1]).astype(BF16)


def _ctx_attn_call(sinks, att, layer):
    b, n, w = att.shape
    rb = CTX_ATTN_REQS
    assert b % rb == 0 and KV_B == 2
    return pl.pallas_call(
        functools.partial(_ctx_attn_kernel, n=n, rb=rb, sink0=layer * H_B),
        grid=(b // rb,),
        in_specs=[pl.BlockSpec(memory_space=pltpu.SMEM),
                  pl.BlockSpec((rb, n, w), lambda i: (i, 0, 0))],
        out_specs=[pl.BlockSpec((rb, n, D_B), lambda i: (i, 0, 0)), pl.BlockSpec((rb, n, D_C), lambda i: (i, 0, 0))],
        out_shape=[jax.ShapeDtypeStruct((b, n, D_B), BF16), jax.ShapeDtypeStruct((b, n, D_C), BF16)],
        compiler_params=_params(1),
        name="ctx_attn",
    )(sinks, att)


def _win_attn_kernel(sink_ref, qkv_ref, kc_ref, vc_ref, o_ref, *, n, sink0):
    j = pl.program_id(1)
    nq = WIN_Q_BLOCKS * WIN_BLK
    nloc = nq + 2 * WIN_BLK
    q0 = pl.multiple_of(j * nq, WIN_BLK)
    ks = pl.multiple_of(jnp.clip(j * nq - WIN_BLK, 0, n - nloc), WIN_BLK)
    row = lax.broadcasted_iota(jnp.int32, (nq, nloc), 0)
    col = lax.broadcasted_iota(jnp.int32, (nq, nloc), 1)
    in_window = jnp.where(jnp.abs((q0 + row) - (ks + col)) <= WINDOW, 0.0, NEG_INF)
    band = jnp.concatenate([in_window] * G_B, axis=0)
    rowh = lax.broadcasted_iota(jnp.int32, (G_B * nq, 1), 0) // nq
    for kv in range(KV_B):
        gw = G_B * HEAD_DIM
        qblk = qkv_ref[0, pl.ds(q0, nq), QB0 + gw * kv: QB0 + gw * (kv + 1)]
        q3 = jnp.concatenate([qblk[:, 64 * g: 64 * (g + 1)] for g in range(G_B)], axis=0)
        kl = qkv_ref[0, pl.ds(ks, nloc), KB0 + 64 * kv: KB0 + 64 * (kv + 1)]
        vl = qkv_ref[0, pl.ds(ks, nloc), VB0 + 64 * kv: VB0 + 64 * (kv + 1)]
        kc_t = kc_ref[0, 0, kv].astype(BF16)
        vc_t = vc_ref[0, 0, kv].astype(BF16)
        s_loc = _dot_nt(q3, kl) + band
        s_ctx = _dot(q3, kc_t)
        s0 = sink0 + G_B * kv
        sk = jnp.where(rowh == 0, sink_ref[s0], jnp.where(rowh == 1, sink_ref[s0 + 1], sink_ref[s0 + 2]))
        o = _softmax_pv([(s_loc, vl, False), (s_ctx, vc_t, True)], sk)
        for g in range(G_B):
            h = kv * G_B + g
            o_ref[0, :, 64 * h: 64 * (h + 1)] = o[g * nq:(g + 1) * nq].astype(BF16)


def _win_attn_call(sinks, qkv, cache_k, cache_v, layer):
    b, n, w = qkv.shape
    past = cache_k.shape[4]
    cache_spec = pl.BlockSpec((1, 1, KV_B, HEAD_DIM, past), lambda i, j: (i, layer, 0, 0, 0))
    nq = WIN_Q_BLOCKS * WIN_BLK
    assert n % nq == 0 and n >= nq + 2 * WIN_BLK and WINDOW == WIN_BLK
    return pl.pallas_call(
        functools.partial(_win_attn_kernel, n=n, sink0=layer * H_B),
        grid=(b, n // nq),
        in_specs=[pl.BlockSpec(memory_space=pltpu.SMEM),
                  pl.BlockSpec((1, n, w), lambda i, j: (i, 0, 0)),
                  cache_spec, cache_spec],
        out_specs=pl.BlockSpec((1, nq, D_B), lambda i, j: (i, j, 0)),
        out_shape=jax.ShapeDtypeStruct((b, n, D_B), BF16),
        compiler_params=_params(2),
        name="win_attn",
    )(sinks, qkv, cache_k, cache_v)


def _nbr_window_start(g, rows):
    return jnp.clip(g * NBR_Q_ROWS - NA_ROWS // 2, 0, rows - NBR_K_ROWS)


def _nbr_attn_kernel(q_ref, k_ref, v_ref, rel_ref, kc_ref, vc_ref, o_ref, bias_s, *, n):
    g = pl.program_id(0)
    nq = NBR_Q_ROWS * GRID_W
    nk = NBR_K_ROWS * GRID_W
    rows = n // GRID_W
    ws = _nbr_window_start(g, rows)
    k0 = pl.multiple_of(ws * GRID_W, GRID_W)

    @pl.when(pl.program_id(1) == 0)
    def _():
        for i in range(NBR_Q_ROWS):
            qr = g * NBR_Q_ROWS + i
            rs = jnp.clip(qr - NA_ROWS // 2, 0, rows - NA_ROWS)
            for jp in range(NBR_K_ROWS // 2):
                tiles = []
                for j in (2 * jp, 2 * jp + 1):
                    kr = ws + j
                    valid = (kr >= rs) & (kr < rs + NA_ROWS)
                    d = jnp.clip(kr - qr + NA_ROWS - 1, 0, 2 * NA_ROWS - 2)
                    tiles.append(jnp.where(valid, rel_ref[0, :, pl.ds(d, 1)][:, 0], NEG_INF))
                bias_s[:, GRID_W * i: GRID_W * (i + 1), LANES * jp: LANES * (jp + 1)] = jnp.concatenate(tiles, axis=-1)
    lo_k = lax.broadcasted_iota(jnp.int32, (nk, LANES), 1) < HEAD_DIM
    lo_q = lax.broadcasted_iota(jnp.int32, (nq, LANES), 1) < HEAD_DIM
    zpad = jnp.zeros((HEAD_DIM, kc_ref.shape[4]), BF16)
    for p in range(H_C // 2):
        qp = q_ref[0, :, LANES * p: LANES * (p + 1)]
        kp = k_ref[0, pl.ds(k0, nk), LANES * p: LANES * (p + 1)]
        vp = v_ref[0, pl.ds(k0, nk), LANES * p: LANES * (p + 1)]
        outs = []
        for half in range(2):
            h = 2 * p + half
            kc_t = kc_ref[0, 0, h].astype(BF16)
            vc_t = vc_ref[0, 0, h].astype(BF16)
            kc_t = jnp.concatenate([kc_t, zpad] if half == 0 else [zpad, kc_t], axis=0)
            vc_t = jnp.concatenate([vc_t, zpad] if half == 0 else [zpad, vc_t], axis=0)
            s_loc = _dot_nt(qp, _keep_half(kp, lo_k, half)) + bias_s[h]
            s_ctx = _dot(qp, kc_t)
            outs.append(_softmax_pv([(s_loc, vp, False), (s_ctx, vc_t, True)], None))
        o_ref[0, :, LANES * p: LANES * (p + 1)] = jnp.where(lo_q, outs[0], outs[1]).astype(BF16)


def _nbr_attn_call(att, rel_tab, cache_k, cache_v, layer):
    b, n, w = att.shape
    past = cache_k.shape[4]
    nq = NBR_Q_ROWS * GRID_W
    nk = NBR_K_ROWS * GRID_W
    gw = H_C * HEAD_DIM
    rows = n // GRID_W
    assert (QC0, KC0, VC0) == (0, gw, 2 * gw) and n % nq == 0 and rows >= NBR_K_ROWS and NBR_K_ROWS % 2 == 0
    cache_spec = pl.BlockSpec((1, 1, H_C, HEAD_DIM, past), lambda g, i: (i, layer, 0, 0, 0))
    return pl.pallas_call(
        functools.partial(_nbr_attn_kernel, n=n),
        grid=(n // nq, b),
        in_specs=[pl.BlockSpec((1, nq, gw), lambda g, i: (i, g, 0)),
                  pl.BlockSpec((1, n, gw), lambda g, i: (i, 0, 1)),
                  pl.BlockSpec((1, n, gw), lambda g, i: (i, 0, 2)),
                  pl.BlockSpec((1,) + rel_tab.shape[1:], lambda g, i: (layer, 0, 0, 0, 0)),
                  cache_spec, cache_spec],
        out_specs=pl.BlockSpec((1, nq, gw), lambda g, i: (i, g, 0)),
        out_shape=jax.ShapeDtypeStruct((b, n, gw), BF16),
        scratch_shapes=[pltpu.VMEM((H_C, nq, nk), F32)],
        compiler_params=_params(2),
        name="nbr_attn",
    )(att, att, att, rel_tab, cache_k, cache_v)


def _outproj_kernel(x_ref, ya_ref, yb_ref, yc_ref, mod_ref, g_ref, wo_ref, wr_ref,
                    xn_ref, h2_ref, aff_ref):
    mix = jnp.concatenate([ya_ref[0], yb_ref[0], yc_ref[0]], axis=1)
    proj = _dot(mix, wo_ref[0])
    xn = x_ref[0] + mod_ref[0, 0, 2:3, :] * proj
    xn_ref[0] = xn
    h2 = _rms_modulate(xn, g_ref[0], mod_ref[0, 0, 3:4, :], mod_ref[0, 0, 4:5, :]).astype(BF16)
    h2_ref[0] = h2
    logits = _dot(h2, wr_ref[0])
    lane = lax.broadcasted_iota(jnp.int32, logits.shape, 1)
    valid = lane < N_EXPERTS
    logits = jnp.where(valid, logits, NEG_INF)
    e = jnp.exp(logits - jnp.max(logits, axis=-1, keepdims=True))
    e = jnp.where(valid, e, 0.0)
    aff_ref[0] = e / jnp.sum(e, axis=-1, keepdims=True)


def _outproj_call(x3, ya, yb, yc, mods, mod_row0, gains, wo_b16, w_router_b, layer):
    g, n, d = x3.shape
    tb = TOKEN_BLOCK
    tok = lambda w: pl.BlockSpec((1, tb, w), lambda i, j: (i, j, 0))
    per_layer = lambda shape: pl.BlockSpec((1,) + shape, lambda i, j: (layer,) + (0,) * len(shape))
    return pl.pallas_call(
        _outproj_kernel,
        grid=(g, n // tb),
        in_specs=[tok(d), tok(D_A), tok(D_B), tok(D_C),
                  pl.BlockSpec((1, 1, N_MOD, d), lambda i, j: (layer, mod_row0 + i, 0, 0)),
                  per_layer((1, d)), per_layer((D_A + D_B + D_C, d)), per_layer((d, LANES))],
        out_specs=[tok(d), tok(d), tok(LANES)],
        out_shape=[jax.ShapeDtypeStruct((g, n, d), F32),
                   jax.ShapeDtypeStruct((g, n, d), BF16),
                   jax.ShapeDtypeStruct((g, n, LANES), F32)],
        compiler_params=_params(2),
        name="outproj_router",
    )(x3, ya, yb, yc, mods, gains, wo_b16, w_router_b)


def _sort_by_affinity(keys, idx, pos, levels, seg_lanes):
    rows = keys.shape[0]
    sign = [jnp.where(((pos >> m) & 1) == 0, 1.0, -1.0) for m in range(levels)]

    def partner(x, j):
        d = 1 << j
        low = sign[j] > 0.0
        if d >= rows:
            sh = (d // rows) * seg_lanes
            return jnp.where(low, pltpu.roll(x, LANES - sh, 1), pltpu.roll(x, sh, 1))
        if d >= 8:
            x4 = x.reshape(rows // (2 * d), 2, d, LANES)
            return jnp.concatenate([x4[:, 1:2], x4[:, 0:1]], axis=1).reshape(rows, LANES)
        x3 = x.reshape(rows // 8, 8, LANES)
        low3 = sign[j].reshape(rows // 8, 8, LANES) > 0.0
        return jnp.where(low3, pltpu.roll(x3, 8 - d, 1), pltpu.roll(x3, d, 1)).reshape(rows, LANES)

    for k in range(1, levels + 1):
        for j in range(k - 1, -1, -1):
            pk, pi = partner(keys, j), partner(idx, j)
            before = jnp.where(keys == pk, jnp.where(idx < pi, 1.0, -1.0), jnp.where(keys > pk, 1.0, -1.0))
            want_first = sign[j] * sign[k] if k < levels else sign[j]
            keep = before * want_first > 0.0
            keys = jnp.where(keep, keys, pk)
            idx = jnp.where(keep, idx, pi)
    return keys, idx


def _route_kernel(aff_ref, h2_ref, xs_ref, vals_ref, rank_ref, *, n, cap, rb):
    ne = N_EXPERTS
    rows = ROUTE_SORT_ROWS
    nseg = n // rows
    seg_lanes = rb * ne
    levels = n.bit_length() - 1
    fcap = float(cap)
    row = lax.broadcasted_iota(jnp.int32, (rows, LANES), 0)
    lane = lax.broadcasted_iota(jnp.int32, (rows, LANES), 1)
    pos = (lane // seg_lanes) * rows + row
    keys = jnp.full((rows, LANES), -1.0, F32)
    for r in range(rb):
        for seg in range(nseg):
            off = seg * seg_lanes + r * ne
            piece = aff_ref[r, seg * rows:(seg + 1) * rows, :]
            if off:
                piece = pltpu.roll(piece, off, 1)
            keys = jnp.where((lane >= off) & (lane < off + ne), piece, keys)
    keys, idx = _sort_by_affinity(keys, pos.astype(F32), pos, levels, seg_lanes)
    top_keys = keys[:cap]
    top_idx = idx[:cap]

    tok = lax.broadcasted_iota(jnp.int32, (cap, n), 1).astype(F32)
    weight = fcap - lax.broadcasted_iota(jnp.int32, (cap, n), 0).astype(F32)
    pad = jnp.full((LANES - ne, n), fcap, F32)
    for r in range(rb):
        picks, rank_rows = [], []
        for e in range(ne):
            col = r * ne + e
            onehot = jnp.where(top_idx[:, col:col + 1] == tok, 1.0, 0.0)
            vals_ref[e, r] = top_keys[:, col:col + 1]
            rank_rows.append(fcap - jnp.sum(onehot * weight, axis=0, keepdims=True))
            picks.append(onehot.astype(BF16))
        rank_ref[r] = jnp.concatenate(rank_rows + [pad], axis=0).T
        xs = _dot(jnp.concatenate(picks, axis=0), h2_ref[r])
        xs_ref[:, r] = xs.reshape(ne, cap, xs.shape[-1]).astype(BF16)


def _route_call(aff, h2, rb):
    b, n, d = h2.shape
    cap = max(1, EC_FACTOR * n // N_EXPERTS)
    nseg = n // ROUTE_SORT_ROWS
    assert b % rb == 0 and n == nseg * ROUTE_SORT_ROWS and n & (n - 1) == 0
    assert nseg * rb * N_EXPERTS <= LANES and cap <= ROUTE_SORT_ROWS
    return pl.pallas_call(
        functools.partial(_route_kernel, n=n, cap=cap, rb=rb),
        grid=(b // rb,),
        in_specs=[pl.BlockSpec((rb, n, LANES), lambda i: (i, 0, 0)),
                  pl.BlockSpec((rb, n, d), lambda i: (i, 0, 0))],
        out_specs=[pl.BlockSpec((N_EXPERTS, rb, cap, d), lambda i: (0, i, 0, 0)),
                   pl.BlockSpec((N_EXPERTS, rb, cap, 1), lambda i: (0, i, 0, 0)),
                   pl.BlockSpec((rb, n, LANES), lambda i: (i, 0, 0))],
        out_shape=[jax.ShapeDtypeStruct((N_EXPERTS, b, cap, d), BF16),
                   jax.ShapeDtypeStruct((N_EXPERTS, b, cap, 1), F32),
                   jax.ShapeDtypeStruct((b, n, LANES), F32)],
        compiler_params=_params(1),
        name="route_gather",
    )(aff, h2)


def _expert_kernel(xc_ref, xl_ref, vc_ref, vl_ref, wg_ref, wu_ref, wd_ref, yc_ref, yl_ref, acc_c, acc_l):
    f = pl.program_id(1)

    def step(first):
        wg = wg_ref[0, 0].astype(BF16)
        wu = wu_ref[0, 0].astype(BF16)
        wd = wd_ref[0, 0].astype(BF16)
        for x_ref, v_ref, y_ref, acc in ((xc_ref, vc_ref, yc_ref, acc_c), (xl_ref, vl_ref, yl_ref, acc_l)):
            rows = x_ref.shape[1]
            chunk = min(EXPERT_ROW_CHUNK, rows)
            for c0 in range(0, rows, chunk):
                sl = slice(c0, c0 + chunk)
                x = x_ref[0, sl, :]
                a = _dot(x, wg)
                u = _dot(x, wu)
                act = ((a * jax.nn.sigmoid(a)) * u).astype(BF16)
                part = _dot(act, wd)
                if first:
                    acc[sl, :] = part
                else:
                    y_ref[0, sl, :] = ((acc[sl, :] + part) * v_ref[0, sl, :]).astype(BF16)

    pl.when(f == 0)(functools.partial(step, True))
    pl.when(f == EXPERT_F_STEPS - 1)(functools.partial(step, False))


def _expert_call(xs_c, xs_l, vals_c, vals_l, w_gate, w_up, w_down, layer):
    e, rc, d = xs_c.shape
    rl = xs_l.shape[1]
    f_total = w_gate.shape[-1]
    assert EXPERT_F_STEPS == 2
    fb = f_total // EXPERT_F_STEPS
    per_e = lambda rows, w: pl.BlockSpec((1, rows, w), lambda i, j: (i, 0, 0))
    return pl.pallas_call(
        _expert_kernel,
        grid=(e, EXPERT_F_STEPS),
        in_specs=[per_e(rc, d), per_e(rl, d), per_e(rc, 1), per_e(rl, 1),
                  pl.BlockSpec((1, 1, d, fb), lambda i, j: (layer, i, 0, j)),
                  pl.BlockSpec((1, 1, d, fb), lambda i, j: (layer, i, 0, j)),
                  pl.BlockSpec((1, 1, fb, d), lambda i, j: (layer, i, j, 0))],
        out_specs=[per_e(rc, d), per_e(rl, d)],
        out_shape=[jax.ShapeDtypeStruct((e, rc, d), BF16),
                   jax.ShapeDtypeStruct((e, rl, d), BF16)],
        scratch_shapes=[pltpu.VMEM((rc, d), F32), pltpu.VMEM((rl, d), F32)],
        compiler_params=_params(2),
        name="experts",
    )(xs_c, xs_l, vals_c, vals_l, w_gate, w_up, w_down)


def _combine_kernel(y_ref, rank_ref, xn_ref, mod_ref, gf_ref, o_ref, *, n, cap, final, rb):
    ec = N_EXPERTS * cap
    d = xn_ref.shape[-1]
    ce = lax.broadcasted_iota(jnp.int32, (LANES, ec), 1) // cap
    ee = lax.broadcasted_iota(jnp.int32, (LANES, ec), 0)
    expand = jnp.where(ce == ee, 1.0, 0.0).astype(BF16)
    slot = (lax.broadcasted_iota(jnp.int32, (n, ec), 1) % cap).astype(F32)
    for r in range(rb):
        rank = rank_ref[r].astype(BF16)
        rexp = _dot(rank, expand)
        onehot = jnp.where(rexp == slot, 1.0, 0.0).astype(BF16)
        moe = _dot(onehot, y_ref[:, r].reshape(ec, d))
        x = xn_ref[r] + mod_ref[0, 0, 5:6, :] * moe
        if final:
            x = (x * lax.rsqrt(jnp.mean(x * x, axis=-1, keepdims=True) + EPS)) * gf_ref[...]
        o_ref[r] = x


def _combine_call(y4, rank, xn, mods, mod_row0, shared_mod, layer, g_final, final):
    e, b, cap, d = y4.shape
    n = xn.shape[1]
    rb = COMBINE_REQS_CTX if shared_mod else 1
    assert b % rb == 0
    return pl.pallas_call(
        functools.partial(_combine_kernel, n=n, cap=cap, final=final, rb=rb),
        grid=(b // rb,),
        in_specs=[pl.BlockSpec((e, rb, cap, d), lambda i: (0, i, 0, 0)),
                  pl.BlockSpec((rb, n, LANES), lambda i: (i, 0, 0)),
                  pl.BlockSpec((rb, n, d), lambda i: (i, 0, 0)),
                  pl.BlockSpec((1, 1, N_MOD, d), (lambda i: (layer, mod_row0, 0, 0)) if shared_mod
                               else (lambda i: (layer, mod_row0 + i, 0, 0))),
                  pl.BlockSpec((1, d), lambda i: (0, 0))],
        out_specs=pl.BlockSpec((rb, n, d), lambda i: (i, 0, 0)),
        out_shape=jax.ShapeDtypeStruct((b, n, d), F32),
        compiler_params=_params(1),
        name="combine",
    )(y4, rank, xn, mods, g_final)


def _rope_tables(n):
    t = jnp.arange(n, dtype=jnp.int32)
    row = (t // GRID_W).astype(F32)
    col = (t % GRID_W).astype(F32)
    half = HEAD_DIM // 4
    inv = jnp.power(ROPE_BASE, -jnp.arange(half, dtype=F32) / half)
    ang_r = row[:, None] * inv[None, :]
    ang_c = col[:, None] * inv[None, :]
    cos_h = jnp.concatenate([jnp.cos(ang_r)] * 2 + [jnp.cos(ang_c)] * 2, axis=-1)
    sin_h = jnp.concatenate([-jnp.sin(ang_r), jnp.sin(ang_r), -jnp.sin(ang_c), jnp.sin(ang_c)], axis=-1)
    reps = H_B + KV_B
    return jnp.tile(cos_h, (1, reps)), jnp.tile(sin_h, (1, reps))


def _nbr_rel_tables(rpb):
    col = np.arange(GRID_W)
    cs = np.clip(col - NA_COLS // 2, 0, GRID_W - NA_COLS)
    col_mask = (col[None, :] >= cs[:, None]) & (col[None, :] < cs[:, None] + NA_COLS)
    dc_idx = np.clip(col[None, :] - col[:, None], -(NA_COLS - 1), NA_COLS - 1) + (NA_COLS - 1)
    sel_col = (dc_idx[None, :, :] == np.arange(2 * NA_COLS - 1)[:, None, None]).astype(np.float32)
    t = jnp.einsum("lhdc,cqk->lhdqk", rpb.astype(F32), sel_col, precision=lax.Precision.HIGHEST)
    return jnp.where(col_mask, t, NEG_INF)


def _block_diag(w):
    nb, bw = w.shape[-3], w.shape[-1]
    eye = jnp.eye(nb, dtype=w.dtype)
    return (eye[:, None, :, None] * w[..., :, :, None, :]).reshape(w.shape[:-3] + (nb * bw, nb * bw))


def kernel(x_prompt, x_sample, state_lru, cache_k_win, cache_v_win, cache_k_nbr, cache_v_nbr, c, c_ctx, w_mod, b_mod, g_norm1, w_in, conv_w, conv_b, w_gate_r, b_gate_r, w_gate_i, b_gate_i, lru_lambda, sink_logit, nbr_bias, w_out, g_norm2, w_router, w_exp_gate, w_exp_up, w_exp_down, g_final):
    bc, seq, d = x_prompt.shape
    bl, n_lat, _ = x_sample.shape
    depth = w_mod.shape[0]
    assert bl + 1 <= 8 and d == 1024

    cond8 = jnp.zeros((8, d), F32).at[0].set(c_ctx).at[1:1 + bl].set(c)
    mods = _adaln_call(cond8, w_mod, b_mod.reshape(depth, 1, -1)).reshape(depth, 8, N_MOD, d)
    rope_tabs = _rope_tables(n_lat)
    bias_tab = _nbr_rel_tables(nbr_bias)
    ckw_t, cvw_t, ckn_t, cvn_t = (jnp.swapaxes(a, -1, -2) for a in (cache_k_win, cache_v_win, cache_k_nbr, cache_v_nbr))
    gf = g_final.reshape(1, d)
    assert SCALE == 0.125
    new_cols = np.arange(w_in.shape[-1]) - LRU_W
    is_q = ((new_cols >= QC0) & (new_cols < KC0)) | ((new_cols >= QB0) & (new_cols < KB0))
    q_scale = jnp.asarray(np.where(is_q, SCALE, 1.0), F32)

    nbr0 = LRU_W + D_B + 2 * KV_W
    w_in_b = (jnp.concatenate([w_in[..., :LRU_W], w_in[..., nbr0:], w_in[..., LRU_W:nbr0]], axis=-1) * q_scale).astype(BF16)
    wo = w_out.astype(BF16)
    wr_b = jnp.pad(w_router.astype(BF16), ((0, 0), (0, 0), (0, LANES - N_EXPERTS)))
    gate_r, gate_i = _block_diag(w_gate_r), _block_diag(w_gate_i)
    w_gates = jnp.concatenate([gate_r[:, 0], gate_i[:, 0], gate_r[:, 1], gate_i[:, 1]], axis=-1).astype(BF16)
    b_gates = jnp.concatenate([b_gate_r[:, 0], b_gate_i[:, 0], b_gate_r[:, 1], b_gate_i[:, 1]], axis=-1)[:, None, :]
    cb = conv_b[:, None, :]
    g1 = g_norm1[:, None, :]
    g2 = g_norm2[:, None, :]
    sinks = sink_logit.reshape(-1)
    ctx_row, lat_row = 0, 1

    xc = x_prompt.reshape(1, bc * seq, d)
    xl = x_sample
    zeros_state = jnp.zeros((bc, 1, 2, D_A), F32)
    caches = None
    st = None
    for l in range(depth):
        final = l == depth - 1
        ulru_c, qkv_c, *caches = _inproj_call(xc, mods, ctx_row, g1, w_in_b, l, cache_seq=seq, depth=depth,
                                              prev_caches=caches)
        ya_c, st = _lru_call(ulru_c.reshape(bc, seq, LRU_W), conv_w, cb, w_gates, b_gates, lru_lambda, l,
                             zeros_state, 0, st_layer=l, st_depth=depth, prev_state=st)
        yb_c, yc_c = _ctx_attn_call(sinks, qkv_c.reshape(bc, seq, -1), l)
        xn_c, h2_c, aff_c = _outproj_call(xc, ya_c.reshape(1, bc * seq, -1), yb_c.reshape(1, bc * seq, -1),
                                          yc_c.reshape(1, bc * seq, -1), mods, ctx_row, g2, wo, wr_b, l)
        xs_c, vals_c, rank_c = _route_call(aff_c.reshape(bc, seq, LANES), h2_c.reshape(bc, seq, d), ROUTE_REQS_CTX)

        ulru_l, qkv_l = _inproj_call(xl, mods, lat_row, g1, w_in_b, l, rope_tabs=rope_tabs)
        ya_l, _ = _lru_call(ulru_l, conv_w, cb, w_gates, b_gates, lru_lambda, l, state_lru, l)
        yb_l = _win_attn_call(sinks, qkv_l, ckw_t, cvw_t, l)
        yc_l = _nbr_attn_call(qkv_l, bias_tab, ckn_t, cvn_t, l)
        xn_l, h2_l, aff_l = _outproj_call(xl, ya_l, yb_l, yc_l, mods, lat_row, g2, wo, wr_b, l)
        xs_l, vals_l, rank_l = _route_call(aff_l, h2_l, ROUTE_REQS_LAT)

        cap_c, cap_l = xs_c.shape[2], xs_l.shape[2]
        y_c, y_l = _expert_call(xs_c.reshape(N_EXPERTS, bc * cap_c, d), xs_l.reshape(N_EXPERTS, bl * cap_l, d),
                                vals_c.reshape(N_EXPERTS, bc * cap_c, 1), vals_l.reshape(N_EXPERTS, bl * cap_l, 1),
                                w_exp_gate, w_exp_up, w_exp_down, l)
        xc = _combine_call(y_c.reshape(N_EXPERTS, bc, cap_c, d), rank_c, xn_c.reshape(bc, seq, d), mods, ctx_row, True,
                           l, gf, final)
        xl = _combine_call(y_l.reshape(N_EXPERTS, bl, cap_l, d), rank_l, xn_l, mods, lat_row, False, l, gf, final)
        xc = xc.reshape(1, bc * seq, d)

    y_prompt = xc.reshape(bc, seq, d)
    y_sample = xl
    return (y_prompt, y_sample, st, *[jnp.swapaxes(buf, -1, -2) for buf in caches])
```

```python
import functools

import numpy as np
import jax
import jax.numpy as jnp
from jax import lax
from jax.experimental import pallas as pl
from jax.experimental.pallas import tpu as pltpu

F32 = jnp.float32
BF16 = jnp.bfloat16

HEAD_DIM = 64
GRID_W = 64
LRU_C = 8.0
CONV_W = 4
KV_B = 2
G_B = 3
H_B = KV_B * G_B
H_C = 6
WINDOW = 128
WIN_BLK = 128
NA_ROWS = 8
NA_COLS = 16
ROPE_BASE = 10000.0
N_EXPERTS = 16
EC_FACTOR = 2
N_MOD = 6
EPS = 1e-6
NEG_INF = -1e30
SCALE = HEAD_DIM ** -0.5

LANES = 128
VMEM_LIMIT_BYTES = 56 * 1024 * 1024

TOKEN_BLOCK = 1024
MOD_COL_BLOCK = 1536
EXPERT_F_STEPS = 2
EXPERT_ROW_CHUNK = 256
D_A = 256
D_B = H_B * HEAD_DIM
D_C = H_C * HEAD_DIM
KV_W = KV_B * HEAD_DIM
LRU_W = 2 * D_A
QC0, KC0, VC0 = 0, D_C, 2 * D_C
QB0 = 3 * D_C
KB0, VB0, ATT_W = QB0 + D_B, QB0 + D_B + KV_W, QB0 + D_B + 2 * KV_W
CTX_ATTN_REQS = 4
WIN_Q_BLOCKS = 2
NBR_Q_ROWS = 4
NBR_K_ROWS = 12
ROUTE_REQS_CTX = 8
ROUTE_REQS_LAT = 2
ROUTE_SORT_ROWS = 256
COMBINE_REQS_CTX = 4
LRU_SEG = 8


def _params(n_axes):
    return pltpu.CompilerParams(dimension_semantics=("arbitrary",) * n_axes,
                                vmem_limit_bytes=VMEM_LIMIT_BYTES)


def _dot(a, b):
    return jnp.dot(a, b, preferred_element_type=F32)


def _dot_nt(a, b):
    return lax.dot_general(a, b, (((1,), (1,)), ((), ())), preferred_element_type=F32)


def _adaln_kernel(cond_ref, w_ref, b_ref, o_ref):
    c = cond_ref[...]
    s = c * jax.nn.sigmoid(c)
    o_ref[0] = _dot(s.astype(BF16), w_ref[0].astype(BF16)) + b_ref[0]


def _adaln_call(cond8, w_mod, b_mod3):
    depth, d, n = w_mod.shape
    nb = MOD_COL_BLOCK
    return pl.pallas_call(
        _adaln_kernel,
        grid=(depth, n // nb),
        in_specs=[pl.BlockSpec((8, d), lambda l, j: (0, 0)),
                  pl.BlockSpec((1, d, nb), lambda l, j: (l, 0, j)),
                  pl.BlockSpec((1, 1, nb), lambda l, j: (l, 0, j))],
        out_specs=pl.BlockSpec((1, 8, nb), lambda l, j: (l, 0, j)),
        out_shape=jax.ShapeDtypeStruct((depth, 8, n), F32),
        compiler_params=_params(2),
        name="adaln",
    )(cond8, w_mod, b_mod3)


def _rms_modulate(x, gain, shift, scale):
    y = x * lax.rsqrt(jnp.mean(x * x, axis=-1, keepdims=True) + EPS)
    return (y * gain) * (1.0 + scale) + shift


def _inproj_kernel(*refs, rope, cache, seq):
    if rope:
        x_ref, mod_ref, g_ref, w_ref, cos_ref, sin_ref, ulru_ref, qkv_ref = refs
    elif cache:
        x_ref, mod_ref, g_ref, w_ref = refs[:4]
        ulru_ref, qkv_ref, kw_ref, vw_ref, kn_ref, vn_ref = refs[-6:]
    else:
        x_ref, mod_ref, g_ref, w_ref, ulru_ref, qkv_ref = refs
    x = x_ref[0]
    h = _rms_modulate(x, g_ref[0], mod_ref[0, 0, 0:1, :], mod_ref[0, 0, 1:2, :])
    u = _dot(h.astype(BF16), w_ref[0])
    tb = u.shape[0]
    ulru_ref[0] = u[:, :LRU_W]
    if rope:
        lo, hi = LRU_W + QB0, LRU_W + VB0
        qk = u[:, lo:hi]
        lane = lax.broadcasted_iota(jnp.int32, (tb, hi - lo), 1)
        first = (lane & 31) < 16
        partner = jnp.where(first, pltpu.roll(qk, hi - lo - 16, 1), pltpu.roll(qk, 16, 1))
        qk = qk * cos_ref[...] + partner * sin_ref[...]
        qkv_ref[0, :, :QB0] = u[:, LRU_W:lo].astype(BF16)
        qkv_ref[0, :, QB0:VB0] = qk.astype(BF16)
        qkv_ref[0, :, VB0:] = u[:, hi:].astype(BF16)
    else:
        qkv_ref[0] = u[:, LRU_W:].astype(BF16)
    if cache:
        for r in range(tb // seq):
            rows = slice(r * seq, (r + 1) * seq)
            for ref, c0, heads in ((kw_ref, KB0, KV_B), (vw_ref, VB0, KV_B), (kn_ref, KC0, H_C), (vn_ref, VC0, H_C)):
                for p in range(heads // 2):
                    pair_t = u[rows, LRU_W + c0 + LANES * p: LRU_W + c0 + LANES * (p + 1)].T
                    ref[r, 0, 2 * p] = pair_t[:HEAD_DIM]
                    ref[r, 0, 2 * p + 1] = pair_t[HEAD_DIM:]


def _inproj_call(x3, mods, mod_row0, gains, w_in_b, layer, rope_tabs=None, cache_seq=None, depth=1, prev_caches=None):
    g, n, d = x3.shape
    d_in = w_in_b.shape[2]
    tb = TOKEN_BLOCK
    rope = rope_tabs is not None
    cache = cache_seq is not None
    in_specs = [pl.BlockSpec((1, tb, d), lambda i, j: (i, j, 0)),
                pl.BlockSpec((1, 1, N_MOD, d), lambda i, j: (layer, mod_row0 + i, 0, 0)),
                pl.BlockSpec((1, 1, d), lambda i, j: (layer, 0, 0)),
                pl.BlockSpec((1, d, d_in), lambda i, j: (layer, 0, 0))]
    args = [x3, mods, gains, w_in_b]
    assert d_in == LRU_W + ATT_W
    out_shape = [jax.ShapeDtypeStruct((g, n, LRU_W), F32),
                 jax.ShapeDtypeStruct((g, n, ATT_W), BF16)]
    out_specs = [pl.BlockSpec((1, tb, LRU_W), lambda i, j: (i, j, 0)),
                 pl.BlockSpec((1, tb, ATT_W), lambda i, j: (i, j, 0))]
    if rope:
        in_specs += [pl.BlockSpec((tb, VB0 - QB0), lambda i, j: (j, 0))] * 2
        args += list(rope_tabs)
    aliases = {}
    if cache:
        assert g == 1 and tb % cache_seq == 0
        rb = tb // cache_seq
        nreq = n // cache_seq
        for heads in (KV_B, KV_B, H_C, H_C):
            out_shape.append(jax.ShapeDtypeStruct((nreq, depth, heads, HEAD_DIM, cache_seq), F32))
            out_specs.append(pl.BlockSpec((rb, 1, heads, HEAD_DIM, cache_seq), lambda i, j: (j, layer, 0, 0, 0)))
        if prev_caches is not None:
            for k, buf in enumerate(prev_caches):
                aliases[len(args)] = 2 + k
                in_specs.append(pl.BlockSpec(memory_space=pl.ANY))
                args.append(buf)
    return pl.pallas_call(
        functools.partial(_inproj_kernel, rope=rope, cache=cache, seq=cache_seq),
        grid=(g, n // tb),
        in_specs=in_specs,
        out_specs=out_specs,
        out_shape=out_shape,
        input_output_aliases=aliases,
        compiler_params=_params(2),
        name="inproj_rope" if rope else "inproj_ctx",
    )(*args)


def _lru_kernel(*refs, n):
    u_ref, cw_ref, cb_ref, wg_ref, bg_ref, lam_ref, h0_ref = refs[:7]
    ya_ref, st_ref, a_s, x_s, y_s, hloc_s, ploc_s, hend_s, pend_s, cin_s = refs[-10:]
    c = D_A
    nblk = n // LRU_SEG
    nh = c // LANES
    u = u_ref[0]
    xa = u[:, :c]
    ga = u[:, c:]
    t = lax.broadcasted_iota(jnp.int32, (n, c), 0)
    cw = cw_ref[0]
    xc = cw[2:3] * xa + cb_ref[0]
    xc = xc + cw[0:1] * jnp.where(t >= 2, pltpu.roll(xa, 2, 0), 0.0)
    xc = xc + cw[1:2] * jnp.where(t >= 1, pltpu.roll(xa, 1, 0), 0.0)
    xc = xc + cw[3:4] * jnp.where(t < n - 1, pltpu.roll(xa, n - 1, 0), 0.0)
    gates = _dot(xc.astype(BF16), wg_ref[0]) + bg_ref[0]
    lam = lam_ref[0]
    log_sig = jnp.minimum(lam, 0.0) - jnp.log1p(jnp.exp(-jnp.abs(lam)))
    for d in range(2):
        r = 0.5 * jnp.tanh(0.5 * gates[:, 2 * d * c:(2 * d + 1) * c]) + 0.5
        i = 0.5 * jnp.tanh(0.5 * gates[:, (2 * d + 1) * c:(2 * d + 2) * c]) + 0.5
        log_a = LRU_C * r * log_sig[d:d + 1]
        a = jnp.exp(log_a)
        t_in = jnp.tanh(-log_a) * (1.0 + a * a)
        xin = jnp.where(t_in > 0.0, t_in * lax.rsqrt(t_in), 0.0) * (i * xc)
        for hf in range(nh):
            a_s[hf] = a[:, LANES * hf: LANES * (hf + 1)]
            x_s[hf] = xin[:, LANES * hf: LANES * (hf + 1)]
        order = range(LRU_SEG) if d == 0 else range(LRU_SEG - 1, -1, -1)
        for hf in range(nh):
            h_run = p_run = None
            for s in order:
                a_row = a_s[hf, pl.ds(s, nblk, stride=LRU_SEG), :]
                x_row = x_s[hf, pl.ds(s, nblk, stride=LRU_SEG), :]
                h_run = x_row if h_run is None else a_row * h_run + x_row
                p_run = a_row if p_run is None else a_row * p_run
                hloc_s[d, hf, s] = h_run
                ploc_s[d, hf, s] = p_run
            hend_s[d, hf] = h_run
            pend_s[d, hf] = p_run

    h0 = h0_ref[0, 0]
    init = tuple(h0[d:d + 1, LANES * hf: LANES * (hf + 1)] for d in range(2) for hf in range(nh))

    def body(k, carry):
        out = []
        for idx, cur in enumerate(carry):
            d, hf = divmod(idx, nh)
            kk = k if d == 0 else nblk - 1 - k
            cin_s[d, hf, pl.ds(kk, 1), :] = cur
            out.append(pend_s[d, hf, pl.ds(kk, 1), :] * cur + hend_s[d, hf, pl.ds(kk, 1), :])
        return tuple(out)

    final = lax.fori_loop(0, nblk, body, init)

    y = None
    for d in range(2):
        for hf in range(nh):
            cin = cin_s[d, hf]
            for s in range(LRU_SEG):
                full_h = hloc_s[d, hf, s] + ploc_s[d, hf, s] * cin
                if d == 0:
                    y_s[hf, pl.ds(s, nblk, stride=LRU_SEG), :] = full_h
                else:
                    y_s[hf, pl.ds(s, nblk, stride=LRU_SEG), :] += full_h
    y = jnp.concatenate([y_s[hf] for hf in range(nh)], axis=1) * jax.nn.gelu(ga)
    ya_ref[0] = y.astype(BF16)
    st_ref[0, 0, 0:1, :] = jnp.concatenate(final[:nh], axis=1)
    st_ref[0, 0, 1:2, :] = jnp.concatenate(final[nh:], axis=1)


def _lru_call(ulru, conv_w, conv_b, w_gates_b, b_gates, lam, layer, h0, h0_layer, st_layer=0, st_depth=1,
              prev_state=None):
    b, n, _ = ulru.shape
    c = D_A
    per_layer = lambda shape: pl.BlockSpec((1,) + shape, lambda i: (layer,) + (0,) * len(shape))
    nh, nblk = c // LANES, n // LRU_SEG
    assert n % LRU_SEG == 0 and nblk % 8 == 0
    scratch = ([pltpu.VMEM((nh, n, LANES), F32)] * 3
               + [pltpu.VMEM((2, nh, LRU_SEG, nblk, LANES), F32)] * 2
               + [pltpu.VMEM((2, nh, nblk, LANES), F32)] * 3)
    in_specs = [pl.BlockSpec((1, n, 2 * c), lambda i: (i, 0, 0)),
                per_layer((CONV_W, c)), per_layer((1, c)), per_layer((c, 4 * c)), per_layer((1, 4 * c)),
                per_layer((2, c)),
                pl.BlockSpec((1, 1, 2, c), lambda i: (i, h0_layer, 0, 0))]
    args = [ulru, conv_w, conv_b, w_gates_b, b_gates, lam, h0]
    aliases = {}
    if prev_state is not None:
        aliases[len(args)] = 1
        in_specs.append(pl.BlockSpec(memory_space=pl.ANY))
        args.append(prev_state)
    return pl.pallas_call(
        functools.partial(_lru_kernel, n=n),
        grid=(b,),
        in_specs=in_specs,
        out_specs=[pl.BlockSpec((1, n, c), lambda i: (i, 0, 0)),
                   pl.BlockSpec((1, 1, 2, c), lambda i: (i, st_layer, 0, 0))],
        out_shape=[jax.ShapeDtypeStruct((b, n, c), BF16),
                   jax.ShapeDtypeStruct((b, st_depth, 2, c), F32)],
        scratch_shapes=scratch,
        input_output_aliases=aliases,
        compiler_params=_params(1),
        name="rglru",
    )(*args)


def _softmax_pv(parts, sink):
    mx = None
    for s, _, _ in parts:
        cur = jnp.max(s, axis=-1, keepdims=True)
        mx = cur if mx is None else jnp.maximum(mx, cur)
    if sink is not None:
        mx = jnp.maximum(mx, sink)
    den = None
    out = None
    for s, v, v_t in parts:
        p = jnp.exp(s - mx)
        cur = jnp.sum(p, axis=-1, keepdims=True)
        den = cur if den is None else den + cur
        o = _dot_nt(p.astype(BF16), v) if v_t else _dot(p.astype(BF16), v)
        out = o if out is None else out + o
    if sink is not None:
        den = den + jnp.exp(sink - mx)
    return out / den


def _swap_halves(x):
    return jnp.concatenate([x[:, HEAD_DIM:], x[:, :HEAD_DIM]], axis=1)


def _keep_half(x, lo_mask, half):
    zero = jnp.zeros_like(x)
    return jnp.where(lo_mask, x, zero) if half == 0 else jnp.where(lo_mask, zero, x)


def _ctx_attn_kernel(sink_ref, att_ref, ob_ref, oc_ref, *, n, rb, sink0):
    lo = lax.broadcasted_iota(jnp.int32, (n, LANES), 1) < HEAD_DIM
    for r in range(rb):
        for p in range(H_C // 2):
            qp = att_ref[r, :, QC0 + LANES * p: QC0 + LANES * (p + 1)]
            kp = att_ref[r, :, KC0 + LANES * p: KC0 + LANES * (p + 1)]
            vp = att_ref[r, :, VC0 + LANES * p: VC0 + LANES * (p + 1)]
            outs = [_softmax_pv([(_dot_nt(qp, _keep_half(kp, lo, half)), vp, False)], None) for half in range(2)]
            oc_ref[r, :, LANES * p: LANES * (p + 1)] = jnp.where(lo, outs[0], outs[1]).astype(BF16)
        kpair = att_ref[r, :, KB0: KB0 + LANES]
        vpair = att_ref[r, :, VB0: VB0 + LANES]
        kpair_sw = _swap_halves(kpair)
        vpair_sw = _swap_halves(vpair)
        for p in range(H_B // 2):
            qp = att_ref[r, :, QB0 + LANES * p: QB0 + LANES * (p + 1)]
            outs = []
            for half in range(2):
                h = 2 * p + half
                aligned = (h // G_B) == half
                ksrc, vsrc = (kpair, vpair) if aligned else (kpair_sw, vpair_sw)
                outs.append(_softmax_pv([(_dot_nt(qp, _keep_half(ksrc, lo, half)), vsrc, False)], sink_ref[sink0 + h]))
            ob_ref[r, :, LANES * p: LANES * (p + 1)] = jnp.where(lo, outs[0], outs[1]).astype(BF16)


def _ctx_attn_call(sinks, att, layer):
    b, n, w = att.shape
    rb = CTX_ATTN_REQS
    assert b % rb == 0 and KV_B == 2
    return pl.pallas_call(
        functools.partial(_ctx_attn_kernel, n=n, rb=rb, sink0=layer * H_B),
        grid=(b // rb,),
        in_specs=[pl.BlockSpec(memory_space=pltpu.SMEM),
                  pl.BlockSpec((rb, n, w), lambda i: (i, 0, 0))],
        out_specs=[pl.BlockSpec((rb, n, D_B), lambda i: (i, 0, 0)), pl.BlockSpec((rb, n, D_C), lambda i: (i, 0, 0))],
        out_shape=[jax.ShapeDtypeStruct((b, n, D_B), BF16), jax.ShapeDtypeStruct((b, n, D_C), BF16)],
        compiler_params=_params(1),
        name="ctx_attn",
    )(sinks, att)


def _win_attn_kernel(sink_ref, qkv_ref, kc_ref, vc_ref, o_ref, *, n, sink0):
    j = pl.program_id(1)
    nq = WIN_Q_BLOCKS * WIN_BLK
    nloc = nq + 2 * WIN_BLK
    q0 = pl.multiple_of(j * nq, WIN_BLK)
    ks = pl.multiple_of(jnp.clip(j * nq - WIN_BLK, 0, n - nloc), WIN_BLK)
    row = lax.broadcasted_iota(jnp.int32, (nq, nloc), 0)
    col = lax.broadcasted_iota(jnp.int32, (nq, nloc), 1)
    in_window = jnp.where(jnp.abs((q0 + row) - (ks + col)) <= WINDOW, 0.0, NEG_INF)
    band = jnp.concatenate([in_window] * G_B, axis=0)
    rowh = lax.broadcasted_iota(jnp.int32, (G_B * nq, 1), 0) // nq
    for kv in range(KV_B):
        gw = G_B * HEAD_DIM
        qblk = qkv_ref[0, pl.ds(q0, nq), QB0 + gw * kv: QB0 + gw * (kv + 1)]
        q3 = jnp.concatenate([qblk[:, 64 * g: 64 * (g + 1)] for g in range(G_B)], axis=0)
        kl = qkv_ref[0, pl.ds(ks, nloc), KB0 + 64 * kv: KB0 + 64 * (kv + 1)]
        vl = qkv_ref[0, pl.ds(ks, nloc), VB0 + 64 * kv: VB0 + 64 * (kv + 1)]
        kc_t = kc_ref[0, 0, kv].astype(BF16)
        vc_t = vc_ref[0, 0, kv].astype(BF16)
        s_loc = _dot_nt(q3, kl) + band
        s_ctx = _dot(q3, kc_t)
        s0 = sink0 + G_B * kv
        sk = jnp.where(rowh == 0, sink_ref[s0], jnp.where(rowh == 1, sink_ref[s0 + 1], sink_ref[s0 + 2]))
        o = _softmax_pv([(s_loc, vl, False), (s_ctx, vc_t, True)], sk)
        for g in range(G_B):
            h = kv * G_B + g
            o_ref[0, :, 64 * h: 64 * (h + 1)] = o[g * nq:(g + 1) * nq].astype(BF16)


def _win_attn_call(sinks, qkv, cache_k, cache_v, layer):
    b, n, w = qkv.shape
    past = cache_k.shape[4]
    cache_spec = pl.BlockSpec((1, 1, KV_B, HEAD_DIM, past), lambda i, j: (i, layer, 0, 0, 0))
    nq = WIN_Q_BLOCKS * WIN_BLK
    assert n % nq == 0 and n >= nq + 2 * WIN_BLK and WINDOW == WIN_BLK
    return pl.pallas_call(
        functools.partial(_win_attn_kernel, n=n, sink0=layer * H_B),
        grid=(b, n // nq),
        in_specs=[pl.BlockSpec(memory_space=pltpu.SMEM),
                  pl.BlockSpec((1, n, w), lambda i, j: (i, 0, 0)),
                  cache_spec, cache_spec],
        out_specs=pl.BlockSpec((1, nq, D_B), lambda i, j: (i, j, 0)),
        out_shape=jax.ShapeDtypeStruct((b, n, D_B), BF16),
        compiler_params=_params(2),
        name="win_attn",
    )(sinks, qkv, cache_k, cache_v)


def _nbr_window_start(g, rows):
    return jnp.clip(g * NBR_Q_ROWS - NA_ROWS // 2, 0, rows - NBR_K_ROWS)


def _nbr_attn_kernel(q_ref, k_ref, v_ref, rel_ref, kc_ref, vc_ref, o_ref, bias_s, *, n):
    g = pl.program_id(0)
    nq = NBR_Q_ROWS * GRID_W
    nk = NBR_K_ROWS * GRID_W
    rows = n // GRID_W
    ws = _nbr_window_start(g, rows)
    k0 = pl.multiple_of(ws * GRID_W, GRID_W)

    @pl.when(pl.program_id(1) == 0)
    def _():
        for i in range(NBR_Q_ROWS):
            qr = g * NBR_Q_ROWS + i
            rs = jnp.clip(qr - NA_ROWS // 2, 0, rows - NA_ROWS)
            for jp in range(NBR_K_ROWS // 2):
                tiles = []
                for j in (2 * jp, 2 * jp + 1):
                    kr = ws + j
                    valid = (kr >= rs) & (kr < rs + NA_ROWS)
                    d = jnp.clip(kr - qr + NA_ROWS - 1, 0, 2 * NA_ROWS - 2)
                    tiles.append(jnp.where(valid, rel_ref[0, :, pl.ds(d, 1)][:, 0], NEG_INF))
                bias_s[:, GRID_W * i: GRID_W * (i + 1), LANES * jp: LANES * (jp + 1)] = jnp.concatenate(tiles, axis=-1)
    lo_k = lax.broadcasted_iota(jnp.int32, (nk, LANES), 1) < HEAD_DIM
    lo_q = lax.broadcasted_iota(jnp.int32, (nq, LANES), 1) < HEAD_DIM
    zpad = jnp.zeros((HEAD_DIM, kc_ref.shape[4]), BF16)
    for p in range(H_C // 2):
        qp = q_ref[0, :, LANES * p: LANES * (p + 1)]
        kp = k_ref[0, pl.ds(k0, nk), LANES * p: LANES * (p + 1)]
        vp = v_ref[0, pl.ds(k0, nk), LANES * p: LANES * (p + 1)]
        outs = []
        for half in range(2):
            h = 2 * p + half
            kc_t = kc_ref[0, 0, h].astype(BF16)
            vc_t = vc_ref[0, 0, h].astype(BF16)
            kc_t = jnp.concatenate([kc_t, zpad] if half == 0 else [zpad, kc_t], axis=0)
            vc_t = jnp.concatenate([vc_t, zpad] if half == 0 else [zpad, vc_t], axis=0)
            s_loc = _dot_nt(qp, _keep_half(kp, lo_k, half)) + bias_s[h]
            s_ctx = _dot(qp, kc_t)
            outs.append(_softmax_pv([(s_loc, vp, False), (s_ctx, vc_t, True)], None))
        o_ref[0, :, LANES * p: LANES * (p + 1)] = jnp.where(lo_q, outs[0], outs[1]).astype(BF16)


def _nbr_attn_call(att, rel_tab, cache_k, cache_v, layer):
    b, n, w = att.shape
    past = cache_k.shape[4]
    nq = NBR_Q_ROWS * GRID_W
    nk = NBR_K_ROWS * GRID_W
    gw = H_C * HEAD_DIM
    rows = n // GRID_W
    assert (QC0, KC0, VC0) == (0, gw, 2 * gw) and n % nq == 0 and rows >= NBR_K_ROWS and NBR_K_ROWS % 2 == 0
    cache_spec = pl.BlockSpec((1, 1, H_C, HEAD_DIM, past), lambda g, i: (i, layer, 0, 0, 0))
    return pl.pallas_call(
        functools.partial(_nbr_attn_kernel, n=n),
        grid=(n // nq, b),
        in_specs=[pl.BlockSpec((1, nq, gw), lambda g, i: (i, g, 0)),
                  pl.BlockSpec((1, n, gw), lambda g, i: (i, 0, 1)),
                  pl.BlockSpec((1, n, gw), lambda g, i: (i, 0, 2)),
                  pl.BlockSpec((1,) + rel_tab.shape[1:], lambda g, i: (layer, 0, 0, 0, 0)),
                  cache_spec, cache_spec],
        out_specs=pl.BlockSpec((1, nq, gw), lambda g, i: (i, g, 0)),
        out_shape=jax.ShapeDtypeStruct((b, n, gw), BF16),
        scratch_shapes=[pltpu.VMEM((H_C, nq, nk), F32)],
        compiler_params=_params(2),
        name="nbr_attn",
    )(att, att, att, rel_tab, cache_k, cache_v)


def _outproj_kernel(x_ref, ya_ref, yb_ref, yc_ref, mod_ref, g_ref, wo_ref, wr_ref,
                    xn_ref, h2_ref, aff_ref):
    mix = jnp.concatenate([ya_ref[0], yb_ref[0], yc_ref[0]], axis=1)
    proj = _dot(mix, wo_ref[0])
    xn = x_ref[0] + mod_ref[0, 0, 2:3, :] * proj
    xn_ref[0] = xn
    h2 = _rms_modulate(xn, g_ref[0], mod_ref[0, 0, 3:4, :], mod_ref[0, 0, 4:5, :]).astype(BF16)
    h2_ref[0] = h2
    logits = _dot(h2, wr_ref[0])
    lane = lax.broadcasted_iota(jnp.int32, logits.shape, 1)
    valid = lane < N_EXPERTS
    logits = jnp.where(valid, logits, NEG_INF)
    e = jnp.exp(logits - jnp.max(logits, axis=-1, keepdims=True))
    e = jnp.where(valid, e, 0.0)
    aff_ref[0] = e / jnp.sum(e, axis=-1, keepdims=True)


def _outproj_call(x3, ya, yb, yc, mods, mod_row0, gains, wo_b16, w_router_b, layer):
    g, n, d = x3.shape
    tb = TOKEN_BLOCK
    tok = lambda w: pl.BlockSpec((1, tb, w), lambda i, j: (i, j, 0))
    per_layer = lambda shape: pl.BlockSpec((1,) + shape, lambda i, j: (layer,) + (0,) * len(shape))
    return pl.pallas_call(
        _outproj_kernel,
        grid=(g, n // tb),
        in_specs=[tok(d), tok(D_A), tok(D_B), tok(D_C),
                  pl.BlockSpec((1, 1, N_MOD, d), lambda i, j: (layer, mod_row0 + i, 0, 0)),
                  per_layer((1, d)), per_layer((D_A + D_B + D_C, d)), per_layer((d, LANES))],
        out_specs=[tok(d), tok(d), tok(LANES)],
        out_shape=[jax.ShapeDtypeStruct((g, n, d), F32),
                   jax.ShapeDtypeStruct((g, n, d), BF16),
                   jax.ShapeDtypeStruct((g, n, LANES), F32)],
        compiler_params=_params(2),
        name="outproj_router",
    )(x3, ya, yb, yc, mods, gains, wo_b16, w_router_b)


def _sort_by_affinity(keys, idx, pos, levels, seg_lanes):
    rows = keys.shape[0]
    sign = [jnp.where(((pos >> m) & 1) == 0, 1.0, -1.0) for m in range(levels)]

    def partner(x, j):
        d = 1 << j
        low = sign[j] > 0.0
        if d >= rows:
            sh = (d // rows) * seg_lanes
            return jnp.where(low, pltpu.roll(x, LANES - sh, 1), pltpu.roll(x, sh, 1))
        if d >= 8:
            x4 = x.reshape(rows // (2 * d), 2, d, LANES)
            return jnp.concatenate([x4[:, 1:2], x4[:, 0:1]], axis=1).reshape(rows, LANES)
        x3 = x.reshape(rows // 8, 8, LANES)
        low3 = sign[j].reshape(rows // 8, 8, LANES) > 0.0
        return jnp.where(low3, pltpu.roll(x3, 8 - d, 1), pltpu.roll(x3, d, 1)).reshape(rows, LANES)

    for k in range(1, levels + 1):
        for j in range(k - 1, -1, -1):
            pk, pi = partner(keys, j), partner(idx, j)
            before = jnp.where(keys == pk, jnp.where(idx < pi, 1.0, -1.0), jnp.where(keys > pk, 1.0, -1.0))
            want_first = sign[j] * sign[k] if k < levels else sign[j]
            keep = before * want_first > 0.0
            keys = jnp.where(keep, keys, pk)
            idx = jnp.where(keep, idx, pi)
    return keys, idx


def _route_kernel(aff_ref, h2_ref, xs_ref, vals_ref, rank_ref, *, n, cap, rb, by_slot):
    ne = N_EXPERTS
    rows = ROUTE_SORT_ROWS
    nseg = n // rows
    seg_lanes = rb * ne
    levels = n.bit_length() - 1
    fcap = float(cap)
    row = lax.broadcasted_iota(jnp.int32, (rows, LANES), 0)
    lane = lax.broadcasted_iota(jnp.int32, (rows, LANES), 1)
    pos = (lane // seg_lanes) * rows + row
    keys = jnp.full((rows, LANES), -1.0, F32)
    for r in range(rb):
        for seg in range(nseg):
            off = seg * seg_lanes + r * ne
            piece = aff_ref[r, seg * rows:(seg + 1) * rows, :]
            if off:
                piece = pltpu.roll(piece, off, 1)
            keys = jnp.where((lane >= off) & (lane < off + ne), piece, keys)
    keys, idx = _sort_by_affinity(keys, pos.astype(F32), pos, levels, seg_lanes)
    top_keys = keys[:cap]
    top_idx = idx[:cap]

    tok = lax.broadcasted_iota(jnp.int32, (cap, n), 1).astype(F32)
    weight = fcap - lax.broadcasted_iota(jnp.int32, (cap, n), 0).astype(F32)
    pad = jnp.full((LANES - ne, n), fcap, F32)
    if by_slot:
        top_idx_t = top_idx.T
    for r in range(rb):
        picks, rank_rows = [], []
        for e in range(ne):
            col = r * ne + e
            onehot = jnp.where(top_idx[:, col:col + 1] == tok, 1.0, 0.0)
            vals_ref[e, r] = top_keys[:, col:col + 1]
            if not by_slot:
                rank_rows.append(fcap - jnp.sum(onehot * weight, axis=0, keepdims=True))
            picks.append(onehot.astype(BF16))
        if by_slot:
            rank_ref[r] = top_idx_t[r * ne:(r + 1) * ne]
        else:
            rank_ref[r] = jnp.concatenate(rank_rows + [pad], axis=0).T
        xs = _dot(jnp.concatenate(picks, axis=0), h2_ref[r])
        xs_ref[:, r] = xs.reshape(ne, cap, xs.shape[-1]).astype(BF16)


def _route_call(aff, h2, rb):
    b, n, d = h2.shape
    cap = max(1, EC_FACTOR * n // N_EXPERTS)
    nseg = n // ROUTE_SORT_ROWS
    assert b % rb == 0 and n == nseg * ROUTE_SORT_ROWS and n & (n - 1) == 0
    assert nseg * rb * N_EXPERTS <= LANES and cap <= ROUTE_SORT_ROWS
    by_slot = cap == LANES
    assign_shape = (N_EXPERTS, cap) if by_slot else (n, LANES)
    return pl.pallas_call(
        functools.partial(_route_kernel, n=n, cap=cap, rb=rb, by_slot=by_slot),
        grid=(b // rb,),
        in_specs=[pl.BlockSpec((rb, n, LANES), lambda i: (i, 0, 0)),
                  pl.BlockSpec((rb, n, d), lambda i: (i, 0, 0))],
        out_specs=[pl.BlockSpec((N_EXPERTS, rb, cap, d), lambda i: (0, i, 0, 0)),
                   pl.BlockSpec((N_EXPERTS, rb, cap, 1), lambda i: (0, i, 0, 0)),
                   pl.BlockSpec((rb,) + assign_shape, lambda i: (i, 0, 0))],
        out_shape=[jax.ShapeDtypeStruct((N_EXPERTS, b, cap, d), BF16),
                   jax.ShapeDtypeStruct((N_EXPERTS, b, cap, 1), F32),
                   jax.ShapeDtypeStruct((b,) + assign_shape, F32)],
        compiler_params=_params(1),
        name="route_gather",
    )(aff, h2)


def _expert_kernel(xc_ref, xl_ref, vc_ref, vl_ref, wg_ref, wu_ref, wd_ref, yc_ref, yl_ref, acc_c, acc_l):
    f = pl.program_id(1)

    def step(first):
        wg = wg_ref[0, 0].astype(BF16)
        wu = wu_ref[0, 0].astype(BF16)
        wd = wd_ref[0, 0].astype(BF16)
        for x_ref, v_ref, y_ref, acc in ((xc_ref, vc_ref, yc_ref, acc_c), (xl_ref, vl_ref, yl_ref, acc_l)):
            rows = x_ref.shape[1]
            chunk = min(EXPERT_ROW_CHUNK, rows)
            for c0 in range(0, rows, chunk):
                sl = slice(c0, c0 + chunk)
                x = x_ref[0, sl, :]
                a = _dot(x, wg)
                u = _dot(x, wu)
                act = ((a * jax.nn.sigmoid(a)) * u).astype(BF16)
                part = _dot(act, wd)
                if first:
                    acc[sl, :] = part
                else:
                    y_ref[0, sl, :] = ((acc[sl, :] + part) * v_ref[0, sl, :]).astype(BF16)

    pl.when(f == 0)(functools.partial(step, True))
    pl.when(f == EXPERT_F_STEPS - 1)(functools.partial(step, False))


def _expert_call(xs_c, xs_l, vals_c, vals_l, w_gate, w_up, w_down, layer):
    e, rc, d = xs_c.shape
    rl = xs_l.shape[1]
    f_total = w_gate.shape[-1]
    assert EXPERT_F_STEPS == 2
    fb = f_total // EXPERT_F_STEPS
    per_e = lambda rows, w: pl.BlockSpec((1, rows, w), lambda i, j: (i, 0, 0))
    return pl.pallas_call(
        _expert_kernel,
        grid=(e, EXPERT_F_STEPS),
        in_specs=[per_e(rc, d), per_e(rl, d), per_e(rc, 1), per_e(rl, 1),
                  pl.BlockSpec((1, 1, d, fb), lambda i, j: (layer, i, 0, j)),
                  pl.BlockSpec((1, 1, d, fb), lambda i, j: (layer, i, 0, j)),
                  pl.BlockSpec((1, 1, fb, d), lambda i, j: (layer, i, j, 0))],
        out_specs=[per_e(rc, d), per_e(rl, d)],
        out_shape=[jax.ShapeDtypeStruct((e, rc, d), BF16),
                   jax.ShapeDtypeStruct((e, rl, d), BF16)],
        scratch_shapes=[pltpu.VMEM((rc, d), F32), pltpu.VMEM((rl, d), F32)],
        compiler_params=_params(2),
        name="experts",
    )(xs_c, xs_l, vals_c, vals_l, w_gate, w_up, w_down)


def _combine_kernel(y_ref, rank_ref, xn_ref, mod_ref, gf_ref, o_ref, *, n, cap, final, rb, by_slot):
    ec = N_EXPERTS * cap
    d = xn_ref.shape[-1]
    if by_slot:
        tok = lax.broadcasted_iota(jnp.int32, (n, cap), 0).astype(F32)
    else:
        ce = lax.broadcasted_iota(jnp.int32, (LANES, ec), 1) // cap
        ee = lax.broadcasted_iota(jnp.int32, (LANES, ec), 0)
        expand = jnp.where(ce == ee, 1.0, 0.0).astype(BF16)
        slot = (lax.broadcasted_iota(jnp.int32, (n, ec), 1) % cap).astype(F32)
    for r in range(rb):
        if by_slot:
            chosen = rank_ref[r]
            onehot = jnp.concatenate([jnp.where(tok == chosen[e:e + 1, :], 1.0, 0.0).astype(BF16)
                                      for e in range(N_EXPERTS)], axis=1)
        else:
            rank = rank_ref[r].astype(BF16)
            rexp = _dot(rank, expand)
            onehot = jnp.where(rexp == slot, 1.0, 0.0).astype(BF16)
        moe = _dot(onehot, y_ref[:, r].reshape(ec, d))
        x = xn_ref[r] + mod_ref[0, 0, 5:6, :] * moe
        if final:
            x = (x * lax.rsqrt(jnp.mean(x * x, axis=-1, keepdims=True) + EPS)) * gf_ref[...]
        o_ref[r] = x


def _combine_call(y4, rank, xn, mods, mod_row0, shared_mod, layer, g_final, final):
    e, b, cap, d = y4.shape
    n = xn.shape[1]
    rb = COMBINE_REQS_CTX if shared_mod else 1
    by_slot = rank.shape[1:] == (e, cap) and cap == LANES
    assert b % rb == 0 and (by_slot or rank.shape[1:] == (n, LANES))
    return pl.pallas_call(
        functools.partial(_combine_kernel, n=n, cap=cap, final=final, rb=rb, by_slot=by_slot),
        grid=(b // rb,),
        in_specs=[pl.BlockSpec((e, rb, cap, d), lambda i: (0, i, 0, 0)),
                  pl.BlockSpec((rb,) + rank.shape[1:], lambda i: (i, 0, 0)),
                  pl.BlockSpec((rb, n, d), lambda i: (i, 0, 0)),
                  pl.BlockSpec((1, 1, N_MOD, d), (lambda i: (layer, mod_row0, 0, 0)) if shared_mod
                               else (lambda i: (layer, mod_row0 + i, 0, 0))),
                  pl.BlockSpec((1, d), lambda i: (0, 0))],
        out_specs=pl.BlockSpec((rb, n, d), lambda i: (i, 0, 0)),
        out_shape=jax.ShapeDtypeStruct((b, n, d), F32),
        compiler_params=_params(1),
        name="combine",
    )(y4, rank, xn, mods, g_final)


def _rope_tables(n):
    t = jnp.arange(n, dtype=jnp.int32)
    row = (t // GRID_W).astype(F32)
    col = (t % GRID_W).astype(F32)
    half = HEAD_DIM // 4
    inv = jnp.power(ROPE_BASE, -jnp.arange(half, dtype=F32) / half)
    ang_r = row[:, None] * inv[None, :]
    ang_c = col[:, None] * inv[None, :]
    cos_h = jnp.concatenate([jnp.cos(ang_r)] * 2 + [jnp.cos(ang_c)] * 2, axis=-1)
    sin_h = jnp.concatenate([-jnp.sin(ang_r), jnp.sin(ang_r), -jnp.sin(ang_c), jnp.sin(ang_c)], axis=-1)
    reps = H_B + KV_B
    return jnp.tile(cos_h, (1, reps)), jnp.tile(sin_h, (1, reps))


def _nbr_rel_tables(rpb):
    col = np.arange(GRID_W)
    cs = np.clip(col - NA_COLS // 2, 0, GRID_W - NA_COLS)
    col_mask = (col[None, :] >= cs[:, None]) & (col[None, :] < cs[:, None] + NA_COLS)
    dc_idx = np.clip(col[None, :] - col[:, None], -(NA_COLS - 1), NA_COLS - 1) + (NA_COLS - 1)
    sel_col = (dc_idx[None, :, :] == np.arange(2 * NA_COLS - 1)[:, None, None]).astype(np.float32)
    t = jnp.einsum("lhdc,cqk->lhdqk", rpb.astype(F32), sel_col, precision=lax.Precision.HIGHEST)
    return jnp.where(col_mask, t, NEG_INF)


def _block_diag(w):
    nb, bw = w.shape[-3], w.shape[-1]
    eye = jnp.eye(nb, dtype=w.dtype)
    return (eye[:, None, :, None] * w[..., :, :, None, :]).reshape(w.shape[:-3] + (nb * bw, nb * bw))


def kernel(x_prompt, x_sample, state_lru, cache_k_win, cache_v_win, cache_k_nbr, cache_v_nbr, c, c_ctx, w_mod, b_mod, g_norm1, w_in, conv_w, conv_b, w_gate_r, b_gate_r, w_gate_i, b_gate_i, lru_lambda, sink_logit, nbr_bias, w_out, g_norm2, w_router, w_exp_gate, w_exp_up, w_exp_down, g_final):
    bc, seq, d = x_prompt.shape
    bl, n_lat, _ = x_sample.shape
    depth = w_mod.shape[0]
    assert bl + 1 <= 8 and d == 1024

    cond8 = jnp.zeros((8, d), F32).at[0].set(c_ctx).at[1:1 + bl].set(c)
    mods = _adaln_call(cond8, w_mod, b_mod.reshape(depth, 1, -1)).reshape(depth, 8, N_MOD, d)
    rope_tabs = _rope_tables(n_lat)
    bias_tab = _nbr_rel_tables(nbr_bias)
    ckw_t, cvw_t, ckn_t, cvn_t = (jnp.swapaxes(a, -1, -2) for a in (cache_k_win, cache_v_win, cache_k_nbr, cache_v_nbr))
    gf = g_final.reshape(1, d)
    assert SCALE == 0.125
    new_cols = np.arange(w_in.shape[-1]) - LRU_W
    is_q = ((new_cols >= QC0) & (new_cols < KC0)) | ((new_cols >= QB0) & (new_cols < KB0))
    q_scale = jnp.asarray(np.where(is_q, SCALE, 1.0), F32)

    nbr0 = LRU_W + D_B + 2 * KV_W
    w_in_b = (jnp.concatenate([w_in[..., :LRU_W], w_in[..., nbr0:], w_in[..., LRU_W:nbr0]], axis=-1) * q_scale).astype(BF16)
    wo = w_out.astype(BF16)
    wr_b = jnp.pad(w_router.astype(BF16), ((0, 0), (0, 0), (0, LANES - N_EXPERTS)))
    gate_r, gate_i = _block_diag(w_gate_r), _block_diag(w_gate_i)
    w_gates = jnp.concatenate([gate_r[:, 0], gate_i[:, 0], gate_r[:, 1], gate_i[:, 1]], axis=-1).astype(BF16)
    b_gates = jnp.concatenate([b_gate_r[:, 0], b_gate_i[:, 0], b_gate_r[:, 1], b_gate_i[:, 1]], axis=-1)[:, None, :]
    cb = conv_b[:, None, :]
    g1 = g_norm1[:, None, :]
    g2 = g_norm2[:, None, :]
    sinks = sink_logit.reshape(-1)
    ctx_row, lat_row = 0, 1

    xc = x_prompt.reshape(1, bc * seq, d)
    xl = x_sample
    zeros_state = jnp.zeros((bc, 1, 2, D_A), F32)
    caches = None
    st = None
    for l in range(depth):
        final = l == depth - 1
        ulru_c, qkv_c, *caches = _inproj_call(xc, mods, ctx_row, g1, w_in_b, l, cache_seq=seq, depth=depth,
                                              prev_caches=caches)
        ya_c, st = _lru_call(ulru_c.reshape(bc, seq, LRU_W), conv_w, cb, w_gates, b_gates, lru_lambda, l,
                             zeros_state, 0, st_layer=l, st_depth=depth, prev_state=st)
        yb_c, yc_c = _ctx_attn_call(sinks, qkv_c.reshape(bc, seq, -1), l)
        xn_c, h2_c, aff_c = _outproj_call(xc, ya_c.reshape(1, bc * seq, -1), yb_c.reshape(1, bc * seq, -1),
                                          yc_c.reshape(1, bc * seq, -1), mods, ctx_row, g2, wo, wr_b, l)
        xs_c, vals_c, rank_c = _route_call(aff_c.reshape(bc, seq, LANES), h2_c.reshape(bc, seq, d), ROUTE_REQS_CTX)

        ulru_l, qkv_l = _inproj_call(xl, mods, lat_row, g1, w_in_b, l, rope_tabs=rope_tabs)
        ya_l, _ = _lru_call(ulru_l, conv_w, cb, w_gates, b_gates, lru_lambda, l, state_lru, l)
        yb_l = _win_attn_call(sinks, qkv_l, ckw_t, cvw_t, l)
        yc_l = _nbr_attn_call(qkv_l, bias_tab, ckn_t, cvn_t, l)
        xn_l, h2_l, aff_l = _outproj_call(xl, ya_l, yb_l, yc_l, mods, lat_row, g2, wo, wr_b, l)
        xs_l, vals_l, rank_l = _route_call(aff_l, h2_l, ROUTE_REQS_LAT)

        cap_c, cap_l = xs_c.shape[2], xs_l.shape[2]
        y_c, y_l = _expert_call(xs_c.reshape(N_EXPERTS, bc * cap_c, d), xs_l.reshape(N_EXPERTS, bl * cap_l, d),
                                vals_c.reshape(N_EXPERTS, bc * cap_c, 1), vals_l.reshape(N_EXPERTS, bl * cap_l, 1),
                                w_exp_gate, w_exp_up, w_exp_down, l)
        xc = _combine_call(y_c.reshape(N_EXPERTS, bc, cap_c, d), rank_c, xn_c.reshape(bc, seq, d), mods, ctx_row, True,
                           l, gf, final)
        xl = _combine_call(y_l.reshape(N_EXPERTS, bl, cap_l, d), rank_l, xn_l, mods, lat_row, False, l, gf, final)
        xc = xc.reshape(1, bc * seq, d)

    y_prompt = xc.reshape(bc, seq, d)
    y_sample = xl
    return (y_prompt, y_sample, st, *[jnp.swapaxes(buf, -1, -2) for buf in caches])
```

```python
import functools

import numpy as np
import jax
import jax.numpy as jnp
from jax import lax
from jax.experimental import pallas as pl
from jax.experimental.pallas import tpu as pltpu

F32 = jnp.float32
BF16 = jnp.bfloat16

HEAD_DIM = 64
GRID_W = 64
LRU_C = 8.0
CONV_W = 4
KV_B = 2
G_B = 3
H_B = KV_B * G_B
H_C = 6
WINDOW = 128
WIN_BLK = 128
NA_ROWS = 8
NA_COLS = 16
ROPE_BASE = 10000.0
N_EXPERTS = 16
EC_FACTOR = 2
N_MOD = 6
EPS = 1e-6
NEG_INF = -1e30
SCALE = HEAD_DIM ** -0.5

LANES = 128
VMEM_LIMIT_BYTES = 56 * 1024 * 1024

TOKEN_BLOCK = 1024
MOD_COL_BLOCK = 1536
EXPERT_F_STEPS = 2
EXPERT_ROW_CHUNK = 256
D_A = 256
D_B = H_B * HEAD_DIM
D_C = H_C * HEAD_DIM
KV_W = KV_B * HEAD_DIM
LRU_W = 2 * D_A
QC0, KC0, VC0 = 0, D_C, 2 * D_C
QB0 = 3 * D_C
KB0, VB0, ATT_W = QB0 + D_B, QB0 + D_B + KV_W, QB0 + D_B + 2 * KV_W
CTX_ATTN_REQS = 4
WIN_Q_BLOCKS = 2
NBR_Q_ROWS = 4
NBR_K_ROWS = 12
ROUTE_REQS_CTX = 8
ROUTE_REQS_LAT = 2
ROUTE_SORT_ROWS = 256
COMBINE_REQS_CTX = 4
LRU_SEG = 8


def _params(n_axes):
    return pltpu.CompilerParams(dimension_semantics=("arbitrary",) * n_axes,
                                vmem_limit_bytes=VMEM_LIMIT_BYTES)


def _dot(a, b):
    return jnp.dot(a, b, preferred_element_type=F32)


def _dot_nt(a, b):
    return lax.dot_general(a, b, (((1,), (1,)), ((), ())), preferred_element_type=F32)


def _adaln_kernel(cond_ref, w_ref, b_ref, o_ref):
    c = cond_ref[...]
    s = c * jax.nn.sigmoid(c)
    o_ref[0] = _dot(s.astype(BF16), w_ref[0].astype(BF16)) + b_ref[0]


def _adaln_call(cond8, w_mod, b_mod3):
    depth, d, n = w_mod.shape
    nb = MOD_COL_BLOCK
    return pl.pallas_call(
        _adaln_kernel,
        grid=(depth, n // nb),
        in_specs=[pl.BlockSpec((8, d), lambda l, j: (0, 0)),
                  pl.BlockSpec((1, d, nb), lambda l, j: (l, 0, j)),
                  pl.BlockSpec((1, 1, nb), lambda l, j: (l, 0, j))],
        out_specs=pl.BlockSpec((1, 8, nb), lambda l, j: (l, 0, j)),
        out_shape=jax.ShapeDtypeStruct((depth, 8, n), F32),
        compiler_params=_params(2),
        name="adaln",
    )(cond8, w_mod, b_mod3)


def _rms_modulate(x, gain, shift, scale):
    y = x * lax.rsqrt(jnp.mean(x * x, axis=-1, keepdims=True) + EPS)
    return (y * gain) * (1.0 + scale) + shift


def _inproj_kernel(*refs, rope, cache, seq):
    if rope:
        x_ref, mod_ref, g_ref, w_ref, cos_ref, sin_ref, ulru_ref, qkv_ref = refs
    elif cache:
        x_ref, mod_ref, g_ref, w_ref = refs[:4]
        ulru_ref, qkv_ref, kw_ref, vw_ref, kn_ref, vn_ref = refs[-6:]
    else:
        x_ref, mod_ref, g_ref, w_ref, ulru_ref, qkv_ref = refs
    x = x_ref[0]
    h = _rms_modulate(x, g_ref[0], mod_ref[0, 0, 0:1, :], mod_ref[0, 0, 1:2, :])
    u = _dot(h.astype(BF16), w_ref[0])
    tb = u.shape[0]
    ulru_ref[0] = u[:, :LRU_W]
    if rope:
        lo, hi = LRU_W + QB0, LRU_W + VB0
        qk = u[:, lo:hi]
        lane = lax.broadcasted_iota(jnp.int32, (tb, hi - lo), 1)
        first = (lane & 31) < 16
        partner = jnp.where(first, pltpu.roll(qk, hi - lo - 16, 1), pltpu.roll(qk, 16, 1))
        qk = qk * cos_ref[...] + partner * sin_ref[...]
        qkv_ref[0, :, :QB0] = u[:, LRU_W:lo].astype(BF16)
        qkv_ref[0, :, QB0:VB0] = qk.astype(BF16)
        qkv_ref[0, :, VB0:] = u[:, hi:].astype(BF16)
    else:
        qkv_ref[0] = u[:, LRU_W:].astype(BF16)
    if cache:
        for r in range(tb // seq):
            rows = slice(r * seq, (r + 1) * seq)
            for ref, c0, heads in ((kw_ref, KB0, KV_B), (vw_ref, VB0, KV_B), (kn_ref, KC0, H_C), (vn_ref, VC0, H_C)):
                for p in range(heads // 2):
                    pair_t = u[rows, LRU_W + c0 + LANES * p: LRU_W + c0 + LANES * (p + 1)].T
                    ref[r, 0, 2 * p] = pair_t[:HEAD_DIM]
                    ref[r, 0, 2 * p + 1] = pair_t[HEAD_DIM:]


def _inproj_call(x3, mods, mod_row0, gains, w_in_b, layer, rope_tabs=None, cache_seq=None, depth=1, prev_caches=None):
    g, n, d = x3.shape
    d_in = w_in_b.shape[2]
    tb = TOKEN_BLOCK
    rope = rope_tabs is not None
    cache = cache_seq is not None
    in_specs = [pl.BlockSpec((1, tb, d), lambda i, j: (i, j, 0)),
                pl.BlockSpec((1, 1, N_MOD, d), lambda i, j: (layer, mod_row0 + i, 0, 0)),
                pl.BlockSpec((1, 1, d), lambda i, j: (layer, 0, 0)),
                pl.BlockSpec((1, d, d_in), lambda i, j: (layer, 0, 0))]
    args = [x3, mods, gains, w_in_b]
    assert d_in == LRU_W + ATT_W
    out_shape = [jax.ShapeDtypeStruct((g, n, LRU_W), F32),
                 jax.ShapeDtypeStruct((g, n, ATT_W), BF16)]
    out_specs = [pl.BlockSpec((1, tb, LRU_W), lambda i, j: (i, j, 0)),
                 pl.BlockSpec((1, tb, ATT_W), lambda i, j: (i, j, 0))]
    if rope:
        in_specs += [pl.BlockSpec((tb, VB0 - QB0), lambda i, j: (j, 0))] * 2
        args += list(rope_tabs)
    aliases = {}
    if cache:
        assert g == 1 and tb % cache_seq == 0
        rb = tb // cache_seq
        nreq = n // cache_seq
        for heads in (KV_B, KV_B, H_C, H_C):
            out_shape.append(jax.ShapeDtypeStruct((nreq, depth, heads, HEAD_DIM, cache_seq), F32))
            out_specs.append(pl.BlockSpec((rb, 1, heads, HEAD_DIM, cache_seq), lambda i, j: (j, layer, 0, 0, 0)))
        if prev_caches is not None:
            for k, buf in enumerate(prev_caches):
                aliases[len(args)] = 2 + k
                in_specs.append(pl.BlockSpec(memory_space=pl.ANY))
                args.append(buf)
    return pl.pallas_call(
        functools.partial(_inproj_kernel, rope=rope, cache=cache, seq=cache_seq),
        grid=(g, n // tb),
        in_specs=in_specs,
        out_specs=out_specs,
        out_shape=out_shape,
        input_output_aliases=aliases,
        compiler_params=_params(2),
        name="inproj_rope" if rope else "inproj_ctx",
    )(*args)


def _lru_kernel(*refs, n):
    u_ref, cw_ref, cb_ref, wg_ref, bg_ref, lam_ref, h0_ref = refs[:7]
    ya_ref, st_ref, a_s, x_s, y_s, hloc_s, ploc_s, hend_s, pend_s, cin_s = refs[-10:]
    c = D_A
    nblk = n // LRU_SEG
    nh = c // LANES
    u = u_ref[0]
    xa = u[:, :c]
    ga = u[:, c:]
    t = lax.broadcasted_iota(jnp.int32, (n, c), 0)
    cw = cw_ref[0]
    xc = cw[2:3] * xa + cb_ref[0]
    xc = xc + cw[0:1] * jnp.where(t >= 2, pltpu.roll(xa, 2, 0), 0.0)
    xc = xc + cw[1:2] * jnp.where(t >= 1, pltpu.roll(xa, 1, 0), 0.0)
    xc = xc + cw[3:4] * jnp.where(t < n - 1, pltpu.roll(xa, n - 1, 0), 0.0)
    gates = _dot(xc.astype(BF16), wg_ref[0]) + bg_ref[0]
    lam = lam_ref[0]
    log_sig = jnp.minimum(lam, 0.0) - jnp.log1p(jnp.exp(-jnp.abs(lam)))
    for d in range(2):
        r = 0.5 * jnp.tanh(0.5 * gates[:, 2 * d * c:(2 * d + 1) * c]) + 0.5
        i = 0.5 * jnp.tanh(0.5 * gates[:, (2 * d + 1) * c:(2 * d + 2) * c]) + 0.5
        log_a = LRU_C * r * log_sig[d:d + 1]
        a = jnp.exp(log_a)
        t_in = jnp.tanh(-log_a) * (1.0 + a * a)
        xin = jnp.where(t_in > 0.0, t_in * lax.rsqrt(t_in), 0.0) * (i * xc)
        for hf in range(nh):
            a_s[hf] = a[:, LANES * hf: LANES * (hf + 1)]
            x_s[hf] = xin[:, LANES * hf: LANES * (hf + 1)]
        order = range(LRU_SEG) if d == 0 else range(LRU_SEG - 1, -1, -1)
        for hf in range(nh):
            h_run = p_run = None
            for s in order:
                a_row = a_s[hf, pl.ds(s, nblk, stride=LRU_SEG), :]
                x_row = x_s[hf, pl.ds(s, nblk, stride=LRU_SEG), :]
                h_run = x_row if h_run is None else a_row * h_run + x_row
                p_run = a_row if p_run is None else a_row * p_run
                hloc_s[d, hf, s] = h_run
                ploc_s[d, hf, s] = p_run
            hend_s[d, hf] = h_run
            pend_s[d, hf] = p_run

    h0 = h0_ref[0, 0]
    init = tuple(h0[d:d + 1, LANES * hf: LANES * (hf + 1)] for d in range(2) for hf in range(nh))

    def body(k, carry):
        out = []
        for idx, cur in enumerate(carry):
            d, hf = divmod(idx, nh)
            kk = k if d == 0 else nblk - 1 - k
            cin_s[d, hf, pl.ds(kk, 1), :] = cur
            out.append(pend_s[d, hf, pl.ds(kk, 1), :] * cur + hend_s[d, hf, pl.ds(kk, 1), :])
        return tuple(out)

    final = lax.fori_loop(0, nblk, body, init)

    y = None
    for d in range(2):
        for hf in range(nh):
            cin = cin_s[d, hf]
            for s in range(LRU_SEG):
                full_h = hloc_s[d, hf, s] + ploc_s[d, hf, s] * cin
                if d == 0:
                    y_s[hf, pl.ds(s, nblk, stride=LRU_SEG), :] = full_h
                else:
                    y_s[hf, pl.ds(s, nblk, stride=LRU_SEG), :] += full_h
    y = jnp.concatenate([y_s[hf] for hf in range(nh)], axis=1) * jax.nn.gelu(ga)
    ya_ref[0] = y.astype(BF16)
    st_ref[0, 0, 0:1, :] = jnp.concatenate(final[:nh], axis=1)
    st_ref[0, 0, 1:2, :] = jnp.concatenate(final[nh:], axis=1)


def _lru_call(ulru, conv_w, conv_b, w_gates_b, b_gates, lam, layer, h0, h0_layer, st_layer=0, st_depth=1,
              prev_state=None):
    b, n, _ = ulru.shape
    c = D_A
    per_layer = lambda shape: pl.BlockSpec((1,) + shape, lambda i: (layer,) + (0,) * len(shape))
    nh, nblk = c // LANES, n // LRU_SEG
    assert n % LRU_SEG == 0 and nblk % 8 == 0
    scratch = ([pltpu.VMEM((nh, n, LANES), F32)] * 3
               + [pltpu.VMEM((2, nh, LRU_SEG, nblk, LANES), F32)] * 2
               + [pltpu.VMEM((2, nh, nblk, LANES), F32)] * 3)
    in_specs = [pl.BlockSpec((1, n, 2 * c), lambda i: (i, 0, 0)),
                per_layer((CONV_W, c)), per_layer((1, c)), per_layer((c, 4 * c)), per_layer((1, 4 * c)),
                per_layer((2, c)),
                pl.BlockSpec((1, 1, 2, c), lambda i: (i, h0_layer, 0, 0))]
    args = [ulru, conv_w, conv_b, w_gates_b, b_gates, lam, h0]
    aliases = {}
    if prev_state is not None:
        aliases[len(args)] = 1
        in_specs.append(pl.BlockSpec(memory_space=pl.ANY))
        args.append(prev_state)
    return pl.pallas_call(
        functools.partial(_lru_kernel, n=n),
        grid=(b,),
        in_specs=in_specs,
        out_specs=[pl.BlockSpec((1, n, c), lambda i: (i, 0, 0)),
                   pl.BlockSpec((1, 1, 2, c), lambda i: (i, st_layer, 0, 0))],
        out_shape=[jax.ShapeDtypeStruct((b, n, c), BF16),
                   jax.ShapeDtypeStruct((b, st_depth, 2, c), F32)],
        scratch_shapes=scratch,
        input_output_aliases=aliases,
        compiler_params=_params(1),
        name="rglru",
    )(*args)


def _softmax_pv(parts, sink):
    mx = None
    for s, _, _ in parts:
        cur = jnp.max(s, axis=-1, keepdims=True)
        mx = cur if mx is None else jnp.maximum(mx, cur)
    if sink is not None:
        mx = jnp.maximum(mx, sink)
    den = None
    out = None
    for s, v, v_t in parts:
        p = jnp.exp(s - mx)
        cur = jnp.sum(p, axis=-1, keepdims=True)
        den = cur if den is None else den + cur
        o = _dot_nt(p.astype(BF16), v) if v_t else _dot(p.astype(BF16), v)
        out = o if out is None else out + o
    if sink is not None:
        den = den + jnp.exp(sink - mx)
    return out / den


def _softmax_pv_ones(s, v_ext, sink):
    mx = jnp.max(s, axis=-1, keepdims=True)
    if sink is not None:
        mx = jnp.maximum(mx, sink)
    o_ext = _dot(jnp.exp(s - mx).astype(BF16), v_ext)
    den = o_ext[:, LANES:]
    if sink is not None:
        den = den + jnp.exp(sink - mx)
    return o_ext[:, :LANES] / den


def _swap_halves(x):
    return jnp.concatenate([x[:, HEAD_DIM:], x[:, :HEAD_DIM]], axis=1)


def _keep_half(x, lo_mask, half):
    zero = jnp.zeros_like(x)
    return jnp.where(lo_mask, x, zero) if half == 0 else jnp.where(lo_mask, zero, x)


def _ctx_attn_kernel(sink_ref, att_ref, ob_ref, oc_ref, *, n, rb, sink0):
    lo = lax.broadcasted_iota(jnp.int32, (n, LANES), 1) < HEAD_DIM
    ones = jnp.ones((n, LANES), BF16)
    for r in range(rb):
        for p in range(H_C // 2):
            qp = att_ref[r, :, QC0 + LANES * p: QC0 + LANES * (p + 1)]
            kp = att_ref[r, :, KC0 + LANES * p: KC0 + LANES * (p + 1)]
            vp = jnp.concatenate([att_ref[r, :, VC0 + LANES * p: VC0 + LANES * (p + 1)], ones], axis=1)
            outs = [_softmax_pv_ones(_dot_nt(qp, _keep_half(kp, lo, half)), vp, None) for half in range(2)]
            oc_ref[r, :, LANES * p: LANES * (p + 1)] = jnp.where(lo, outs[0], outs[1]).astype(BF16)
        kpair = att_ref[r, :, KB0: KB0 + LANES]
        vpair = att_ref[r, :, VB0: VB0 + LANES]
        kpair_sw = _swap_halves(kpair)
        vpair_ext = jnp.concatenate([vpair, ones], axis=1)
        vpair_sw_ext = jnp.concatenate([_swap_halves(vpair), ones], axis=1)
        for p in range(H_B // 2):
            qp = att_ref[r, :, QB0 + LANES * p: QB0 + LANES * (p + 1)]
            outs = []
            for half in range(2):
                h = 2 * p + half
                aligned = (h // G_B) == half
                ksrc, vsrc = (kpair, vpair_ext) if aligned else (kpair_sw, vpair_sw_ext)
                outs.append(_softmax_pv_ones(_dot_nt(qp, _keep_half(ksrc, lo, half)), vsrc, sink_ref[sink0 + h]))
            ob_ref[r, :, LANES * p: LANES * (p + 1)] = jnp.where(lo, outs[0], outs[1]).astype(BF16)


def _ctx_attn_call(sinks, att, layer):
    b, n, w = att.shape
    rb = CTX_ATTN_REQS
    assert b % rb == 0 and KV_B == 2
    return pl.pallas_call(
        functools.partial(_ctx_attn_kernel, n=n, rb=rb, sink0=layer * H_B),
        grid=(b // rb,),
        in_specs=[pl.BlockSpec(memory_space=pltpu.SMEM),
                  pl.BlockSpec((rb, n, w), lambda i: (i, 0, 0))],
        out_specs=[pl.BlockSpec((rb, n, D_B), lambda i: (i, 0, 0)), pl.BlockSpec((rb, n, D_C), lambda i: (i, 0, 0))],
        out_shape=[jax.ShapeDtypeStruct((b, n, D_B), BF16), jax.ShapeDtypeStruct((b, n, D_C), BF16)],
        compiler_params=_params(1),
        name="ctx_attn",
    )(sinks, att)


def _win_attn_kernel(sink_ref, qkv_ref, kc_ref, vc_ref, o_ref, *, n, sink0):
    j = pl.program_id(1)
    nq = WIN_Q_BLOCKS * WIN_BLK
    nloc = nq + 2 * WIN_BLK
    q0 = pl.multiple_of(j * nq, WIN_BLK)
    ks = pl.multiple_of(jnp.clip(j * nq - WIN_BLK, 0, n - nloc), WIN_BLK)
    row = lax.broadcasted_iota(jnp.int32, (nq, nloc), 0)
    col = lax.broadcasted_iota(jnp.int32, (nq, nloc), 1)
    in_window = jnp.where(jnp.abs((q0 + row) - (ks + col)) <= WINDOW, 0.0, NEG_INF)
    band = jnp.concatenate([in_window] * G_B, axis=0)
    rowh = lax.broadcasted_iota(jnp.int32, (G_B * nq, 1), 0) // nq
    for kv in range(KV_B):
        gw = G_B * HEAD_DIM
        qblk = qkv_ref[0, pl.ds(q0, nq), QB0 + gw * kv: QB0 + gw * (kv + 1)]
        q3 = jnp.concatenate([qblk[:, 64 * g: 64 * (g + 1)] for g in range(G_B)], axis=0)
        kl = qkv_ref[0, pl.ds(ks, nloc), KB0 + 64 * kv: KB0 + 64 * (kv + 1)]
        vl = qkv_ref[0, pl.ds(ks, nloc), VB0 + 64 * kv: VB0 + 64 * (kv + 1)]
        kc_t = kc_ref[0, 0, kv].astype(BF16)
        vc_t = vc_ref[0, 0, kv].astype(BF16)
        s_loc = _dot_nt(q3, kl) + band
        s_ctx = _dot(q3, kc_t)
        s0 = sink0 + G_B * kv
        sk = jnp.where(rowh == 0, sink_ref[s0], jnp.where(rowh == 1, sink_ref[s0 + 1], sink_ref[s0 + 2]))
        o = _softmax_pv([(s_loc, vl, False), (s_ctx, vc_t, True)], sk)
        for g in range(G_B):
            h = kv * G_B + g
            o_ref[0, :, 64 * h: 64 * (h + 1)] = o[g * nq:(g + 1) * nq].astype(BF16)


def _win_attn_call(sinks, qkv, cache_k, cache_v, layer):
    b, n, w = qkv.shape
    past = cache_k.shape[4]
    cache_spec = pl.BlockSpec((1, 1, KV_B, HEAD_DIM, past), lambda i, j: (i, layer, 0, 0, 0))
    nq = WIN_Q_BLOCKS * WIN_BLK
    assert n % nq == 0 and n >= nq + 2 * WIN_BLK and WINDOW == WIN_BLK
    return pl.pallas_call(
        functools.partial(_win_attn_kernel, n=n, sink0=layer * H_B),
        grid=(b, n // nq),
        in_specs=[pl.BlockSpec(memory_space=pltpu.SMEM),
                  pl.BlockSpec((1, n, w), lambda i, j: (i, 0, 0)),
                  cache_spec, cache_spec],
        out_specs=pl.BlockSpec((1, nq, D_B), lambda i, j: (i, j, 0)),
        out_shape=jax.ShapeDtypeStruct((b, n, D_B), BF16),
        compiler_params=_params(2),
        name="win_attn",
    )(sinks, qkv, cache_k, cache_v)


def _nbr_window_start(g, rows):
    return jnp.clip(g * NBR_Q_ROWS - NA_ROWS // 2, 0, rows - NBR_K_ROWS)


def _nbr_attn_kernel(q_ref, k_ref, v_ref, rel_ref, kc_ref, vc_ref, o_ref, bias_s, *, n):
    g = pl.program_id(0)
    nq = NBR_Q_ROWS * GRID_W
    nk = NBR_K_ROWS * GRID_W
    rows = n // GRID_W
    ws = _nbr_window_start(g, rows)
    k0 = pl.multiple_of(ws * GRID_W, GRID_W)

    @pl.when(pl.program_id(1) == 0)
    def _():
        for i in range(NBR_Q_ROWS):
            qr = g * NBR_Q_ROWS + i
            rs = jnp.clip(qr - NA_ROWS // 2, 0, rows - NA_ROWS)
            for jp in range(NBR_K_ROWS // 2):
                tiles = []
                for j in (2 * jp, 2 * jp + 1):
                    kr = ws + j
                    valid = (kr >= rs) & (kr < rs + NA_ROWS)
                    d = jnp.clip(kr - qr + NA_ROWS - 1, 0, 2 * NA_ROWS - 2)
                    tiles.append(jnp.where(valid, rel_ref[0, :, pl.ds(d, 1)][:, 0], NEG_INF))
                bias_s[:, GRID_W * i: GRID_W * (i + 1), LANES * jp: LANES * (jp + 1)] = jnp.concatenate(tiles, axis=-1)
    lo_k = lax.broadcasted_iota(jnp.int32, (nk, LANES), 1) < HEAD_DIM
    lo_q = lax.broadcasted_iota(jnp.int32, (nq, LANES), 1) < HEAD_DIM
    zpad = jnp.zeros((HEAD_DIM, kc_ref.shape[4]), BF16)
    for p in range(H_C // 2):
        qp = q_ref[0, :, LANES * p: LANES * (p + 1)]
        kp = k_ref[0, pl.ds(k0, nk), LANES * p: LANES * (p + 1)]
        vp = v_ref[0, pl.ds(k0, nk), LANES * p: LANES * (p + 1)]
        outs = []
        for half in range(2):
            h = 2 * p + half
            kc_t = kc_ref[0, 0, h].astype(BF16)
            vc_t = vc_ref[0, 0, h].astype(BF16)
            kc_t = jnp.concatenate([kc_t, zpad] if half == 0 else [zpad, kc_t], axis=0)
            vc_t = jnp.concatenate([vc_t, zpad] if half == 0 else [zpad, vc_t], axis=0)
            s_loc = _dot_nt(qp, _keep_half(kp, lo_k, half)) + bias_s[h]
            s_ctx = _dot(qp, kc_t)
            outs.append(_softmax_pv([(s_loc, vp, False), (s_ctx, vc_t, True)], None))
        o_ref[0, :, LANES * p: LANES * (p + 1)] = jnp.where(lo_q, outs[0], outs[1]).astype(BF16)


def _nbr_attn_call(att, rel_tab, cache_k, cache_v, layer):
    b, n, w = att.shape
    past = cache_k.shape[4]
    nq = NBR_Q_ROWS * GRID_W
    nk = NBR_K_ROWS * GRID_W
    gw = H_C * HEAD_DIM
    rows = n // GRID_W
    assert (QC0, KC0, VC0) == (0, gw, 2 * gw) and n % nq == 0 and rows >= NBR_K_ROWS and NBR_K_ROWS % 2 == 0
    cache_spec = pl.BlockSpec((1, 1, H_C, HEAD_DIM, past), lambda g, i: (i, layer, 0, 0, 0))
    return pl.pallas_call(
        functools.partial(_nbr_attn_kernel, n=n),
        grid=(n // nq, b),
        in_specs=[pl.BlockSpec((1, nq, gw), lambda g, i: (i, g, 0)),
                  pl.BlockSpec((1, n, gw), lambda g, i: (i, 0, 1)),
                  pl.BlockSpec((1, n, gw), lambda g, i: (i, 0, 2)),
                  pl.BlockSpec((1,) + rel_tab.shape[1:], lambda g, i: (layer, 0, 0, 0, 0)),
                  cache_spec, cache_spec],
        out_specs=pl.BlockSpec((1, nq, gw), lambda g, i: (i, g, 0)),
        out_shape=jax.ShapeDtypeStruct((b, n, gw), BF16),
        scratch_shapes=[pltpu.VMEM((H_C, nq, nk), F32)],
        compiler_params=_params(2),
        name="nbr_attn",
    )(att, att, att, rel_tab, cache_k, cache_v)


def _outproj_kernel(x_ref, ya_ref, yb_ref, yc_ref, mod_ref, g_ref, wo_ref, wr_ref,
                    xn_ref, h2_ref, aff_ref):
    mix = jnp.concatenate([ya_ref[0], yb_ref[0], yc_ref[0]], axis=1)
    proj = _dot(mix, wo_ref[0])
    xn = x_ref[0] + mod_ref[0, 0, 2:3, :] * proj
    xn_ref[0] = xn
    h2 = _rms_modulate(xn, g_ref[0], mod_ref[0, 0, 3:4, :], mod_ref[0, 0, 4:5, :]).astype(BF16)
    h2_ref[0] = h2
    logits = _dot(h2, wr_ref[0])
    lane = lax.broadcasted_iota(jnp.int32, logits.shape, 1)
    valid = lane < N_EXPERTS
    logits = jnp.where(valid, logits, NEG_INF)
    e = jnp.exp(logits - jnp.max(logits, axis=-1, keepdims=True))
    e = jnp.where(valid, e, 0.0)
    aff_ref[0] = e / jnp.sum(e, axis=-1, keepdims=True)


def _outproj_call(x3, ya, yb, yc, mods, mod_row0, gains, wo_b16, w_router_b, layer):
    g, n, d = x3.shape
    tb = TOKEN_BLOCK
    tok = lambda w: pl.BlockSpec((1, tb, w), lambda i, j: (i, j, 0))
    per_layer = lambda shape: pl.BlockSpec((1,) + shape, lambda i, j: (layer,) + (0,) * len(shape))
    return pl.pallas_call(
        _outproj_kernel,
        grid=(g, n // tb),
        in_specs=[tok(d), tok(D_A), tok(D_B), tok(D_C),
                  pl.BlockSpec((1, 1, N_MOD, d), lambda i, j: (layer, mod_row0 + i, 0, 0)),
                  per_layer((1, d)), per_layer((D_A + D_B + D_C, d)), per_layer((d, LANES))],
        out_specs=[tok(d), tok(d), tok(LANES)],
        out_shape=[jax.ShapeDtypeStruct((g, n, d), F32),
                   jax.ShapeDtypeStruct((g, n, d), BF16),
                   jax.ShapeDtypeStruct((g, n, LANES), F32)],
        compiler_params=_params(2),
        name="outproj_router",
    )(x3, ya, yb, yc, mods, gains, wo_b16, w_router_b)


def _sort_by_affinity(keys, idx, pos, levels, seg_lanes):
    rows = keys.shape[0]
    sign = [jnp.where(((pos >> m) & 1) == 0, 1.0, -1.0) for m in range(levels)]

    def partner(x, j):
        d = 1 << j
        low = sign[j] > 0.0
        if d >= rows:
            sh = (d // rows) * seg_lanes
            return jnp.where(low, pltpu.roll(x, LANES - sh, 1), pltpu.roll(x, sh, 1))
        if d >= 8:
            x4 = x.reshape(rows // (2 * d), 2, d, LANES)
            return jnp.concatenate([x4[:, 1:2], x4[:, 0:1]], axis=1).reshape(rows, LANES)
        x3 = x.reshape(rows // 8, 8, LANES)
        low3 = sign[j].reshape(rows // 8, 8, LANES) > 0.0
        return jnp.where(low3, pltpu.roll(x3, 8 - d, 1), pltpu.roll(x3, d, 1)).reshape(rows, LANES)

    for k in range(1, levels + 1):
        for j in range(k - 1, -1, -1):
            pk, pi = partner(keys, j), partner(idx, j)
            before = jnp.where(keys == pk, jnp.where(idx < pi, 1.0, -1.0), jnp.where(keys > pk, 1.0, -1.0))
            want_first = sign[j] * sign[k] if k < levels else sign[j]
            keep = before * want_first > 0.0
            keys = jnp.where(keep, keys, pk)
            idx = jnp.where(keep, idx, pi)
    return keys, idx


def _route_kernel(aff_ref, h2_ref, xs_ref, vals_ref, rank_ref, *, n, cap, rb, by_slot):
    ne = N_EXPERTS
    rows = ROUTE_SORT_ROWS
    nseg = n // rows
    seg_lanes = rb * ne
    levels = n.bit_length() - 1
    fcap = float(cap)
    row = lax.broadcasted_iota(jnp.int32, (rows, LANES), 0)
    lane = lax.broadcasted_iota(jnp.int32, (rows, LANES), 1)
    pos = (lane // seg_lanes) * rows + row
    keys = jnp.full((rows, LANES), -1.0, F32)
    for r in range(rb):
        for seg in range(nseg):
            off = seg * seg_lanes + r * ne
            piece = aff_ref[r, seg * rows:(seg + 1) * rows, :]
            if off:
                piece = pltpu.roll(piece, off, 1)
            keys = jnp.where((lane >= off) & (lane < off + ne), piece, keys)
    keys, idx = _sort_by_affinity(keys, pos.astype(F32), pos, levels, seg_lanes)
    top_keys = keys[:cap]
    top_idx = idx[:cap]

    tok = lax.broadcasted_iota(jnp.int32, (cap, n), 1).astype(F32)
    weight = fcap - lax.broadcasted_iota(jnp.int32, (cap, n), 0).astype(F32)
    pad = jnp.full((LANES - ne, n), fcap, F32)
    if by_slot:
        top_idx_t = top_idx.T
    for r in range(rb):
        picks, rank_rows = [], []
        for e in range(ne):
            col = r * ne + e
            onehot = jnp.where(top_idx[:, col:col + 1] == tok, 1.0, 0.0)
            vals_ref[e, r] = top_keys[:, col:col + 1]
            if not by_slot:
                rank_rows.append(fcap - jnp.sum(onehot * weight, axis=0, keepdims=True))
            picks.append(onehot.astype(BF16))
        if by_slot:
            rank_ref[r] = top_idx_t[r * ne:(r + 1) * ne]
        else:
            rank_ref[r] = jnp.concatenate(rank_rows + [pad], axis=0).T
        xs = _dot(jnp.concatenate(picks, axis=0), h2_ref[r])
        xs_ref[:, r] = xs.reshape(ne, cap, xs.shape[-1]).astype(BF16)


def _route_call(aff, h2, rb):
    b, n, d = h2.shape
    cap = max(1, EC_FACTOR * n // N_EXPERTS)
    nseg = n // ROUTE_SORT_ROWS
    assert b % rb == 0 and n == nseg * ROUTE_SORT_ROWS and n & (n - 1) == 0
    assert nseg * rb * N_EXPERTS <= LANES and cap <= ROUTE_SORT_ROWS
    by_slot = cap == LANES
    assign_shape = (N_EXPERTS, cap) if by_slot else (n, LANES)
    return pl.pallas_call(
        functools.partial(_route_kernel, n=n, cap=cap, rb=rb, by_slot=by_slot),
        grid=(b // rb,),
        in_specs=[pl.BlockSpec((rb, n, LANES), lambda i: (i, 0, 0)),
                  pl.BlockSpec((rb, n, d), lambda i: (i, 0, 0))],
        out_specs=[pl.BlockSpec((N_EXPERTS, rb, cap, d), lambda i: (0, i, 0, 0)),
                   pl.BlockSpec((N_EXPERTS, rb, cap, 1), lambda i: (0, i, 0, 0)),
                   pl.BlockSpec((rb,) + assign_shape, lambda i: (i, 0, 0))],
        out_shape=[jax.ShapeDtypeStruct((N_EXPERTS, b, cap, d), BF16),
                   jax.ShapeDtypeStruct((N_EXPERTS, b, cap, 1), F32),
                   jax.ShapeDtypeStruct((b,) + assign_shape, F32)],
        compiler_params=_params(1),
        name="route_gather",
    )(aff, h2)


def _expert_kernel(xc_ref, xl_ref, vc_ref, vl_ref, wg_ref, wu_ref, wd_ref, yc_ref, yl_ref, acc_c, acc_l):
    f = pl.program_id(1)

    def step(first):
        wg = wg_ref[0, 0].astype(BF16)
        wu = wu_ref[0, 0].astype(BF16)
        wd = wd_ref[0, 0].astype(BF16)
        for x_ref, v_ref, y_ref, acc in ((xc_ref, vc_ref, yc_ref, acc_c), (xl_ref, vl_ref, yl_ref, acc_l)):
            rows = x_ref.shape[1]
            chunk = min(EXPERT_ROW_CHUNK, rows)
            for c0 in range(0, rows, chunk):
                sl = slice(c0, c0 + chunk)
                x = x_ref[0, sl, :]
                a = _dot(x, wg)
                u = _dot(x, wu)
                act = ((a * jax.nn.sigmoid(a)) * u).astype(BF16)
                part = _dot(act, wd)
                if first:
                    acc[sl, :] = part
                else:
                    y_ref[0, sl, :] = ((acc[sl, :] + part) * v_ref[0, sl, :]).astype(BF16)

    pl.when(f == 0)(functools.partial(step, True))
    pl.when(f == EXPERT_F_STEPS - 1)(functools.partial(step, False))


def _expert_call(xs_c, xs_l, vals_c, vals_l, w_gate, w_up, w_down, layer):
    e, rc, d = xs_c.shape
    rl = xs_l.shape[1]
    f_total = w_gate.shape[-1]
    assert EXPERT_F_STEPS == 2
    fb = f_total // EXPERT_F_STEPS
    per_e = lambda rows, w: pl.BlockSpec((1, rows, w), lambda i, j: (i, 0, 0))
    return pl.pallas_call(
        _expert_kernel,
        grid=(e, EXPERT_F_STEPS),
        in_specs=[per_e(rc, d), per_e(rl, d), per_e(rc, 1), per_e(rl, 1),
                  pl.BlockSpec((1, 1, d, fb), lambda i, j: (layer, i, 0, j)),
                  pl.BlockSpec((1, 1, d, fb), lambda i, j: (layer, i, 0, j)),
                  pl.BlockSpec((1, 1, fb, d), lambda i, j: (layer, i, j, 0))],
        out_specs=[per_e(rc, d), per_e(rl, d)],
        out_shape=[jax.ShapeDtypeStruct((e, rc, d), BF16),
                   jax.ShapeDtypeStruct((e, rl, d), BF16)],
        scratch_shapes=[pltpu.VMEM((rc, d), F32), pltpu.VMEM((rl, d), F32)],
        compiler_params=_params(2),
        name="experts",
    )(xs_c, xs_l, vals_c, vals_l, w_gate, w_up, w_down)


def _combine_kernel(y_ref, rank_ref, xn_ref, mod_ref, gf_ref, o_ref, *, n, cap, final, rb, by_slot):
    ec = N_EXPERTS * cap
    d = xn_ref.shape[-1]
    if by_slot:
        tok = lax.broadcasted_iota(jnp.int32, (n, cap), 0).astype(F32)
    else:
        ce = lax.broadcasted_iota(jnp.int32, (LANES, ec), 1) // cap
        ee = lax.broadcasted_iota(jnp.int32, (LANES, ec), 0)
        expand = jnp.where(ce == ee, 1.0, 0.0).astype(BF16)
        slot = (lax.broadcasted_iota(jnp.int32, (n, ec), 1) % cap).astype(F32)
    for r in range(rb):
        if by_slot:
            chosen = rank_ref[r]
            onehot = jnp.concatenate([jnp.where(tok == chosen[e:e + 1, :], 1.0, 0.0).astype(BF16)
                                      for e in range(N_EXPERTS)], axis=1)
        else:
            rank = rank_ref[r].astype(BF16)
            rexp = _dot(rank, expand)
            onehot = jnp.where(rexp == slot, 1.0, 0.0).astype(BF16)
        moe = _dot(onehot, y_ref[:, r].reshape(ec, d))
        x = xn_ref[r] + mod_ref[0, 0, 5:6, :] * moe
        if final:
            x = (x * lax.rsqrt(jnp.mean(x * x, axis=-1, keepdims=True) + EPS)) * gf_ref[...]
        o_ref[r] = x


def _combine_call(y4, rank, xn, mods, mod_row0, shared_mod, layer, g_final, final):
    e, b, cap, d = y4.shape
    n = xn.shape[1]
    rb = COMBINE_REQS_CTX if shared_mod else 1
    by_slot = rank.shape[1:] == (e, cap) and cap == LANES
    assert b % rb == 0 and (by_slot or rank.shape[1:] == (n, LANES))
    return pl.pallas_call(
        functools.partial(_combine_kernel, n=n, cap=cap, final=final, rb=rb, by_slot=by_slot),
        grid=(b // rb,),
        in_specs=[pl.BlockSpec((e, rb, cap, d), lambda i: (0, i, 0, 0)),
                  pl.BlockSpec((rb,) + rank.shape[1:], lambda i: (i, 0, 0)),
                  pl.BlockSpec((rb, n, d), lambda i: (i, 0, 0)),
                  pl.BlockSpec((1, 1, N_MOD, d), (lambda i: (layer, mod_row0, 0, 0)) if shared_mod
                               else (lambda i: (layer, mod_row0 + i, 0, 0))),
                  pl.BlockSpec((1, d), lambda i: (0, 0))],
        out_specs=pl.BlockSpec((rb, n, d), lambda i: (i, 0, 0)),
        out_shape=jax.ShapeDtypeStruct((b, n, d), F32),
        compiler_params=_params(1),
        name="combine",
    )(y4, rank, xn, mods, g_final)


def _rope_tables(n):
    t = jnp.arange(n, dtype=jnp.int32)
    row = (t // GRID_W).astype(F32)
    col = (t % GRID_W).astype(F32)
    half = HEAD_DIM // 4
    inv = jnp.power(ROPE_BASE, -jnp.arange(half, dtype=F32) / half)
    ang_r = row[:, None] * inv[None, :]
    ang_c = col[:, None] * inv[None, :]
    cos_h = jnp.concatenate([jnp.cos(ang_r)] * 2 + [jnp.cos(ang_c)] * 2, axis=-1)
    sin_h = jnp.concatenate([-jnp.sin(ang_r), jnp.sin(ang_r), -jnp.sin(ang_c), jnp.sin(ang_c)], axis=-1)
    reps = H_B + KV_B
    return jnp.tile(cos_h, (1, reps)), jnp.tile(sin_h, (1, reps))


def _nbr_rel_tables(rpb):
    col = np.arange(GRID_W)
    cs = np.clip(col - NA_COLS // 2, 0, GRID_W - NA_COLS)
    col_mask = (col[None, :] >= cs[:, None]) & (col[None, :] < cs[:, None] + NA_COLS)
    dc_idx = np.clip(col[None, :] - col[:, None], -(NA_COLS - 1), NA_COLS - 1) + (NA_COLS - 1)
    sel_col = (dc_idx[None, :, :] == np.arange(2 * NA_COLS - 1)[:, None, None]).astype(np.float32)
    t = jnp.einsum("lhdc,cqk->lhdqk", rpb.astype(F32), sel_col, precision=lax.Precision.HIGHEST)
    return jnp.where(col_mask, t, NEG_INF)


def _block_diag(w):
    nb, bw = w.shape[-3], w.shape[-1]
    eye = jnp.eye(nb, dtype=w.dtype)
    return (eye[:, None, :, None] * w[..., :, :, None, :]).reshape(w.shape[:-3] + (nb * bw, nb * bw))


def kernel(x_prompt, x_sample, state_lru, cache_k_win, cache_v_win, cache_k_nbr, cache_v_nbr, c, c_ctx, w_mod, b_mod, g_norm1, w_in, conv_w, conv_b, w_gate_r, b_gate_r, w_gate_i, b_gate_i, lru_lambda, sink_logit, nbr_bias, w_out, g_norm2, w_router, w_exp_gate, w_exp_up, w_exp_down, g_final):
    bc, seq, d = x_prompt.shape
    bl, n_lat, _ = x_sample.shape
    depth = w_mod.shape[0]
    assert bl + 1 <= 8 and d == 1024

    cond8 = jnp.zeros((8, d), F32).at[0].set(c_ctx).at[1:1 + bl].set(c)
    mods = _adaln_call(cond8, w_mod, b_mod.reshape(depth, 1, -1)).reshape(depth, 8, N_MOD, d)
    rope_tabs = _rope_tables(n_lat)
    bias_tab = _nbr_rel_tables(nbr_bias)
    ckw_t, cvw_t, ckn_t, cvn_t = (jnp.swapaxes(a, -1, -2) for a in (cache_k_win, cache_v_win, cache_k_nbr, cache_v_nbr))
    gf = g_final.reshape(1, d)
    assert SCALE == 0.125
    new_cols = np.arange(w_in.shape[-1]) - LRU_W
    is_q = ((new_cols >= QC0) & (new_cols < KC0)) | ((new_cols >= QB0) & (new_cols < KB0))
    q_scale = jnp.asarray(np.where(is_q, SCALE, 1.0), F32)

    nbr0 = LRU_W + D_B + 2 * KV_W
    w_in_b = (jnp.concatenate([w_in[..., :LRU_W], w_in[..., nbr0:], w_in[..., LRU_W:nbr0]], axis=-1) * q_scale).astype(BF16)
    wo = w_out.astype(BF16)
    wr_b = jnp.pad(w_router.astype(BF16), ((0, 0), (0, 0), (0, LANES - N_EXPERTS)))
    gate_r, gate_i = _block_diag(w_gate_r), _block_diag(w_gate_i)
    w_gates = jnp.concatenate([gate_r[:, 0], gate_i[:, 0], gate_r[:, 1], gate_i[:, 1]], axis=-1).astype(BF16)
    b_gates = jnp.concatenate([b_gate_r[:, 0], b_gate_i[:, 0], b_gate_r[:, 1], b_gate_i[:, 1]], axis=-1)[:, None, :]
    cb = conv_b[:, None, :]
    g1 = g_norm1[:, None, :]
    g2 = g_norm2[:, None, :]
    sinks = sink_logit.reshape(-1)
    ctx_row, lat_row = 0, 1

    xc = x_prompt.reshape(1, bc * seq, d)
    xl = x_sample
    zeros_state = jnp.zeros((bc, 1, 2, D_A), F32)
    caches = None
    st = None
    for l in range(depth):
        final = l == depth - 1
        ulru_c, qkv_c, *caches = _inproj_call(xc, mods, ctx_row, g1, w_in_b, l, cache_seq=seq, depth=depth,
                                              prev_caches=caches)
        ya_c, st = _lru_call(ulru_c.reshape(bc, seq, LRU_W), conv_w, cb, w_gates, b_gates, lru_lambda, l,
                             zeros_state, 0, st_layer=l, st_depth=depth, prev_state=st)
        yb_c, yc_c = _ctx_attn_call(sinks, qkv_c.reshape(bc, seq, -1), l)
        xn_c, h2_c, aff_c = _outproj_call(xc, ya_c.reshape(1, bc * seq, -1), yb_c.reshape(1, bc * seq, -1),
                                          yc_c.reshape(1, bc * seq, -1), mods, ctx_row, g2, wo, wr_b, l)
        xs_c, vals_c, rank_c = _route_call(aff_c.reshape(bc, seq, LANES), h2_c.reshape(bc, seq, d), ROUTE_REQS_CTX)

        ulru_l, qkv_l = _inproj_call(xl, mods, lat_row, g1, w_in_b, l, rope_tabs=rope_tabs)
        ya_l, _ = _lru_call(ulru_l, conv_w, cb, w_gates, b_gates, lru_lambda, l, state_lru, l)
        yb_l = _win_attn_call(sinks, qkv_l, ckw_t, cvw_t, l)
        yc_l = _nbr_attn_call(qkv_l, bias_tab, ckn_t, cvn_t, l)
        xn_l, h2_l, aff_l = _outproj_call(xl, ya_l, yb_l, yc_l, mods, lat_row, g2, wo, wr_b, l)
        xs_l, vals_l, rank_l = _route_call(aff_l, h2_l, ROUTE_REQS_LAT)

        cap_c, cap_l = xs_c.shape[2], xs_l.shape[2]
        y_c, y_l = _expert_call(xs_c.reshape(N_EXPERTS, bc * cap_c, d), xs_l.reshape(N_EXPERTS, bl * cap_l, d),
                                vals_c.reshape(N_EXPERTS, bc * cap_c, 1), vals_l.reshape(N_EXPERTS, bl * cap_l, 1),
                                w_exp_gate, w_exp_up, w_exp_down, l)
        xc = _combine_call(y_c.reshape(N_EXPERTS, bc, cap_c, d), rank_c, xn_c.reshape(bc, seq, d), mods, ctx_row, True,
                           l, gf, final)
        xl = _combine_call(y_l.reshape(N_EXPERTS, bl, cap_l, d), rank_l, xn_l, mods, lat_row, False, l, gf, final)
        xc = xc.reshape(1, bc * seq, d)

    y_prompt = xc.reshape(bc, seq, d)
    y_sample = xl
    return (y_prompt, y_sample, st, *[jnp.swapaxes(buf, -1, -2) for buf in caches])
```

```python
import functools

import numpy as np
import jax
import jax.numpy as jnp
from jax import lax
from jax.experimental import pallas as pl
from jax.experimental.pallas import tpu as pltpu

F32 = jnp.float32
BF16 = jnp.bfloat16

HEAD_DIM = 64
GRID_W = 64
LRU_C = 8.0
CONV_W = 4
KV_B = 2
G_B = 3
H_B = KV_B * G_B
H_C = 6
WINDOW = 128
WIN_BLK = 128
NA_ROWS = 8
NA_COLS = 16
ROPE_BASE = 10000.0
N_EXPERTS = 16
EC_FACTOR = 2
N_MOD = 6
EPS = 1e-6
NEG_INF = -1e30
SCALE = HEAD_DIM ** -0.5

LANES = 128
VMEM_LIMIT_BYTES = 56 * 1024 * 1024

TOKEN_BLOCK = 1024
MOD_COL_BLOCK = 1536
EXPERT_F_STEPS = 2
EXPERT_ROW_CHUNK = 256
D_A = 256
D_B = H_B * HEAD_DIM
D_C = H_C * HEAD_DIM
KV_W = KV_B * HEAD_DIM
LRU_W = 2 * D_A
QC0, KC0, VC0 = 0, D_C, 2 * D_C
QB0 = 3 * D_C
KB0, VB0, ATT_W = QB0 + D_B, QB0 + D_B + KV_W, QB0 + D_B + 2 * KV_W
CTX_ATTN_REQS = 4
WIN_Q_BLOCKS = 2
NBR_Q_ROWS = 4
NBR_K_ROWS = 12
ROUTE_REQS_CTX = 8
ROUTE_REQS_LAT = 2
ROUTE_SORT_ROWS = 256
COMBINE_REQS_CTX = 4
LRU_SEG = 8


def _params(n_axes):
    return pltpu.CompilerParams(dimension_semantics=("arbitrary",) * n_axes,
                                vmem_limit_bytes=VMEM_LIMIT_BYTES)


def _dot(a, b):
    return jnp.dot(a, b, preferred_element_type=F32)


def _dot_nt(a, b):
    return lax.dot_general(a, b, (((1,), (1,)), ((), ())), preferred_element_type=F32)


def _adaln_kernel(cond_ref, w_ref, b_ref, o_ref):
    c = cond_ref[...]
    s = c * jax.nn.sigmoid(c)
    o_ref[0] = _dot(s.astype(BF16), w_ref[0].astype(BF16)) + b_ref[0]


def _adaln_call(cond8, w_mod, b_mod3):
    depth, d, n = w_mod.shape
    nb = MOD_COL_BLOCK
    return pl.pallas_call(
        _adaln_kernel,
        grid=(depth, n // nb),
        in_specs=[pl.BlockSpec((8, d), lambda l, j: (0, 0)),
                  pl.BlockSpec((1, d, nb), lambda l, j: (l, 0, j)),
                  pl.BlockSpec((1, 1, nb), lambda l, j: (l, 0, j))],
        out_specs=pl.BlockSpec((1, 8, nb), lambda l, j: (l, 0, j)),
        out_shape=jax.ShapeDtypeStruct((depth, 8, n), F32),
        compiler_params=_params(2),
        name="adaln",
    )(cond8, w_mod, b_mod3)


def _rms_modulate(x, gain, shift, scale):
    y = x * lax.rsqrt(jnp.mean(x * x, axis=-1, keepdims=True) + EPS)
    return (y * gain) * (1.0 + scale) + shift


def _inproj_kernel(*refs, rope, cache, seq):
    if rope:
        x_ref, mod_ref, g_ref, w_ref, cos_ref, sin_ref, ulru_ref, qkv_ref = refs
    elif cache:
        x_ref, mod_ref, g_ref, w_ref = refs[:4]
        ulru_ref, qkv_ref, kw_ref, vw_ref, kn_ref, vn_ref = refs[-6:]
    else:
        x_ref, mod_ref, g_ref, w_ref, ulru_ref, qkv_ref = refs
    x = x_ref[0]
    h = _rms_modulate(x, g_ref[0], mod_ref[0, 0, 0:1, :], mod_ref[0, 0, 1:2, :])
    u = _dot(h.astype(BF16), w_ref[0])
    tb = u.shape[0]
    ulru_ref[0] = u[:, :LRU_W]
    if rope:
        lo, hi = LRU_W + QB0, LRU_W + VB0
        qk = u[:, lo:hi]
        lane = lax.broadcasted_iota(jnp.int32, (tb, hi - lo), 1)
        first = (lane & 31) < 16
        partner = jnp.where(first, pltpu.roll(qk, hi - lo - 16, 1), pltpu.roll(qk, 16, 1))
        qk = qk * cos_ref[...] + partner * sin_ref[...]
        qkv_ref[0, :, :QB0] = u[:, LRU_W:lo].astype(BF16)
        qkv_ref[0, :, QB0:VB0] = qk.astype(BF16)
        qkv_ref[0, :, VB0:] = u[:, hi:].astype(BF16)
    else:
        qkv_ref[0] = u[:, LRU_W:].astype(BF16)
    if cache:
        for r in range(tb // seq):
            rows = slice(r * seq, (r + 1) * seq)
            for ref, c0, heads in ((kw_ref, KB0, KV_B), (vw_ref, VB0, KV_B), (kn_ref, KC0, H_C), (vn_ref, VC0, H_C)):
                for p in range(heads // 2):
                    pair_t = u[rows, LRU_W + c0 + LANES * p: LRU_W + c0 + LANES * (p + 1)].T
                    ref[r, 0, 2 * p] = pair_t[:HEAD_DIM]
                    ref[r, 0, 2 * p + 1] = pair_t[HEAD_DIM:]


def _inproj_call(x3, mods, mod_row0, gains, w_in_b, layer, rope_tabs=None, cache_seq=None, depth=1, prev_caches=None):
    g, n, d = x3.shape
    d_in = w_in_b.shape[2]
    tb = TOKEN_BLOCK
    rope = rope_tabs is not None
    cache = cache_seq is not None
    in_specs = [pl.BlockSpec((1, tb, d), lambda i, j: (i, j, 0)),
                pl.BlockSpec((1, 1, N_MOD, d), lambda i, j: (layer, mod_row0 + i, 0, 0)),
                pl.BlockSpec((1, 1, d), lambda i, j: (layer, 0, 0)),
                pl.BlockSpec((1, d, d_in), lambda i, j: (layer, 0, 0))]
    args = [x3, mods, gains, w_in_b]
    assert d_in == LRU_W + ATT_W
    out_shape = [jax.ShapeDtypeStruct((g, n, LRU_W), F32),
                 jax.ShapeDtypeStruct((g, n, ATT_W), BF16)]
    out_specs = [pl.BlockSpec((1, tb, LRU_W), lambda i, j: (i, j, 0)),
                 pl.BlockSpec((1, tb, ATT_W), lambda i, j: (i, j, 0))]
    if rope:
        in_specs += [pl.BlockSpec((tb, VB0 - QB0), lambda i, j: (j, 0))] * 2
        args += list(rope_tabs)
    aliases = {}
    if cache:
        assert g == 1 and tb % cache_seq == 0
        rb = tb // cache_seq
        nreq = n // cache_seq
        for heads in (KV_B, KV_B, H_C, H_C):
            out_shape.append(jax.ShapeDtypeStruct((nreq, depth, heads, HEAD_DIM, cache_seq), F32))
            out_specs.append(pl.BlockSpec((rb, 1, heads, HEAD_DIM, cache_seq), lambda i, j: (j, layer, 0, 0, 0)))
        if prev_caches is not None:
            for k, buf in enumerate(prev_caches):
                aliases[len(args)] = 2 + k
                in_specs.append(pl.BlockSpec(memory_space=pl.ANY))
                args.append(buf)
    return pl.pallas_call(
        functools.partial(_inproj_kernel, rope=rope, cache=cache, seq=cache_seq),
        grid=(g, n // tb),
        in_specs=in_specs,
        out_specs=out_specs,
        out_shape=out_shape,
        input_output_aliases=aliases,
        compiler_params=_params(2),
        name="inproj_rope" if rope else "inproj_ctx",
    )(*args)


def _lru_kernel(*refs, n):
    u_ref, cw_ref, cb_ref, wg_ref, bg_ref, lam_ref, h0_ref = refs[:7]
    ya_ref, st_ref, a_s, x_s, y_s, hloc_s, ploc_s, hend_s, pend_s, cin_s = refs[-10:]
    c = D_A
    nblk = n // LRU_SEG
    nh = c // LANES
    u = u_ref[0]
    xa = u[:, :c]
    ga = u[:, c:]
    t = lax.broadcasted_iota(jnp.int32, (n, c), 0)
    cw = cw_ref[0]
    xc = cw[2:3] * xa + cb_ref[0]
    xc = xc + cw[0:1] * jnp.where(t >= 2, pltpu.roll(xa, 2, 0), 0.0)
    xc = xc + cw[1:2] * jnp.where(t >= 1, pltpu.roll(xa, 1, 0), 0.0)
    xc = xc + cw[3:4] * jnp.where(t < n - 1, pltpu.roll(xa, n - 1, 0), 0.0)
    gates = _dot(xc.astype(BF16), wg_ref[0]) + bg_ref[0]
    lam = lam_ref[0]
    log_sig = jnp.minimum(lam, 0.0) - jnp.log1p(jnp.exp(-jnp.abs(lam)))
    for d in range(2):
        r = 0.5 * jnp.tanh(0.5 * gates[:, 2 * d * c:(2 * d + 1) * c]) + 0.5
        i = 0.5 * jnp.tanh(0.5 * gates[:, (2 * d + 1) * c:(2 * d + 2) * c]) + 0.5
        log_a = LRU_C * r * log_sig[d:d + 1]
        a = jnp.exp(log_a)
        t_in = jnp.tanh(-log_a) * (1.0 + a * a)
        xin = jnp.where(t_in > 0.0, t_in * lax.rsqrt(t_in), 0.0) * (i * xc)
        for hf in range(nh):
            a_s[hf] = a[:, LANES * hf: LANES * (hf + 1)]
            x_s[hf] = xin[:, LANES * hf: LANES * (hf + 1)]
        order = range(LRU_SEG) if d == 0 else range(LRU_SEG - 1, -1, -1)
        for hf in range(nh):
            h_run = p_run = None
            for s in order:
                a_row = a_s[hf, pl.ds(s, nblk, stride=LRU_SEG), :]
                x_row = x_s[hf, pl.ds(s, nblk, stride=LRU_SEG), :]
                h_run = x_row if h_run is None else a_row * h_run + x_row
                p_run = a_row if p_run is None else a_row * p_run
                hloc_s[d, hf, s] = h_run
                ploc_s[d, hf, s] = p_run
            hend_s[d, hf] = h_run
            pend_s[d, hf] = p_run

    h0 = h0_ref[0, 0]
    init = tuple(h0[d:d + 1, LANES * hf: LANES * (hf + 1)] for d in range(2) for hf in range(nh))

    def body(k, carry):
        out = []
        for idx, cur in enumerate(carry):
            d, hf = divmod(idx, nh)
            kk = k if d == 0 else nblk - 1 - k
            cin_s[d, hf, pl.ds(kk, 1), :] = cur
            out.append(pend_s[d, hf, pl.ds(kk, 1), :] * cur + hend_s[d, hf, pl.ds(kk, 1), :])
        return tuple(out)

    final = lax.fori_loop(0, nblk, body, init)

    y = None
    for d in range(2):
        for hf in range(nh):
            cin = cin_s[d, hf]
            for s in range(LRU_SEG):
                full_h = hloc_s[d, hf, s] + ploc_s[d, hf, s] * cin
                if d == 0:
                    y_s[hf, pl.ds(s, nblk, stride=LRU_SEG), :] = full_h
                else:
                    y_s[hf, pl.ds(s, nblk, stride=LRU_SEG), :] += full_h
    y = jnp.concatenate([y_s[hf] for hf in range(nh)], axis=1) * jax.nn.gelu(ga)
    ya_ref[0] = y.astype(BF16)
    st_ref[0, 0, 0:1, :] = jnp.concatenate(final[:nh], axis=1)
    st_ref[0, 0, 1:2, :] = jnp.concatenate(final[nh:], axis=1)


def _lru_call(ulru, conv_w, conv_b, w_gates_b, b_gates, lam, layer, h0, h0_layer, st_layer=0, st_depth=1,
              prev_state=None):
    b, n, _ = ulru.shape
    c = D_A
    per_layer = lambda shape: pl.BlockSpec((1,) + shape, lambda i: (layer,) + (0,) * len(shape))
    nh, nblk = c // LANES, n // LRU_SEG
    assert n % LRU_SEG == 0 and nblk % 8 == 0
    scratch = ([pltpu.VMEM((nh, n, LANES), F32)] * 3
               + [pltpu.VMEM((2, nh, LRU_SEG, nblk, LANES), F32)] * 2
               + [pltpu.VMEM((2, nh, nblk, LANES), F32)] * 3)
    in_specs = [pl.BlockSpec((1, n, 2 * c), lambda i: (i, 0, 0)),
                per_layer((CONV_W, c)), per_layer((1, c)), per_layer((c, 4 * c)), per_layer((1, 4 * c)),
                per_layer((2, c)),
                pl.BlockSpec((1, 1, 2, c), lambda i: (i, h0_layer, 0, 0))]
    args = [ulru, conv_w, conv_b, w_gates_b, b_gates, lam, h0]
    aliases = {}
    if prev_state is not None:
        aliases[len(args)] = 1
        in_specs.append(pl.BlockSpec(memory_space=pl.ANY))
        args.append(prev_state)
    return pl.pallas_call(
        functools.partial(_lru_kernel, n=n),
        grid=(b,),
        in_specs=in_specs,
        out_specs=[pl.BlockSpec((1, n, c), lambda i: (i, 0, 0)),
                   pl.BlockSpec((1, 1, 2, c), lambda i: (i, st_layer, 0, 0))],
        out_shape=[jax.ShapeDtypeStruct((b, n, c), BF16),
                   jax.ShapeDtypeStruct((b, st_depth, 2, c), F32)],
        scratch_shapes=scratch,
        input_output_aliases=aliases,
        compiler_params=_params(1),
        name="rglru",
    )(*args)


def _with_ones(v, transposed=False):
    axis = 0 if transposed else 1
    w = v.shape[axis]
    pieces = [v]
    if w < LANES:
        pieces.append(jnp.zeros(v.shape[:axis] + (LANES - w,) + v.shape[axis + 1:], v.dtype))
    pieces.append(jnp.ones(v.shape[:axis] + (LANES,) + v.shape[axis + 1:], v.dtype))
    return jnp.concatenate(pieces, axis=axis)


def _softmax_pv(parts, sink):
    mx = None
    for s, _, _ in parts:
        cur = jnp.max(s, axis=-1, keepdims=True)
        mx = cur if mx is None else jnp.maximum(mx, cur)
    if sink is not None:
        mx = jnp.maximum(mx, sink)
    out = None
    for s, v, v_t in parts:
        p = jnp.exp(s - mx).astype(BF16)
        o = _dot_nt(p, v) if v_t else _dot(p, v)
        out = o if out is None else out + o
    den = out[:, LANES:]
    if sink is not None:
        den = den + jnp.exp(sink - mx)
    return out[:, :LANES] / den


def _swap_halves(x):
    return jnp.concatenate([x[:, HEAD_DIM:], x[:, :HEAD_DIM]], axis=1)


def _keep_half(x, lo_mask, half):
    zero = jnp.zeros_like(x)
    return jnp.where(lo_mask, x, zero) if half == 0 else jnp.where(lo_mask, zero, x)


def _ctx_attn_kernel(sink_ref, att_ref, ob_ref, oc_ref, *, n, rb, sink0):
    lo = lax.broadcasted_iota(jnp.int32, (n, LANES), 1) < HEAD_DIM
    for r in range(rb):
        for p in range(H_C // 2):
            qp = att_ref[r, :, QC0 + LANES * p: QC0 + LANES * (p + 1)]
            kp = att_ref[r, :, KC0 + LANES * p: KC0 + LANES * (p + 1)]
            vp = _with_ones(att_ref[r, :, VC0 + LANES * p: VC0 + LANES * (p + 1)])
            outs = [_softmax_pv([(_dot_nt(qp, _keep_half(kp, lo, half)), vp, False)], None) for half in range(2)]
            oc_ref[r, :, LANES * p: LANES * (p + 1)] = jnp.where(lo, outs[0], outs[1]).astype(BF16)
        kpair = att_ref[r, :, KB0: KB0 + LANES]
        vpair = att_ref[r, :, VB0: VB0 + LANES]
        kpair_sw = _swap_halves(kpair)
        vpair_ext = _with_ones(vpair)
        vpair_sw_ext = _with_ones(_swap_halves(vpair))
        for p in range(H_B // 2):
            qp = att_ref[r, :, QB0 + LANES * p: QB0 + LANES * (p + 1)]
            outs = []
            for half in range(2):
                h = 2 * p + half
                aligned = (h // G_B) == half
                ksrc, vsrc = (kpair, vpair_ext) if aligned else (kpair_sw, vpair_sw_ext)
                outs.append(_softmax_pv([(_dot_nt(qp, _keep_half(ksrc, lo, half)), vsrc, False)], sink_ref[sink0 + h]))
            ob_ref[r, :, LANES * p: LANES * (p + 1)] = jnp.where(lo, outs[0], outs[1]).astype(BF16)


def _ctx_attn_call(sinks, att, layer):
    b, n, w = att.shape
    rb = CTX_ATTN_REQS
    assert b % rb == 0 and KV_B == 2
    return pl.pallas_call(
        functools.partial(_ctx_attn_kernel, n=n, rb=rb, sink0=layer * H_B),
        grid=(b // rb,),
        in_specs=[pl.BlockSpec(memory_space=pltpu.SMEM),
                  pl.BlockSpec((rb, n, w), lambda i: (i, 0, 0))],
        out_specs=[pl.BlockSpec((rb, n, D_B), lambda i: (i, 0, 0)), pl.BlockSpec((rb, n, D_C), lambda i: (i, 0, 0))],
        out_shape=[jax.ShapeDtypeStruct((b, n, D_B), BF16), jax.ShapeDtypeStruct((b, n, D_C), BF16)],
        compiler_params=_params(1),
        name="ctx_attn",
    )(sinks, att)


def _win_attn_kernel(sink_ref, qkv_ref, kc_ref, vc_ref, o_ref, *, n, sink0):
    j = pl.program_id(1)
    nq = WIN_Q_BLOCKS * WIN_BLK
    nloc = nq + 2 * WIN_BLK
    q0 = pl.multiple_of(j * nq, WIN_BLK)
    ks = pl.multiple_of(jnp.clip(j * nq - WIN_BLK, 0, n - nloc), WIN_BLK)
    row = lax.broadcasted_iota(jnp.int32, (nq, nloc), 0)
    col = lax.broadcasted_iota(jnp.int32, (nq, nloc), 1)
    in_window = jnp.where(jnp.abs((q0 + row) - (ks + col)) <= WINDOW, 0.0, NEG_INF)
    band = jnp.concatenate([in_window] * G_B, axis=0)
    rowh = lax.broadcasted_iota(jnp.int32, (G_B * nq, 1), 0) // nq
    for kv in range(KV_B):
        gw = G_B * HEAD_DIM
        qblk = qkv_ref[0, pl.ds(q0, nq), QB0 + gw * kv: QB0 + gw * (kv + 1)]
        q3 = jnp.concatenate([qblk[:, 64 * g: 64 * (g + 1)] for g in range(G_B)], axis=0)
        kl = qkv_ref[0, pl.ds(ks, nloc), KB0 + 64 * kv: KB0 + 64 * (kv + 1)]
        vl = _with_ones(qkv_ref[0, pl.ds(ks, nloc), VB0 + 64 * kv: VB0 + 64 * (kv + 1)])
        kc_t = kc_ref[0, 0, kv].astype(BF16)
        vc_t = _with_ones(vc_ref[0, 0, kv].astype(BF16), transposed=True)
        s_loc = _dot_nt(q3, kl) + band
        s_ctx = _dot(q3, kc_t)
        s0 = sink0 + G_B * kv
        sk = jnp.where(rowh == 0, sink_ref[s0], jnp.where(rowh == 1, sink_ref[s0 + 1], sink_ref[s0 + 2]))
        o = _softmax_pv([(s_loc, vl, False), (s_ctx, vc_t, True)], sk)
        for g in range(G_B):
            h = kv * G_B + g
            o_ref[0, :, 64 * h: 64 * (h + 1)] = o[g * nq:(g + 1) * nq, :HEAD_DIM].astype(BF16)


def _win_attn_call(sinks, qkv, cache_k, cache_v, layer):
    b, n, w = qkv.shape
    past = cache_k.shape[4]
    cache_spec = pl.BlockSpec((1, 1, KV_B, HEAD_DIM, past), lambda i, j: (i, layer, 0, 0, 0))
    nq = WIN_Q_BLOCKS * WIN_BLK
    assert n % nq == 0 and n >= nq + 2 * WIN_BLK and WINDOW == WIN_BLK
    return pl.pallas_call(
        functools.partial(_win_attn_kernel, n=n, sink0=layer * H_B),
        grid=(b, n // nq),
        in_specs=[pl.BlockSpec(memory_space=pltpu.SMEM),
                  pl.BlockSpec((1, n, w), lambda i, j: (i, 0, 0)),
                  cache_spec, cache_spec],
        out_specs=pl.BlockSpec((1, nq, D_B), lambda i, j: (i, j, 0)),
        out_shape=jax.ShapeDtypeStruct((b, n, D_B), BF16),
        compiler_params=_params(2),
        name="win_attn",
    )(sinks, qkv, cache_k, cache_v)


def _nbr_window_start(g, rows):
    return jnp.clip(g * NBR_Q_ROWS - NA_ROWS // 2, 0, rows - NBR_K_ROWS)


def _nbr_attn_kernel(q_ref, k_ref, v_ref, rel_ref, kc_ref, vc_ref, o_ref, bias_s, *, n):
    g = pl.program_id(0)
    nq = NBR_Q_ROWS * GRID_W
    nk = NBR_K_ROWS * GRID_W
    rows = n // GRID_W
    ws = _nbr_window_start(g, rows)
    k0 = pl.multiple_of(ws * GRID_W, GRID_W)

    @pl.when(pl.program_id(1) == 0)
    def _():
        for i in range(NBR_Q_ROWS):
            qr = g * NBR_Q_ROWS + i
            rs = jnp.clip(qr - NA_ROWS // 2, 0, rows - NA_ROWS)
            for jp in range(NBR_K_ROWS // 2):
                tiles = []
                for j in (2 * jp, 2 * jp + 1):
                    kr = ws + j
                    valid = (kr >= rs) & (kr < rs + NA_ROWS)
                    d = jnp.clip(kr - qr + NA_ROWS - 1, 0, 2 * NA_ROWS - 2)
                    tiles.append(jnp.where(valid, rel_ref[0, :, pl.ds(d, 1)][:, 0], NEG_INF))
                bias_s[:, GRID_W * i: GRID_W * (i + 1), LANES * jp: LANES * (jp + 1)] = jnp.concatenate(tiles, axis=-1)
    lo_k = lax.broadcasted_iota(jnp.int32, (nk, LANES), 1) < HEAD_DIM
    lo_q = lax.broadcasted_iota(jnp.int32, (nq, LANES), 1) < HEAD_DIM
    zpad = jnp.zeros((HEAD_DIM, kc_ref.shape[4]), BF16)
    for p in range(H_C // 2):
        qp = q_ref[0, :, LANES * p: LANES * (p + 1)]
        kp = k_ref[0, pl.ds(k0, nk), LANES * p: LANES * (p + 1)]
        vp = _with_ones(v_ref[0, pl.ds(k0, nk), LANES * p: LANES * (p + 1)])
        outs = []
        for half in range(2):
            h = 2 * p + half
            kc_t = kc_ref[0, 0, h].astype(BF16)
            vc_t = vc_ref[0, 0, h].astype(BF16)
            kc_t = jnp.concatenate([kc_t, zpad] if half == 0 else [zpad, kc_t], axis=0)
            vc_t = _with_ones(jnp.concatenate([vc_t, zpad] if half == 0 else [zpad, vc_t], axis=0), transposed=True)
            s_loc = _dot_nt(qp, _keep_half(kp, lo_k, half)) + bias_s[h]
            s_ctx = _dot(qp, kc_t)
            outs.append(_softmax_pv([(s_loc, vp, False), (s_ctx, vc_t, True)], None))
        o_ref[0, :, LANES * p: LANES * (p + 1)] = jnp.where(lo_q, outs[0], outs[1]).astype(BF16)


def _nbr_attn_call(att, rel_tab, cache_k, cache_v, layer):
    b, n, w = att.shape
    past = cache_k.shape[4]
    nq = NBR_Q_ROWS * GRID_W
    nk = NBR_K_ROWS * GRID_W
    gw = H_C * HEAD_DIM
    rows = n // GRID_W
    assert (QC0, KC0, VC0) == (0, gw, 2 * gw) and n % nq == 0 and rows >= NBR_K_ROWS and NBR_K_ROWS % 2 == 0
    cache_spec = pl.BlockSpec((1, 1, H_C, HEAD_DIM, past), lambda g, i: (i, layer, 0, 0, 0))
    return pl.pallas_call(
        functools.partial(_nbr_attn_kernel, n=n),
        grid=(n // nq, b),
        in_specs=[pl.BlockSpec((1, nq, gw), lambda g, i: (i, g, 0)),
                  pl.BlockSpec((1, n, gw), lambda g, i: (i, 0, 1)),
                  pl.BlockSpec((1, n, gw), lambda g, i: (i, 0, 2)),
                  pl.BlockSpec((1,) + rel_tab.shape[1:], lambda g, i: (layer, 0, 0, 0, 0)),
                  cache_spec, cache_spec],
        out_specs=pl.BlockSpec((1, nq, gw), lambda g, i: (i, g, 0)),
        out_shape=jax.ShapeDtypeStruct((b, n, gw), BF16),
        scratch_shapes=[pltpu.VMEM((H_C, nq, nk), F32)],
        compiler_params=_params(2),
        name="nbr_attn",
    )(att, att, att, rel_tab, cache_k, cache_v)


def _outproj_kernel(x_ref, ya_ref, yb_ref, yc_ref, mod_ref, g_ref, wo_ref, wr_ref,
                    xn_ref, h2_ref, aff_ref):
    mix = jnp.concatenate([ya_ref[0], yb_ref[0], yc_ref[0]], axis=1)
    proj = _dot(mix, wo_ref[0])
    xn = x_ref[0] + mod_ref[0, 0, 2:3, :] * proj
    xn_ref[0] = xn
    h2 = _rms_modulate(xn, g_ref[0], mod_ref[0, 0, 3:4, :], mod_ref[0, 0, 4:5, :]).astype(BF16)
    h2_ref[0] = h2
    logits = _dot(h2, wr_ref[0])
    lane = lax.broadcasted_iota(jnp.int32, logits.shape, 1)
    valid = lane < N_EXPERTS
    logits = jnp.where(valid, logits, NEG_INF)
    e = jnp.exp(logits - jnp.max(logits, axis=-1, keepdims=True))
    e = jnp.where(valid, e, 0.0)
    aff_ref[0] = e / jnp.sum(e, axis=-1, keepdims=True)


def _outproj_call(x3, ya, yb, yc, mods, mod_row0, gains, wo_b16, w_router_b, layer):
    g, n, d = x3.shape
    tb = TOKEN_BLOCK
    tok = lambda w: pl.BlockSpec((1, tb, w), lambda i, j: (i, j, 0))
    per_layer = lambda shape: pl.BlockSpec((1,) + shape, lambda i, j: (layer,) + (0,) * len(shape))
    return pl.pallas_call(
        _outproj_kernel,
        grid=(g, n // tb),
        in_specs=[tok(d), tok(D_A), tok(D_B), tok(D_C),
                  pl.BlockSpec((1, 1, N_MOD, d), lambda i, j: (layer, mod_row0 + i, 0, 0)),
                  per_layer((1, d)), per_layer((D_A + D_B + D_C, d)), per_layer((d, LANES))],
        out_specs=[tok(d), tok(d), tok(LANES)],
        out_shape=[jax.ShapeDtypeStruct((g, n, d), F32),
                   jax.ShapeDtypeStruct((g, n, d), BF16),
                   jax.ShapeDtypeStruct((g, n, LANES), F32)],
        compiler_params=_params(2),
        name="outproj_router",
    )(x3, ya, yb, yc, mods, gains, wo_b16, w_router_b)


def _sort_by_affinity(keys, idx, pos, levels, seg_lanes):
    rows = keys.shape[0]
    sign = [jnp.where(((pos >> m) & 1) == 0, 1.0, -1.0) for m in range(levels)]

    def partner(x, j):
        d = 1 << j
        low = sign[j] > 0.0
        if d >= rows:
            sh = (d // rows) * seg_lanes
            return jnp.where(low, pltpu.roll(x, LANES - sh, 1), pltpu.roll(x, sh, 1))
        if d >= 8:
            x4 = x.reshape(rows // (2 * d), 2, d, LANES)
            return jnp.concatenate([x4[:, 1:2], x4[:, 0:1]], axis=1).reshape(rows, LANES)
        x3 = x.reshape(rows // 8, 8, LANES)
        low3 = sign[j].reshape(rows // 8, 8, LANES) > 0.0
        return jnp.where(low3, pltpu.roll(x3, 8 - d, 1), pltpu.roll(x3, d, 1)).reshape(rows, LANES)

    for k in range(1, levels + 1):
        for j in range(k - 1, -1, -1):
            pk, pi = partner(keys, j), partner(idx, j)
            before = jnp.where(keys == pk, jnp.where(idx < pi, 1.0, -1.0), jnp.where(keys > pk, 1.0, -1.0))
            want_first = sign[j] * sign[k] if k < levels else sign[j]
            keep = before * want_first > 0.0
            keys = jnp.where(keep, keys, pk)
            idx = jnp.where(keep, idx, pi)
    return keys, idx


def _route_kernel(aff_ref, h2_ref, xs_ref, vals_ref, rank_ref, *, n, cap, rb, by_slot):
    ne = N_EXPERTS
    rows = ROUTE_SORT_ROWS
    nseg = n // rows
    seg_lanes = rb * ne
    levels = n.bit_length() - 1
    fcap = float(cap)
    row = lax.broadcasted_iota(jnp.int32, (rows, LANES), 0)
    lane = lax.broadcasted_iota(jnp.int32, (rows, LANES), 1)
    pos = (lane // seg_lanes) * rows + row
    keys = jnp.full((rows, LANES), -1.0, F32)
    for r in range(rb):
        for seg in range(nseg):
            off = seg * seg_lanes + r * ne
            piece = aff_ref[r, seg * rows:(seg + 1) * rows, :]
            if off:
                piece = pltpu.roll(piece, off, 1)
            keys = jnp.where((lane >= off) & (lane < off + ne), piece, keys)
    keys, idx = _sort_by_affinity(keys, pos.astype(F32), pos, levels, seg_lanes)
    top_keys = keys[:cap]
    top_idx = idx[:cap]

    tok = lax.broadcasted_iota(jnp.int32, (cap, n), 1).astype(F32)
    weight = fcap - lax.broadcasted_iota(jnp.int32, (cap, n), 0).astype(F32)
    pad = jnp.full((LANES - ne, n), fcap, F32)
    if by_slot:
        top_idx_t = top_idx.T
    for r in range(rb):
        picks, rank_rows = [], []
        for e in range(ne):
            col = r * ne + e
            onehot = jnp.where(top_idx[:, col:col + 1] == tok, 1.0, 0.0)
            vals_ref[e, r] = top_keys[:, col:col + 1]
            if not by_slot:
                rank_rows.append(fcap - jnp.sum(onehot * weight, axis=0, keepdims=True))
            picks.append(onehot.astype(BF16))
        if by_slot:
            rank_ref[r] = top_idx_t[r * ne:(r + 1) * ne]
        else:
            rank_ref[r] = jnp.concatenate(rank_rows + [pad], axis=0).T
        xs = _dot(jnp.concatenate(picks, axis=0), h2_ref[r])
        xs_ref[:, r] = xs.reshape(ne, cap, xs.shape[-1]).astype(BF16)


def _route_call(aff, h2, rb):
    b, n, d = h2.shape
    cap = max(1, EC_FACTOR * n // N_EXPERTS)
    nseg = n // ROUTE_SORT_ROWS
    assert b % rb == 0 and n == nseg * ROUTE_SORT_ROWS and n & (n - 1) == 0
    assert nseg * rb * N_EXPERTS <= LANES and cap <= ROUTE_SORT_ROWS
    by_slot = cap == LANES
    assign_shape = (N_EXPERTS, cap) if by_slot else (n, LANES)
    return pl.pallas_call(
        functools.partial(_route_kernel, n=n, cap=cap, rb=rb, by_slot=by_slot),
        grid=(b // rb,),
        in_specs=[pl.BlockSpec((rb, n, LANES), lambda i: (i, 0, 0)),
                  pl.BlockSpec((rb, n, d), lambda i: (i, 0, 0))],
        out_specs=[pl.BlockSpec((N_EXPERTS, rb, cap, d), lambda i: (0, i, 0, 0)),
                   pl.BlockSpec((N_EXPERTS, rb, cap, 1), lambda i: (0, i, 0, 0)),
                   pl.BlockSpec((rb,) + assign_shape, lambda i: (i, 0, 0))],
        out_shape=[jax.ShapeDtypeStruct((N_EXPERTS, b, cap, d), BF16),
                   jax.ShapeDtypeStruct((N_EXPERTS, b, cap, 1), F32),
                   jax.ShapeDtypeStruct((b,) + assign_shape, F32)],
        compiler_params=_params(1),
        name="route_gather",
    )(aff, h2)


def _expert_kernel(xc_ref, xl_ref, vc_ref, vl_ref, wg_ref, wu_ref, wd_ref, yc_ref, yl_ref, acc_c, acc_l):
    f = pl.program_id(1)

    def step(first):
        wg = wg_ref[0, 0].astype(BF16)
        wu = wu_ref[0, 0].astype(BF16)
        wd = wd_ref[0, 0].astype(BF16)
        for x_ref, v_ref, y_ref, acc in ((xc_ref, vc_ref, yc_ref, acc_c), (xl_ref, vl_ref, yl_ref, acc_l)):
            rows = x_ref.shape[1]
            chunk = min(EXPERT_ROW_CHUNK, rows)
            for c0 in range(0, rows, chunk):
                sl = slice(c0, c0 + chunk)
                x = x_ref[0, sl, :]
                a = _dot(x, wg)
                u = _dot(x, wu)
                act = ((a * jax.nn.sigmoid(a)) * u).astype(BF16)
                part = _dot(act, wd)
                if first:
                    acc[sl, :] = part
                else:
                    y_ref[0, sl, :] = ((acc[sl, :] + part) * v_ref[0, sl, :]).astype(BF16)

    pl.when(f == 0)(functools.partial(step, True))
    pl.when(f == EXPERT_F_STEPS - 1)(functools.partial(step, False))


def _expert_call(xs_c, xs_l, vals_c, vals_l, w_gate, w_up, w_down, layer):
    e, rc, d = xs_c.shape
    rl = xs_l.shape[1]
    f_total = w_gate.shape[-1]
    assert EXPERT_F_STEPS == 2
    fb = f_total // EXPERT_F_STEPS
    per_e = lambda rows, w: pl.BlockSpec((1, rows, w), lambda i, j: (i, 0, 0))
    return pl.pallas_call(
        _expert_kernel,
        grid=(e, EXPERT_F_STEPS),
        in_specs=[per_e(rc, d), per_e(rl, d), per_e(rc, 1), per_e(rl, 1),
                  pl.BlockSpec((1, 1, d, fb), lambda i, j: (layer, i, 0, j)),
                  pl.BlockSpec((1, 1, d, fb), lambda i, j: (layer, i, 0, j)),
                  pl.BlockSpec((1, 1, fb, d), lambda i, j: (layer, i, j, 0))],
        out_specs=[per_e(rc, d), per_e(rl, d)],
        out_shape=[jax.ShapeDtypeStruct((e, rc, d), BF16),
                   jax.ShapeDtypeStruct((e, rl, d), BF16)],
        scratch_shapes=[pltpu.VMEM((rc, d), F32), pltpu.VMEM((rl, d), F32)],
        compiler_params=_params(2),
        name="experts",
    )(xs_c, xs_l, vals_c, vals_l, w_gate, w_up, w_down)


def _combine_kernel(y_ref, rank_ref, xn_ref, mod_ref, gf_ref, o_ref, *, n, cap, final, rb, by_slot):
    ec = N_EXPERTS * cap
    d = xn_ref.shape[-1]
    if by_slot:
        tok = lax.broadcasted_iota(jnp.int32, (n, cap), 0).astype(F32)
    else:
        ce = lax.broadcasted_iota(jnp.int32, (LANES, ec), 1) // cap
        ee = lax.broadcasted_iota(jnp.int32, (LANES, ec), 0)
        expand = jnp.where(ce == ee, 1.0, 0.0).astype(BF16)
        slot = (lax.broadcasted_iota(jnp.int32, (n, ec), 1) % cap).astype(F32)
    for r in range(rb):
        if by_slot:
            chosen = rank_ref[r]
            onehot = jnp.concatenate([jnp.where(tok == chosen[e:e + 1, :], 1.0, 0.0).astype(BF16)
                                      for e in range(N_EXPERTS)], axis=1)
        else:
            rank = rank_ref[r].astype(BF16)
            rexp = _dot(rank, expand)
            onehot = jnp.where(rexp == slot, 1.0, 0.0).astype(BF16)
        moe = _dot(onehot, y_ref[:, r].reshape(ec, d))
        x = xn_ref[r] + mod_ref[0, 0, 5:6, :] * moe
        if final:
            x = (x * lax.rsqrt(jnp.mean(x * x, axis=-1, keepdims=True) + EPS)) * gf_ref[...]
        o_ref[r] = x


def _combine_call(y4, rank, xn, mods, mod_row0, shared_mod, layer, g_final, final):
    e, b, cap, d = y4.shape
    n = xn.shape[1]
    rb = COMBINE_REQS_CTX if shared_mod else 1
    by_slot = rank.shape[1:] == (e, cap) and cap == LANES
    assert b % rb == 0 and (by_slot or rank.shape[1:] == (n, LANES))
    return pl.pallas_call(
        functools.partial(_combine_kernel, n=n, cap=cap, final=final, rb=rb, by_slot=by_slot),
        grid=(b // rb,),
        in_specs=[pl.BlockSpec((e, rb, cap, d), lambda i: (0, i, 0, 0)),
                  pl.BlockSpec((rb,) + rank.shape[1:], lambda i: (i, 0, 0)),
                  pl.BlockSpec((rb, n, d), lambda i: (i, 0, 0)),
                  pl.BlockSpec((1, 1, N_MOD, d), (lambda i: (layer, mod_row0, 0, 0)) if shared_mod
                               else (lambda i: (layer, mod_row0 + i, 0, 0))),
                  pl.BlockSpec((1, d), lambda i: (0, 0))],
        out_specs=pl.BlockSpec((rb, n, d), lambda i: (i, 0, 0)),
        out_shape=jax.ShapeDtypeStruct((b, n, d), F32),
        compiler_params=_params(1),
        name="combine",
    )(y4, rank, xn, mods, g_final)


def _rope_tables(n):
    t = jnp.arange(n, dtype=jnp.int32)
    row = (t // GRID_W).astype(F32)
    col = (t % GRID_W).astype(F32)
    half = HEAD_DIM // 4
    inv = jnp.power(ROPE_BASE, -jnp.arange(half, dtype=F32) / half)
    ang_r = row[:, None] * inv[None, :]
    ang_c = col[:, None] * inv[None, :]
    cos_h = jnp.concatenate([jnp.cos(ang_r)] * 2 + [jnp.cos(ang_c)] * 2, axis=-1)
    sin_h = jnp.concatenate([-jnp.sin(ang_r), jnp.sin(ang_r), -jnp.sin(ang_c), jnp.sin(ang_c)], axis=-1)
    reps = H_B + KV_B
    return jnp.tile(cos_h, (1, reps)), jnp.tile(sin_h, (1, reps))


def _nbr_rel_tables(rpb):
    col = np.arange(GRID_W)
    cs = np.clip(col - NA_COLS // 2, 0, GRID_W - NA_COLS)
    col_mask = (col[None, :] >= cs[:, None]) & (col[None, :] < cs[:, None] + NA_COLS)
    dc_idx = np.clip(col[None, :] - col[:, None], -(NA_COLS - 1), NA_COLS - 1) + (NA_COLS - 1)
    sel_col = (dc_idx[None, :, :] == np.arange(2 * NA_COLS - 1)[:, None, None]).astype(np.float32)
    t = jnp.einsum("lhdc,cqk->lhdqk", rpb.astype(F32), sel_col, precision=lax.Precision.HIGHEST)
    return jnp.where(col_mask, t, NEG_INF)


def _block_diag(w):
    nb, bw = w.shape[-3], w.shape[-1]
    eye = jnp.eye(nb, dtype=w.dtype)
    return (eye[:, None, :, None] * w[..., :, :, None, :]).reshape(w.shape[:-3] + (nb * bw, nb * bw))


def kernel(x_prompt, x_sample, state_lru, cache_k_win, cache_v_win, cache_k_nbr, cache_v_nbr, c, c_ctx, w_mod, b_mod, g_norm1, w_in, conv_w, conv_b, w_gate_r, b_gate_r, w_gate_i, b_gate_i, lru_lambda, sink_logit, nbr_bias, w_out, g_norm2, w_router, w_exp_gate, w_exp_up, w_exp_down, g_final):
    bc, seq, d = x_prompt.shape
    bl, n_lat, _ = x_sample.shape
    depth = w_mod.shape[0]
    assert bl + 1 <= 8 and d == 1024

    cond8 = jnp.zeros((8, d), F32).at[0].set(c_ctx).at[1:1 + bl].set(c)
    mods = _adaln_call(cond8, w_mod, b_mod.reshape(depth, 1, -1)).reshape(depth, 8, N_MOD, d)
    rope_tabs = _rope_tables(n_lat)
    bias_tab = _nbr_rel_tables(nbr_bias)
    ckw_t, cvw_t, ckn_t, cvn_t = (jnp.swapaxes(a, -1, -2) for a in (cache_k_win, cache_v_win, cache_k_nbr, cache_v_nbr))
    gf = g_final.reshape(1, d)
    assert SCALE == 0.125
    new_cols = np.arange(w_in.shape[-1]) - LRU_W
    is_q = ((new_cols >= QC0) & (new_cols < KC0)) | ((new_cols >= QB0) & (new_cols < KB0))
    q_scale = jnp.asarray(np.where(is_q, SCALE, 1.0), F32)

    nbr0 = LRU_W + D_B + 2 * KV_W
    w_in_b = (jnp.concatenate([w_in[..., :LRU_W], w_in[..., nbr0:], w_in[..., LRU_W:nbr0]], axis=-1) * q_scale).astype(BF16)
    wo = w_out.astype(BF16)
    wr_b = jnp.pad(w_router.astype(BF16), ((0, 0), (0, 0), (0, LANES - N_EXPERTS)))
    gate_r, gate_i = _block_diag(w_gate_r), _block_diag(w_gate_i)
    w_gates = jnp.concatenate([gate_r[:, 0], gate_i[:, 0], gate_r[:, 1], gate_i[:, 1]], axis=-1).astype(BF16)
    b_gates = jnp.concatenate([b_gate_r[:, 0], b_gate_i[:, 0], b_gate_r[:, 1], b_gate_i[:, 1]], axis=-1)[:, None, :]
    cb = conv_b[:, None, :]
    g1 = g_norm1[:, None, :]
    g2 = g_norm2[:, None, :]
    sinks = sink_logit.reshape(-1)
    ctx_row, lat_row = 0, 1

    xc = x_prompt.reshape(1, bc * seq, d)
    xl = x_sample
    zeros_state = jnp.zeros((bc, 1, 2, D_A), F32)
    caches = None
    st = None
    for l in range(depth):
        final = l == depth - 1
        ulru_c, qkv_c, *caches = _inproj_call(xc, mods, ctx_row, g1, w_in_b, l, cache_seq=seq, depth=depth,
                                              prev_caches=caches)
        ya_c, st = _lru_call(ulru_c.reshape(bc, seq, LRU_W), conv_w, cb, w_gates, b_gates, lru_lambda, l,
                             zeros_state, 0, st_layer=l, st_depth=depth, prev_state=st)
        yb_c, yc_c = _ctx_attn_call(sinks, qkv_c.reshape(bc, seq, -1), l)
        xn_c, h2_c, aff_c = _outproj_call(xc, ya_c.reshape(1, bc * seq, -1), yb_c.reshape(1, bc * seq, -1),
                                          yc_c.reshape(1, bc * seq, -1), mods, ctx_row, g2, wo, wr_b, l)
        xs_c, vals_c, rank_c = _route_call(aff_c.reshape(bc, seq, LANES), h2_c.reshape(bc, seq, d), ROUTE_REQS_CTX)

        ulru_l, qkv_l = _inproj_call(xl, mods, lat_row, g1, w_in_b, l, rope_tabs=rope_tabs)
        ya_l, _ = _lru_call(ulru_l, conv_w, cb, w_gates, b_gates, lru_lambda, l, state_lru, l)
        yb_l = _win_attn_call(sinks, qkv_l, ckw_t, cvw_t, l)
        yc_l = _nbr_attn_call(qkv_l, bias_tab, ckn_t, cvn_t, l)
        xn_l, h2_l, aff_l = _outproj_call(xl, ya_l, yb_l, yc_l, mods, lat_row, g2, wo, wr_b, l)
        xs_l, vals_l, rank_l = _route_call(aff_l, h2_l, ROUTE_REQS_LAT)

        cap_c, cap_l = xs_c.shape[2], xs_l.shape[2]
        y_c, y_l = _expert_call(xs_c.reshape(N_EXPERTS, bc * cap_c, d), xs_l.reshape(N_EXPERTS, bl * cap_l, d),
                                vals_c.reshape(N_EXPERTS, bc * cap_c, 1), vals_l.reshape(N_EXPERTS, bl * cap_l, 1),
                                w_exp_gate, w_exp_up, w_exp_down, l)
        xc = _combine_call(y_c.reshape(N_EXPERTS, bc, cap_c, d), rank_c, xn_c.reshape(bc, seq, d), mods, ctx_row, True,
                           l, gf, final)
        xl = _combine_call(y_l.reshape(N_EXPERTS, bl, cap_l, d), rank_l, xn_l, mods, lat_row, False, l, gf, final)
        xc = xc.reshape(1, bc * seq, d)

    y_prompt = xc.reshape(bc, seq, d)
    y_sample = xl
    return (y_prompt, y_sample, st, *[jnp.swapaxes(buf, -1, -2) for buf in caches])
```

```python
import functools

import numpy as np
import jax
import jax.numpy as jnp
from jax import lax
from jax.experimental import pallas as pl
from jax.experimental.pallas import tpu as pltpu

F32 = jnp.float32
BF16 = jnp.bfloat16

HEAD_DIM = 64
GRID_W = 64
LRU_C = 8.0
CONV_W = 4
KV_B = 2
G_B = 3
H_B = KV_B * G_B
H_C = 6
WINDOW = 128
WIN_BLK = 128
NA_ROWS = 8
NA_COLS = 16
ROPE_BASE = 10000.0
N_EXPERTS = 16
EC_FACTOR = 2
N_MOD = 6
EPS = 1e-6
NEG_INF = -1e30
SCALE = HEAD_DIM ** -0.5

LANES = 128
VMEM_LIMIT_BYTES = 56 * 1024 * 1024

TOKEN_BLOCK = 1024
MOD_COL_BLOCK = 1536
EXPERT_F_STEPS = 2
EXPERT_ROW_CHUNK = 256
D_A = 256
D_B = H_B * HEAD_DIM
D_C = H_C * HEAD_DIM
KV_W = KV_B * HEAD_DIM
LRU_W = 2 * D_A
QC0, KC0, VC0 = 0, D_C, 2 * D_C
QB0 = 3 * D_C
KB0, VB0, ATT_W = QB0 + D_B, QB0 + D_B + KV_W, QB0 + D_B + 2 * KV_W
D_MIX = D_A + D_B + D_C
MIX_B0, MIX_C0, MIX_A0 = 0, D_B, D_B + D_C
assert MIX_C0 % D_C == 0 and MIX_A0 % D_A == 0
CTX_ATTN_REQS = 4
WIN_Q_BLOCKS = 2
NBR_Q_ROWS = 4
NBR_K_ROWS = 12
ROUTE_REQS_CTX = 8
ROUTE_REQS_LAT = 2
ROUTE_SORT_ROWS = 256
COMBINE_REQS_CTX = 4
LRU_SEG = 8


def _params(n_axes):
    return pltpu.CompilerParams(dimension_semantics=("arbitrary",) * n_axes,
                                vmem_limit_bytes=VMEM_LIMIT_BYTES)


def _dot(a, b):
    return jnp.dot(a, b, preferred_element_type=F32)


def _dot_nt(a, b):
    return lax.dot_general(a, b, (((1,), (1,)), ((), ())), preferred_element_type=F32)


def _adaln_kernel(cond_ref, w_ref, b_ref, o_ref):
    c = cond_ref[...]
    s = c * jax.nn.sigmoid(c)
    o_ref[0] = _dot(s.astype(BF16), w_ref[0].astype(BF16)) + b_ref[0]


def _adaln_call(cond8, w_mod, b_mod3):
    depth, d, n = w_mod.shape
    nb = MOD_COL_BLOCK
    return pl.pallas_call(
        _adaln_kernel,
        grid=(depth, n // nb),
        in_specs=[pl.BlockSpec((8, d), lambda l, j: (0, 0)),
                  pl.BlockSpec((1, d, nb), lambda l, j: (l, 0, j)),
                  pl.BlockSpec((1, 1, nb), lambda l, j: (l, 0, j))],
        out_specs=pl.BlockSpec((1, 8, nb), lambda l, j: (l, 0, j)),
        out_shape=jax.ShapeDtypeStruct((depth, 8, n), F32),
        compiler_params=_params(2),
        name="adaln",
    )(cond8, w_mod, b_mod3)


def _rms_modulate(x, gain, shift, scale):
    y = x * lax.rsqrt(jnp.mean(x * x, axis=-1, keepdims=True) + EPS)
    return (y * gain) * (1.0 + scale) + shift


def _inproj_kernel(*refs, rope, cache, seq):
    if rope:
        x_ref, mod_ref, g_ref, w_ref, cos_ref, sin_ref, ulru_ref, qkv_ref = refs
    elif cache:
        x_ref, mod_ref, g_ref, w_ref = refs[:4]
        ulru_ref, qkv_ref, kw_ref, vw_ref, kn_ref, vn_ref = refs[-6:]
    else:
        x_ref, mod_ref, g_ref, w_ref, ulru_ref, qkv_ref = refs
    x = x_ref[0]
    h = _rms_modulate(x, g_ref[0], mod_ref[0, 0, 0:1, :], mod_ref[0, 0, 1:2, :])
    u = _dot(h.astype(BF16), w_ref[0])
    tb = u.shape[0]
    ulru_ref[0] = u[:, :LRU_W]
    if rope:
        lo, hi = LRU_W + QB0, LRU_W + VB0
        qk = u[:, lo:hi]
        lane = lax.broadcasted_iota(jnp.int32, (tb, hi - lo), 1)
        first = (lane & 31) < 16
        partner = jnp.where(first, pltpu.roll(qk, hi - lo - 16, 1), pltpu.roll(qk, 16, 1))
        qk = qk * cos_ref[...] + partner * sin_ref[...]
        qkv_ref[0, :, :QB0] = u[:, LRU_W:lo].astype(BF16)
        qkv_ref[0, :, QB0:VB0] = qk.astype(BF16)
        qkv_ref[0, :, VB0:] = u[:, hi:].astype(BF16)
    else:
        qkv_ref[0] = u[:, LRU_W:].astype(BF16)
    if cache:
        for r in range(tb // seq):
            rows = slice(r * seq, (r + 1) * seq)
            for ref, c0, heads in ((kw_ref, KB0, KV_B), (vw_ref, VB0, KV_B), (kn_ref, KC0, H_C), (vn_ref, VC0, H_C)):
                for p in range(heads // 2):
                    pair_t = u[rows, LRU_W + c0 + LANES * p: LRU_W + c0 + LANES * (p + 1)].T
                    ref[r, 0, 2 * p] = pair_t[:HEAD_DIM]
                    ref[r, 0, 2 * p + 1] = pair_t[HEAD_DIM:]


def _inproj_call(x3, mods, mod_row0, gains, w_in_b, layer, rope_tabs=None, cache_seq=None, depth=1, prev_caches=None):
    g, n, d = x3.shape
    d_in = w_in_b.shape[2]
    tb = TOKEN_BLOCK
    rope = rope_tabs is not None
    cache = cache_seq is not None
    in_specs = [pl.BlockSpec((1, tb, d), lambda i, j: (i, j, 0)),
                pl.BlockSpec((1, 1, N_MOD, d), lambda i, j: (layer, mod_row0 + i, 0, 0)),
                pl.BlockSpec((1, 1, d), lambda i, j: (layer, 0, 0)),
                pl.BlockSpec((1, d, d_in), lambda i, j: (layer, 0, 0))]
    args = [x3, mods, gains, w_in_b]
    assert d_in == LRU_W + ATT_W
    out_shape = [jax.ShapeDtypeStruct((g, n, LRU_W), F32),
                 jax.ShapeDtypeStruct((g, n, ATT_W), BF16)]
    out_specs = [pl.BlockSpec((1, tb, LRU_W), lambda i, j: (i, j, 0)),
                 pl.BlockSpec((1, tb, ATT_W), lambda i, j: (i, j, 0))]
    if rope:
        in_specs += [pl.BlockSpec((tb, VB0 - QB0), lambda i, j: (j, 0))] * 2
        args += list(rope_tabs)
    aliases = {}
    if cache:
        assert g == 1 and tb % cache_seq == 0
        rb = tb // cache_seq
        nreq = n // cache_seq
        for heads in (KV_B, KV_B, H_C, H_C):
            out_shape.append(jax.ShapeDtypeStruct((nreq, depth, heads, HEAD_DIM, cache_seq), F32))
            out_specs.append(pl.BlockSpec((rb, 1, heads, HEAD_DIM, cache_seq), lambda i, j: (j, layer, 0, 0, 0)))
        if prev_caches is not None:
            for k, buf in enumerate(prev_caches):
                aliases[len(args)] = 2 + k
                in_specs.append(pl.BlockSpec(memory_space=pl.ANY))
                args.append(buf)
    return pl.pallas_call(
        functools.partial(_inproj_kernel, rope=rope, cache=cache, seq=cache_seq),
        grid=(g, n // tb),
        in_specs=in_specs,
        out_specs=out_specs,
        out_shape=out_shape,
        input_output_aliases=aliases,
        compiler_params=_params(2),
        name="inproj_rope" if rope else "inproj_ctx",
    )(*args)


def _lru_kernel(*refs, n):
    u_ref, cw_ref, cb_ref, wg_ref, bg_ref, lam_ref, h0_ref = refs[:7]
    ya_ref, st_ref, a_s, x_s, y_s, hloc_s, ploc_s, hend_s, pend_s, cin_s = refs[-10:]
    c = D_A
    nblk = n // LRU_SEG
    nh = c // LANES
    u = u_ref[0]
    xa = u[:, :c]
    ga = u[:, c:]
    t = lax.broadcasted_iota(jnp.int32, (n, c), 0)
    cw = cw_ref[0]
    xc = cw[2:3] * xa + cb_ref[0]
    xc = xc + cw[0:1] * jnp.where(t >= 2, pltpu.roll(xa, 2, 0), 0.0)
    xc = xc + cw[1:2] * jnp.where(t >= 1, pltpu.roll(xa, 1, 0), 0.0)
    xc = xc + cw[3:4] * jnp.where(t < n - 1, pltpu.roll(xa, n - 1, 0), 0.0)
    gates = _dot(xc.astype(BF16), wg_ref[0]) + bg_ref[0]
    lam = lam_ref[0]
    log_sig = jnp.minimum(lam, 0.0) - jnp.log1p(jnp.exp(-jnp.abs(lam)))
    for d in range(2):
        r = 0.5 * jnp.tanh(0.5 * gates[:, 2 * d * c:(2 * d + 1) * c]) + 0.5
        i = 0.5 * jnp.tanh(0.5 * gates[:, (2 * d + 1) * c:(2 * d + 2) * c]) + 0.5
        log_a = LRU_C * r * log_sig[d:d + 1]
        a = jnp.exp(log_a)
        t_in = jnp.tanh(-log_a) * (1.0 + a * a)
        xin = jnp.where(t_in > 0.0, t_in * lax.rsqrt(t_in), 0.0) * (i * xc)
        for hf in range(nh):
            a_s[hf] = a[:, LANES * hf: LANES * (hf + 1)]
            x_s[hf] = xin[:, LANES * hf: LANES * (hf + 1)]
        order = range(LRU_SEG) if d == 0 else range(LRU_SEG - 1, -1, -1)
        for hf in range(nh):
            h_run = p_run = None
            for s in order:
                a_row = a_s[hf, pl.ds(s, nblk, stride=LRU_SEG), :]
                x_row = x_s[hf, pl.ds(s, nblk, stride=LRU_SEG), :]
                h_run = x_row if h_run is None else a_row * h_run + x_row
                p_run = a_row if p_run is None else a_row * p_run
                hloc_s[d, hf, s] = h_run
                ploc_s[d, hf, s] = p_run
            hend_s[d, hf] = h_run
            pend_s[d, hf] = p_run

    h0 = h0_ref[0, 0]
    init = tuple(h0[d:d + 1, LANES * hf: LANES * (hf + 1)] for d in range(2) for hf in range(nh))

    def body(k, carry):
        out = []
        for idx, cur in enumerate(carry):
            d, hf = divmod(idx, nh)
            kk = k if d == 0 else nblk - 1 - k
            cin_s[d, hf, pl.ds(kk, 1), :] = cur
            out.append(pend_s[d, hf, pl.ds(kk, 1), :] * cur + hend_s[d, hf, pl.ds(kk, 1), :])
        return tuple(out)

    final = lax.fori_loop(0, nblk, body, init)

    y = None
    for d in range(2):
        for hf in range(nh):
            cin = cin_s[d, hf]
            for s in range(LRU_SEG):
                full_h = hloc_s[d, hf, s] + ploc_s[d, hf, s] * cin
                if d == 0:
                    y_s[hf, pl.ds(s, nblk, stride=LRU_SEG), :] = full_h
                else:
                    y_s[hf, pl.ds(s, nblk, stride=LRU_SEG), :] += full_h
    y = jnp.concatenate([y_s[hf] for hf in range(nh)], axis=1) * jax.nn.gelu(ga)
    ya_ref[0] = y.astype(BF16)
    st_ref[0, 0, 0:1, :] = jnp.concatenate(final[:nh], axis=1)
    st_ref[0, 0, 1:2, :] = jnp.concatenate(final[nh:], axis=1)


def _lru_call(ulru, conv_w, conv_b, w_gates_b, b_gates, lam, layer, h0, h0_layer, st_layer=0, st_depth=1,
              prev_state=None):
    b, n, _ = ulru.shape
    c = D_A
    per_layer = lambda shape: pl.BlockSpec((1,) + shape, lambda i: (layer,) + (0,) * len(shape))
    nh, nblk = c // LANES, n // LRU_SEG
    assert n % LRU_SEG == 0 and nblk % 8 == 0
    scratch = ([pltpu.VMEM((nh, n, LANES), F32)] * 3
               + [pltpu.VMEM((2, nh, LRU_SEG, nblk, LANES), F32)] * 2
               + [pltpu.VMEM((2, nh, nblk, LANES), F32)] * 3)
    in_specs = [pl.BlockSpec((1, n, 2 * c), lambda i: (i, 0, 0)),
                per_layer((CONV_W, c)), per_layer((1, c)), per_layer((c, 4 * c)), per_layer((1, 4 * c)),
                per_layer((2, c)),
                pl.BlockSpec((1, 1, 2, c), lambda i: (i, h0_layer, 0, 0))]
    args = [ulru, conv_w, conv_b, w_gates_b, b_gates, lam, h0]
    aliases = {}
    if prev_state is not None:
        aliases[len(args)] = 1
        in_specs.append(pl.BlockSpec(memory_space=pl.ANY))
        args.append(prev_state)
    return pl.pallas_call(
        functools.partial(_lru_kernel, n=n),
        grid=(b,),
        in_specs=in_specs,
        out_specs=[pl.BlockSpec((1, n, c), lambda i: (i, 0, MIX_A0 // c)),
                   pl.BlockSpec((1, 1, 2, c), lambda i: (i, st_layer, 0, 0))],
        out_shape=[jax.ShapeDtypeStruct((b, n, D_MIX), BF16),
                   jax.ShapeDtypeStruct((b, st_depth, 2, c), F32)],
        scratch_shapes=scratch,
        input_output_aliases=aliases,
        compiler_params=_params(1),
        name="rglru",
    )(*args)


def _with_ones(v, transposed=False):
    axis = 0 if transposed else 1
    w = v.shape[axis]
    pieces = [v]
    if w < LANES:
        pieces.append(jnp.zeros(v.shape[:axis] + (LANES - w,) + v.shape[axis + 1:], v.dtype))
    pieces.append(jnp.ones(v.shape[:axis] + (LANES,) + v.shape[axis + 1:], v.dtype))
    return jnp.concatenate(pieces, axis=axis)


def _softmax_pv(parts, sink):
    mx = None
    for s, _, _ in parts:
        cur = jnp.max(s, axis=-1, keepdims=True)
        mx = cur if mx is None else jnp.maximum(mx, cur)
    if sink is not None:
        mx = jnp.maximum(mx, sink)
    out = None
    for s, v, v_t in parts:
        p = jnp.exp(s - mx).astype(BF16)
        o = _dot_nt(p, v) if v_t else _dot(p, v)
        out = o if out is None else out + o
    den = out[:, LANES:]
    if sink is not None:
        den = den + jnp.exp(sink - mx)
    return out[:, :LANES] / den


def _swap_halves(x):
    return jnp.concatenate([x[:, HEAD_DIM:], x[:, :HEAD_DIM]], axis=1)


def _keep_half(x, lo_mask, half):
    zero = jnp.zeros_like(x)
    return jnp.where(lo_mask, x, zero) if half == 0 else jnp.where(lo_mask, zero, x)


def _ctx_attn_kernel(sink_ref, att_ref, _mix_ref, o_ref, *, n, rb, sink0):
    lo = lax.broadcasted_iota(jnp.int32, (n, LANES), 1) < HEAD_DIM
    for r in range(rb):
        for p in range(H_C // 2):
            qp = att_ref[r, :, QC0 + LANES * p: QC0 + LANES * (p + 1)]
            kp = att_ref[r, :, KC0 + LANES * p: KC0 + LANES * (p + 1)]
            vp = _with_ones(att_ref[r, :, VC0 + LANES * p: VC0 + LANES * (p + 1)])
            outs = [_softmax_pv([(_dot_nt(qp, _keep_half(kp, lo, half)), vp, False)], None) for half in range(2)]
            o_ref[r, :, MIX_C0 + LANES * p: MIX_C0 + LANES * (p + 1)] = jnp.where(lo, outs[0], outs[1]).astype(BF16)
        kpair = att_ref[r, :, KB0: KB0 + LANES]
        vpair = att_ref[r, :, VB0: VB0 + LANES]
        kpair_sw = _swap_halves(kpair)
        vpair_ext = _with_ones(vpair)
        vpair_sw_ext = _with_ones(_swap_halves(vpair))
        for p in range(H_B // 2):
            qp = att_ref[r, :, QB0 + LANES * p: QB0 + LANES * (p + 1)]
            outs = []
            for half in range(2):
                h = 2 * p + half
                aligned = (h // G_B) == half
                ksrc, vsrc = (kpair, vpair_ext) if aligned else (kpair_sw, vpair_sw_ext)
                outs.append(_softmax_pv([(_dot_nt(qp, _keep_half(ksrc, lo, half)), vsrc, False)], sink_ref[sink0 + h]))
            o_ref[r, :, MIX_B0 + LANES * p: MIX_B0 + LANES * (p + 1)] = jnp.where(lo, outs[0], outs[1]).astype(BF16)


def _ctx_attn_call(sinks, att, layer, mix):
    b, n, w = att.shape
    rb = CTX_ATTN_REQS
    assert b % rb == 0 and KV_B == 2 and (MIX_B0, MIX_C0) == (0, D_B)
    return pl.pallas_call(
        functools.partial(_ctx_attn_kernel, n=n, rb=rb, sink0=layer * H_B),
        grid=(b // rb,),
        in_specs=[pl.BlockSpec(memory_space=pltpu.SMEM),
                  pl.BlockSpec((rb, n, w), lambda i: (i, 0, 0)),
                  pl.BlockSpec(memory_space=pl.ANY)],
        out_specs=pl.BlockSpec((rb, n, D_B + D_C), lambda i: (i, 0, 0)),
        out_shape=jax.ShapeDtypeStruct(mix.shape, BF16),
        input_output_aliases={2: 0},
        compiler_params=_params(1),
        name="ctx_attn",
    )(sinks, att, mix)


def _win_attn_kernel(sink_ref, qkv_ref, kc_ref, vc_ref, _mix_ref, o_ref, *, n, sink0):
    j = pl.program_id(1)
    nq = WIN_Q_BLOCKS * WIN_BLK
    nloc = nq + 2 * WIN_BLK
    q0 = pl.multiple_of(j * nq, WIN_BLK)
    ks = pl.multiple_of(jnp.clip(j * nq - WIN_BLK, 0, n - nloc), WIN_BLK)
    row = lax.broadcasted_iota(jnp.int32, (nq, nloc), 0)
    col = lax.broadcasted_iota(jnp.int32, (nq, nloc), 1)
    in_window = jnp.where(jnp.abs((q0 + row) - (ks + col)) <= WINDOW, 0.0, NEG_INF)
    band = jnp.concatenate([in_window] * G_B, axis=0)
    rowh = lax.broadcasted_iota(jnp.int32, (G_B * nq, 1), 0) // nq
    for kv in range(KV_B):
        gw = G_B * HEAD_DIM
        qblk = qkv_ref[0, pl.ds(q0, nq), QB0 + gw * kv: QB0 + gw * (kv + 1)]
        q3 = jnp.concatenate([qblk[:, 64 * g: 64 * (g + 1)] for g in range(G_B)], axis=0)
        kl = qkv_ref[0, pl.ds(ks, nloc), KB0 + 64 * kv: KB0 + 64 * (kv + 1)]
        vl = _with_ones(qkv_ref[0, pl.ds(ks, nloc), VB0 + 64 * kv: VB0 + 64 * (kv + 1)])
        kc_t = kc_ref[0, 0, kv].astype(BF16)
        vc_t = _with_ones(vc_ref[0, 0, kv].astype(BF16), transposed=True)
        s_loc = _dot_nt(q3, kl) + band
        s_ctx = _dot(q3, kc_t)
        s0 = sink0 + G_B * kv
        sk = jnp.where(rowh == 0, sink_ref[s0], jnp.where(rowh == 1, sink_ref[s0 + 1], sink_ref[s0 + 2]))
        o = _softmax_pv([(s_loc, vl, False), (s_ctx, vc_t, True)], sk)
        for g in range(G_B):
            h = kv * G_B + g
            o_ref[0, :, 64 * h: 64 * (h + 1)] = o[g * nq:(g + 1) * nq, :HEAD_DIM].astype(BF16)


def _win_attn_call(sinks, qkv, cache_k, cache_v, layer, mix):
    b, n, w = qkv.shape
    past = cache_k.shape[4]
    cache_spec = pl.BlockSpec((1, 1, KV_B, HEAD_DIM, past), lambda i, j: (i, layer, 0, 0, 0))
    nq = WIN_Q_BLOCKS * WIN_BLK
    assert n % nq == 0 and n >= nq + 2 * WIN_BLK and WINDOW == WIN_BLK
    return pl.pallas_call(
        functools.partial(_win_attn_kernel, n=n, sink0=layer * H_B),
        grid=(b, n // nq),
        in_specs=[pl.BlockSpec(memory_space=pltpu.SMEM),
                  pl.BlockSpec((1, n, w), lambda i, j: (i, 0, 0)),
                  cache_spec, cache_spec, pl.BlockSpec(memory_space=pl.ANY)],
        out_specs=pl.BlockSpec((1, nq, D_B), lambda i, j: (i, j, MIX_B0 // D_B)),
        out_shape=jax.ShapeDtypeStruct(mix.shape, BF16),
        input_output_aliases={4: 0},
        compiler_params=_params(2),
        name="win_attn",
    )(sinks, qkv, cache_k, cache_v, mix)


def _nbr_window_start(g, rows):
    return jnp.clip(g * NBR_Q_ROWS - NA_ROWS // 2, 0, rows - NBR_K_ROWS)


def _nbr_attn_kernel(q_ref, k_ref, v_ref, rel_ref, kc_ref, vc_ref, _mix_ref, o_ref, bias_s, *, n):
    g = pl.program_id(0)
    nq = NBR_Q_ROWS * GRID_W
    nk = NBR_K_ROWS * GRID_W
    rows = n // GRID_W
    ws = _nbr_window_start(g, rows)
    k0 = pl.multiple_of(ws * GRID_W, GRID_W)

    @pl.when(pl.program_id(1) == 0)
    def _():
        for i in range(NBR_Q_ROWS):
            qr = g * NBR_Q_ROWS + i
            rs = jnp.clip(qr - NA_ROWS // 2, 0, rows - NA_ROWS)
            for jp in range(NBR_K_ROWS // 2):
                tiles = []
                for j in (2 * jp, 2 * jp + 1):
                    kr = ws + j
                    valid = (kr >= rs) & (kr < rs + NA_ROWS)
                    d = jnp.clip(kr - qr + NA_ROWS - 1, 0, 2 * NA_ROWS - 2)
                    tiles.append(jnp.where(valid, rel_ref[0, :, pl.ds(d, 1)][:, 0], NEG_INF))
                bias_s[:, GRID_W * i: GRID_W * (i + 1), LANES * jp: LANES * (jp + 1)] = jnp.concatenate(tiles, axis=-1)
    lo_k = lax.broadcasted_iota(jnp.int32, (nk, LANES), 1) < HEAD_DIM
    lo_q = lax.broadcasted_iota(jnp.int32, (nq, LANES), 1) < HEAD_DIM
    zpad = jnp.zeros((HEAD_DIM, kc_ref.shape[4]), BF16)
    for p in range(H_C // 2):
        qp = q_ref[0, :, LANES * p: LANES * (p + 1)]
        kp = k_ref[0, pl.ds(k0, nk), LANES * p: LANES * (p + 1)]
        vp = _with_ones(v_ref[0, pl.ds(k0, nk), LANES * p: LANES * (p + 1)])
        outs = []
        for half in range(2):
            h = 2 * p + half
            kc_t = kc_ref[0, 0, h].astype(BF16)
            vc_t = vc_ref[0, 0, h].astype(BF16)
            kc_t = jnp.concatenate([kc_t, zpad] if half == 0 else [zpad, kc_t], axis=0)
            vc_t = _with_ones(jnp.concatenate([vc_t, zpad] if half == 0 else [zpad, vc_t], axis=0), transposed=True)
            s_loc = _dot_nt(qp, _keep_half(kp, lo_k, half)) + bias_s[h]
            s_ctx = _dot(qp, kc_t)
            outs.append(_softmax_pv([(s_loc, vp, False), (s_ctx, vc_t, True)], None))
        o_ref[0, :, LANES * p: LANES * (p + 1)] = jnp.where(lo_q, outs[0], outs[1]).astype(BF16)


def _nbr_attn_call(att, rel_tab, cache_k, cache_v, layer, mix):
    b, n, w = att.shape
    past = cache_k.shape[4]
    nq = NBR_Q_ROWS * GRID_W
    nk = NBR_K_ROWS * GRID_W
    gw = H_C * HEAD_DIM
    rows = n // GRID_W
    assert (QC0, KC0, VC0) == (0, gw, 2 * gw) and n % nq == 0 and rows >= NBR_K_ROWS and NBR_K_ROWS % 2 == 0
    cache_spec = pl.BlockSpec((1, 1, H_C, HEAD_DIM, past), lambda g, i: (i, layer, 0, 0, 0))
    return pl.pallas_call(
        functools.partial(_nbr_attn_kernel, n=n),
        grid=(n // nq, b),
        in_specs=[pl.BlockSpec((1, nq, gw), lambda g, i: (i, g, 0)),
                  pl.BlockSpec((1, n, gw), lambda g, i: (i, 0, 1)),
                  pl.BlockSpec((1, n, gw), lambda g, i: (i, 0, 2)),
                  pl.BlockSpec((1,) + rel_tab.shape[1:], lambda g, i: (layer, 0, 0, 0, 0)),
                  cache_spec, cache_spec, pl.BlockSpec(memory_space=pl.ANY)],
        out_specs=pl.BlockSpec((1, nq, gw), lambda g, i: (i, g, MIX_C0 // gw)),
        out_shape=jax.ShapeDtypeStruct(mix.shape, BF16),
        scratch_shapes=[pltpu.VMEM((H_C, nq, nk), F32)],
        input_output_aliases={6: 0},
        compiler_params=_params(2),
        name="nbr_attn",
    )(att, att, att, rel_tab, cache_k, cache_v, mix)


def _outproj_kernel(x_ref, mix_ref, mod_ref, g_ref, wo_ref, wr_ref,
                    xn_ref, h2_ref, aff_ref):
    proj = _dot(mix_ref[0], wo_ref[0])
    xn = x_ref[0] + mod_ref[0, 0, 2:3, :] * proj
    xn_ref[0] = xn
    h2 = _rms_modulate(xn, g_ref[0], mod_ref[0, 0, 3:4, :], mod_ref[0, 0, 4:5, :]).astype(BF16)
    h2_ref[0] = h2
    logits = _dot(h2, wr_ref[0])
    lane = lax.broadcasted_iota(jnp.int32, logits.shape, 1)
    valid = lane < N_EXPERTS
    logits = jnp.where(valid, logits, NEG_INF)
    e = jnp.exp(logits - jnp.max(logits, axis=-1, keepdims=True))
    e = jnp.where(valid, e, 0.0)
    aff_ref[0] = e / jnp.sum(e, axis=-1, keepdims=True)


def _outproj_call(x3, mix, mods, mod_row0, gains, wo_b16, w_router_b, layer):
    g, n, d = x3.shape
    tb = TOKEN_BLOCK
    tok = lambda w: pl.BlockSpec((1, tb, w), lambda i, j: (i, j, 0))
    per_layer = lambda shape: pl.BlockSpec((1,) + shape, lambda i, j: (layer,) + (0,) * len(shape))
    return pl.pallas_call(
        _outproj_kernel,
        grid=(g, n // tb),
        in_specs=[tok(d), tok(D_MIX),
                  pl.BlockSpec((1, 1, N_MOD, d), lambda i, j: (layer, mod_row0 + i, 0, 0)),
                  per_layer((1, d)), per_layer((D_MIX, d)), per_layer((d, LANES))],
        out_specs=[tok(d), tok(d), tok(LANES)],
        out_shape=[jax.ShapeDtypeStruct((g, n, d), F32),
                   jax.ShapeDtypeStruct((g, n, d), BF16),
                   jax.ShapeDtypeStruct((g, n, LANES), F32)],
        compiler_params=_params(2),
        name="outproj_router",
    )(x3, mix, mods, gains, wo_b16, w_router_b)


def _sort_by_affinity(keys, idx, pos, levels, seg_lanes):
    rows = keys.shape[0]
    sign = [jnp.where(((pos >> m) & 1) == 0, 1.0, -1.0) for m in range(levels)]

    def partner(x, j):
        d = 1 << j
        low = sign[j] > 0.0
        if d >= rows:
            sh = (d // rows) * seg_lanes
            return jnp.where(low, pltpu.roll(x, LANES - sh, 1), pltpu.roll(x, sh, 1))
        if d >= 8:
            x4 = x.reshape(rows // (2 * d), 2, d, LANES)
            return jnp.concatenate([x4[:, 1:2], x4[:, 0:1]], axis=1).reshape(rows, LANES)
        x3 = x.reshape(rows // 8, 8, LANES)
        low3 = sign[j].reshape(rows // 8, 8, LANES) > 0.0
        return jnp.where(low3, pltpu.roll(x3, 8 - d, 1), pltpu.roll(x3, d, 1)).reshape(rows, LANES)

    for k in range(1, levels + 1):
        for j in range(k - 1, -1, -1):
            pk, pi = partner(keys, j), partner(idx, j)
            before = jnp.where(keys == pk, jnp.where(idx < pi, 1.0, -1.0), jnp.where(keys > pk, 1.0, -1.0))
            want_first = sign[j] * sign[k] if k < levels else sign[j]
            keep = before * want_first > 0.0
            keys = jnp.where(keep, keys, pk)
            idx = jnp.where(keep, idx, pi)
    return keys, idx


def _route_kernel(aff_ref, h2_ref, xs_ref, vals_ref, rank_ref, *, n, cap, rb, by_slot):
    ne = N_EXPERTS
    rows = ROUTE_SORT_ROWS
    nseg = n // rows
    seg_lanes = rb * ne
    levels = n.bit_length() - 1
    fcap = float(cap)
    row = lax.broadcasted_iota(jnp.int32, (rows, LANES), 0)
    lane = lax.broadcasted_iota(jnp.int32, (rows, LANES), 1)
    pos = (lane // seg_lanes) * rows + row
    keys = jnp.full((rows, LANES), -1.0, F32)
    for r in range(rb):
        for seg in range(nseg):
            off = seg * seg_lanes + r * ne
            piece = aff_ref[r, seg * rows:(seg + 1) * rows, :]
            if off:
                piece = pltpu.roll(piece, off, 1)
            keys = jnp.where((lane >= off) & (lane < off + ne), piece, keys)
    keys, idx = _sort_by_affinity(keys, pos.astype(F32), pos, levels, seg_lanes)
    top_keys = keys[:cap]
    top_idx = idx[:cap]

    tok = lax.broadcasted_iota(jnp.int32, (cap, n), 1).astype(F32)
    weight = fcap - lax.broadcasted_iota(jnp.int32, (cap, n), 0).astype(F32)
    pad = jnp.full((LANES - ne, n), fcap, F32)
    if by_slot:
        top_idx_t = top_idx.T
    for r in range(rb):
        picks, rank_rows = [], []
        for e in range(ne):
            col = r * ne + e
            onehot = jnp.where(top_idx[:, col:col + 1] == tok, 1.0, 0.0)
            vals_ref[e, r] = top_keys[:, col:col + 1]
            if not by_slot:
                rank_rows.append(fcap - jnp.sum(onehot * weight, axis=0, keepdims=True))
            picks.append(onehot.astype(BF16))
        if by_slot:
            rank_ref[r] = top_idx_t[r * ne:(r + 1) * ne]
        else:
            rank_ref[r] = jnp.concatenate(rank_rows + [pad], axis=0).T
        xs = _dot(jnp.concatenate(picks, axis=0), h2_ref[r])
        xs_ref[:, r] = xs.reshape(ne, cap, xs.shape[-1]).astype(BF16)


def _route_call(aff, h2, rb):
    b, n, d = h2.shape
    cap = max(1, EC_FACTOR * n // N_EXPERTS)
    nseg = n // ROUTE_SORT_ROWS
    assert b % rb == 0 and n == nseg * ROUTE_SORT_ROWS and n & (n - 1) == 0
    assert nseg * rb * N_EXPERTS <= LANES and cap <= ROUTE_SORT_ROWS
    by_slot = cap == LANES
    assign_shape = (N_EXPERTS, cap) if by_slot else (n, LANES)
    return pl.pallas_call(
        functools.partial(_route_kernel, n=n, cap=cap, rb=rb, by_slot=by_slot),
        grid=(b // rb,),
        in_specs=[pl.BlockSpec((rb, n, LANES), lambda i: (i, 0, 0)),
                  pl.BlockSpec((rb, n, d), lambda i: (i, 0, 0))],
        out_specs=[pl.BlockSpec((N_EXPERTS, rb, cap, d), lambda i: (0, i, 0, 0)),
                   pl.BlockSpec((N_EXPERTS, rb, cap, 1), lambda i: (0, i, 0, 0)),
                   pl.BlockSpec((rb,) + assign_shape, lambda i: (i, 0, 0))],
        out_shape=[jax.ShapeDtypeStruct((N_EXPERTS, b, cap, d), BF16),
                   jax.ShapeDtypeStruct((N_EXPERTS, b, cap, 1), F32),
                   jax.ShapeDtypeStruct((b,) + assign_shape, F32)],
        compiler_params=_params(1),
        name="route_gather",
    )(aff, h2)


def _expert_kernel(xc_ref, xl_ref, vc_ref, vl_ref, wg_ref, wu_ref, wd_ref, yc_ref, yl_ref, acc_c, acc_l):
    f = pl.program_id(1)

    def step(first):
        wg = wg_ref[0, 0].astype(BF16)
        wu = wu_ref[0, 0].astype(BF16)
        wd = wd_ref[0, 0].astype(BF16)
        for x_ref, v_ref, y_ref, acc in ((xc_ref, vc_ref, yc_ref, acc_c), (xl_ref, vl_ref, yl_ref, acc_l)):
            rows = x_ref.shape[1]
            chunk = min(EXPERT_ROW_CHUNK, rows)
            for c0 in range(0, rows, chunk):
                sl = slice(c0, c0 + chunk)
                x = x_ref[0, sl, :]
                a = _dot(x, wg)
                u = _dot(x, wu)
                act = ((a * jax.nn.sigmoid(a)) * u).astype(BF16)
                part = _dot(act, wd)
                if first:
                    acc[sl, :] = part
                else:
                    y_ref[0, sl, :] = ((acc[sl, :] + part) * v_ref[0, sl, :]).astype(BF16)

    pl.when(f == 0)(functools.partial(step, True))
    pl.when(f == EXPERT_F_STEPS - 1)(functools.partial(step, False))


def _expert_call(xs_c, xs_l, vals_c, vals_l, w_gate, w_up, w_down, layer):
    e, rc, d = xs_c.shape
    rl = xs_l.shape[1]
    f_total = w_gate.shape[-1]
    assert EXPERT_F_STEPS == 2
    fb = f_total // EXPERT_F_STEPS
    per_e = lambda rows, w: pl.BlockSpec((1, rows, w), lambda i, j: (i, 0, 0))
    return pl.pallas_call(
        _expert_kernel,
        grid=(e, EXPERT_F_STEPS),
        in_specs=[per_e(rc, d), per_e(rl, d), per_e(rc, 1), per_e(rl, 1),
                  pl.BlockSpec((1, 1, d, fb), lambda i, j: (layer, i, 0, j)),
                  pl.BlockSpec((1, 1, d, fb), lambda i, j: (layer, i, 0, j)),
                  pl.BlockSpec((1, 1, fb, d), lambda i, j: (layer, i, j, 0))],
        out_specs=[per_e(rc, d), per_e(rl, d)],
        out_shape=[jax.ShapeDtypeStruct((e, rc, d), BF16),
                   jax.ShapeDtypeStruct((e, rl, d), BF16)],
        scratch_shapes=[pltpu.VMEM((rc, d), F32), pltpu.VMEM((rl, d), F32)],
        compiler_params=_params(2),
        name="experts",
    )(xs_c, xs_l, vals_c, vals_l, w_gate, w_up, w_down)


def _combine_kernel(y_ref, rank_ref, xn_ref, mod_ref, gf_ref, o_ref, *, n, cap, final, rb, by_slot):
    ec = N_EXPERTS * cap
    d = xn_ref.shape[-1]
    if by_slot:
        tok = lax.broadcasted_iota(jnp.int32, (n, cap), 0).astype(F32)
    else:
        ce = lax.broadcasted_iota(jnp.int32, (LANES, ec), 1) // cap
        ee = lax.broadcasted_iota(jnp.int32, (LANES, ec), 0)
        expand = jnp.where(ce == ee, 1.0, 0.0).astype(BF16)
        slot = (lax.broadcasted_iota(jnp.int32, (n, ec), 1) % cap).astype(F32)
    for r in range(rb):
        if by_slot:
            chosen = rank_ref[r]
            onehot = jnp.concatenate([jnp.where(tok == chosen[e:e + 1, :], 1.0, 0.0).astype(BF16)
                                      for e in range(N_EXPERTS)], axis=1)
        else:
            rank = rank_ref[r].astype(BF16)
            rexp = _dot(rank, expand)
            onehot = jnp.where(rexp == slot, 1.0, 0.0).astype(BF16)
        moe = _dot(onehot, y_ref[:, r].reshape(ec, d))
        x = xn_ref[r] + mod_ref[0, 0, 5:6, :] * moe
        if final:
            x = (x * lax.rsqrt(jnp.mean(x * x, axis=-1, keepdims=True) + EPS)) * gf_ref[...]
        o_ref[r] = x


def _combine_call(y4, rank, xn, mods, mod_row0, shared_mod, layer, g_final, final):
    e, b, cap, d = y4.shape
    n = xn.shape[1]
    rb = COMBINE_REQS_CTX if shared_mod else 1
    by_slot = rank.shape[1:] == (e, cap) and cap == LANES
    assert b % rb == 0 and (by_slot or rank.shape[1:] == (n, LANES))
    return pl.pallas_call(
        functools.partial(_combine_kernel, n=n, cap=cap, final=final, rb=rb, by_slot=by_slot),
        grid=(b // rb,),
        in_specs=[pl.BlockSpec((e, rb, cap, d), lambda i: (0, i, 0, 0)),
                  pl.BlockSpec((rb,) + rank.shape[1:], lambda i: (i, 0, 0)),
                  pl.BlockSpec((rb, n, d), lambda i: (i, 0, 0)),
                  pl.BlockSpec((1, 1, N_MOD, d), (lambda i: (layer, mod_row0, 0, 0)) if shared_mod
                               else (lambda i: (layer, mod_row0 + i, 0, 0))),
                  pl.BlockSpec((1, d), lambda i: (0, 0))],
        out_specs=pl.BlockSpec((rb, n, d), lambda i: (i, 0, 0)),
        out_shape=jax.ShapeDtypeStruct((b, n, d), F32),
        compiler_params=_params(1),
        name="combine",
    )(y4, rank, xn, mods, g_final)


def _rope_tables(n):
    t = jnp.arange(n, dtype=jnp.int32)
    row = (t // GRID_W).astype(F32)
    col = (t % GRID_W).astype(F32)
    half = HEAD_DIM // 4
    inv = jnp.power(ROPE_BASE, -jnp.arange(half, dtype=F32) / half)
    ang_r = row[:, None] * inv[None, :]
    ang_c = col[:, None] * inv[None, :]
    cos_h = jnp.concatenate([jnp.cos(ang_r)] * 2 + [jnp.cos(ang_c)] * 2, axis=-1)
    sin_h = jnp.concatenate([-jnp.sin(ang_r), jnp.sin(ang_r), -jnp.sin(ang_c), jnp.sin(ang_c)], axis=-1)
    reps = H_B + KV_B
    return jnp.tile(cos_h, (1, reps)), jnp.tile(sin_h, (1, reps))


def _nbr_rel_tables(rpb):
    col = np.arange(GRID_W)
    cs = np.clip(col - NA_COLS // 2, 0, GRID_W - NA_COLS)
    col_mask = (col[None, :] >= cs[:, None]) & (col[None, :] < cs[:, None] + NA_COLS)
    dc_idx = np.clip(col[None, :] - col[:, None], -(NA_COLS - 1), NA_COLS - 1) + (NA_COLS - 1)
    sel_col = (dc_idx[None, :, :] == np.arange(2 * NA_COLS - 1)[:, None, None]).astype(np.float32)
    t = jnp.einsum("lhdc,cqk->lhdqk", rpb.astype(F32), sel_col, precision=lax.Precision.HIGHEST)
    return jnp.where(col_mask, t, NEG_INF)


def _block_diag(w):
    nb, bw = w.shape[-3], w.shape[-1]
    eye = jnp.eye(nb, dtype=w.dtype)
    return (eye[:, None, :, None] * w[..., :, :, None, :]).reshape(w.shape[:-3] + (nb * bw, nb * bw))


def kernel(x_prompt, x_sample, state_lru, cache_k_win, cache_v_win, cache_k_nbr, cache_v_nbr, c, c_ctx, w_mod, b_mod, g_norm1, w_in, conv_w, conv_b, w_gate_r, b_gate_r, w_gate_i, b_gate_i, lru_lambda, sink_logit, nbr_bias, w_out, g_norm2, w_router, w_exp_gate, w_exp_up, w_exp_down, g_final):
    bc, seq, d = x_prompt.shape
    bl, n_lat, _ = x_sample.shape
    depth = w_mod.shape[0]
    assert bl + 1 <= 8 and d == 1024

    cond8 = jnp.zeros((8, d), F32).at[0].set(c_ctx).at[1:1 + bl].set(c)
    mods = _adaln_call(cond8, w_mod, b_mod.reshape(depth, 1, -1)).reshape(depth, 8, N_MOD, d)
    rope_tabs = _rope_tables(n_lat)
    bias_tab = _nbr_rel_tables(nbr_bias)
    ckw_t, cvw_t, ckn_t, cvn_t = (jnp.swapaxes(a, -1, -2) for a in (cache_k_win, cache_v_win, cache_k_nbr, cache_v_nbr))
    gf = g_final.reshape(1, d)
    assert SCALE == 0.125
    new_cols = np.arange(w_in.shape[-1]) - LRU_W
    is_q = ((new_cols >= QC0) & (new_cols < KC0)) | ((new_cols >= QB0) & (new_cols < KB0))
    q_scale = jnp.asarray(np.where(is_q, SCALE, 1.0), F32)

    nbr0 = LRU_W + D_B + 2 * KV_W
    w_in_b = (jnp.concatenate([w_in[..., :LRU_W], w_in[..., nbr0:], w_in[..., LRU_W:nbr0]], axis=-1) * q_scale).astype(BF16)
    wo = jnp.concatenate([w_out[:, D_A:], w_out[:, :D_A]], axis=1).astype(BF16)
    wr_b = jnp.pad(w_router.astype(BF16), ((0, 0), (0, 0), (0, LANES - N_EXPERTS)))
    gate_r, gate_i = _block_diag(w_gate_r), _block_diag(w_gate_i)
    w_gates = jnp.concatenate([gate_r[:, 0], gate_i[:, 0], gate_r[:, 1], gate_i[:, 1]], axis=-1).astype(BF16)
    b_gates = jnp.concatenate([b_gate_r[:, 0], b_gate_i[:, 0], b_gate_r[:, 1], b_gate_i[:, 1]], axis=-1)[:, None, :]
    cb = conv_b[:, None, :]
    g1 = g_norm1[:, None, :]
    g2 = g_norm2[:, None, :]
    sinks = sink_logit.reshape(-1)
    ctx_row, lat_row = 0, 1

    xc = x_prompt.reshape(1, bc * seq, d)
    xl = x_sample
    zeros_state = jnp.zeros((bc, 1, 2, D_A), F32)
    caches = None
    st = None
    for l in range(depth):
        final = l == depth - 1
        ulru_c, qkv_c, *caches = _inproj_call(xc, mods, ctx_row, g1, w_in_b, l, cache_seq=seq, depth=depth,
                                              prev_caches=caches)
        mix_c, st = _lru_call(ulru_c.reshape(bc, seq, LRU_W), conv_w, cb, w_gates, b_gates, lru_lambda, l,
                              zeros_state, 0, st_layer=l, st_depth=depth, prev_state=st)
        mix_c = _ctx_attn_call(sinks, qkv_c.reshape(bc, seq, -1), l, mix_c)
        xn_c, h2_c, aff_c = _outproj_call(xc, mix_c.reshape(1, bc * seq, D_MIX), mods, ctx_row, g2, wo, wr_b, l)
        xs_c, vals_c, rank_c = _route_call(aff_c.reshape(bc, seq, LANES), h2_c.reshape(bc, seq, d), ROUTE_REQS_CTX)

        ulru_l, qkv_l = _inproj_call(xl, mods, lat_row, g1, w_in_b, l, rope_tabs=rope_tabs)
        mix_l, _ = _lru_call(ulru_l, conv_w, cb, w_gates, b_gates, lru_lambda, l, state_lru, l)
        mix_l = _win_attn_call(sinks, qkv_l, ckw_t, cvw_t, l, mix_l)
        mix_l = _nbr_attn_call(qkv_l, bias_tab, ckn_t, cvn_t, l, mix_l)
        xn_l, h2_l, aff_l = _outproj_call(xl, mix_l, mods, lat_row, g2, wo, wr_b, l)
        xs_l, vals_l, rank_l = _route_call(aff_l, h2_l, ROUTE_REQS_LAT)

        cap_c, cap_l = xs_c.shape[2], xs_l.shape[2]
        y_c, y_l = _expert_call(xs_c.reshape(N_EXPERTS, bc * cap_c, d), xs_l.reshape(N_EXPERTS, bl * cap_l, d),
                                vals_c.reshape(N_EXPERTS, bc * cap_c, 1), vals_l.reshape(N_EXPERTS, bl * cap_l, 1),
                                w_exp_gate, w_exp_up, w_exp_down, l)
        xc = _combine_call(y_c.reshape(N_EXPERTS, bc, cap_c, d), rank_c, xn_c.reshape(bc, seq, d), mods, ctx_row, True,
                           l, gf, final)
        xl = _combine_call(y_l.reshape(N_EXPERTS, bl, cap_l, d), rank_l, xn_l, mods, lat_row, False, l, gf, final)
        xc = xc.reshape(1, bc * seq, d)

    y_prompt = xc.reshape(bc, seq, d)
    y_sample = xl
    return (y_prompt, y_sample, st, *[jnp.swapaxes(buf, -1, -2) for buf in caches])
```

```python
import functools

import numpy as np
import jax
import jax.numpy as jnp
from jax import lax
from jax.experimental import pallas as pl
from jax.experimental.pallas import tpu as pltpu

F32 = jnp.float32
BF16 = jnp.bfloat16

HEAD_DIM = 64
GRID_W = 64
LRU_C = 8.0
CONV_W = 4
KV_B = 2
G_B = 3
H_B = KV_B * G_B
H_C = 6
WINDOW = 128
WIN_BLK = 128
NA_ROWS = 8
NA_COLS = 16
ROPE_BASE = 10000.0
N_EXPERTS = 16
EC_FACTOR = 2
N_MOD = 6
EPS = 1e-6
NEG_INF = -1e30
SCALE = HEAD_DIM ** -0.5

LANES = 128
VMEM_LIMIT_BYTES = 56 * 1024 * 1024

TOKEN_BLOCK = 1024
MOD_COL_BLOCK = 1536
EXPERT_F_STEPS = 2
EXPERT_ROW_CHUNK = 256
D_A = 256
D_B = H_B * HEAD_DIM
D_C = H_C * HEAD_DIM
KV_W = KV_B * HEAD_DIM
LRU_W = 2 * D_A
QC0, KC0, VC0 = 0, D_C, 2 * D_C
QB0 = 3 * D_C
KB0, VB0, ATT_W = QB0 + D_B, QB0 + D_B + KV_W, QB0 + D_B + 2 * KV_W
CTX_ATTN_REQS = 4
WIN_Q_BLOCKS = 2
NBR_Q_ROWS = 4
NBR_K_ROWS = 12
PROJ_ROW_CHUNK = 256
ROUTE_REQS_CTX = 8
ROUTE_REQS_LAT = 2
ROUTE_SORT_ROWS = 256
COMBINE_REQS_CTX = 4
LRU_SEG = 8


def _params(n_axes):
    return pltpu.CompilerParams(dimension_semantics=("arbitrary",) * n_axes,
                                vmem_limit_bytes=VMEM_LIMIT_BYTES)


def _dot(a, b):
    return jnp.dot(a, b, preferred_element_type=F32)


def _dot_nt(a, b):
    return lax.dot_general(a, b, (((1,), (1,)), ((), ())), preferred_element_type=F32)


def _adaln_kernel(cond_ref, w_ref, b_ref, o_ref):
    c = cond_ref[...]
    s = c * jax.nn.sigmoid(c)
    o_ref[0] = _dot(s.astype(BF16), w_ref[0].astype(BF16)) + b_ref[0]


def _adaln_call(cond8, w_mod, b_mod3):
    depth, d, n = w_mod.shape
    nb = MOD_COL_BLOCK
    return pl.pallas_call(
        _adaln_kernel,
        grid=(depth, n // nb),
        in_specs=[pl.BlockSpec((8, d), lambda l, j: (0, 0)),
                  pl.BlockSpec((1, d, nb), lambda l, j: (l, 0, j)),
                  pl.BlockSpec((1, 1, nb), lambda l, j: (l, 0, j))],
        out_specs=pl.BlockSpec((1, 8, nb), lambda l, j: (l, 0, j)),
        out_shape=jax.ShapeDtypeStruct((depth, 8, n), F32),
        compiler_params=_params(2),
        name="adaln",
    )(cond8, w_mod, b_mod3)


def _rms_modulate(x, gain, shift, scale):
    y = x * lax.rsqrt(jnp.mean(x * x, axis=-1, keepdims=True) + EPS)
    return (y * gain) * (1.0 + scale) + shift


def _inproj_kernel(*refs, rope, cache, seq):
    if rope:
        x_ref, mod_ref, g_ref, w_ref, cos_ref, sin_ref, ulru_ref, qkv_ref = refs
    elif cache:
        x_ref, mod_ref, g_ref, w_ref = refs[:4]
        ulru_ref, qkv_ref, kw_ref, vw_ref, kn_ref, vn_ref = refs[-6:]
    else:
        x_ref, mod_ref, g_ref, w_ref, ulru_ref, qkv_ref = refs
    tb = x_ref.shape[1]
    rc = seq if cache else PROJ_ROW_CHUNK
    for r in range(tb // rc):
        rows = slice(r * rc, (r + 1) * rc)
        h = _rms_modulate(x_ref[0, rows], g_ref[0], mod_ref[0, 0, 0:1, :], mod_ref[0, 0, 1:2, :])
        u = _dot(h.astype(BF16), w_ref[0])
        ulru_ref[0, rows] = u[:, :LRU_W]
        if rope:
            lo, hi = LRU_W + QB0, LRU_W + VB0
            qk = u[:, lo:hi]
            lane = lax.broadcasted_iota(jnp.int32, (rc, hi - lo), 1)
            first = (lane & 31) < 16
            partner = jnp.where(first, pltpu.roll(qk, hi - lo - 16, 1), pltpu.roll(qk, 16, 1))
            qk = qk * cos_ref[rows, :] + partner * sin_ref[rows, :]
            qkv_ref[0, rows, :QB0] = u[:, LRU_W:lo].astype(BF16)
            qkv_ref[0, rows, QB0:VB0] = qk.astype(BF16)
            qkv_ref[0, rows, VB0:] = u[:, hi:].astype(BF16)
        else:
            qkv_ref[0, rows] = u[:, LRU_W:].astype(BF16)
        if cache:
            for ref, c0, heads in ((kw_ref, KB0, KV_B), (vw_ref, VB0, KV_B), (kn_ref, KC0, H_C), (vn_ref, VC0, H_C)):
                for p in range(heads // 2):
                    pair_t = u[:, LRU_W + c0 + LANES * p: LRU_W + c0 + LANES * (p + 1)].T
                    ref[r, 0, 2 * p] = pair_t[:HEAD_DIM]
                    ref[r, 0, 2 * p + 1] = pair_t[HEAD_DIM:]


def _inproj_call(x3, mods, mod_row0, gains, w_in_b, layer, rope_tabs=None, cache_seq=None, depth=1, prev_caches=None):
    g, n, d = x3.shape
    d_in = w_in_b.shape[2]
    tb = TOKEN_BLOCK
    rope = rope_tabs is not None
    cache = cache_seq is not None
    in_specs = [pl.BlockSpec((1, tb, d), lambda i, j: (i, j, 0)),
                pl.BlockSpec((1, 1, N_MOD, d), lambda i, j: (layer, mod_row0 + i, 0, 0)),
                pl.BlockSpec((1, 1, d), lambda i, j: (layer, 0, 0)),
                pl.BlockSpec((1, d, d_in), lambda i, j: (layer, 0, 0))]
    args = [x3, mods, gains, w_in_b]
    assert d_in == LRU_W + ATT_W
    out_shape = [jax.ShapeDtypeStruct((g, n, LRU_W), F32),
                 jax.ShapeDtypeStruct((g, n, ATT_W), BF16)]
    out_specs = [pl.BlockSpec((1, tb, LRU_W), lambda i, j: (i, j, 0)),
                 pl.BlockSpec((1, tb, ATT_W), lambda i, j: (i, j, 0))]
    if rope:
        in_specs += [pl.BlockSpec((tb, VB0 - QB0), lambda i, j: (j, 0))] * 2
        args += list(rope_tabs)
    aliases = {}
    if cache:
        assert g == 1 and tb % cache_seq == 0
        rb = tb // cache_seq
        nreq = n // cache_seq
        for heads in (KV_B, KV_B, H_C, H_C):
            out_shape.append(jax.ShapeDtypeStruct((nreq, depth, heads, HEAD_DIM, cache_seq), F32))
            out_specs.append(pl.BlockSpec((rb, 1, heads, HEAD_DIM, cache_seq), lambda i, j: (j, layer, 0, 0, 0)))
        if prev_caches is not None:
            for k, buf in enumerate(prev_caches):
                aliases[len(args)] = 2 + k
                in_specs.append(pl.BlockSpec(memory_space=pl.ANY))
                args.append(buf)
    return pl.pallas_call(
        functools.partial(_inproj_kernel, rope=rope, cache=cache, seq=cache_seq),
        grid=(g, n // tb),
        in_specs=in_specs,
        out_specs=out_specs,
        out_shape=out_shape,
        input_output_aliases=aliases,
        compiler_params=_params(2),
        name="inproj_rope" if rope else "inproj_ctx",
    )(*args)


def _lru_kernel(*refs, n):
    u_ref, cw_ref, cb_ref, wg_ref, bg_ref, lam_ref, h0_ref = refs[:7]
    ya_ref, st_ref, a_s, x_s, y_s, hloc_s, ploc_s, hend_s, pend_s, cin_s = refs[-10:]
    c = D_A
    nblk = n // LRU_SEG
    nh = c // LANES
    u = u_ref[0]
    xa = u[:, :c]
    ga = u[:, c:]
    t = lax.broadcasted_iota(jnp.int32, (n, c), 0)
    cw = cw_ref[0]
    xc = cw[2:3] * xa + cb_ref[0]
    xc = xc + cw[0:1] * jnp.where(t >= 2, pltpu.roll(xa, 2, 0), 0.0)
    xc = xc + cw[1:2] * jnp.where(t >= 1, pltpu.roll(xa, 1, 0), 0.0)
    xc = xc + cw[3:4] * jnp.where(t < n - 1, pltpu.roll(xa, n - 1, 0), 0.0)
    gates = _dot(xc.astype(BF16), wg_ref[0]) + bg_ref[0]
    lam = lam_ref[0]
    log_sig = jnp.minimum(lam, 0.0) - jnp.log1p(jnp.exp(-jnp.abs(lam)))
    for d in range(2):
        r = 0.5 * jnp.tanh(0.5 * gates[:, 2 * d * c:(2 * d + 1) * c]) + 0.5
        i = 0.5 * jnp.tanh(0.5 * gates[:, (2 * d + 1) * c:(2 * d + 2) * c]) + 0.5
        log_a = LRU_C * r * log_sig[d:d + 1]
        a = jnp.exp(log_a)
        t_in = jnp.tanh(-log_a) * (1.0 + a * a)
        xin = jnp.where(t_in > 0.0, t_in * lax.rsqrt(t_in), 0.0) * (i * xc)
        for hf in range(nh):
            a_s[hf] = a[:, LANES * hf: LANES * (hf + 1)]
            x_s[hf] = xin[:, LANES * hf: LANES * (hf + 1)]
        order = range(LRU_SEG) if d == 0 else range(LRU_SEG - 1, -1, -1)
        for hf in range(nh):
            h_run = p_run = None
            for s in order:
                a_row = a_s[hf, pl.ds(s, nblk, stride=LRU_SEG), :]
                x_row = x_s[hf, pl.ds(s, nblk, stride=LRU_SEG), :]
                h_run = x_row if h_run is None else a_row * h_run + x_row
                p_run = a_row if p_run is None else a_row * p_run
                hloc_s[d, hf, s] = h_run
                ploc_s[d, hf, s] = p_run
            hend_s[d, hf] = h_run
            pend_s[d, hf] = p_run

    h0 = h0_ref[0, 0]
    init = tuple(h0[d:d + 1, LANES * hf: LANES * (hf + 1)] for d in range(2) for hf in range(nh))

    def body(k, carry):
        out = []
        for idx, cur in enumerate(carry):
            d, hf = divmod(idx, nh)
            kk = k if d == 0 else nblk - 1 - k
            cin_s[d, hf, pl.ds(kk, 1), :] = cur
            out.append(pend_s[d, hf, pl.ds(kk, 1), :] * cur + hend_s[d, hf, pl.ds(kk, 1), :])
        return tuple(out)

    final = lax.fori_loop(0, nblk, body, init)

    y = None
    for d in range(2):
        for hf in range(nh):
            cin = cin_s[d, hf]
            for s in range(LRU_SEG):
                full_h = hloc_s[d, hf, s] + ploc_s[d, hf, s] * cin
                if d == 0:
                    y_s[hf, pl.ds(s, nblk, stride=LRU_SEG), :] = full_h
                else:
                    y_s[hf, pl.ds(s, nblk, stride=LRU_SEG), :] += full_h
    y = jnp.concatenate([y_s[hf] for hf in range(nh)], axis=1) * jax.nn.gelu(ga)
    ya_ref[0] = y.astype(BF16)
    st_ref[0, 0, 0:1, :] = jnp.concatenate(final[:nh], axis=1)
    st_ref[0, 0, 1:2, :] = jnp.concatenate(final[nh:], axis=1)


def _lru_call(ulru, conv_w, conv_b, w_gates_b, b_gates, lam, layer, h0, h0_layer, st_layer=0, st_depth=1,
              prev_state=None):
    b, n, _ = ulru.shape
    c = D_A
    per_layer = lambda shape: pl.BlockSpec((1,) + shape, lambda i: (layer,) + (0,) * len(shape))
    nh, nblk = c // LANES, n // LRU_SEG
    assert n % LRU_SEG == 0 and nblk % 8 == 0
    scratch = ([pltpu.VMEM((nh, n, LANES), F32)] * 3
               + [pltpu.VMEM((2, nh, LRU_SEG, nblk, LANES), F32)] * 2
               + [pltpu.VMEM((2, nh, nblk, LANES), F32)] * 3)
    in_specs = [pl.BlockSpec((1, n, 2 * c), lambda i: (i, 0, 0)),
                per_layer((CONV_W, c)), per_layer((1, c)), per_layer((c, 4 * c)), per_layer((1, 4 * c)),
                per_layer((2, c)),
                pl.BlockSpec((1, 1, 2, c), lambda i: (i, h0_layer, 0, 0))]
    args = [ulru, conv_w, conv_b, w_gates_b, b_gates, lam, h0]
    aliases = {}
    if prev_state is not None:
        aliases[len(args)] = 1
        in_specs.append(pl.BlockSpec(memory_space=pl.ANY))
        args.append(prev_state)
    return pl.pallas_call(
        functools.partial(_lru_kernel, n=n),
        grid=(b,),
        in_specs=in_specs,
        out_specs=[pl.BlockSpec((1, n, c), lambda i: (i, 0, 0)),
                   pl.BlockSpec((1, 1, 2, c), lambda i: (i, st_layer, 0, 0))],
        out_shape=[jax.ShapeDtypeStruct((b, n, c), BF16),
                   jax.ShapeDtypeStruct((b, st_depth, 2, c), F32)],
        scratch_shapes=scratch,
        input_output_aliases=aliases,
        compiler_params=_params(1),
        name="rglru",
    )(*args)


def _with_ones(v, transposed=False):
    axis = 0 if transposed else 1
    w = v.shape[axis]
    pieces = [v]
    if w < LANES:
        pieces.append(jnp.zeros(v.shape[:axis] + (LANES - w,) + v.shape[axis + 1:], v.dtype))
    pieces.append(jnp.ones(v.shape[:axis] + (LANES,) + v.shape[axis + 1:], v.dtype))
    return jnp.concatenate(pieces, axis=axis)


def _softmax_pv(parts, sink):
    mx = None
    for s, _, _ in parts:
        cur = jnp.max(s, axis=-1, keepdims=True)
        mx = cur if mx is None else jnp.maximum(mx, cur)
    if sink is not None:
        mx = jnp.maximum(mx, sink)
    out = None
    for s, v, v_t in parts:
        p = jnp.exp(s - mx).astype(BF16)
        o = _dot_nt(p, v) if v_t else _dot(p, v)
        out = o if out is None else out + o
    den = out[:, LANES:]
    if sink is not None:
        den = den + jnp.exp(sink - mx)
    return out[:, :LANES] / den


def _swap_halves(x):
    return jnp.concatenate([x[:, HEAD_DIM:], x[:, :HEAD_DIM]], axis=1)


def _keep_half(x, lo_mask, half):
    zero = jnp.zeros_like(x)
    return jnp.where(lo_mask, x, zero) if half == 0 else jnp.where(lo_mask, zero, x)


def _ctx_attn_kernel(sink_ref, att_ref, ob_ref, oc_ref, *, n, rb, sink0):
    lo = lax.broadcasted_iota(jnp.int32, (n, LANES), 1) < HEAD_DIM
    for r in range(rb):
        for p in range(H_C // 2):
            qp = att_ref[r, :, QC0 + LANES * p: QC0 + LANES * (p + 1)]
            kp = att_ref[r, :, KC0 + LANES * p: KC0 + LANES * (p + 1)]
            vp = _with_ones(att_ref[r, :, VC0 + LANES * p: VC0 + LANES * (p + 1)])
            outs = [_softmax_pv([(_dot_nt(qp, _keep_half(kp, lo, half)), vp, False)], None) for half in range(2)]
            oc_ref[r, :, LANES * p: LANES * (p + 1)] = jnp.where(lo, outs[0], outs[1]).astype(BF16)
        kpair = att_ref[r, :, KB0: KB0 + LANES]
        vpair = att_ref[r, :, VB0: VB0 + LANES]
        kpair_sw = _swap_halves(kpair)
        vpair_ext = _with_ones(vpair)
        vpair_sw_ext = _with_ones(_swap_halves(vpair))
        for p in range(H_B // 2):
            qp = att_ref[r, :, QB0 + LANES * p: QB0 + LANES * (p + 1)]
            outs = []
            for half in range(2):
                h = 2 * p + half
                aligned = (h // G_B) == half
                ksrc, vsrc = (kpair, vpair_ext) if aligned else (kpair_sw, vpair_sw_ext)
                outs.append(_softmax_pv([(_dot_nt(qp, _keep_half(ksrc, lo, half)), vsrc, False)], sink_ref[sink0 + h]))
            ob_ref[r, :, LANES * p: LANES * (p + 1)] = jnp.where(lo, outs[0], outs[1]).astype(BF16)


def _ctx_attn_call(sinks, att, layer):
    b, n, w = att.shape
    rb = CTX_ATTN_REQS
    assert b % rb == 0 and KV_B == 2
    return pl.pallas_call(
        functools.partial(_ctx_attn_kernel, n=n, rb=rb, sink0=layer * H_B),
        grid=(b // rb,),
        in_specs=[pl.BlockSpec(memory_space=pltpu.SMEM),
                  pl.BlockSpec((rb, n, w), lambda i: (i, 0, 0))],
        out_specs=[pl.BlockSpec((rb, n, D_B), lambda i: (i, 0, 0)), pl.BlockSpec((rb, n, D_C), lambda i: (i, 0, 0))],
        out_shape=[jax.ShapeDtypeStruct((b, n, D_B), BF16), jax.ShapeDtypeStruct((b, n, D_C), BF16)],
        compiler_params=_params(1),
        name="ctx_attn",
    )(sinks, att)


def _win_attn_kernel(sink_ref, qkv_ref, kc_ref, vc_ref, o_ref, *, n, sink0):
    j = pl.program_id(1)
    nq = WIN_Q_BLOCKS * WIN_BLK
    nloc = nq + 2 * WIN_BLK
    q0 = pl.multiple_of(j * nq, WIN_BLK)
    ks = pl.multiple_of(jnp.clip(j * nq - WIN_BLK, 0, n - nloc), WIN_BLK)
    row = lax.broadcasted_iota(jnp.int32, (nq, nloc), 0)
    col = lax.broadcasted_iota(jnp.int32, (nq, nloc), 1)
    in_window = jnp.where(jnp.abs((q0 + row) - (ks + col)) <= WINDOW, 0.0, NEG_INF)
    band = jnp.concatenate([in_window] * G_B, axis=0)
    rowh = lax.broadcasted_iota(jnp.int32, (G_B * nq, 1), 0) // nq
    for kv in range(KV_B):
        gw = G_B * HEAD_DIM
        qblk = qkv_ref[0, pl.ds(q0, nq), QB0 + gw * kv: QB0 + gw * (kv + 1)]
        q3 = jnp.concatenate([qblk[:, 64 * g: 64 * (g + 1)] for g in range(G_B)], axis=0)
        kl = qkv_ref[0, pl.ds(ks, nloc), KB0 + 64 * kv: KB0 + 64 * (kv + 1)]
        vl = _with_ones(qkv_ref[0, pl.ds(ks, nloc), VB0 + 64 * kv: VB0 + 64 * (kv + 1)])
        kc_t = kc_ref[0, 0, kv].astype(BF16)
        vc_t = _with_ones(vc_ref[0, 0, kv].astype(BF16), transposed=True)
        s_loc = _dot_nt(q3, kl) + band
        s_ctx = _dot(q3, kc_t)
        s0 = sink0 + G_B * kv
        sk = jnp.where(rowh == 0, sink_ref[s0], jnp.where(rowh == 1, sink_ref[s0 + 1], sink_ref[s0 + 2]))
        o = _softmax_pv([(s_loc, vl, False), (s_ctx, vc_t, True)], sk)
        for g in range(G_B):
            h = kv * G_B + g
            o_ref[0, :, 64 * h: 64 * (h + 1)] = o[g * nq:(g + 1) * nq, :HEAD_DIM].astype(BF16)


def _win_attn_call(sinks, qkv, cache_k, cache_v, layer):
    b, n, w = qkv.shape
    past = cache_k.shape[4]
    cache_spec = pl.BlockSpec((1, 1, KV_B, HEAD_DIM, past), lambda i, j: (i, layer, 0, 0, 0))
    nq = WIN_Q_BLOCKS * WIN_BLK
    assert n % nq == 0 and n >= nq + 2 * WIN_BLK and WINDOW == WIN_BLK
    return pl.pallas_call(
        functools.partial(_win_attn_kernel, n=n, sink0=layer * H_B),
        grid=(b, n // nq),
        in_specs=[pl.BlockSpec(memory_space=pltpu.SMEM),
                  pl.BlockSpec((1, n, w), lambda i, j: (i, 0, 0)),
                  cache_spec, cache_spec],
        out_specs=pl.BlockSpec((1, nq, D_B), lambda i, j: (i, j, 0)),
        out_shape=jax.ShapeDtypeStruct((b, n, D_B), BF16),
        compiler_params=_params(2),
        name="win_attn",
    )(sinks, qkv, cache_k, cache_v)


def _nbr_window_start(g, rows):
    return jnp.clip(g * NBR_Q_ROWS - NA_ROWS // 2, 0, rows - NBR_K_ROWS)


def _nbr_attn_kernel(q_ref, k_ref, v_ref, rel_ref, kc_ref, vc_ref, o_ref, bias_s, *, n):
    g = pl.program_id(0)
    nq = NBR_Q_ROWS * GRID_W
    nk = NBR_K_ROWS * GRID_W
    rows = n // GRID_W
    ws = _nbr_window_start(g, rows)
    k0 = pl.multiple_of(ws * GRID_W, GRID_W)

    @pl.when(pl.program_id(1) == 0)
    def _():
        for i in range(NBR_Q_ROWS):
            qr = g * NBR_Q_ROWS + i
            rs = jnp.clip(qr - NA_ROWS // 2, 0, rows - NA_ROWS)
            for jp in range(NBR_K_ROWS // 2):
                tiles = []
                for j in (2 * jp, 2 * jp + 1):
                    kr = ws + j
                    valid = (kr >= rs) & (kr < rs + NA_ROWS)
                    d = jnp.clip(kr - qr + NA_ROWS - 1, 0, 2 * NA_ROWS - 2)
                    tiles.append(jnp.where(valid, rel_ref[0, :, pl.ds(d, 1)][:, 0], NEG_INF))
                bias_s[:, GRID_W * i: GRID_W * (i + 1), LANES * jp: LANES * (jp + 1)] = jnp.concatenate(tiles, axis=-1)
    lo_k = lax.broadcasted_iota(jnp.int32, (nk, LANES), 1) < HEAD_DIM
    lo_q = lax.broadcasted_iota(jnp.int32, (nq, LANES), 1) < HEAD_DIM
    zpad = jnp.zeros((HEAD_DIM, kc_ref.shape[4]), BF16)
    for p in range(H_C // 2):
        qp = q_ref[0, :, LANES * p: LANES * (p + 1)]
        kp = k_ref[0, pl.ds(k0, nk), LANES * p: LANES * (p + 1)]
        vp = _with_ones(v_ref[0, pl.ds(k0, nk), LANES * p: LANES * (p + 1)])
        outs = []
        for half in range(2):
            h = 2 * p + half
            kc_t = kc_ref[0, 0, h].astype(BF16)
            vc_t = vc_ref[0, 0, h].astype(BF16)
            kc_t = jnp.concatenate([kc_t, zpad] if half == 0 else [zpad, kc_t], axis=0)
            vc_t = _with_ones(jnp.concatenate([vc_t, zpad] if half == 0 else [zpad, vc_t], axis=0), transposed=True)
            s_loc = _dot_nt(qp, _keep_half(kp, lo_k, half)) + bias_s[h]
            s_ctx = _dot(qp, kc_t)
            outs.append(_softmax_pv([(s_loc, vp, False), (s_ctx, vc_t, True)], None))
        o_ref[0, :, LANES * p: LANES * (p + 1)] = jnp.where(lo_q, outs[0], outs[1]).astype(BF16)


def _nbr_attn_call(att, rel_tab, cache_k, cache_v, layer):
    b, n, w = att.shape
    past = cache_k.shape[4]
    nq = NBR_Q_ROWS * GRID_W
    nk = NBR_K_ROWS * GRID_W
    gw = H_C * HEAD_DIM
    rows = n // GRID_W
    assert (QC0, KC0, VC0) == (0, gw, 2 * gw) and n % nq == 0 and rows >= NBR_K_ROWS and NBR_K_ROWS % 2 == 0
    cache_spec = pl.BlockSpec((1, 1, H_C, HEAD_DIM, past), lambda g, i: (i, layer, 0, 0, 0))
    return pl.pallas_call(
        functools.partial(_nbr_attn_kernel, n=n),
        grid=(n // nq, b),
        in_specs=[pl.BlockSpec((1, nq, gw), lambda g, i: (i, g, 0)),
                  pl.BlockSpec((1, n, gw), lambda g, i: (i, 0, 1)),
                  pl.BlockSpec((1, n, gw), lambda g, i: (i, 0, 2)),
                  pl.BlockSpec((1,) + rel_tab.shape[1:], lambda g, i: (layer, 0, 0, 0, 0)),
                  cache_spec, cache_spec],
        out_specs=pl.BlockSpec((1, nq, gw), lambda g, i: (i, g, 0)),
        out_shape=jax.ShapeDtypeStruct((b, n, gw), BF16),
        scratch_shapes=[pltpu.VMEM((H_C, nq, nk), F32)],
        compiler_params=_params(2),
        name="nbr_attn",
    )(att, att, att, rel_tab, cache_k, cache_v)


def _outproj_kernel(x_ref, ya_ref, yb_ref, yc_ref, mod_ref, g_ref, wo_ref, wr_ref,
                    xn_ref, h2_ref, aff_ref):
    for c in range(x_ref.shape[1] // PROJ_ROW_CHUNK):
        rows = slice(c * PROJ_ROW_CHUNK, (c + 1) * PROJ_ROW_CHUNK)
        mix = jnp.concatenate([ya_ref[0, rows], yb_ref[0, rows], yc_ref[0, rows]], axis=1)
        proj = _dot(mix, wo_ref[0])
        xn = x_ref[0, rows] + mod_ref[0, 0, 2:3, :] * proj
        xn_ref[0, rows] = xn
        h2 = _rms_modulate(xn, g_ref[0], mod_ref[0, 0, 3:4, :], mod_ref[0, 0, 4:5, :]).astype(BF16)
        h2_ref[0, rows] = h2
        logits = _dot(h2, wr_ref[0])
        lane = lax.broadcasted_iota(jnp.int32, logits.shape, 1)
        valid = lane < N_EXPERTS
        logits = jnp.where(valid, logits, NEG_INF)
        e = jnp.exp(logits - jnp.max(logits, axis=-1, keepdims=True))
        e = jnp.where(valid, e, 0.0)
        aff_ref[0, rows] = e / jnp.sum(e, axis=-1, keepdims=True)


def _outproj_call(x3, ya, yb, yc, mods, mod_row0, gains, wo_b16, w_router_b, layer):
    g, n, d = x3.shape
    tb = TOKEN_BLOCK
    tok = lambda w: pl.BlockSpec((1, tb, w), lambda i, j: (i, j, 0))
    per_layer = lambda shape: pl.BlockSpec((1,) + shape, lambda i, j: (layer,) + (0,) * len(shape))
    return pl.pallas_call(
        _outproj_kernel,
        grid=(g, n // tb),
        in_specs=[tok(d), tok(D_A), tok(D_B), tok(D_C),
                  pl.BlockSpec((1, 1, N_MOD, d), lambda i, j: (layer, mod_row0 + i, 0, 0)),
                  per_layer((1, d)), per_layer((D_A + D_B + D_C, d)), per_layer((d, LANES))],
        out_specs=[tok(d), tok(d), tok(LANES)],
        out_shape=[jax.ShapeDtypeStruct((g, n, d), F32),
                   jax.ShapeDtypeStruct((g, n, d), BF16),
                   jax.ShapeDtypeStruct((g, n, LANES), F32)],
        compiler_params=_params(2),
        name="outproj_router",
    )(x3, ya, yb, yc, mods, gains, wo_b16, w_router_b)


def _sort_by_affinity(keys, idx, pos, levels, seg_lanes):
    rows = keys.shape[0]
    sign = [jnp.where(((pos >> m) & 1) == 0, 1.0, -1.0) for m in range(levels)]

    def partner(x, j):
        d = 1 << j
        low = sign[j] > 0.0
        if d >= rows:
            sh = (d // rows) * seg_lanes
            return jnp.where(low, pltpu.roll(x, LANES - sh, 1), pltpu.roll(x, sh, 1))
        if d >= 8:
            x4 = x.reshape(rows // (2 * d), 2, d, LANES)
            return jnp.concatenate([x4[:, 1:2], x4[:, 0:1]], axis=1).reshape(rows, LANES)
        x3 = x.reshape(rows // 8, 8, LANES)
        low3 = sign[j].reshape(rows // 8, 8, LANES) > 0.0
        return jnp.where(low3, pltpu.roll(x3, 8 - d, 1), pltpu.roll(x3, d, 1)).reshape(rows, LANES)

    for k in range(1, levels + 1):
        for j in range(k - 1, -1, -1):
            pk, pi = partner(keys, j), partner(idx, j)
            before = jnp.where(keys == pk, jnp.where(idx < pi, 1.0, -1.0), jnp.where(keys > pk, 1.0, -1.0))
            want_first = sign[j] * sign[k] if k < levels else sign[j]
            keep = before * want_first > 0.0
            keys = jnp.where(keep, keys, pk)
            idx = jnp.where(keep, idx, pi)
    return keys, idx


def _route_kernel(aff_ref, h2_ref, xs_ref, vals_ref, rank_ref, *, n, cap, rb, by_slot):
    ne = N_EXPERTS
    rows = ROUTE_SORT_ROWS
    nseg = n // rows
    seg_lanes = rb * ne
    levels = n.bit_length() - 1
    fcap = float(cap)
    row = lax.broadcasted_iota(jnp.int32, (rows, LANES), 0)
    lane = lax.broadcasted_iota(jnp.int32, (rows, LANES), 1)
    pos = (lane // seg_lanes) * rows + row
    keys = jnp.full((rows, LANES), -1.0, F32)
    for r in range(rb):
        for seg in range(nseg):
            off = seg * seg_lanes + r * ne
            piece = aff_ref[r, seg * rows:(seg + 1) * rows, :]
            if off:
                piece = pltpu.roll(piece, off, 1)
            keys = jnp.where((lane >= off) & (lane < off + ne), piece, keys)
    keys, idx = _sort_by_affinity(keys, pos.astype(F32), pos, levels, seg_lanes)
    top_keys = keys[:cap]
    top_idx = idx[:cap]

    tok = lax.broadcasted_iota(jnp.int32, (cap, n), 1).astype(F32)
    weight = fcap - lax.broadcasted_iota(jnp.int32, (cap, n), 0).astype(F32)
    pad = jnp.full((LANES - ne, n), fcap, F32)
    if by_slot:
        top_idx_t = top_idx.T
    for r in range(rb):
        picks, rank_rows = [], []
        for e in range(ne):
            col = r * ne + e
            onehot = jnp.where(top_idx[:, col:col + 1] == tok, 1.0, 0.0)
            vals_ref[e, r] = top_keys[:, col:col + 1]
            if not by_slot:
                rank_rows.append(fcap - jnp.sum(onehot * weight, axis=0, keepdims=True))
            picks.append(onehot.astype(BF16))
        if by_slot:
            rank_ref[r] = top_idx_t[r * ne:(r + 1) * ne]
        else:
            rank_ref[r] = jnp.concatenate(rank_rows + [pad], axis=0).T
        xs = _dot(jnp.concatenate(picks, axis=0), h2_ref[r])
        xs_ref[:, r] = xs.reshape(ne, cap, xs.shape[-1]).astype(BF16)


def _route_call(aff, h2, rb):
    b, n, d = h2.shape
    cap = max(1, EC_FACTOR * n // N_EXPERTS)
    nseg = n // ROUTE_SORT_ROWS
    assert b % rb == 0 and n == nseg * ROUTE_SORT_ROWS and n & (n - 1) == 0
    assert nseg * rb * N_EXPERTS <= LANES and cap <= ROUTE_SORT_ROWS
    by_slot = cap == LANES
    assign_shape = (N_EXPERTS, cap) if by_slot else (n, LANES)
    return pl.pallas_call(
        functools.partial(_route_kernel, n=n, cap=cap, rb=rb, by_slot=by_slot),
        grid=(b // rb,),
        in_specs=[pl.BlockSpec((rb, n, LANES), lambda i: (i, 0, 0)),
                  pl.BlockSpec((rb, n, d), lambda i: (i, 0, 0))],
        out_specs=[pl.BlockSpec((N_EXPERTS, rb, cap, d), lambda i: (0, i, 0, 0)),
                   pl.BlockSpec((N_EXPERTS, rb, cap, 1), lambda i: (0, i, 0, 0)),
                   pl.BlockSpec((rb,) + assign_shape, lambda i: (i, 0, 0))],
        out_shape=[jax.ShapeDtypeStruct((N_EXPERTS, b, cap, d), BF16),
                   jax.ShapeDtypeStruct((N_EXPERTS, b, cap, 1), F32),
                   jax.ShapeDtypeStruct((b,) + assign_shape, F32)],
        compiler_params=_params(1),
        name="route_gather",
    )(aff, h2)


def _expert_kernel(xc_ref, xl_ref, vc_ref, vl_ref, wg_ref, wu_ref, wd_ref, yc_ref, yl_ref, acc_c, acc_l):
    f = pl.program_id(1)

    def step(first):
        wg = wg_ref[0, 0].astype(BF16)
        wu = wu_ref[0, 0].astype(BF16)
        wd = wd_ref[0, 0].astype(BF16)
        for x_ref, v_ref, y_ref, acc in ((xc_ref, vc_ref, yc_ref, acc_c), (xl_ref, vl_ref, yl_ref, acc_l)):
            rows = x_ref.shape[1]
            chunk = min(EXPERT_ROW_CHUNK, rows)
            for c0 in range(0, rows, chunk):
                sl = slice(c0, c0 + chunk)
                x = x_ref[0, sl, :]
                a = _dot(x, wg)
                u = _dot(x, wu)
                act = ((a * jax.nn.sigmoid(a)) * u).astype(BF16)
                part = _dot(act, wd)
                if first:
                    acc[sl, :] = part
                else:
                    y_ref[0, sl, :] = ((acc[sl, :] + part) * v_ref[0, sl, :]).astype(BF16)

    pl.when(f == 0)(functools.partial(step, True))
    pl.when(f == EXPERT_F_STEPS - 1)(functools.partial(step, False))


def _expert_call(xs_c, xs_l, vals_c, vals_l, w_gate, w_up, w_down, layer):
    e, rc, d = xs_c.shape
    rl = xs_l.shape[1]
    f_total = w_gate.shape[-1]
    assert EXPERT_F_STEPS == 2
    fb = f_total // EXPERT_F_STEPS
    per_e = lambda rows, w: pl.BlockSpec((1, rows, w), lambda i, j: (i, 0, 0))
    return pl.pallas_call(
        _expert_kernel,
        grid=(e, EXPERT_F_STEPS),
        in_specs=[per_e(rc, d), per_e(rl, d), per_e(rc, 1), per_e(rl, 1),
                  pl.BlockSpec((1, 1, d, fb), lambda i, j: (layer, i, 0, j)),
                  pl.BlockSpec((1, 1, d, fb), lambda i, j: (layer, i, 0, j)),
                  pl.BlockSpec((1, 1, fb, d), lambda i, j: (layer, i, j, 0))],
        out_specs=[per_e(rc, d), per_e(rl, d)],
        out_shape=[jax.ShapeDtypeStruct((e, rc, d), BF16),
                   jax.ShapeDtypeStruct((e, rl, d), BF16)],
        scratch_shapes=[pltpu.VMEM((rc, d), F32), pltpu.VMEM((rl, d), F32)],
        compiler_params=_params(2),
        name="experts",
    )(xs_c, xs_l, vals_c, vals_l, w_gate, w_up, w_down)


def _combine_kernel(y_ref, rank_ref, xn_ref, mod_ref, gf_ref, o_ref, *, n, cap, final, rb, by_slot):
    ec = N_EXPERTS * cap
    d = xn_ref.shape[-1]
    if by_slot:
        tok = lax.broadcasted_iota(jnp.int32, (n, cap), 0).astype(F32)
    else:
        ce = lax.broadcasted_iota(jnp.int32, (LANES, ec), 1) // cap
        ee = lax.broadcasted_iota(jnp.int32, (LANES, ec), 0)
        expand = jnp.where(ce == ee, 1.0, 0.0).astype(BF16)
        slot = (lax.broadcasted_iota(jnp.int32, (n, ec), 1) % cap).astype(F32)
    for r in range(rb):
        if by_slot:
            chosen = rank_ref[r]
            onehot = jnp.concatenate([jnp.where(tok == chosen[e:e + 1, :], 1.0, 0.0).astype(BF16)
                                      for e in range(N_EXPERTS)], axis=1)
        else:
            rank = rank_ref[r].astype(BF16)
            rexp = _dot(rank, expand)
            onehot = jnp.where(rexp == slot, 1.0, 0.0).astype(BF16)
        moe = _dot(onehot, y_ref[:, r].reshape(ec, d))
        x = xn_ref[r] + mod_ref[0, 0, 5:6, :] * moe
        if final:
            x = (x * lax.rsqrt(jnp.mean(x * x, axis=-1, keepdims=True) + EPS)) * gf_ref[...]
        o_ref[r] = x


def _combine_call(y4, rank, xn, mods, mod_row0, shared_mod, layer, g_final, final):
    e, b, cap, d = y4.shape
    n = xn.shape[1]
    rb = COMBINE_REQS_CTX if shared_mod else 1
    by_slot = rank.shape[1:] == (e, cap) and cap == LANES
    assert b % rb == 0 and (by_slot or rank.shape[1:] == (n, LANES))
    return pl.pallas_call(
        functools.partial(_combine_kernel, n=n, cap=cap, final=final, rb=rb, by_slot=by_slot),
        grid=(b // rb,),
        in_specs=[pl.BlockSpec((e, rb, cap, d), lambda i: (0, i, 0, 0)),
                  pl.BlockSpec((rb,) + rank.shape[1:], lambda i: (i, 0, 0)),
                  pl.BlockSpec((rb, n, d), lambda i: (i, 0, 0)),
                  pl.BlockSpec((1, 1, N_MOD, d), (lambda i: (layer, mod_row0, 0, 0)) if shared_mod
                               else (lambda i: (layer, mod_row0 + i, 0, 0))),
                  pl.BlockSpec((1, d), lambda i: (0, 0))],
        out_specs=pl.BlockSpec((rb, n, d), lambda i: (i, 0, 0)),
        out_shape=jax.ShapeDtypeStruct((b, n, d), F32),
        compiler_params=_params(1),
        name="combine",
    )(y4, rank, xn, mods, g_final)


def _rope_tables(n):
    t = jnp.arange(n, dtype=jnp.int32)
    row = (t // GRID_W).astype(F32)
    col = (t % GRID_W).astype(F32)
    half = HEAD_DIM // 4
    inv = jnp.power(ROPE_BASE, -jnp.arange(half, dtype=F32) / half)
    ang_r = row[:, None] * inv[None, :]
    ang_c = col[:, None] * inv[None, :]
    cos_h = jnp.concatenate([jnp.cos(ang_r)] * 2 + [jnp.cos(ang_c)] * 2, axis=-1)
    sin_h = jnp.concatenate([-jnp.sin(ang_r), jnp.sin(ang_r), -jnp.sin(ang_c), jnp.sin(ang_c)], axis=-1)
    reps = H_B + KV_B
    return jnp.tile(cos_h, (1, reps)), jnp.tile(sin_h, (1, reps))


def _nbr_rel_tables(rpb):
    col = np.arange(GRID_W)
    cs = np.clip(col - NA_COLS // 2, 0, GRID_W - NA_COLS)
    col_mask = (col[None, :] >= cs[:, None]) & (col[None, :] < cs[:, None] + NA_COLS)
    dc_idx = np.clip(col[None, :] - col[:, None], -(NA_COLS - 1), NA_COLS - 1) + (NA_COLS - 1)
    sel_col = (dc_idx[None, :, :] == np.arange(2 * NA_COLS - 1)[:, None, None]).astype(np.float32)
    t = jnp.einsum("lhdc,cqk->lhdqk", rpb.astype(F32), sel_col, precision=lax.Precision.HIGHEST)
    return jnp.where(col_mask, t, NEG_INF)


def _block_diag(w):
    nb, bw = w.shape[-3], w.shape[-1]
    eye = jnp.eye(nb, dtype=w.dtype)
    return (eye[:, None, :, None] * w[..., :, :, None, :]).reshape(w.shape[:-3] + (nb * bw, nb * bw))


def kernel(x_prompt, x_sample, state_lru, cache_k_win, cache_v_win, cache_k_nbr, cache_v_nbr, c, c_ctx, w_mod, b_mod, g_norm1, w_in, conv_w, conv_b, w_gate_r, b_gate_r, w_gate_i, b_gate_i, lru_lambda, sink_logit, nbr_bias, w_out, g_norm2, w_router, w_exp_gate, w_exp_up, w_exp_down, g_final):
    bc, seq, d = x_prompt.shape
    bl, n_lat, _ = x_sample.shape
    depth = w_mod.shape[0]
    assert bl + 1 <= 8 and d == 1024

    cond8 = jnp.zeros((8, d), F32).at[0].set(c_ctx).at[1:1 + bl].set(c)
    mods = _adaln_call(cond8, w_mod, b_mod.reshape(depth, 1, -1)).reshape(depth, 8, N_MOD, d)
    rope_tabs = _rope_tables(n_lat)
    bias_tab = _nbr_rel_tables(nbr_bias)
    ckw_t, cvw_t, ckn_t, cvn_t = (jnp.swapaxes(a, -1, -2) for a in (cache_k_win, cache_v_win, cache_k_nbr, cache_v_nbr))
    gf = g_final.reshape(1, d)
    assert SCALE == 0.125
    new_cols = np.arange(w_in.shape[-1]) - LRU_W
    is_q = ((new_cols >= QC0) & (new_cols < KC0)) | ((new_cols >= QB0) & (new_cols < KB0))
    q_scale = jnp.asarray(np.where(is_q, SCALE, 1.0), F32)

    nbr0 = LRU_W + D_B + 2 * KV_W
    w_in_b = (jnp.concatenate([w_in[..., :LRU_W], w_in[..., nbr0:], w_in[..., LRU_W:nbr0]], axis=-1) * q_scale).astype(BF16)
    wo = w_out.astype(BF16)
    wr_b = jnp.pad(w_router.astype(BF16), ((0, 0), (0, 0), (0, LANES - N_EXPERTS)))
    gate_r, gate_i = _block_diag(w_gate_r), _block_diag(w_gate_i)
    w_gates = jnp.concatenate([gate_r[:, 0], gate_i[:, 0], gate_r[:, 1], gate_i[:, 1]], axis=-1).astype(BF16)
    b_gates = jnp.concatenate([b_gate_r[:, 0], b_gate_i[:, 0], b_gate_r[:, 1], b_gate_i[:, 1]], axis=-1)[:, None, :]
    cb = conv_b[:, None, :]
    g1 = g_norm1[:, None, :]
    g2 = g_norm2[:, None, :]
    sinks = sink_logit.reshape(-1)
    ctx_row, lat_row = 0, 1

    xc = x_prompt.reshape(1, bc * seq, d)
    xl = x_sample
    zeros_state = jnp.zeros((bc, 1, 2, D_A), F32)
    caches = None
    st = None
    for l in range(depth):
        final = l == depth - 1
        ulru_c, qkv_c, *caches = _inproj_call(xc, mods, ctx_row, g1, w_in_b, l, cache_seq=seq, depth=depth,
                                              prev_caches=caches)
        ya_c, st = _lru_call(ulru_c.reshape(bc, seq, LRU_W), conv_w, cb, w_gates, b_gates, lru_lambda, l,
                             zeros_state, 0, st_layer=l, st_depth=depth, prev_state=st)
        yb_c, yc_c = _ctx_attn_call(sinks, qkv_c.reshape(bc, seq, -1), l)
        xn_c, h2_c, aff_c = _outproj_call(xc, ya_c.reshape(1, bc * seq, -1), yb_c.reshape(1, bc * seq, -1),
                                          yc_c.reshape(1, bc * seq, -1), mods, ctx_row, g2, wo, wr_b, l)
        xs_c, vals_c, rank_c = _route_call(aff_c.reshape(bc, seq, LANES), h2_c.reshape(bc, seq, d), ROUTE_REQS_CTX)

        ulru_l, qkv_l = _inproj_call(xl, mods, lat_row, g1, w_in_b, l, rope_tabs=rope_tabs)
        ya_l, _ = _lru_call(ulru_l, conv_w, cb, w_gates, b_gates, lru_lambda, l, state_lru, l)
        yb_l = _win_attn_call(sinks, qkv_l, ckw_t, cvw_t, l)
        yc_l = _nbr_attn_call(qkv_l, bias_tab, ckn_t, cvn_t, l)
        xn_l, h2_l, aff_l = _outproj_call(xl, ya_l, yb_l, yc_l, mods, lat_row, g2, wo, wr_b, l)
        xs_l, vals_l, rank_l = _route_call(aff_l, h2_l, ROUTE_REQS_LAT)

        cap_c, cap_l = xs_c.shape[2], xs_l.shape[2]
        y_c, y_l = _expert_call(xs_c.reshape(N_EXPERTS, bc * cap_c, d), xs_l.reshape(N_EXPERTS, bl * cap_l, d),
                                vals_c.reshape(N_EXPERTS, bc * cap_c, 1), vals_l.reshape(N_EXPERTS, bl * cap_l, 1),
                                w_exp_gate, w_exp_up, w_exp_down, l)
        xc = _combine_call(y_c.reshape(N_EXPERTS, bc, cap_c, d), rank_c, xn_c.reshape(bc, seq, d), mods, ctx_row, True,
                           l, gf, final)
        xl = _combine_call(y_l.reshape(N_EXPERTS, bl, cap_l, d), rank_l, xn_l, mods, lat_row, False, l, gf, final)
        xc = xc.reshape(1, bc * seq, d)

    y_prompt = xc.reshape(bc, seq, d)
    y_sample = xl
    return (y_prompt, y_sample, st, *[jnp.swapaxes(buf, -1, -2) for buf in caches])
```

```python
import functools

import numpy as np
import jax
import jax.numpy as jnp
from jax import lax
from jax.experimental import pallas as pl
from jax.experimental.pallas import tpu as pltpu

F32 = jnp.float32
BF16 = jnp.bfloat16

HEAD_DIM = 64
GRID_W = 64
LRU_C = 8.0
CONV_W = 4
KV_B = 2
G_B = 3
H_B = KV_B * G_B
H_C = 6
WINDOW = 128
WIN_BLK = 128
NA_ROWS = 8
NA_COLS = 16
ROPE_BASE = 10000.0
N_EXPERTS = 16
EC_FACTOR = 2
N_MOD = 6
EPS = 1e-6
NEG_INF = -1e30
SCALE = HEAD_DIM ** -0.5

LANES = 128
VMEM_LIMIT_BYTES = 56 * 1024 * 1024

TOKEN_BLOCK = 1024
MOD_COL_BLOCK = 1536
EXPERT_F_STEPS = 2
EXPERT_ROW_CHUNK = 256
D_A = 256
D_B = H_B * HEAD_DIM
D_C = H_C * HEAD_DIM
KV_W = KV_B * HEAD_DIM
LRU_W = 2 * D_A
QC0, KC0, VC0 = 0, D_C, 2 * D_C
QB0 = 3 * D_C
KB0, VB0, ATT_W = QB0 + D_B, QB0 + D_B + KV_W, QB0 + D_B + 2 * KV_W
CTX_ATTN_REQS = 4
WIN_Q_BLOCKS = 2
NBR_Q_ROWS = 4
NBR_K_ROWS = 12
PROJ_ROW_CHUNK = 256
ROUTE_REQS_CTX = 8
ROUTE_REQS_LAT = 2
ROUTE_SORT_ROWS = 256
COMBINE_REQS_CTX = 4
LRU_SEG = 8


def _params(n_axes):
    return pltpu.CompilerParams(dimension_semantics=("arbitrary",) * n_axes,
                                vmem_limit_bytes=VMEM_LIMIT_BYTES)


def _dot(a, b):
    return jnp.dot(a, b, preferred_element_type=F32)


def _dot_nt(a, b):
    return lax.dot_general(a, b, (((1,), (1,)), ((), ())), preferred_element_type=F32)


def _adaln_kernel(cond_ref, w_ref, b_ref, o_ref):
    c = cond_ref[...]
    s = c * jax.nn.sigmoid(c)
    o_ref[0] = _dot(s.astype(BF16), w_ref[0].astype(BF16)) + b_ref[0]


def _adaln_call(cond8, w_mod, b_mod3):
    depth, d, n = w_mod.shape
    nb = MOD_COL_BLOCK
    return pl.pallas_call(
        _adaln_kernel,
        grid=(depth, n // nb),
        in_specs=[pl.BlockSpec((8, d), lambda l, j: (0, 0)),
                  pl.BlockSpec((1, d, nb), lambda l, j: (l, 0, j)),
                  pl.BlockSpec((1, 1, nb), lambda l, j: (l, 0, j))],
        out_specs=pl.BlockSpec((1, 8, nb), lambda l, j: (l, 0, j)),
        out_shape=jax.ShapeDtypeStruct((depth, 8, n), F32),
        compiler_params=_params(2),
        name="adaln",
    )(cond8, w_mod, b_mod3)


def _rms_modulate(x, gain, shift, scale):
    y = x * lax.rsqrt(jnp.mean(x * x, axis=-1, keepdims=True) + EPS)
    return (y * gain) * (1.0 + scale) + shift


def _inproj_kernel(*refs, rope, cache, seq):
    if rope:
        x_ref, mod_ref, g_ref, w_ref, cos_ref, sin_ref, ulru_ref, qkv_ref = refs
    elif cache:
        x_ref, mod_ref, g_ref, w_ref = refs[:4]
        ulru_ref, qkv_ref, kw_ref, vw_ref, kn_ref, vn_ref = refs[-6:]
    else:
        x_ref, mod_ref, g_ref, w_ref, ulru_ref, qkv_ref = refs
    tb = x_ref.shape[1]
    rc = seq if cache else PROJ_ROW_CHUNK
    for r in range(tb // rc):
        rows = slice(r * rc, (r + 1) * rc)
        h = _rms_modulate(x_ref[0, rows], g_ref[0], mod_ref[0, 0, 0:1, :], mod_ref[0, 0, 1:2, :])
        u = _dot(h.astype(BF16), w_ref[0])
        ulru_ref[0, rows] = u[:, :LRU_W]
        if rope:
            lo, hi = LRU_W + QB0, LRU_W + VB0
            qk = u[:, lo:hi]
            lane = lax.broadcasted_iota(jnp.int32, (rc, hi - lo), 1)
            first = (lane & 31) < 16
            partner = jnp.where(first, pltpu.roll(qk, hi - lo - 16, 1), pltpu.roll(qk, 16, 1))
            qk = qk * cos_ref[rows, :] + partner * sin_ref[rows, :]
            qkv_ref[0, rows, :QB0] = u[:, LRU_W:lo].astype(BF16)
            qkv_ref[0, rows, QB0:VB0] = qk.astype(BF16)
            qkv_ref[0, rows, VB0:] = u[:, hi:].astype(BF16)
        else:
            qkv_ref[0, rows] = u[:, LRU_W:].astype(BF16)
        if cache:
            for ref, c0, heads in ((kw_ref, KB0, KV_B), (vw_ref, VB0, KV_B), (kn_ref, KC0, H_C), (vn_ref, VC0, H_C)):
                for p in range(heads // 2):
                    pair_t = u[:, LRU_W + c0 + LANES * p: LRU_W + c0 + LANES * (p + 1)].T
                    ref[r, 0, 2 * p] = pair_t[:HEAD_DIM]
                    ref[r, 0, 2 * p + 1] = pair_t[HEAD_DIM:]


def _inproj_call(x3, mods, mod_row0, gains, w_in_b, layer, rope_tabs=None, cache_seq=None, depth=1, prev_caches=None):
    g, n, d = x3.shape
    d_in = w_in_b.shape[2]
    tb = TOKEN_BLOCK
    rope = rope_tabs is not None
    cache = cache_seq is not None
    in_specs = [pl.BlockSpec((1, tb, d), lambda i, j: (i, j, 0)),
                pl.BlockSpec((1, 1, N_MOD, d), lambda i, j: (layer, mod_row0 + i, 0, 0)),
                pl.BlockSpec((1, 1, d), lambda i, j: (layer, 0, 0)),
                pl.BlockSpec((1, d, d_in), lambda i, j: (layer, 0, 0))]
    args = [x3, mods, gains, w_in_b]
    assert d_in == LRU_W + ATT_W
    out_shape = [jax.ShapeDtypeStruct((g, n, LRU_W), F32),
                 jax.ShapeDtypeStruct((g, n, ATT_W), BF16)]
    out_specs = [pl.BlockSpec((1, tb, LRU_W), lambda i, j: (i, j, 0)),
                 pl.BlockSpec((1, tb, ATT_W), lambda i, j: (i, j, 0))]
    if rope:
        in_specs += [pl.BlockSpec((tb, VB0 - QB0), lambda i, j: (j, 0))] * 2
        args += list(rope_tabs)
    aliases = {}
    if cache:
        assert g == 1 and tb % cache_seq == 0
        rb = tb // cache_seq
        nreq = n // cache_seq
        for heads in (KV_B, KV_B, H_C, H_C):
            out_shape.append(jax.ShapeDtypeStruct((nreq, depth, heads, HEAD_DIM, cache_seq), F32))
            out_specs.append(pl.BlockSpec((rb, 1, heads, HEAD_DIM, cache_seq), lambda i, j: (j, layer, 0, 0, 0)))
        if prev_caches is not None:
            for k, buf in enumerate(prev_caches):
                aliases[len(args)] = 2 + k
                in_specs.append(pl.BlockSpec(memory_space=pl.ANY))
                args.append(buf)
    return pl.pallas_call(
        functools.partial(_inproj_kernel, rope=rope, cache=cache, seq=cache_seq),
        grid=(g, n // tb),
        in_specs=in_specs,
        out_specs=out_specs,
        out_shape=out_shape,
        input_output_aliases=aliases,
        compiler_params=_params(2),
        name="inproj_rope" if rope else "inproj_ctx",
    )(*args)


def _lru_kernel(*refs, n):
    u_ref, cw_ref, cb_ref, wg_ref, bg_ref, lam_ref, h0_ref = refs[:7]
    ya_ref, st_ref, a_s, x_s, y_s, hloc_s, ploc_s, hend_s, pend_s, cin_s = refs[-10:]
    c = D_A
    nblk = n // LRU_SEG
    nh = c // LANES
    u = u_ref[0]
    xa = u[:, :c]
    ga = u[:, c:]
    t = lax.broadcasted_iota(jnp.int32, (n, c), 0)
    cw = cw_ref[0]
    xc = cw[2:3] * xa + cb_ref[0]
    xc = xc + cw[0:1] * jnp.where(t >= 2, pltpu.roll(xa, 2, 0), 0.0)
    xc = xc + cw[1:2] * jnp.where(t >= 1, pltpu.roll(xa, 1, 0), 0.0)
    xc = xc + cw[3:4] * jnp.where(t < n - 1, pltpu.roll(xa, n - 1, 0), 0.0)
    gates = _dot(xc.astype(BF16), wg_ref[0]) + bg_ref[0]
    lam = lam_ref[0]
    log_sig = jnp.minimum(lam, 0.0) - jnp.log1p(jnp.exp(-jnp.abs(lam)))
    for d in range(2):
        r = 0.5 * jnp.tanh(0.5 * gates[:, 2 * d * c:(2 * d + 1) * c]) + 0.5
        i = 0.5 * jnp.tanh(0.5 * gates[:, (2 * d + 1) * c:(2 * d + 2) * c]) + 0.5
        log_a = LRU_C * r * log_sig[d:d + 1]
        a = jnp.exp(log_a)
        t_in = jnp.tanh(-log_a) * (1.0 + a * a)
        xin = jnp.where(t_in > 0.0, t_in * lax.rsqrt(t_in), 0.0) * (i * xc)
        for hf in range(nh):
            a_s[hf] = a[:, LANES * hf: LANES * (hf + 1)]
            x_s[hf] = xin[:, LANES * hf: LANES * (hf + 1)]
        order = range(LRU_SEG) if d == 0 else range(LRU_SEG - 1, -1, -1)
        for hf in range(nh):
            h_run = p_run = None
            for s in order:
                a_row = a_s[hf, pl.ds(s, nblk, stride=LRU_SEG), :]
                x_row = x_s[hf, pl.ds(s, nblk, stride=LRU_SEG), :]
                h_run = x_row if h_run is None else a_row * h_run + x_row
                p_run = a_row if p_run is None else a_row * p_run
                hloc_s[d, hf, s] = h_run
                ploc_s[d, hf, s] = p_run
            hend_s[d, hf] = h_run
            pend_s[d, hf] = p_run

    h0 = h0_ref[0, 0]
    init = tuple(h0[d:d + 1, LANES * hf: LANES * (hf + 1)] for d in range(2) for hf in range(nh))

    def body(k, carry):
        out = []
        for idx, cur in enumerate(carry):
            d, hf = divmod(idx, nh)
            kk = k if d == 0 else nblk - 1 - k
            cin_s[d, hf, pl.ds(kk, 1), :] = cur
            out.append(pend_s[d, hf, pl.ds(kk, 1), :] * cur + hend_s[d, hf, pl.ds(kk, 1), :])
        return tuple(out)

    final = lax.fori_loop(0, nblk, body, init)

    y = None
    for d in range(2):
        for hf in range(nh):
            cin = cin_s[d, hf]
            for s in range(LRU_SEG):
                full_h = hloc_s[d, hf, s] + ploc_s[d, hf, s] * cin
                if d == 0:
                    y_s[hf, pl.ds(s, nblk, stride=LRU_SEG), :] = full_h
                else:
                    y_s[hf, pl.ds(s, nblk, stride=LRU_SEG), :] += full_h
    y = jnp.concatenate([y_s[hf] for hf in range(nh)], axis=1) * jax.nn.gelu(ga)
    ya_ref[0] = y.astype(BF16)
    st_ref[0, 0, 0:1, :] = jnp.concatenate(final[:nh], axis=1)
    st_ref[0, 0, 1:2, :] = jnp.concatenate(final[nh:], axis=1)


def _lru_call(ulru, conv_w, conv_b, w_gates_b, b_gates, lam, layer, h0, h0_layer, st_layer=0, st_depth=1,
              prev_state=None):
    b, n, _ = ulru.shape
    c = D_A
    per_layer = lambda shape: pl.BlockSpec((1,) + shape, lambda i: (layer,) + (0,) * len(shape))
    nh, nblk = c // LANES, n // LRU_SEG
    assert n % LRU_SEG == 0 and nblk % 8 == 0
    scratch = ([pltpu.VMEM((nh, n, LANES), F32)] * 3
               + [pltpu.VMEM((2, nh, LRU_SEG, nblk, LANES), F32)] * 2
               + [pltpu.VMEM((2, nh, nblk, LANES), F32)] * 3)
    in_specs = [pl.BlockSpec((1, n, 2 * c), lambda i: (i, 0, 0)),
                per_layer((CONV_W, c)), per_layer((1, c)), per_layer((c, 4 * c)), per_layer((1, 4 * c)),
                per_layer((2, c)),
                pl.BlockSpec((1, 1, 2, c), lambda i: (i, h0_layer, 0, 0))]
    args = [ulru, conv_w, conv_b, w_gates_b, b_gates, lam, h0]
    aliases = {}
    if prev_state is not None:
        aliases[len(args)] = 1
        in_specs.append(pl.BlockSpec(memory_space=pl.ANY))
        args.append(prev_state)
    return pl.pallas_call(
        functools.partial(_lru_kernel, n=n),
        grid=(b,),
        in_specs=in_specs,
        out_specs=[pl.BlockSpec((1, n, c), lambda i: (i, 0, 0)),
                   pl.BlockSpec((1, 1, 2, c), lambda i: (i, st_layer, 0, 0))],
        out_shape=[jax.ShapeDtypeStruct((b, n, c), BF16),
                   jax.ShapeDtypeStruct((b, st_depth, 2, c), F32)],
        scratch_shapes=scratch,
        input_output_aliases=aliases,
        compiler_params=_params(1),
        name="rglru",
    )(*args)


def _with_ones(v, transposed=False):
    axis = 0 if transposed else 1
    w = v.shape[axis]
    pieces = [v]
    if w < LANES:
        pieces.append(jnp.zeros(v.shape[:axis] + (LANES - w,) + v.shape[axis + 1:], v.dtype))
    pieces.append(jnp.ones(v.shape[:axis] + (LANES,) + v.shape[axis + 1:], v.dtype))
    return jnp.concatenate(pieces, axis=axis)


def _softmax_pv(parts, sink):
    mx = None
    for s, _, _ in parts:
        cur = jnp.max(s, axis=-1, keepdims=True)
        mx = cur if mx is None else jnp.maximum(mx, cur)
    if sink is not None:
        mx = jnp.maximum(mx, sink)
    out = None
    for s, v, v_t in parts:
        p = jnp.exp(s - mx).astype(BF16)
        o = _dot_nt(p, v) if v_t else _dot(p, v)
        out = o if out is None else out + o
    den = out[:, LANES:]
    if sink is not None:
        den = den + jnp.exp(sink - mx)
    return out[:, :LANES] / den


def _swap_halves(x):
    return jnp.concatenate([x[:, HEAD_DIM:], x[:, :HEAD_DIM]], axis=1)


def _keep_half(x, lo_mask, half):
    zero = jnp.zeros_like(x)
    return jnp.where(lo_mask, x, zero) if half == 0 else jnp.where(lo_mask, zero, x)


def _ctx_attn_kernel(sink_ref, att_ref, ob_ref, oc_ref, *, n, rb, sink0):
    lo = lax.broadcasted_iota(jnp.int32, (n, LANES), 1) < HEAD_DIM
    for r in range(rb):
        for p in range(H_C // 2):
            qp = att_ref[r, :, QC0 + LANES * p: QC0 + LANES * (p + 1)]
            kp = att_ref[r, :, KC0 + LANES * p: KC0 + LANES * (p + 1)]
            vp = _with_ones(att_ref[r, :, VC0 + LANES * p: VC0 + LANES * (p + 1)])
            outs = [_softmax_pv([(_dot_nt(qp, _keep_half(kp, lo, half)), vp, False)], None) for half in range(2)]
            oc_ref[r, :, LANES * p: LANES * (p + 1)] = jnp.where(lo, outs[0], outs[1]).astype(BF16)
        kpair = att_ref[r, :, KB0: KB0 + LANES]
        vpair = att_ref[r, :, VB0: VB0 + LANES]
        kpair_sw = _swap_halves(kpair)
        vpair_ext = _with_ones(vpair)
        vpair_sw_ext = _with_ones(_swap_halves(vpair))
        for p in range(H_B // 2):
            qp = att_ref[r, :, QB0 + LANES * p: QB0 + LANES * (p + 1)]
            outs = []
            for half in range(2):
                h = 2 * p + half
                aligned = (h // G_B) == half
                ksrc, vsrc = (kpair, vpair_ext) if aligned else (kpair_sw, vpair_sw_ext)
                outs.append(_softmax_pv([(_dot_nt(qp, _keep_half(ksrc, lo, half)), vsrc, False)], sink_ref[sink0 + h]))
            ob_ref[r, :, LANES * p: LANES * (p + 1)] = jnp.where(lo, outs[0], outs[1]).astype(BF16)


def _ctx_attn_call(sinks, att, layer):
    b, n, w = att.shape
    rb = CTX_ATTN_REQS
    assert b % rb == 0 and KV_B == 2
    return pl.pallas_call(
        functools.partial(_ctx_attn_kernel, n=n, rb=rb, sink0=layer * H_B),
        grid=(b // rb,),
        in_specs=[pl.BlockSpec(memory_space=pltpu.SMEM),
                  pl.BlockSpec((rb, n, w), lambda i: (i, 0, 0))],
        out_specs=[pl.BlockSpec((rb, n, D_B), lambda i: (i, 0, 0)), pl.BlockSpec((rb, n, D_C), lambda i: (i, 0, 0))],
        out_shape=[jax.ShapeDtypeStruct((b, n, D_B), BF16), jax.ShapeDtypeStruct((b, n, D_C), BF16)],
        compiler_params=_params(1),
        name="ctx_attn",
    )(sinks, att)


def _win_attn_kernel(sink_ref, qkv_ref, kc_ref, vc_ref, o_ref, *, n, sink0):
    j = pl.program_id(1)
    nq = WIN_Q_BLOCKS * WIN_BLK
    nloc = nq + 2 * WIN_BLK
    q0 = pl.multiple_of(j * nq, WIN_BLK)
    ks = pl.multiple_of(jnp.clip(j * nq - WIN_BLK, 0, n - nloc), WIN_BLK)
    row = lax.broadcasted_iota(jnp.int32, (nq, nloc), 0)
    col = lax.broadcasted_iota(jnp.int32, (nq, nloc), 1)
    in_window = jnp.where(jnp.abs((q0 + row) - (ks + col)) <= WINDOW, 0.0, NEG_INF)
    band = jnp.concatenate([in_window] * G_B, axis=0)
    rowh = lax.broadcasted_iota(jnp.int32, (G_B * nq, 1), 0) // nq
    for kv in range(KV_B):
        gw = G_B * HEAD_DIM
        qblk = qkv_ref[0, pl.ds(q0, nq), QB0 + gw * kv: QB0 + gw * (kv + 1)]
        q3 = jnp.concatenate([qblk[:, 64 * g: 64 * (g + 1)] for g in range(G_B)], axis=0)
        kl = qkv_ref[0, pl.ds(ks, nloc), KB0 + 64 * kv: KB0 + 64 * (kv + 1)]
        vl = _with_ones(qkv_ref[0, pl.ds(ks, nloc), VB0 + 64 * kv: VB0 + 64 * (kv + 1)])
        kc_t = kc_ref[0, 0, kv].astype(BF16)
        vc_t = _with_ones(vc_ref[0, 0, kv].astype(BF16), transposed=True)
        s_loc = _dot_nt(q3, kl) + band
        s_ctx = _dot(q3, kc_t)
        s0 = sink0 + G_B * kv
        sk = jnp.where(rowh == 0, sink_ref[s0], jnp.where(rowh == 1, sink_ref[s0 + 1], sink_ref[s0 + 2]))
        o = _softmax_pv([(s_loc, vl, False), (s_ctx, vc_t, True)], sk)
        for g in range(G_B):
            h = kv * G_B + g
            o_ref[0, :, 64 * h: 64 * (h + 1)] = o[g * nq:(g + 1) * nq, :HEAD_DIM].astype(BF16)


def _win_attn_call(sinks, qkv, cache_k, cache_v, layer):
    b, n, w = qkv.shape
    past = cache_k.shape[4]
    cache_spec = pl.BlockSpec((1, 1, KV_B, HEAD_DIM, past), lambda i, j: (i, layer, 0, 0, 0))
    nq = WIN_Q_BLOCKS * WIN_BLK
    assert n % nq == 0 and n >= nq + 2 * WIN_BLK and WINDOW == WIN_BLK
    return pl.pallas_call(
        functools.partial(_win_attn_kernel, n=n, sink0=layer * H_B),
        grid=(b, n // nq),
        in_specs=[pl.BlockSpec(memory_space=pltpu.SMEM),
                  pl.BlockSpec((1, n, w), lambda i, j: (i, 0, 0)),
                  cache_spec, cache_spec],
        out_specs=pl.BlockSpec((1, nq, D_B), lambda i, j: (i, j, 0)),
        out_shape=jax.ShapeDtypeStruct((b, n, D_B), BF16),
        compiler_params=_params(2),
        name="win_attn",
    )(sinks, qkv, cache_k, cache_v)


def _nbr_window_start(g, rows):
    return jnp.clip(g * NBR_Q_ROWS - NA_ROWS // 2, 0, rows - NBR_K_ROWS)


def _nbr_attn_kernel(q_ref, k_ref, v_ref, rel_ref, kc_ref, vc_ref, o_ref, bias_s, *, n):
    g = pl.program_id(0)
    nq = NBR_Q_ROWS * GRID_W
    nk = NBR_K_ROWS * GRID_W
    rows = n // GRID_W
    ws = _nbr_window_start(g, rows)
    k0 = pl.multiple_of(ws * GRID_W, GRID_W)

    @pl.when(pl.program_id(1) == 0)
    def _():
        for i in range(NBR_Q_ROWS):
            qr = g * NBR_Q_ROWS + i
            rs = jnp.clip(qr - NA_ROWS // 2, 0, rows - NA_ROWS)
            for jp in range(NBR_K_ROWS // 2):
                tiles = []
                for j in (2 * jp, 2 * jp + 1):
                    kr = ws + j
                    valid = (kr >= rs) & (kr < rs + NA_ROWS)
                    d = jnp.clip(kr - qr + NA_ROWS - 1, 0, 2 * NA_ROWS - 2)
                    tiles.append(jnp.where(valid, rel_ref[0, :, pl.ds(d, 1)][:, 0], NEG_INF))
                bias_s[:, GRID_W * i: GRID_W * (i + 1), LANES * jp: LANES * (jp + 1)] = jnp.concatenate(tiles, axis=-1)
    lo_k = lax.broadcasted_iota(jnp.int32, (nk, LANES), 1) < HEAD_DIM
    lo_q = lax.broadcasted_iota(jnp.int32, (nq, LANES), 1) < HEAD_DIM
    zpad = jnp.zeros((HEAD_DIM, kc_ref.shape[4]), BF16)
    for p in range(H_C // 2):
        qp = q_ref[0, :, LANES * p: LANES * (p + 1)]
        kp = k_ref[0, pl.ds(k0, nk), LANES * p: LANES * (p + 1)]
        vp = _with_ones(v_ref[0, pl.ds(k0, nk), LANES * p: LANES * (p + 1)])
        outs = []
        for half in range(2):
            h = 2 * p + half
            kc_t = kc_ref[0, 0, h].astype(BF16)
            vc_t = vc_ref[0, 0, h].astype(BF16)
            kc_t = jnp.concatenate([kc_t, zpad] if half == 0 else [zpad, kc_t], axis=0)
            vc_t = _with_ones(jnp.concatenate([vc_t, zpad] if half == 0 else [zpad, vc_t], axis=0), transposed=True)
            s_loc = _dot_nt(qp, _keep_half(kp, lo_k, half)) + bias_s[h]
            s_ctx = _dot(qp, kc_t)
            outs.append(_softmax_pv([(s_loc, vp, False), (s_ctx, vc_t, True)], None))
        o_ref[0, :, LANES * p: LANES * (p + 1)] = jnp.where(lo_q, outs[0], outs[1]).astype(BF16)


def _nbr_attn_call(att, rel_tab, cache_k, cache_v, layer):
    b, n, w = att.shape
    past = cache_k.shape[4]
    nq = NBR_Q_ROWS * GRID_W
    nk = NBR_K_ROWS * GRID_W
    gw = H_C * HEAD_DIM
    rows = n // GRID_W
    assert (QC0, KC0, VC0) == (0, gw, 2 * gw) and n % nq == 0 and rows >= NBR_K_ROWS and NBR_K_ROWS % 2 == 0
    cache_spec = pl.BlockSpec((1, 1, H_C, HEAD_DIM, past), lambda g, i: (i, layer, 0, 0, 0))
    return pl.pallas_call(
        functools.partial(_nbr_attn_kernel, n=n),
        grid=(n // nq, b),
        in_specs=[pl.BlockSpec((1, nq, gw), lambda g, i: (i, g, 0)),
                  pl.BlockSpec((1, n, gw), lambda g, i: (i, 0, 1)),
                  pl.BlockSpec((1, n, gw), lambda g, i: (i, 0, 2)),
                  pl.BlockSpec((1,) + rel_tab.shape[1:], lambda g, i: (layer, 0, 0, 0, 0)),
                  cache_spec, cache_spec],
        out_specs=pl.BlockSpec((1, nq, gw), lambda g, i: (i, g, 0)),
        out_shape=jax.ShapeDtypeStruct((b, n, gw), BF16),
        scratch_shapes=[pltpu.VMEM((H_C, nq, nk), F32)],
        compiler_params=_params(2),
        name="nbr_attn",
    )(att, att, att, rel_tab, cache_k, cache_v)


def _outproj_kernel(x_ref, ya_ref, yb_ref, yc_ref, mod_ref, g_ref, wo_ref, wr_ref,
                    xn_ref, h2_ref, aff_ref):
    mix = jnp.concatenate([ya_ref[0], yb_ref[0], yc_ref[0]], axis=1)
    proj = _dot(mix, wo_ref[0])
    xn = x_ref[0] + mod_ref[0, 0, 2:3, :] * proj
    xn_ref[0] = xn
    h2 = _rms_modulate(xn, g_ref[0], mod_ref[0, 0, 3:4, :], mod_ref[0, 0, 4:5, :]).astype(BF16)
    h2_ref[0] = h2
    logits = _dot(h2, wr_ref[0])
    lane = lax.broadcasted_iota(jnp.int32, logits.shape, 1)
    valid = lane < N_EXPERTS
    logits = jnp.where(valid, logits, NEG_INF)
    e = jnp.exp(logits - jnp.max(logits, axis=-1, keepdims=True))
    e = jnp.where(valid, e, 0.0)
    aff_ref[0] = e / jnp.sum(e, axis=-1, keepdims=True)


def _outproj_call(x3, ya, yb, yc, mods, mod_row0, gains, wo_b16, w_router_b, layer):
    g, n, d = x3.shape
    tb = TOKEN_BLOCK
    tok = lambda w: pl.BlockSpec((1, tb, w), lambda i, j: (i, j, 0))
    per_layer = lambda shape: pl.BlockSpec((1,) + shape, lambda i, j: (layer,) + (0,) * len(shape))
    return pl.pallas_call(
        _outproj_kernel,
        grid=(g, n // tb),
        in_specs=[tok(d), tok(D_A), tok(D_B), tok(D_C),
                  pl.BlockSpec((1, 1, N_MOD, d), lambda i, j: (layer, mod_row0 + i, 0, 0)),
                  per_layer((1, d)), per_layer((D_A + D_B + D_C, d)), per_layer((d, LANES))],
        out_specs=[tok(d), tok(d), tok(LANES)],
        out_shape=[jax.ShapeDtypeStruct((g, n, d), F32),
                   jax.ShapeDtypeStruct((g, n, d), BF16),
                   jax.ShapeDtypeStruct((g, n, LANES), F32)],
        compiler_params=_params(2),
        name="outproj_router",
    )(x3, ya, yb, yc, mods, gains, wo_b16, w_router_b)


def _sort_by_affinity(keys, idx, pos, levels, seg_lanes):
    rows = keys.shape[0]
    sign = [jnp.where(((pos >> m) & 1) == 0, 1.0, -1.0) for m in range(levels)]

    def partner(x, j):
        d = 1 << j
        low = sign[j] > 0.0
        if d >= rows:
            sh = (d // rows) * seg_lanes
            return jnp.where(low, pltpu.roll(x, LANES - sh, 1), pltpu.roll(x, sh, 1))
        if d >= 8:
            x4 = x.reshape(rows // (2 * d), 2, d, LANES)
            return jnp.concatenate([x4[:, 1:2], x4[:, 0:1]], axis=1).reshape(rows, LANES)
        x3 = x.reshape(rows // 8, 8, LANES)
        low3 = sign[j].reshape(rows // 8, 8, LANES) > 0.0
        return jnp.where(low3, pltpu.roll(x3, 8 - d, 1), pltpu.roll(x3, d, 1)).reshape(rows, LANES)

    for k in range(1, levels + 1):
        for j in range(k - 1, -1, -1):
            pk, pi = partner(keys, j), partner(idx, j)
            before = jnp.where(keys == pk, jnp.where(idx < pi, 1.0, -1.0), jnp.where(keys > pk, 1.0, -1.0))
            want_first = sign[j] * sign[k] if k < levels else sign[j]
            keep = before * want_first > 0.0
            keys = jnp.where(keep, keys, pk)
            idx = jnp.where(keep, idx, pi)
    return keys, idx


def _route_kernel(aff_ref, h2_ref, xs_ref, vals_ref, rank_ref, *, n, cap, rb, by_slot):
    ne = N_EXPERTS
    rows = ROUTE_SORT_ROWS
    nseg = n // rows
    seg_lanes = rb * ne
    levels = n.bit_length() - 1
    fcap = float(cap)
    row = lax.broadcasted_iota(jnp.int32, (rows, LANES), 0)
    lane = lax.broadcasted_iota(jnp.int32, (rows, LANES), 1)
    pos = (lane // seg_lanes) * rows + row
    keys = jnp.full((rows, LANES), -1.0, F32)
    for r in range(rb):
        for seg in range(nseg):
            off = seg * seg_lanes + r * ne
            piece = aff_ref[r, seg * rows:(seg + 1) * rows, :]
            if off:
                piece = pltpu.roll(piece, off, 1)
            keys = jnp.where((lane >= off) & (lane < off + ne), piece, keys)
    keys, idx = _sort_by_affinity(keys, pos.astype(F32), pos, levels, seg_lanes)
    top_keys = keys[:cap]
    top_idx = idx[:cap]

    tok = lax.broadcasted_iota(jnp.int32, (cap, n), 1).astype(F32)
    weight = fcap - lax.broadcasted_iota(jnp.int32, (cap, n), 0).astype(F32)
    pad = jnp.full((LANES - ne, n), fcap, F32)
    if by_slot:
        top_idx_t = top_idx.T
    for r in range(rb):
        picks, rank_rows = [], []
        for e in range(ne):
            col = r * ne + e
            onehot = jnp.where(top_idx[:, col:col + 1] == tok, 1.0, 0.0)
            vals_ref[e, r] = top_keys[:, col:col + 1]
            if not by_slot:
                rank_rows.append(fcap - jnp.sum(onehot * weight, axis=0, keepdims=True))
            picks.append(onehot.astype(BF16))
        if by_slot:
            rank_ref[r] = top_idx_t[r * ne:(r + 1) * ne]
        else:
            rank_ref[r] = jnp.concatenate(rank_rows + [pad], axis=0).T
        xs = _dot(jnp.concatenate(picks, axis=0), h2_ref[r])
        xs_ref[:, r] = xs.reshape(ne, cap, xs.shape[-1]).astype(BF16)


def _route_call(aff, h2, rb):
    b, n, d = h2.shape
    cap = max(1, EC_FACTOR * n // N_EXPERTS)
    nseg = n // ROUTE_SORT_ROWS
    assert b % rb == 0 and n == nseg * ROUTE_SORT_ROWS and n & (n - 1) == 0
    assert nseg * rb * N_EXPERTS <= LANES and cap <= ROUTE_SORT_ROWS
    by_slot = cap == LANES
    assign_shape = (N_EXPERTS, cap) if by_slot else (n, LANES)
    return pl.pallas_call(
        functools.partial(_route_kernel, n=n, cap=cap, rb=rb, by_slot=by_slot),
        grid=(b // rb,),
        in_specs=[pl.BlockSpec((rb, n, LANES), lambda i: (i, 0, 0)),
                  pl.BlockSpec((rb, n, d), lambda i: (i, 0, 0))],
        out_specs=[pl.BlockSpec((N_EXPERTS, rb, cap, d), lambda i: (0, i, 0, 0)),
                   pl.BlockSpec((N_EXPERTS, rb, cap, 1), lambda i: (0, i, 0, 0)),
                   pl.BlockSpec((rb,) + assign_shape, lambda i: (i, 0, 0))],
        out_shape=[jax.ShapeDtypeStruct((N_EXPERTS, b, cap, d), BF16),
                   jax.ShapeDtypeStruct((N_EXPERTS, b, cap, 1), F32),
                   jax.ShapeDtypeStruct((b,) + assign_shape, F32)],
        compiler_params=_params(1),
        name="route_gather",
    )(aff, h2)


def _expert_kernel(xc_ref, xl_ref, vc_ref, vl_ref, wg_ref, wu_ref, wd_ref, yc_ref, yl_ref, acc_c, acc_l):
    f = pl.program_id(1)

    def step(first):
        wg = wg_ref[0, 0].astype(BF16)
        wu = wu_ref[0, 0].astype(BF16)
        wd = wd_ref[0, 0].astype(BF16)
        for x_ref, v_ref, y_ref, acc in ((xc_ref, vc_ref, yc_ref, acc_c), (xl_ref, vl_ref, yl_ref, acc_l)):
            rows = x_ref.shape[1]
            chunk = min(EXPERT_ROW_CHUNK, rows)
            for c0 in range(0, rows, chunk):
                sl = slice(c0, c0 + chunk)
                x = x_ref[0, sl, :]
                a = _dot(x, wg)
                u = _dot(x, wu)
                act = ((a * jax.nn.sigmoid(a)) * u).astype(BF16)
                part = _dot(act, wd)
                if first:
                    acc[sl, :] = part
                else:
                    y_ref[0, sl, :] = ((acc[sl, :] + part) * v_ref[0, sl, :]).astype(BF16)

    pl.when(f == 0)(functools.partial(step, True))
    pl.when(f == EXPERT_F_STEPS - 1)(functools.partial(step, False))


def _expert_call(xs_c, xs_l, vals_c, vals_l, w_gate, w_up, w_down, layer):
    e, rc, d = xs_c.shape
    rl = xs_l.shape[1]
    f_total = w_gate.shape[-1]
    assert EXPERT_F_STEPS == 2
    fb = f_total // EXPERT_F_STEPS
    per_e = lambda rows, w: pl.BlockSpec((1, rows, w), lambda i, j: (i, 0, 0))
    return pl.pallas_call(
        _expert_kernel,
        grid=(e, EXPERT_F_STEPS),
        in_specs=[per_e(rc, d), per_e(rl, d), per_e(rc, 1), per_e(rl, 1),
                  pl.BlockSpec((1, 1, d, fb), lambda i, j: (layer, i, 0, j)),
                  pl.BlockSpec((1, 1, d, fb), lambda i, j: (layer, i, 0, j)),
                  pl.BlockSpec((1, 1, fb, d), lambda i, j: (layer, i, j, 0))],
        out_specs=[per_e(rc, d), per_e(rl, d)],
        out_shape=[jax.ShapeDtypeStruct((e, rc, d), BF16),
                   jax.ShapeDtypeStruct((e, rl, d), BF16)],
        scratch_shapes=[pltpu.VMEM((rc, d), F32), pltpu.VMEM((rl, d), F32)],
        compiler_params=_params(2),
        name="experts",
    )(xs_c, xs_l, vals_c, vals_l, w_gate, w_up, w_down)


def _combine_kernel(y_ref, rank_ref, xn_ref, mod_ref, gf_ref, o_ref, *, n, cap, final, rb, by_slot):
    ec = N_EXPERTS * cap
    d = xn_ref.shape[-1]
    if by_slot:
        tok = lax.broadcasted_iota(jnp.int32, (n, cap), 0).astype(F32)
    else:
        ce = lax.broadcasted_iota(jnp.int32, (LANES, ec), 1) // cap
        ee = lax.broadcasted_iota(jnp.int32, (LANES, ec), 0)
        expand = jnp.where(ce == ee, 1.0, 0.0).astype(BF16)
        slot = (lax.broadcasted_iota(jnp.int32, (n, ec), 1) % cap).astype(F32)
    for r in range(rb):
        if by_slot:
            chosen = rank_ref[r]
            onehot = jnp.concatenate([jnp.where(tok == chosen[e:e + 1, :], 1.0, 0.0).astype(BF16)
                                      for e in range(N_EXPERTS)], axis=1)
        else:
            rank = rank_ref[r].astype(BF16)
            rexp = _dot(rank, expand)
            onehot = jnp.where(rexp == slot, 1.0, 0.0).astype(BF16)
        moe = _dot(onehot, y_ref[:, r].reshape(ec, d))
        x = xn_ref[r] + mod_ref[0, 0, 5:6, :] * moe
        if final:
            x = (x * lax.rsqrt(jnp.mean(x * x, axis=-1, keepdims=True) + EPS)) * gf_ref[...]
        o_ref[r] = x


def _combine_call(y4, rank, xn, mods, mod_row0, shared_mod, layer, g_final, final):
    e, b, cap, d = y4.shape
    n = xn.shape[1]
    rb = COMBINE_REQS_CTX if shared_mod else 1
    by_slot = rank.shape[1:] == (e, cap) and cap == LANES
    assert b % rb == 0 and (by_slot or rank.shape[1:] == (n, LANES))
    return pl.pallas_call(
        functools.partial(_combine_kernel, n=n, cap=cap, final=final, rb=rb, by_slot=by_slot),
        grid=(b // rb,),
        in_specs=[pl.BlockSpec((e, rb, cap, d), lambda i: (0, i, 0, 0)),
                  pl.BlockSpec((rb,) + rank.shape[1:], lambda i: (i, 0, 0)),
                  pl.BlockSpec((rb, n, d), lambda i: (i, 0, 0)),
                  pl.BlockSpec((1, 1, N_MOD, d), (lambda i: (layer, mod_row0, 0, 0)) if shared_mod
                               else (lambda i: (layer, mod_row0 + i, 0, 0))),
                  pl.BlockSpec((1, d), lambda i: (0, 0))],
        out_specs=pl.BlockSpec((rb, n, d), lambda i: (i, 0, 0)),
        out_shape=jax.ShapeDtypeStruct((b, n, d), F32),
        compiler_params=_params(1),
        name="combine",
    )(y4, rank, xn, mods, g_final)


def _rope_tables(n):
    t = jnp.arange(n, dtype=jnp.int32)
    row = (t // GRID_W).astype(F32)
    col = (t % GRID_W).astype(F32)
    half = HEAD_DIM // 4
    inv = jnp.power(ROPE_BASE, -jnp.arange(half, dtype=F32) / half)
    ang_r = row[:, None] * inv[None, :]
    ang_c = col[:, None] * inv[None, :]
    cos_h = jnp.concatenate([jnp.cos(ang_r)] * 2 + [jnp.cos(ang_c)] * 2, axis=-1)
    sin_h = jnp.concatenate([-jnp.sin(ang_r), jnp.sin(ang_r), -jnp.sin(ang_c), jnp.sin(ang_c)], axis=-1)
    reps = H_B + KV_B
    return jnp.tile(cos_h, (1, reps)), jnp.tile(sin_h, (1, reps))


def _nbr_rel_tables(rpb):
    col = np.arange(GRID_W)
    cs = np.clip(col - NA_COLS // 2, 0, GRID_W - NA_COLS)
    col_mask = (col[None, :] >= cs[:, None]) & (col[None, :] < cs[:, None] + NA_COLS)
    dc_idx = np.clip(col[None, :] - col[:, None], -(NA_COLS - 1), NA_COLS - 1) + (NA_COLS - 1)
    sel_col = (dc_idx[None, :, :] == np.arange(2 * NA_COLS - 1)[:, None, None]).astype(np.float32)
    t = jnp.einsum("lhdc,cqk->lhdqk", rpb.astype(F32), sel_col, precision=lax.Precision.HIGHEST)
    return jnp.where(col_mask, t, NEG_INF)


def _block_diag(w):
    nb, bw = w.shape[-3], w.shape[-1]
    eye = jnp.eye(nb, dtype=w.dtype)
    return (eye[:, None, :, None] * w[..., :, :, None, :]).reshape(w.shape[:-3] + (nb * bw, nb * bw))


def kernel(x_prompt, x_sample, state_lru, cache_k_win, cache_v_win, cache_k_nbr, cache_v_nbr, c, c_ctx, w_mod, b_mod, g_norm1, w_in, conv_w, conv_b, w_gate_r, b_gate_r, w_gate_i, b_gate_i, lru_lambda, sink_logit, nbr_bias, w_out, g_norm2, w_router, w_exp_gate, w_exp_up, w_exp_down, g_final):
    bc, seq, d = x_prompt.shape
    bl, n_lat, _ = x_sample.shape
    depth = w_mod.shape[0]
    assert bl + 1 <= 8 and d == 1024

    cond8 = jnp.zeros((8, d), F32).at[0].set(c_ctx).at[1:1 + bl].set(c)
    mods = _adaln_call(cond8, w_mod, b_mod.reshape(depth, 1, -1)).reshape(depth, 8, N_MOD, d)
    rope_tabs = _rope_tables(n_lat)
    bias_tab = _nbr_rel_tables(nbr_bias)
    ckw_t, cvw_t, ckn_t, cvn_t = (jnp.swapaxes(a, -1, -2) for a in (cache_k_win, cache_v_win, cache_k_nbr, cache_v_nbr))
    gf = g_final.reshape(1, d)
    assert SCALE == 0.125
    new_cols = np.arange(w_in.shape[-1]) - LRU_W
    is_q = ((new_cols >= QC0) & (new_cols < KC0)) | ((new_cols >= QB0) & (new_cols < KB0))
    q_scale = jnp.asarray(np.where(is_q, SCALE, 1.0), F32)

    nbr0 = LRU_W + D_B + 2 * KV_W
    w_in_b = (jnp.concatenate([w_in[..., :LRU_W], w_in[..., nbr0:], w_in[..., LRU_W:nbr0]], axis=-1) * q_scale).astype(BF16)
    wo = w_out.astype(BF16)
    wr_b = jnp.pad(w_router.astype(BF16), ((0, 0), (0, 0), (0, LANES - N_EXPERTS)))
    gate_r, gate_i = _block_diag(w_gate_r), _block_diag(w_gate_i)
    w_gates = jnp.concatenate([gate_r[:, 0], gate_i[:, 0], gate_r[:, 1], gate_i[:, 1]], axis=-1).astype(BF16)
    b_gates = jnp.concatenate([b_gate_r[:, 0], b_gate_i[:, 0], b_gate_r[:, 1], b_gate_i[:, 1]], axis=-1)[:, None, :]
    cb = conv_b[:, None, :]
    g1 = g_norm1[:, None, :]
    g2 = g_norm2[:, None, :]
    sinks = sink_logit.reshape(-1)
    ctx_row, lat_row = 0, 1

    xc = x_prompt.reshape(1, bc * seq, d)
    xl = x_sample
    zeros_state = jnp.zeros((bc, 1, 2, D_A), F32)
    caches = None
    st = None
    for l in range(depth):
        final = l == depth - 1
        ulru_c, qkv_c, *caches = _inproj_call(xc, mods, ctx_row, g1, w_in_b, l, cache_seq=seq, depth=depth,
                                              prev_caches=caches)
        ya_c, st = _lru_call(ulru_c.reshape(bc, seq, LRU_W), conv_w, cb, w_gates, b_gates, lru_lambda, l,
                             zeros_state, 0, st_layer=l, st_depth=depth, prev_state=st)
        yb_c, yc_c = _ctx_attn_call(sinks, qkv_c.reshape(bc, seq, -1), l)
        xn_c, h2_c, aff_c = _outproj_call(xc, ya_c.reshape(1, bc * seq, -1), yb_c.reshape(1, bc * seq, -1),
                                          yc_c.reshape(1, bc * seq, -1), mods, ctx_row, g2, wo, wr_b, l)
        xs_c, vals_c, rank_c = _route_call(aff_c.reshape(bc, seq, LANES), h2_c.reshape(bc, seq, d), ROUTE_REQS_CTX)

        ulru_l, qkv_l = _inproj_call(xl, mods, lat_row, g1, w_in_b, l, rope_tabs=rope_tabs)
        ya_l, _ = _lru_call(ulru_l, conv_w, cb, w_gates, b_gates, lru_lambda, l, state_lru, l)
        yb_l = _win_attn_call(sinks, qkv_l, ckw_t, cvw_t, l)
        yc_l = _nbr_attn_call(qkv_l, bias_tab, ckn_t, cvn_t, l)
        xn_l, h2_l, aff_l = _outproj_call(xl, ya_l, yb_l, yc_l, mods, lat_row, g2, wo, wr_b, l)
        xs_l, vals_l, rank_l = _route_call(aff_l, h2_l, ROUTE_REQS_LAT)

        cap_c, cap_l = xs_c.shape[2], xs_l.shape[2]
        y_c, y_l = _expert_call(xs_c.reshape(N_EXPERTS, bc * cap_c, d), xs_l.reshape(N_EXPERTS, bl * cap_l, d),
                                vals_c.reshape(N_EXPERTS, bc * cap_c, 1), vals_l.reshape(N_EXPERTS, bl * cap_l, 1),
                                w_exp_gate, w_exp_up, w_exp_down, l)
        xc = _combine_call(y_c.reshape(N_EXPERTS, bc, cap_c, d), rank_c, xn_c.reshape(bc, seq, d), mods, ctx_row, True,
                           l, gf, final)
        xl = _combine_call(y_l.reshape(N_EXPERTS, bl, cap_l, d), rank_l, xn_l, mods, lat_row, False, l, gf, final)
        xc = xc.reshape(1, bc * seq, d)

    y_prompt = xc.reshape(bc, seq, d)
    y_sample = xl
    return (y_prompt, y_sample, st, *[jnp.swapaxes(buf, -1, -2) for buf in caches])
```
